```python
import jax, jax.numpy as jnp
from jax import lax
import numpy as np

D_MODEL = 2048
BATCH = 1
SEQ = 16384
DEPTH = 1

MEM_LEN = 256
GLA_HEADS = 4
GLA_DK = 256
GLA_DV = 512
GLA_GATE_RANK = 16
GLA_GATE_TEMP = 16.0
GLA_CHUNK = 64
CONV_DIM = D_MODEL
CONV_WIDTH = 3
XATTN_HEADS = 4
XATTN_HEAD_DIM = 128
N_EXPERTS = 32
TOP_K = 4
D_FF = D_MODEL
SWIGLU_LIMIT = 7.0
SWIGLU_ALPHA = 1.702
MOE_BLOCK = 128
NORM_EPS = 1e-5

GLA_QK_DIM = GLA_HEADS * GLA_DK
GLA_V_DIM = GLA_HEADS * GLA_DV
XATTN_DIM = XATTN_HEADS * XATTN_HEAD_DIM
IN_SPLIT_SIZES = (GLA_QK_DIM, GLA_QK_DIM, GLA_V_DIM, GLA_V_DIM, GLA_GATE_RANK, CONV_DIM, CONV_DIM, CONV_DIM, D_MODEL, D_MODEL)
IN_PROJ_DIM = sum(IN_SPLIT_SIZES)

kernel_name = 'hybrid_gla_shortconv_xattn_moe'


def rms_norm(t, g):
    t32 = t.astype(jnp.float32)
    y = t32 * lax.rsqrt(jnp.mean(t32 * t32, axis=-1, keepdims=True) + NORM_EPS)
    return (y * g.astype(jnp.float32)).astype(t.dtype)


def gla_chunked(q, k, v, log_a):
    Bsz, S, H, DK = q.shape
    DV = v.shape[-1]
    nc = S // GLA_CHUNK

    def to_chunks(t):
        return t.reshape(Bsz, nc, GLA_CHUNK, H, t.shape[-1]).transpose(1, 0, 3, 2, 4)

    qc, kc, vc = to_chunks(q), to_chunks(k), to_chunks(v)
    bc = lax.cumsum(to_chunks(log_a.astype(jnp.float32)), axis=3)
    causal = jnp.tril(jnp.ones((GLA_CHUNK, GLA_CHUNK), bool))[:, :, None]

    def step(state, inp):
        qi, ki, vi, bi = inp
        o_inter = jnp.einsum('bhck,bhkv->bhcv', qi * jnp.exp(bi), state)
        diff = bi[:, :, :, None, :] - bi[:, :, None, :, :]
        decay = jnp.where(causal, jnp.exp(jnp.minimum(diff, 0.0)), 0.0)
        scores = jnp.einsum('bhik,bhjk,bhijk->bhij', qi, ki, decay)
        o_intra = jnp.einsum('bhij,bhjv->bhiv', scores, vi)
        b_last = bi[:, :, -1:, :]
        new_state = jnp.exp(b_last[:, :, 0, :, None]) * state + jnp.einsum('bhck,bhcv->bhkv', ki * jnp.exp(b_last - bi), vi)
        return new_state, o_inter + o_intra

    state0 = jnp.zeros((Bsz, H, DK, DV), jnp.float32)
    _, o = lax.scan(step, state0, (qc, kc, vc, bc))
    return o.transpose(1, 0, 3, 2, 4).reshape(Bsz, S, H, DV)


def token_mixers(h, w_in, gla_gate_w, gla_gate_b, gla_norm_g, gla_out_w, conv_w, conv_b, conv_out_w, mix_out_w):
    Bsz, S, _ = h.shape
    split_points = np.cumsum(IN_SPLIT_SIZES)[:-1].tolist()
    proj = h @ w_in
    q, k, v, r, a_low, cb, cc, ch, ga, gb = jnp.split(proj, split_points, axis=-1)

    log_a = jax.nn.log_sigmoid((a_low @ gla_gate_w + gla_gate_b).astype(jnp.float32)) / GLA_GATE_TEMP
    o = gla_chunked(q.reshape(Bsz, S, GLA_HEADS, GLA_DK) * (GLA_DK ** -0.5),
                    k.reshape(Bsz, S, GLA_HEADS, GLA_DK),
                    v.reshape(Bsz, S, GLA_HEADS, GLA_DV),
                    log_a.reshape(Bsz, S, GLA_HEADS, GLA_DK))
    o = rms_norm(o, gla_norm_g).reshape(Bsz, S, GLA_V_DIM).astype(h.dtype)
    y_a = (o * jax.nn.silu(r)) @ gla_out_w

    u = cc * ch
    uc = lax.conv_general_dilated(u, conv_w.astype(u.dtype)[:, None, :], window_strides=(1,),
                                  padding=[(CONV_WIDTH - 1, 0)], dimension_numbers=('NWC', 'WIO', 'NWC'),
                                  feature_group_count=CONV_DIM) + conv_b
    y_b = (cb * uc) @ conv_out_w

    merged = jax.nn.sigmoid(ga) * y_a + jax.nn.sigmoid(gb) * y_b
    return merged @ mix_out_w


def memory_cross_attention(h, mem_n, q_w, k_w, v_w, o_w):
    Bsz, S, _ = h.shape
    M = mem_n.shape[1]
    q = (h @ q_w).reshape(Bsz, S, XATTN_HEADS, XATTN_HEAD_DIM)
    k = (mem_n @ k_w).reshape(Bsz, M, XATTN_HEADS, XATTN_HEAD_DIM)
    v = (mem_n @ v_w).reshape(Bsz, M, XATTN_HEADS, XATTN_HEAD_DIM)
    s = jnp.einsum('bshd,bmhd->bhsm', q, k).astype(jnp.float32) * (XATTN_HEAD_DIM ** -0.5)
    p = jax.nn.softmax(s, axis=-1).astype(v.dtype)
    o = jnp.einsum('bhsm,bmhd->bshd', p, v).reshape(Bsz, S, XATTN_DIM)
    return o @ o_w


def moe_ffn(h, router_w, router_b, gu_w, gu_b, down_w, down_b):
    Bsz, S, D = h.shape
    T = Bsz * S
    xt = h.reshape(T, D)
    logits = (xt @ router_w + router_b).astype(jnp.float32)
    top_val, top_idx = lax.top_k(logits, TOP_K)
    gates = jax.nn.softmax(top_val, axis=-1)
    A = T * TOP_K
    flat_e = top_idx.reshape(A)
    flat_tok = jnp.repeat(jnp.arange(T, dtype=jnp.int32), TOP_K)
    flat_w = gates.reshape(A)
    order = jnp.argsort(flat_e)
    sorted_e = flat_e[order]
    counts = jnp.bincount(flat_e, length=N_EXPERTS)
    group_start = jnp.cumsum(counts) - counts
    padded = (counts + MOE_BLOCK - 1) // MOE_BLOCK * MOE_BLOCK
    padded_end = jnp.cumsum(padded)
    padded_start = padded_end - padded
    dest = padded_start[sorted_e] + jnp.arange(A) - group_start[sorted_e]
    n_blocks = -(-A // MOE_BLOCK) + N_EXPERTS
    P = n_blocks * MOE_BLOCK
    row_tok = jnp.full((P,), T, jnp.int32).at[dest].set(flat_tok[order])
    row_w = jnp.zeros((P,), jnp.float32).at[dest].set(flat_w[order])
    block_e = jnp.minimum(jnp.searchsorted(padded_end, jnp.arange(n_blocks) * MOE_BLOCK, side='right'), N_EXPERTS - 1)
    x_pad = jnp.concatenate([xt, jnp.zeros((1, D), xt.dtype)], axis=0)

    def expert_block(args):
        tok, e = args
        xb = x_pad[tok]
        gu = xb @ gu_w[e] + gu_b[e]
        gate = jnp.minimum(gu[:, :D_FF], SWIGLU_LIMIT)
        up = jnp.clip(gu[:, D_FF:], -SWIGLU_LIMIT, SWIGLU_LIMIT)
        act = (up + 1.0) * gate * jax.nn.sigmoid(SWIGLU_ALPHA * gate)
        return act @ down_w[e] + down_b[e]

    ys = lax.map(expert_block, (row_tok.reshape(n_blocks, MOE_BLOCK), block_e))
    out = jax.ops.segment_sum(ys.reshape(P, D) * row_w[:, None], row_tok, num_segments=T + 1)[:T]
    return out.reshape(Bsz, S, D).astype(h.dtype)


def setup_inputs(seed: int = 0) -> dict:
    key = jax.random.key(seed)
    ks = jax.random.split(key, 32)
    n = lambda i, shape, scale: jax.random.normal(ks[i], shape, jnp.float32) * scale
    gain = lambda i, shape: 1.0 + 0.02 * jax.random.normal(ks[i], shape, jnp.float32)
    L, D = DEPTH, D_MODEL
    return {
        'x': n(0, (BATCH, SEQ, D), 1.0),
        'mem': n(1, (BATCH, MEM_LEN, D), 1.0),
        'norm_mix_g': gain(2, (L, D)),
        'w_in': n(3, (L, D, IN_PROJ_DIM), D ** -0.5),
        'gla_gate_w': n(4, (L, GLA_GATE_RANK, GLA_QK_DIM), GLA_GATE_RANK ** -0.5),
        'gla_gate_b': n(5, (L, GLA_QK_DIM), 0.1),
        'gla_norm_g': gain(6, (L, GLA_DV)),
        'gla_out_w': n(7, (L, GLA_V_DIM, D), GLA_V_DIM ** -0.5),
        'conv_w': n(8, (L, CONV_WIDTH, CONV_DIM), CONV_WIDTH ** -0.5),
        'conv_b': n(9, (L, CONV_DIM), 0.02),
        'conv_out_w': n(10, (L, CONV_DIM, D), CONV_DIM ** -0.5),
        'mix_out_w': n(11, (L, D, D), D ** -0.5),
        'norm_xattn_g': gain(12, (L, D)),
        'norm_mem_g': gain(13, (L, D)),
        'xq_w': n(14, (L, D, XATTN_DIM), D ** -0.5),
        'xk_w': n(15, (L, D, XATTN_DIM), D ** -0.5),
        'xv_w': n(16, (L, D, XATTN_DIM), D ** -0.5),
        'xo_w': n(17, (L, XATTN_DIM, D), XATTN_DIM ** -0.5),
        'norm_moe_g': gain(18, (L, D)),
        'router_w': n(19, (L, D, N_EXPERTS), D ** -0.5),
        'router_b': n(20, (L, N_EXPERTS), 0.01),
        'expert_gu_w': n(21, (L, N_EXPERTS, D, 2 * D_FF), D ** -0.5),
        'expert_gu_b': n(22, (L, N_EXPERTS, 2 * D_FF), 0.01),
        'expert_down_w': n(23, (L, N_EXPERTS, D_FF, D), D_FF ** -0.5),
        'expert_down_b': n(24, (L, N_EXPERTS, D), 0.01),
        'norm_final_g': gain(25, (D,)),
    }


def reference(x, mem, norm_mix_g, w_in, gla_gate_w, gla_gate_b, gla_norm_g, gla_out_w, conv_w, conv_b,
              conv_out_w, mix_out_w, norm_xattn_g, norm_mem_g, xq_w, xk_w, xv_w, xo_w, norm_moe_g,
              router_w, router_b, expert_gu_w, expert_gu_b, expert_down_w, expert_down_b, norm_final_g):
    for l in range(DEPTH):
        x = x + token_mixers(rms_norm(x, norm_mix_g[l]), w_in[l], gla_gate_w[l], gla_gate_b[l], gla_norm_g[l],
                             gla_out_w[l], conv_w[l], conv_b[l], conv_out_w[l], mix_out_w[l])
        mem_n = rms_norm(mem, norm_mem_g[l])
        x = x + memory_cross_attention(rms_norm(x, norm_xattn_g[l]), mem_n, xq_w[l], xk_w[l], xv_w[l], xo_w[l])
        x = x + moe_ffn(rms_norm(x, norm_moe_g[l]), router_w[l], router_b[l], expert_gu_w[l], expert_gu_b[l],
                        expert_down_w[l], expert_down_b[l])
    return rms_norm(x, norm_final_g)
```

```python
import functools

import jax
import jax.numpy as jnp
from jax import lax
from jax.experimental import pallas as pl
from jax.experimental.pallas import tpu as pltpu

F32 = jnp.float32
BF16 = jnp.bfloat16

D_MODEL = 2048
MEM_LEN = 256
GLA_HEADS = 4
GLA_DK = 256
GLA_DV = 512
GLA_GATE_RANK = 16
GLA_GATE_TEMP = 16.0
CONV_WIDTH = 3
XATTN_HEADS = 4
XATTN_HEAD_DIM = 128
XATTN_DIM = XATTN_HEADS * XATTN_HEAD_DIM
N_EXPERTS = 32
TOP_K = 4
D_FF = D_MODEL
SWIGLU_LIMIT = 7.0
SWIGLU_ALPHA = 1.702
NORM_EPS = 1e-5

LANES = 128
P_COLS = 8 * D_MODEL
VMEM_LIMIT = 56 * 1024 * 1024

IN_TM, IN_TN = 1024, 1024
GLA_CH = 128
GLA_RB = 512
MG_TM, MG_TN = 512, 1024
MX_TM, MX_TN = 1024, 1024
XA_TM = 512
RT_TM = 512
MOE_TM = 256
MOE_TF = 1024
CB_TM = 256


def _params(sem):
    return pltpu.CompilerParams(dimension_semantics=sem, vmem_limit_bytes=VMEM_LIMIT)


def _rms(x, g):
    return x * lax.rsqrt(jnp.mean(x * x, axis=-1, keepdims=True) + NORM_EPS) * g


def _dot(a, b):
    return jnp.dot(a, b, preferred_element_type=F32)


def _dot_nt(a, b):
    return lax.dot_general(a, b, (((1,), (1,)), ((), ())), preferred_element_type=F32)


def _dot_tn(a, b):
    return lax.dot_general(a, b, (((0,), (0,)), ((), ())), preferred_element_type=F32)


def _split2(x):
    hi = x.astype(BF16)
    lo = (x - hi.astype(F32)).astype(BF16)
    return hi, lo


def _split3(x):
    hi = x.astype(BF16)
    r = x - hi.astype(F32)
    mid = r.astype(BF16)
    lo = (r - mid.astype(F32)).astype(BF16)
    return hi, mid, lo


def _sigmoid(x):
    return 1.0 / (1.0 + jnp.exp(-x))


def _inproj_kernel(x_ref, g_ref, w_ref, wa_ref, p_ref, a_ref, h_scr):
    @pl.when(pl.program_id(1) == 0)
    def _():
        hb = _rms(x_ref[...], g_ref[...]).astype(BF16)
        h_scr[...] = hb
        a_ref[...] = _dot(hb, wa_ref[...])

    p_ref[...] = _dot(h_scr[...], w_ref[...]).astype(BF16)


def _in_proj(x, g, w_main, w_alow):
    T = x.shape[0]
    return pl.pallas_call(
        _inproj_kernel,
        grid=(T // IN_TM, P_COLS // IN_TN),
        in_specs=[
            pl.BlockSpec((IN_TM, D_MODEL), lambda i, j: (i, 0)),
            pl.BlockSpec((1, D_MODEL), lambda i, j: (0, 0)),
            pl.BlockSpec((D_MODEL, IN_TN), lambda i, j: (0, j)),
            pl.BlockSpec((D_MODEL, LANES), lambda i, j: (0, 0)),
        ],
        out_specs=[
            pl.BlockSpec((IN_TM, IN_TN), lambda i, j: (i, j)),
            pl.BlockSpec((IN_TM, LANES), lambda i, j: (i, 0)),
        ],
        out_shape=[
            jax.ShapeDtypeStruct((T, P_COLS), BF16),
            jax.ShapeDtypeStruct((T, LANES), F32),
        ],
        scratch_shapes=[pltpu.VMEM((IN_TM, D_MODEL), BF16)],
        compiler_params=_params(("arbitrary", "arbitrary")),
        name="in_proj",
    )(x, g, w_main, w_alow)


def _gla_level_weights(b, la, row):
    ch, dk = b.shape
    out = []
    s = ch // 2
    while s >= 4:
        bb = b.reshape(ch // (2 * s), 2 * s, dk)
        d = (bb - bb[:, s - 1:s, :]).reshape(ch, dk)
        out.append((s, jnp.exp(-jnp.abs(d))))
        s //= 2
    la_prev = pltpu.roll(la, 1, 0)
    la_next = pltpu.roll(la, ch - 1, 0)
    r4 = row & 3
    d2 = jnp.where(r4 == 0, la_next, jnp.where(r4 == 1, 0.0, jnp.where(r4 == 2, la, la + la_prev)))
    out.append((2, jnp.exp(-jnp.abs(d2))))
    d1 = jnp.where((row & 1) == 1, la, 0.0)
    out.append((1, jnp.exp(-jnp.abs(d1))))
    return out


def _gla_kernel(q_ref, k_ref, v_ref, r_ref, a_ref, gw_ref, gb_ref, ng_ref, o_ref, st_scr):
    ch = GLA_CH

    @pl.when(pl.program_id(1) == 0)
    def _():
        st_scr[...] = jnp.zeros_like(st_scr)

    row = lax.broadcasted_iota(jnp.int32, (ch, 1), 0)
    col = lax.broadcasted_iota(jnp.int32, (1, ch), 1)
    tril = (row >= col).astype(BF16)
    gw_hi, gw_lo = _split2(gw_ref[...])
    gate_b = gb_ref[...]
    norm_g = ng_ref[...]

    def chunk(c, carry):
        sl = pl.ds(pl.multiple_of(c * ch, ch), ch)
        q = q_ref[sl, :].astype(F32) * (GLA_DK ** -0.5)
        k = k_ref[sl, :].astype(F32)
        v = v_ref[sl, :]
        a_hi, a_lo = _split2(a_ref[sl, :])
        z = _dot(a_hi, gw_hi) + _dot(a_lo, gw_hi) + _dot(a_hi, gw_lo) + gate_b
        la = (jnp.minimum(z, 0.0) - jnp.log(1.0 + jnp.exp(-jnp.abs(z)))) * (1.0 / GLA_GATE_TEMP)
        l_hi, l_mid, l_lo = _split3(la)
        b = _dot(tril, l_hi) + _dot(tril, l_mid) + _dot(tril, l_lo)

        scores = jnp.where(row == col, _dot_nt(q.astype(BF16), k.astype(BF16)), 0.0)
        for s, w in _gla_level_weights(b, la, row):
            in_right = (row & (2 * s - 1)) >= s
            ql = jnp.where(in_right, q * w, 0.0).astype(BF16)
            kl = jnp.where(in_right, 0.0, k * w).astype(BF16)
            part = _dot_nt(ql, kl)
            if 2 * s < ch:
                part = jnp.where((row ^ col) < 2 * s, part, 0.0)
            scores = scores + part
        o = _dot(scores.astype(BF16), v)

        st = st_scr[...]
        o = o + _dot_nt((q * jnp.exp(b)).astype(BF16), st.astype(BF16))
        b_last = b[ch - 1:ch, :]
        kd = (k * jnp.exp(b_last - b)).astype(BF16)
        st_scr[...] = st * jnp.exp(b_last) + _dot_tn(v, kd)

        o = _rms(o, norm_g)
        r = r_ref[sl, :].astype(F32)
        o_ref[sl, :] = (o * (r * _sigmoid(r))).astype(BF16)
        return carry

    lax.fori_loop(0, GLA_RB // ch, chunk, 0)


def _gla(p, a_low, gate_w_pad, gate_b, norm_g):
    T = p.shape[0]
    nq = GLA_DK // LANES
    del nq
    return pl.pallas_call(
        _gla_kernel,
        grid=(GLA_HEADS, T // GLA_RB),
        in_specs=[
            pl.BlockSpec((GLA_RB, GLA_DK), lambda h, i: (i, h)),
            pl.BlockSpec((GLA_RB, GLA_DK), lambda h, i: (i, GLA_HEADS + h)),
            pl.BlockSpec((GLA_RB, GLA_DV), lambda h, i: (i, GLA_HEADS + h)),
            pl.BlockSpec((GLA_RB, GLA_DV), lambda h, i: (i, 2 * GLA_HEADS + h)),
            pl.BlockSpec((GLA_RB, LANES), lambda h, i: (i, 0)),
            pl.BlockSpec((LANES, GLA_DK), lambda h, i: (0, h)),
            pl.BlockSpec((1, GLA_DK), lambda h, i: (0, h)),
            pl.BlockSpec((1, GLA_DV), lambda h, i: (0, 0)),
        ],
        out_specs=pl.BlockSpec((GLA_RB, GLA_DV), lambda h, i: (i, h)),
        out_shape=jax.ShapeDtypeStruct((T, GLA_HEADS * GLA_DV), BF16),
        scratch_shapes=[pltpu.VMEM((GLA_DV, GLA_DK), F32)],
        compiler_params=_params(("arbitrary", "arbitrary")),
        name="gla",
    )(p, p, p, p, a_low, gate_w_pad, gate_b, norm_g)


def _merge_kernel(og_ref, cb_ref, cc_ref, ch_ref, ccp_ref, chp_ref, cw_ref, cbias_ref,
                  ga_ref, gb_ref, wa_ref, wb_ref, o_ref, zb_scr):
    @pl.when(pl.program_id(1) == 0)
    def _():
        tm = cc_ref.shape[0]
        u = cc_ref[...].astype(F32) * ch_ref[...].astype(F32)
        up = ccp_ref[...].astype(F32) * chp_ref[...].astype(F32)
        up = jnp.where(pl.program_id(0) == 0, 0.0, up)
        hp = up.shape[0]
        row = lax.broadcasted_iota(jnp.int32, (tm, 1), 0)
        u1 = jnp.where(row == 0, up[hp - 1:hp, :], pltpu.roll(u, 1, 0))
        u2 = jnp.where(row == 0, up[hp - 2:hp - 1, :],
                       jnp.where(row == 1, up[hp - 1:hp, :], pltpu.roll(u, 2, 0)))
        cw = cw_ref[...]
        uc = cw[0:1, :] * u2 + cw[1:2, :] * u1 + cw[2:3, :] * u + cbias_ref[...]
        zb_scr[...] = (cb_ref[...].astype(F32) * uc).astype(BF16)

    ya = _dot(og_ref[...], wa_ref[...])
    yb = _dot(zb_scr[...], wb_ref[...])
    o_ref[...] = (_sigmoid(ga_ref[...].astype(F32)) * ya
                  + _sigmoid(gb_ref[...].astype(F32)) * yb).astype(BF16)


def _merge(og, p, conv_w8, conv_b, gla_out_w, conv_out_w):
    T = og.shape[0]
    nj = D_MODEL // MG_TN
    halo = 16
    prev = lambda blk: (lambda i, j: (jnp.maximum(i * (MG_TM // halo) - 1, 0), blk))
    return pl.pallas_call(
        _merge_kernel,
        grid=(T // MG_TM, nj),
        in_specs=[
            pl.BlockSpec((MG_TM, D_MODEL), lambda i, j: (i, 0)),
            pl.BlockSpec((MG_TM, D_MODEL), lambda i, j: (i, 3)),
            pl.BlockSpec((MG_TM, D_MODEL), lambda i, j: (i, 4)),
            pl.BlockSpec((MG_TM, D_MODEL), lambda i, j: (i, 5)),
            pl.BlockSpec((halo, D_MODEL), prev(4)),
            pl.BlockSpec((halo, D_MODEL), prev(5)),
            pl.BlockSpec((8, D_MODEL), lambda i, j: (0, 0)),
            pl.BlockSpec((1, D_MODEL), lambda i, j: (0, 0)),
            pl.BlockSpec((MG_TM, MG_TN), lambda i, j: (i, 6 * nj + j)),
            pl.BlockSpec((MG_TM, MG_TN), lambda i, j: (i, 7 * nj + j)),
            pl.BlockSpec((D_MODEL, MG_TN), lambda i, j: (0, j)),
            pl.BlockSpec((D_MODEL, MG_TN), lambda i, j: (0, j)),
        ],
        out_specs=pl.BlockSpec((MG_TM, MG_TN), lambda i, j: (i, j)),
        out_shape=jax.ShapeDtypeStruct((T, D_MODEL), BF16),
        scratch_shapes=[pltpu.VMEM((MG_TM, D_MODEL), BF16)],
        compiler_params=_params(("arbitrary", "arbitrary")),
        name="merge",
    )(og, p, p, p, p, p, conv_w8, conv_b, p, p, gla_out_w, conv_out_w)


def _mixout_kernel(a_ref, w_ref, x_ref, o_ref):
    o_ref[...] = x_ref[...] + _dot(a_ref[...], w_ref[...])


def _mix_out(merged, w, x):
    T = x.shape[0]
    return pl.pallas_call(
        _mixout_kernel,
        grid=(T // MX_TM, D_MODEL // MX_TN),
        in_specs=[
            pl.BlockSpec((MX_TM, D_MODEL), lambda i, j: (i, 0)),
            pl.BlockSpec((D_MODEL, MX_TN), lambda i, j: (0, j)),
            pl.BlockSpec((MX_TM, MX_TN), lambda i, j: (i, j)),
        ],
        out_specs=pl.BlockSpec((MX_TM, MX_TN), lambda i, j: (i, j)),
        out_shape=jax.ShapeDtypeStruct((T, D_MODEL), F32),
        compiler_params=_params(("arbitrary", "arbitrary")),
        name="mix_out",
    )(merged, w, x)


def _memkv_kernel(m_ref, g_ref, wk_ref, wv_ref, k_ref, v_ref):
    mn = _rms(m_ref[...], g_ref[...]).astype(BF16)
    k_ref[...] = _dot(mn, wk_ref[...]).astype(BF16)
    v_ref[...] = _dot(mn, wv_ref[...]).astype(BF16)


def _mem_kv(mem, g, wk, wv):
    return pl.pallas_call(
        _memkv_kernel,
        out_shape=[jax.ShapeDtypeStruct((MEM_LEN, XATTN_DIM), BF16)] * 2,
        compiler_params=pltpu.CompilerParams(vmem_limit_bytes=VMEM_LIMIT),
        name="mem_kv",
    )(mem, g, wk, wv)


def _xattn_kernel(x_ref, g_ref, wq_ref, k_ref, v_ref, wo_ref, o_ref):
    x = x_ref[...]
    hq = _rms(x, g_ref[...]).astype(BF16)
    q = (_dot(hq, wq_ref[...]) * (XATTN_HEAD_DIM ** -0.5)).astype(BF16)
    outs = []
    for h in range(XATTN_HEADS):
        cs = slice(h * XATTN_HEAD_DIM, (h + 1) * XATTN_HEAD_DIM)
        s = _dot_nt(q[:, cs], k_ref[:, cs])
        s = s - jnp.max(s, axis=-1, keepdims=True)
        e = jnp.exp(s)
        p = e / jnp.sum(e, axis=-1, keepdims=True)
        outs.append(_dot(p.astype(BF16), v_ref[:, cs]))
    o = jnp.concatenate(outs, axis=-1).astype(BF16)
    o_ref[...] = x + _dot(o, wo_ref[...])


def _xattn(x1, g, wq, kmem, vmem, wo):
    T = x1.shape[0]
    full = lambda shape: pl.BlockSpec(shape, lambda i: (0, 0))
    return pl.pallas_call(
        _xattn_kernel,
        grid=(T // XA_TM,),
        in_specs=[
            pl.BlockSpec((XA_TM, D_MODEL), lambda i: (i, 0)),
            full((1, D_MODEL)),
            full((D_MODEL, XATTN_DIM)),
            full((MEM_LEN, XATTN_DIM)),
            full((MEM_LEN, XATTN_DIM)),
            full((XATTN_DIM, D_MODEL)),
        ],
        out_specs=pl.BlockSpec((XA_TM, D_MODEL), lambda i: (i, 0)),
        out_shape=jax.ShapeDtypeStruct((T, D_MODEL), F32),
        compiler_params=_params(("arbitrary",)),
        name="xattn",
    )(x1, g, wq, kmem, vmem, wo)


def _router_kernel(x_ref, g_ref, w_ref, b_ref, hm_ref, idx_ref, gate_ref, rank_ref, cnt_ref, carry_scr):
    tm = x_ref.shape[0]

    @pl.when(pl.program_id(0) == 0)
    def _():
        carry_scr[...] = jnp.zeros_like(carry_scr)

    hb = _rms(x_ref[...], g_ref[...]).astype(BF16)
    hm_ref[...] = hb
    lane = lax.broadcasted_iota(jnp.int32, (tm, LANES), 1)
    logits = jnp.where(lane < N_EXPERTS, _dot(hb, w_ref[...]) + b_ref[...], -jnp.inf)

    vals, idxs = [], []
    work = logits
    for _ in range(TOP_K):
        m = jnp.max(work, axis=-1, keepdims=True)
        am = jnp.min(jnp.where(work == m, lane, LANES), axis=-1, keepdims=True)
        vals.append(m)
        idxs.append(am)
        work = jnp.where(lane == am, -jnp.inf, work)
    es = [jnp.exp(v - vals[0]) for v in vals]
    inv = 1.0 / (es[0] + es[1] + es[2] + es[3])

    onehot = (work != logits).astype(BF16)
    r_ = lax.broadcasted_iota(jnp.int32, (tm, tm), 0)
    c_ = lax.broadcasted_iota(jnp.int32, (tm, tm), 1)
    before = _dot((r_ > c_).astype(BF16), onehot) + carry_scr[...]
    carry_scr[...] = carry_scr[...] + jnp.sum(onehot.astype(F32), axis=0, keepdims=True)
    cnt_ref[...] = carry_scr[...]

    idx_out = jnp.zeros((tm, LANES), jnp.int32)
    gate_out = jnp.zeros((tm, LANES), F32)
    rank_out = jnp.zeros((tm, LANES), jnp.int32)
    for kk in range(TOP_K):
        rk = jnp.sum(jnp.where(lane == idxs[kk], before, 0.0), axis=-1, keepdims=True)
        idx_out = jnp.where(lane == kk, idxs[kk], idx_out)
        gate_out = jnp.where(lane == kk, es[kk] * inv, gate_out)
        rank_out = jnp.where(lane == kk, rk.astype(jnp.int32), rank_out)
    idx_ref[...] = idx_out
    gate_ref[...] = gate_out
    rank_ref[...] = rank_out


def _router(x2, g, w_pad, b_pad):
    T = x2.shape[0]
    row_blk = pl.BlockSpec((RT_TM, LANES), lambda i: (i, 0))
    return pl.pallas_call(
        _router_kernel,
        grid=(T // RT_TM,),
        in_specs=[
            pl.BlockSpec((RT_TM, D_MODEL), lambda i: (i, 0)),
            pl.BlockSpec((1, D_MODEL), lambda i: (0, 0)),
            pl.BlockSpec((D_MODEL, LANES), lambda i: (0, 0)),
            pl.BlockSpec((1, LANES), lambda i: (0, 0)),
        ],
        out_specs=[
            pl.BlockSpec((RT_TM, D_MODEL), lambda i: (i, 0)),
            row_blk, row_blk, row_blk,
            pl.BlockSpec((1, LANES), lambda i: (0, 0)),
        ],
        out_shape=[
            jax.ShapeDtypeStruct((T, D_MODEL), BF16),
            jax.ShapeDtypeStruct((T, LANES), jnp.int32),
            jax.ShapeDtypeStruct((T, LANES), F32),
            jax.ShapeDtypeStruct((T, LANES), jnp.int32),
            jax.ShapeDtypeStruct((1, LANES), F32),
        ],
        scratch_shapes=[pltpu.VMEM((1, LANES), F32)],
        compiler_params=_params(("arbitrary",)),
        name="router",
    )(x2, g, w_pad, b_pad)


def _expert_kernel(be_ref, nb_ref, x_ref, wg_ref, wu_ref, bg_ref, bu_ref, wd_ref, bd_ref, y_ref):
    c = pl.program_id(0)
    i = pl.program_id(1)

    @pl.when(i < nb_ref[0])
    def _():
        x = x_ref[...]
        gate = jnp.minimum(_dot(x, wg_ref[...]) + bg_ref[...], SWIGLU_LIMIT)
        up = jnp.clip(_dot(x, wu_ref[...]) + bu_ref[...], -SWIGLU_LIMIT, SWIGLU_LIMIT)
        act = (up + 1.0) * gate * _sigmoid(SWIGLU_ALPHA * gate)
        y = _dot(act.astype(BF16), wd_ref[...])
        y_ref[...] = y + jnp.where(c == 0, bd_ref[...], 0.0)


def _experts(block_e, nb_used, x_sorted, gu_w, gu_b, down_w, down_b):
    P = x_sorted.shape[0]
    nb = P // MOE_TM
    nc = D_FF // MOE_TF
    rowblk = lambda c, i, be, nbu: (jnp.minimum(i, nbu[0] - 1), 0)
    grid_spec = pltpu.PrefetchScalarGridSpec(
        num_scalar_prefetch=2,
        grid=(nc, nb),
        in_specs=[
            pl.BlockSpec((MOE_TM, D_MODEL), rowblk),
            pl.BlockSpec((None, D_MODEL, MOE_TF), lambda c, i, be, nbu: (be[i], 0, c)),
            pl.BlockSpec((None, D_MODEL, MOE_TF), lambda c, i, be, nbu: (be[i], 0, nc + c)),
            pl.BlockSpec((None, 1, MOE_TF), lambda c, i, be, nbu: (be[i], 0, c)),
            pl.BlockSpec((None, 1, MOE_TF), lambda c, i, be, nbu: (be[i], 0, nc + c)),
            pl.BlockSpec((None, MOE_TF, D_MODEL), lambda c, i, be, nbu: (be[i], c, 0)),
            pl.BlockSpec((None, 1, D_MODEL), lambda c, i, be, nbu: (be[i], 0, 0)),
        ],
        out_specs=pl.BlockSpec((None, MOE_TM, D_MODEL),
                               lambda c, i, be, nbu: (c, jnp.minimum(i, nbu[0] - 1), 0)),
    )
    return pl.pallas_call(
        _expert_kernel,
        grid_spec=grid_spec,
        out_shape=jax.ShapeDtypeStruct((nc, P, D_MODEL), F32),
        compiler_params=_params(("arbitrary", "arbitrary")),
        name="experts",
    )(block_e, nb_used, x_sorted, gu_w, gu_w, gu_b, gu_b, down_w, down_b)


def _combine_kernel(x_ref, y_ref, gate_ref, g_ref, o_ref):
    acc = x_ref[...]
    gates = gate_ref[...]
    for kk in range(TOP_K):
        yk = y_ref[0, kk]
        for c in range(1, y_ref.shape[0]):
            yk = yk + y_ref[c, kk]
        acc = acc + gates[:, kk:kk + 1] * yk
    o_ref[...] = _rms(acc, g_ref[...])


def _combine(x2, y_g, gates, g):
    T = x2.shape[0]
    nc = y_g.shape[0]
    return pl.pallas_call(
        _combine_kernel,
        grid=(T // CB_TM,),
        in_specs=[
            pl.BlockSpec((CB_TM, D_MODEL), lambda i: (i, 0)),
            pl.BlockSpec((nc, TOP_K, CB_TM, D_MODEL), lambda i: (0, 0, i, 0)),
            pl.BlockSpec((CB_TM, LANES), lambda i: (i, 0)),
            pl.BlockSpec((1, D_MODEL), lambda i: (0, 0)),
        ],
        out_specs=pl.BlockSpec((CB_TM, D_MODEL), lambda i: (i, 0)),
        out_shape=jax.ShapeDtypeStruct((T, D_MODEL), F32),
        compiler_params=_params(("arbitrary",)),
        name="combine",
    )(x2, y_g, gates, g)


def _row(v):
    return v.reshape(1, -1).astype(F32)


def _layer(x, mem, norm_mix_g, w_in, gla_gate_w, gla_gate_b, gla_norm_g, gla_out_w, conv_w, conv_b,
           conv_out_w, mix_out_w, norm_xattn_g, norm_mem_g, xq_w, xk_w, xv_w, xo_w, norm_moe_g,
           router_w, router_b, gu_w, gu_b, down_w, down_b):
    T = x.shape[0]
    a0 = 3 * D_MODEL
    w_main = jnp.concatenate([w_in[:, :a0], w_in[:, a0 + GLA_GATE_RANK:]], axis=1).astype(BF16)
    w_alow = jnp.pad(w_in[:, a0:a0 + GLA_GATE_RANK], ((0, 0), (0, LANES - GLA_GATE_RANK))).astype(BF16)
    gate_w_pad = jnp.pad(gla_gate_w, ((0, LANES - GLA_GATE_RANK), (0, 0)))
    conv_w8 = jnp.pad(conv_w, ((0, 8 - CONV_WIDTH), (0, 0)))

    p, a_low = _in_proj(x, _row(norm_mix_g), w_main, w_alow)
    og = _gla(p, a_low, gate_w_pad, _row(gla_gate_b), _row(gla_norm_g))
    merged = _merge(og, p, conv_w8, _row(conv_b), gla_out_w.astype(BF16), conv_out_w.astype(BF16))
    x1 = _mix_out(merged, mix_out_w.astype(BF16), x)

    kmem, vmem = _mem_kv(mem, _row(norm_mem_g), xk_w.astype(BF16), xv_w.astype(BF16))
    x2 = _xattn(x1, _row(norm_xattn_g), xq_w.astype(BF16), kmem, vmem, xo_w.astype(BF16))

    rw = jnp.pad(router_w, ((0, 0), (0, LANES - N_EXPERTS))).astype(BF16)
    rb = jnp.pad(router_b, (0, LANES - N_EXPERTS)).reshape(1, LANES)
    hm, idx, gates, rank, counts = _router(x2, _row(norm_moe_g), rw, rb)

    idx4 = idx[:, :TOP_K]
    cnt = counts[0, :N_EXPERTS].astype(jnp.int32)
    padded = (cnt + MOE_TM - 1) // MOE_TM * MOE_TM
    pend = jnp.cumsum(padded)
    pstart = pend - padded
    dest = pstart[idx4] + rank[:, :TOP_K]
    nb = (T * TOP_K) // MOE_TM + N_EXPERTS
    P = nb * MOE_TM
    block_e = jnp.minimum(jnp.searchsorted(pend, jnp.arange(nb, dtype=jnp.int32) * MOE_TM, side='right'),
                          N_EXPERTS - 1).astype(jnp.int32)
    nb_used = (pend[-1] // MOE_TM).astype(jnp.int32).reshape(1)
    tok = jnp.broadcast_to(jnp.arange(T, dtype=jnp.int32)[:, None], (T, TOP_K))
    row_tok = jnp.zeros((P,), jnp.int32).at[dest.reshape(-1)].set(tok.reshape(-1))
    x_sorted = hm[row_tok]

    y = _experts(block_e, nb_used, x_sorted, gu_w.astype(BF16), gu_b.reshape(N_EXPERTS, 1, 2 * D_FF),
                 down_w.astype(BF16), down_b.reshape(N_EXPERTS, 1, D_MODEL))
    y_g = jnp.take(y, dest.T.reshape(-1), axis=1).reshape(y.shape[0], TOP_K, T, D_MODEL)
    return x2, y_g, gates


def kernel(x, mem, norm_mix_g, w_in, gla_gate_w, gla_gate_b, gla_norm_g, gla_out_w, conv_w, conv_b, conv_out_w, mix_out_w, norm_xattn_g, norm_mem_g, xq_w, xk_w, xv_w, xo_w, norm_moe_g, router_w, router_b, expert_gu_w, expert_gu_b, expert_down_w, expert_down_b, norm_final_g):
    assert x.shape[0] == 1 and mem.shape[0] == 1 and w_in.shape[0] == 1
    x2, y_g, gates = _layer(
        x[0], mem[0], norm_mix_g[0], w_in[0], gla_gate_w[0], gla_gate_b[0], gla_norm_g[0], gla_out_w[0],
        conv_w[0], conv_b[0], conv_out_w[0], mix_out_w[0], norm_xattn_g[0], norm_mem_g[0], xq_w[0], xk_w[0],
        xv_w[0], xo_w[0], norm_moe_g[0], router_w[0], router_b[0], expert_gu_w[0], expert_gu_b[0],
        expert_down_w[0], expert_down_b[0])
    out = _combine(x2, y_g, gates, _row(norm_final_g))
    return out[None]
```

```python
import functools

import jax
import jax.numpy as jnp
from jax import lax
from jax.experimental import pallas as pl
from jax.experimental.pallas import tpu as pltpu
from jax.experimental.pallas import tpu_sc as plsc

F32 = jnp.float32
BF16 = jnp.bfloat16
I32 = jnp.int32
U32 = jnp.uint32

D_MODEL = 2048
MEM_LEN = 256
GLA_HEADS = 4
GLA_DK = 256
GLA_DV = 512
GLA_GATE_RANK = 16
GLA_GATE_TEMP = 16.0
CONV_WIDTH = 3
XATTN_HEADS = 4
XATTN_HEAD_DIM = 128
XATTN_DIM = XATTN_HEADS * XATTN_HEAD_DIM
N_EXPERTS = 32
TOP_K = 4
D_FF = D_MODEL
SWIGLU_LIMIT = 7.0
SWIGLU_ALPHA = 1.702
NORM_EPS = 1e-5

LANES = 128
SC_CORES, SC_SUBCORES, SC_LANES = 2, 16, 16
PACK_W = D_MODEL // 2
P_COLS = 8 * D_MODEL
VMEM_LIMIT = 56 * 1024 * 1024

IN_TM, IN_TN = 1024, 1024
GLA_CH = 128
GLA_RB = 512
MG_TM, MG_TN = 512, 1024
MX_TM, MX_TN = 1024, 1024
XA_TM = 512
RT_TM = 512
MOE_TM = 256
MOE_TF = 1024
CB_TM = 256
SC_G = 32
SC_CH = 4096
CAST_ROWS = 512


def _params(sem):
    return pltpu.CompilerParams(dimension_semantics=sem, vmem_limit_bytes=VMEM_LIMIT)


def _rms(x, g):
    return x * lax.rsqrt(jnp.mean(x * x, axis=-1, keepdims=True) + NORM_EPS) * g


def _dot(a, b):
    return jnp.dot(a, b, preferred_element_type=F32)


def _dot_nt(a, b):
    return lax.dot_general(a, b, (((1,), (1,)), ((), ())), preferred_element_type=F32)


def _dot_tn(a, b):
    return lax.dot_general(a, b, (((0,), (0,)), ((), ())), preferred_element_type=F32)


def _split2(x):
    hi = x.astype(BF16)
    lo = (x - hi.astype(F32)).astype(BF16)
    return hi, lo


def _split3(x):
    hi = x.astype(BF16)
    r = x - hi.astype(F32)
    mid = r.astype(BF16)
    lo = (r - mid.astype(F32)).astype(BF16)
    return hi, mid, lo


def _sigmoid(x):
    return 1.0 / (1.0 + jnp.exp(-x))


def _pack_pairs(x):
    w = x.shape[1] // 2
    u = lax.bitcast_convert_type(x.astype(BF16).astype(F32), U32)
    return (u[:, :w] >> 16) | u[:, w:]


def _unpack_pairs(u):
    lo = lax.bitcast_convert_type(u << 16, F32)
    hi = lax.bitcast_convert_type(u & jnp.uint32(0xFFFF0000), F32)
    return jnp.concatenate([lo, hi], axis=1)


def _inproj_kernel(x_ref, g_ref, w_ref, wa_ref, p_ref, a_ref, h_scr):
    @pl.when(pl.program_id(1) == 0)
    def _():
        hb = _rms(x_ref[...], g_ref[...]).astype(BF16)
        h_scr[...] = hb
        a_ref[...] = _dot(hb, wa_ref[...])

    p_ref[...] = _dot(h_scr[...], w_ref[...]).astype(BF16)


def _in_proj(x, g, w_main, w_alow):
    T = x.shape[0]
    return pl.pallas_call(
        _inproj_kernel,
        grid=(T // IN_TM, P_COLS // IN_TN),
        in_specs=[
            pl.BlockSpec((IN_TM, D_MODEL), lambda i, j: (i, 0)),
            pl.BlockSpec((1, D_MODEL), lambda i, j: (0, 0)),
            pl.BlockSpec((D_MODEL, IN_TN), lambda i, j: (0, j)),
            pl.BlockSpec((D_MODEL, LANES), lambda i, j: (0, 0)),
        ],
        out_specs=[
            pl.BlockSpec((IN_TM, IN_TN), lambda i, j: (i, j)),
            pl.BlockSpec((IN_TM, LANES), lambda i, j: (i, 0)),
        ],
        out_shape=[
            jax.ShapeDtypeStruct((T, P_COLS), BF16),
            jax.ShapeDtypeStruct((T, LANES), F32),
        ],
        scratch_shapes=[pltpu.VMEM((IN_TM, D_MODEL), BF16)],
        compiler_params=_params(("arbitrary", "arbitrary")),
        name="in_proj",
    )(x, g, w_main, w_alow)


def _gla_level_weights(b, la, row):
    ch, dk = b.shape
    out = []
    s = ch // 2
    while s >= 4:
        bb = b.reshape(ch // (2 * s), 2 * s, dk)
        d = (bb - bb[:, s - 1:s, :]).reshape(ch, dk)
        out.append((s, jnp.exp(-jnp.abs(d))))
        s //= 2
    la_prev = pltpu.roll(la, 1, 0)
    la_next = pltpu.roll(la, ch - 1, 0)
    r4 = row & 3
    d2 = jnp.where(r4 == 0, la_next, jnp.where(r4 == 1, 0.0, jnp.where(r4 == 2, la, la + la_prev)))
    out.append((2, jnp.exp(-jnp.abs(d2))))
    d1 = jnp.where((row & 1) == 1, la, 0.0)
    out.append((1, jnp.exp(-jnp.abs(d1))))
    return out


def _gla_kernel(q_ref, k_ref, v_ref, r_ref, a_ref, gw_ref, gb_ref, ng_ref, o_ref, st_scr):
    ch = GLA_CH

    @pl.when(pl.program_id(1) == 0)
    def _():
        st_scr[...] = jnp.zeros_like(st_scr)

    row = lax.broadcasted_iota(jnp.int32, (ch, 1), 0)
    col = lax.broadcasted_iota(jnp.int32, (1, ch), 1)
    tril = (row >= col).astype(BF16)
    gw_hi, gw_lo = _split2(gw_ref[...])
    gate_b = gb_ref[...]
    norm_g = ng_ref[...]

    def chunk(c, carry):
        sl = pl.ds(pl.multiple_of(c * ch, ch), ch)
        q = q_ref[sl, :].astype(F32) * (GLA_DK ** -0.5)
        k = k_ref[sl, :].astype(F32)
        v = v_ref[sl, :]
        a_hi, a_lo = _split2(a_ref[sl, :])
        z = _dot(a_hi, gw_hi) + _dot(a_lo, gw_hi) + _dot(a_hi, gw_lo) + gate_b
        la = (jnp.minimum(z, 0.0) - jnp.log(1.0 + jnp.exp(-jnp.abs(z)))) * (1.0 / GLA_GATE_TEMP)
        l_hi, l_mid, l_lo = _split3(la)
        b = _dot(tril, l_hi) + _dot(tril, l_mid) + _dot(tril, l_lo)

        scores = jnp.where(row == col, _dot_nt(q.astype(BF16), k.astype(BF16)), 0.0)
        for s, w in _gla_level_weights(b, la, row):
            in_right = (row & (2 * s - 1)) >= s
            ql = jnp.where(in_right, q * w, 0.0).astype(BF16)
            kl = jnp.where(in_right, 0.0, k * w).astype(BF16)
            part = _dot_nt(ql, kl)
            if 2 * s < ch:
                part = jnp.where((row ^ col) < 2 * s, part, 0.0)
            scores = scores + part
        o = _dot(scores.astype(BF16), v)

        st = st_scr[...]
        o = o + _dot_nt((q * jnp.exp(b)).astype(BF16), st.astype(BF16))
        b_last = b[ch - 1:ch, :]
        kd = (k * jnp.exp(b_last - b)).astype(BF16)
        st_scr[...] = st * jnp.exp(b_last) + _dot_tn(v, kd)

        o = _rms(o, norm_g)
        r = r_ref[sl, :].astype(F32)
        o_ref[sl, :] = (o * (r * _sigmoid(r))).astype(BF16)
        return carry

    lax.fori_loop(0, GLA_RB // ch, chunk, 0)


def _gla(p, a_low, gate_w_pad, gate_b, norm_g):
    T = p.shape[0]
    nq = GLA_DK // LANES
    del nq
    return pl.pallas_call(
        _gla_kernel,
        grid=(GLA_HEADS, T // GLA_RB),
        in_specs=[
            pl.BlockSpec((GLA_RB, GLA_DK), lambda h, i: (i, h)),
            pl.BlockSpec((GLA_RB, GLA_DK), lambda h, i: (i, GLA_HEADS + h)),
            pl.BlockSpec((GLA_RB, GLA_DV), lambda h, i: (i, GLA_HEADS + h)),
            pl.BlockSpec((GLA_RB, GLA_DV), lambda h, i: (i, 2 * GLA_HEADS + h)),
            pl.BlockSpec((GLA_RB, LANES), lambda h, i: (i, 0)),
            pl.BlockSpec((LANES, GLA_DK), lambda h, i: (0, h)),
            pl.BlockSpec((1, GLA_DK), lambda h, i: (0, h)),
            pl.BlockSpec((1, GLA_DV), lambda h, i: (0, 0)),
        ],
        out_specs=pl.BlockSpec((GLA_RB, GLA_DV), lambda h, i: (i, h)),
        out_shape=jax.ShapeDtypeStruct((T, GLA_HEADS * GLA_DV), BF16),
        scratch_shapes=[pltpu.VMEM((GLA_DV, GLA_DK), F32)],
        compiler_params=_params(("arbitrary", "arbitrary")),
        name="gla",
    )(p, p, p, p, a_low, gate_w_pad, gate_b, norm_g)


def _merge_kernel(og_ref, cb_ref, cc_ref, ch_ref, ccp_ref, chp_ref, cw_ref, cbias_ref,
                  ga_ref, gb_ref, wa_ref, wb_ref, o_ref, zb_scr):
    @pl.when(pl.program_id(1) == 0)
    def _():
        tm = cc_ref.shape[0]
        u = cc_ref[...].astype(F32) * ch_ref[...].astype(F32)
        up = ccp_ref[...].astype(F32) * chp_ref[...].astype(F32)
        up = jnp.where(pl.program_id(0) == 0, 0.0, up)
        hp = up.shape[0]
        row = lax.broadcasted_iota(jnp.int32, (tm, 1), 0)
        u1 = jnp.where(row == 0, up[hp - 1:hp, :], pltpu.roll(u, 1, 0))
        u2 = jnp.where(row == 0, up[hp - 2:hp - 1, :],
                       jnp.where(row == 1, up[hp - 1:hp, :], pltpu.roll(u, 2, 0)))
        cw = cw_ref[...]
        uc = cw[0:1, :] * u2 + cw[1:2, :] * u1 + cw[2:3, :] * u + cbias_ref[...]
        zb_scr[...] = (cb_ref[...].astype(F32) * uc).astype(BF16)

    ya = _dot(og_ref[...], wa_ref[...])
    yb = _dot(zb_scr[...], wb_ref[...])
    o_ref[...] = (_sigmoid(ga_ref[...].astype(F32)) * ya
                  + _sigmoid(gb_ref[...].astype(F32)) * yb).astype(BF16)


def _merge(og, p, conv_w8, conv_b, gla_out_w, conv_out_w):
    T = og.shape[0]
    nj = D_MODEL // MG_TN
    halo = 16
    prev = lambda blk: (lambda i, j: (jnp.maximum(i * (MG_TM // halo) - 1, 0), blk))
    return pl.pallas_call(
        _merge_kernel,
        grid=(T // MG_TM, nj),
        in_specs=[
            pl.BlockSpec((MG_TM, D_MODEL), lambda i, j: (i, 0)),
            pl.BlockSpec((MG_TM, D_MODEL), lambda i, j: (i, 3)),
            pl.BlockSpec((MG_TM, D_MODEL), lambda i, j: (i, 4)),
            pl.BlockSpec((MG_TM, D_MODEL), lambda i, j: (i, 5)),
            pl.BlockSpec((halo, D_MODEL), prev(4)),
            pl.BlockSpec((halo, D_MODEL), prev(5)),
            pl.BlockSpec((8, D_MODEL), lambda i, j: (0, 0)),
            pl.BlockSpec((1, D_MODEL), lambda i, j: (0, 0)),
            pl.BlockSpec((MG_TM, MG_TN), lambda i, j: (i, 6 * nj + j)),
            pl.BlockSpec((MG_TM, MG_TN), lambda i, j: (i, 7 * nj + j)),
            pl.BlockSpec((D_MODEL, MG_TN), lambda i, j: (0, j)),
            pl.BlockSpec((D_MODEL, MG_TN), lambda i, j: (0, j)),
        ],
        out_specs=pl.BlockSpec((MG_TM, MG_TN), lambda i, j: (i, j)),
        out_shape=jax.ShapeDtypeStruct((T, D_MODEL), BF16),
        scratch_shapes=[pltpu.VMEM((MG_TM, D_MODEL), BF16)],
        compiler_params=_params(("arbitrary", "arbitrary")),
        name="merge",
    )(og, p, p, p, p, p, conv_w8, conv_b, p, p, gla_out_w, conv_out_w)


def _mixout_kernel(a_ref, w_ref, x_ref, o_ref):
    o_ref[...] = x_ref[...] + _dot(a_ref[...], w_ref[...])


def _mix_out(merged, w, x):
    T = x.shape[0]
    return pl.pallas_call(
        _mixout_kernel,
        grid=(T // MX_TM, D_MODEL // MX_TN),
        in_specs=[
            pl.BlockSpec((MX_TM, D_MODEL), lambda i, j: (i, 0)),
            pl.BlockSpec((D_MODEL, MX_TN), lambda i, j: (0, j)),
            pl.BlockSpec((MX_TM, MX_TN), lambda i, j: (i, j)),
        ],
        out_specs=pl.BlockSpec((MX_TM, MX_TN), lambda i, j: (i, j)),
        out_shape=jax.ShapeDtypeStruct((T, D_MODEL), F32),
        compiler_params=_params(("arbitrary", "arbitrary")),
        name="mix_out",
    )(merged, w, x)


def _memkv_kernel(m_ref, g_ref, wk_ref, wv_ref, k_ref, v_ref):
    mn = _rms(m_ref[...], g_ref[...]).astype(BF16)
    k_ref[...] = _dot(mn, wk_ref[...]).astype(BF16)
    v_ref[...] = _dot(mn, wv_ref[...]).astype(BF16)


def _mem_kv(mem, g, wk, wv):
    return pl.pallas_call(
        _memkv_kernel,
        out_shape=[jax.ShapeDtypeStruct((MEM_LEN, XATTN_DIM), BF16)] * 2,
        compiler_params=pltpu.CompilerParams(vmem_limit_bytes=VMEM_LIMIT),
        name="mem_kv",
    )(mem, g, wk, wv)


def _xattn_kernel(x_ref, g_ref, wq_ref, k_ref, v_ref, wo_ref, o_ref):
    x = x_ref[...]
    hq = _rms(x, g_ref[...]).astype(BF16)
    q = (_dot(hq, wq_ref[...]) * (XATTN_HEAD_DIM ** -0.5)).astype(BF16)
    outs = []
    for h in range(XATTN_HEADS):
        cs = slice(h * XATTN_HEAD_DIM, (h + 1) * XATTN_HEAD_DIM)
        s = _dot_nt(q[:, cs], k_ref[:, cs])
        s = s - jnp.max(s, axis=-1, keepdims=True)
        e = jnp.exp(s)
        p = e / jnp.sum(e, axis=-1, keepdims=True)
        outs.append(_dot(p.astype(BF16), v_ref[:, cs]))
    o = jnp.concatenate(outs, axis=-1).astype(BF16)
    o_ref[...] = x + _dot(o, wo_ref[...])


def _xattn(x1, g, wq, kmem, vmem, wo):
    T = x1.shape[0]
    full = lambda shape: pl.BlockSpec(shape, lambda i: (0, 0))
    return pl.pallas_call(
        _xattn_kernel,
        grid=(T // XA_TM,),
        in_specs=[
            pl.BlockSpec((XA_TM, D_MODEL), lambda i: (i, 0)),
            full((1, D_MODEL)),
            full((D_MODEL, XATTN_DIM)),
            full((MEM_LEN, XATTN_DIM)),
            full((MEM_LEN, XATTN_DIM)),
            full((XATTN_DIM, D_MODEL)),
        ],
        out_specs=pl.BlockSpec((XA_TM, D_MODEL), lambda i: (i, 0)),
        out_shape=jax.ShapeDtypeStruct((T, D_MODEL), F32),
        compiler_params=_params(("arbitrary",)),
        name="xattn",
    )(x1, g, wq, kmem, vmem, wo)


def _router_kernel(x_ref, g_ref, w_ref, b_ref, hm_ref, idx_ref, gate_ref, rank_ref, cnt_ref, carry_scr):
    tm = x_ref.shape[0]

    @pl.when(pl.program_id(0) == 0)
    def _():
        carry_scr[...] = jnp.zeros_like(carry_scr)

    h = _rms(x_ref[...], g_ref[...])
    hb = h.astype(BF16)
    hm_ref[...] = _pack_pairs(h)
    lane = lax.broadcasted_iota(jnp.int32, (tm, LANES), 1)
    logits = jnp.where(lane < N_EXPERTS, _dot(hb, w_ref[...]) + b_ref[...], -jnp.inf)

    vals, idxs = [], []
    work = logits
    for _ in range(TOP_K):
        m = jnp.max(work, axis=-1, keepdims=True)
        am = jnp.min(jnp.where(work == m, lane, LANES), axis=-1, keepdims=True)
        vals.append(m)
        idxs.append(am)
        work = jnp.where(lane == am, -jnp.inf, work)
    es = [jnp.exp(v - vals[0]) for v in vals]
    inv = 1.0 / (es[0] + es[1] + es[2] + es[3])

    onehot = (work != logits).astype(BF16)
    r_ = lax.broadcasted_iota(jnp.int32, (tm, tm), 0)
    c_ = lax.broadcasted_iota(jnp.int32, (tm, tm), 1)
    before = _dot((r_ > c_).astype(BF16), onehot) + carry_scr[...]
    carry_scr[...] = carry_scr[...] + jnp.sum(onehot.astype(F32), axis=0, keepdims=True)
    cnt_ref[...] = carry_scr[...]

    idx_out = jnp.zeros((tm, LANES), jnp.int32)
    gate_out = jnp.zeros((tm, LANES), F32)
    rank_out = jnp.zeros((tm, LANES), jnp.int32)
    for kk in range(TOP_K):
        rk = jnp.sum(jnp.where(lane == idxs[kk], before, 0.0), axis=-1, keepdims=True)
        idx_out = jnp.where(lane == kk, idxs[kk], idx_out)
        gate_out = jnp.where(lane == kk, es[kk] * inv, gate_out)
        rank_out = jnp.where(lane == kk, rk.astype(jnp.int32), rank_out)
    idx_ref[...] = idx_out
    gate_ref[...] = gate_out
    rank_ref[...] = rank_out


def _router(x2, g, w_pad, b_pad):
    T = x2.shape[0]
    row_blk = pl.BlockSpec((RT_TM, LANES), lambda i: (i, 0))
    return pl.pallas_call(
        _router_kernel,
        grid=(T // RT_TM,),
        in_specs=[
            pl.BlockSpec((RT_TM, D_MODEL), lambda i: (i, 0)),
            pl.BlockSpec((1, D_MODEL), lambda i: (0, 0)),
            pl.BlockSpec((D_MODEL, LANES), lambda i: (0, 0)),
            pl.BlockSpec((1, LANES), lambda i: (0, 0)),
        ],
        out_specs=[
            pl.BlockSpec((RT_TM, PACK_W), lambda i: (i, 0)),
            row_blk, row_blk, row_blk,
            pl.BlockSpec((1, LANES), lambda i: (0, 0)),
        ],
        out_shape=[
            jax.ShapeDtypeStruct((T, PACK_W), U32),
            jax.ShapeDtypeStruct((T, LANES), jnp.int32),
            jax.ShapeDtypeStruct((T, LANES), F32),
            jax.ShapeDtypeStruct((T, LANES), jnp.int32),
            jax.ShapeDtypeStruct((1, LANES), F32),
        ],
        scratch_shapes=[pltpu.VMEM((1, LANES), F32)],
        compiler_params=_params(("arbitrary",)),
        name="router",
    )(x2, g, w_pad, b_pad)


def _dispatch_kernel(idx_ref, rank_ref, pstart_ref, dest_ref):
    tm = idx_ref.shape[0]
    lane = lax.broadcasted_iota(I32, (tm, LANES), 1)
    idx = idx_ref[...]
    pstart = pstart_ref[...]
    dest = rank_ref[...]
    for kk in range(TOP_K):
        e = jnp.sum(jnp.where(lane == kk, idx, 0), axis=-1, keepdims=True)
        start = jnp.sum(jnp.where(lane == e, pstart, 0), axis=-1, keepdims=True)
        dest = dest + jnp.where(lane == kk, start, 0)
    dest_ref[...] = dest


def _dispatch(idx, rank, pstart_row):
    T = idx.shape[0]
    blk = pl.BlockSpec((RT_TM, LANES), lambda i: (i, 0))
    return pl.pallas_call(
        _dispatch_kernel,
        grid=(T // RT_TM,),
        in_specs=[blk, blk, pl.BlockSpec((1, LANES), lambda i: (0, 0))],
        out_specs=blk,
        out_shape=jax.ShapeDtypeStruct((T, LANES), I32),
        compiler_params=_params(("arbitrary",)),
        name="dispatch",
    )(idx, rank, pstart_row)


def _sc_mesh():
    return plsc.VectorSubcoreMesh(core_axis_name="c", subcore_axis_name="s",
                                  num_cores=SC_CORES, num_subcores=SC_SUBCORES)


def _sc_gather_rows(table_hbm, idx_v, out_hbm, out_base, n_steps, rows_v, gsem, wsem):
    def gather(s, b):
        return pltpu.make_async_copy(table_hbm.at[idx_v.at[pl.ds(s * SC_G, SC_G)]], rows_v.at[b], gsem.at[b])

    def write(s, b):
        return pltpu.make_async_copy(rows_v.at[b], out_hbm.at[pl.ds(out_base + s * SC_G, SC_G)], wsem.at[b])

    gather(0, 0).start()

    @pl.loop(0, n_steps, step=2)
    def _(s0):
        for b in range(2):
            s = s0 + b
            gather(s, b).wait()

            @pl.when(s + 1 < n_steps)
            def _():
                @pl.when(s >= 1)
                def _():
                    write(s - 1, 1 - b).wait()
                gather(s + 1, 1 - b).start()

            write(s, b).start()

    write(n_steps - 2, 0).wait()
    write(n_steps - 1, 1).wait()


def _sc_dispatch_gather(table, dest_flat, n_slots):
    T, width = table.shape
    n_workers = SC_CORES * SC_SUBCORES
    per_w = n_slots // n_workers
    n_steps = per_w // SC_G
    assert n_slots % n_workers == 0 and per_w % (2 * SC_G) == 0
    assert dest_flat.shape[0] % SC_CH == 0 and T & (T - 1) == 0

    @functools.partial(
        pl.kernel, mesh=_sc_mesh(), compiler_params=pltpu.CompilerParams(needs_layout_passes=False),
        out_type=jax.ShapeDtypeStruct((n_slots, width), table.dtype),
        scratch_types=[
            pltpu.VMEM((per_w,), I32),
            pltpu.VMEM((SC_CH,), I32),
            pltpu.VMEM((2, SC_G, width), table.dtype),
            pltpu.SemaphoreType.DMA((2,)),
            pltpu.SemaphoreType.DMA((2,)),
        ],
        name="sc_dispatch_gather",
    )
    def k(table_hbm, dest_hbm, out_hbm, inv_v, dst_v, rows_v, gsem, wsem):
        wid = lax.axis_index("s") * SC_CORES + lax.axis_index("c")
        lo = wid * per_w
        lane = lax.iota(I32, SC_LANES)

        @pl.loop(0, per_w // SC_LANES)
        def _(j):
            inv_v[pl.ds(j * SC_LANES, SC_LANES)] = jnp.zeros((SC_LANES,), I32)

        @pl.loop(0, dest_flat.shape[0] // SC_CH)
        def _(c):
            pltpu.sync_copy(dest_hbm.at[pl.ds(c * SC_CH, SC_CH)], dst_v)

            @pl.loop(0, SC_CH // SC_LANES)
            def _(j):
                local = dst_v[pl.ds(j * SC_LANES, SC_LANES)] - lo
                mine = (local >= 0) & (local < per_w)
                tok = (c * SC_CH + j * SC_LANES + lane) & (T - 1)
                plsc.store_scatter(inv_v, [jnp.clip(local, 0, per_w - 1)], tok, mask=mine)

        _sc_gather_rows(table_hbm, inv_v, out_hbm, lo, n_steps, rows_v, gsem, wsem)

    return k(table, dest_flat)


def _sc_combine_gather(table, dest_flat):
    n_rows, width = dest_flat.shape[0], table.shape[1]
    n_workers = SC_CORES * SC_SUBCORES
    per_w = n_rows // n_workers
    n_steps = per_w // SC_G
    assert n_rows % n_workers == 0 and per_w % (2 * SC_G) == 0

    @functools.partial(
        pl.kernel, mesh=_sc_mesh(),
        out_type=jax.ShapeDtypeStruct((n_rows, width), table.dtype),
        scratch_types=[
            pltpu.VMEM((per_w,), I32),
            pltpu.VMEM((2, SC_G, width), table.dtype),
            pltpu.SemaphoreType.DMA((2,)),
            pltpu.SemaphoreType.DMA((2,)),
        ],
        name="sc_combine_gather",
    )
    def k(table_hbm, dest_hbm, out_hbm, dst_v, rows_v, gsem, wsem):
        wid = lax.axis_index("s") * SC_CORES + lax.axis_index("c")
        lo = wid * per_w
        pltpu.sync_copy(dest_hbm.at[pl.ds(lo, per_w)], dst_v)
        _sc_gather_rows(table_hbm, dst_v, out_hbm, lo, n_steps, rows_v, gsem, wsem)

    return k(table, dest_flat)


def _cast_kernel(w_ref, o_ref):
    o_ref[...] = w_ref[...].astype(BF16)


def _cast_bf16(w):
    e, r, c = w.shape
    blk = pl.BlockSpec((None, CAST_ROWS, c), lambda i, j: (i, j, 0))
    return pl.pallas_call(
        _cast_kernel,
        grid=(e, r // CAST_ROWS),
        in_specs=[blk],
        out_specs=blk,
        out_shape=jax.ShapeDtypeStruct(w.shape, BF16),
        compiler_params=_params(("arbitrary", "arbitrary")),
        name="cast_bf16",
    )(w)


def _expert_kernel(be_ref, nb_ref, x_ref, wg_ref, wu_ref, bg_ref, bu_ref, wd_ref, prev_ref, y_ref, *, last):
    @pl.when(pl.program_id(0) < nb_ref[0])
    def _():
        x = _unpack_pairs(x_ref[...]).astype(BF16)
        gate = jnp.minimum(_dot(x, wg_ref[...]) + bg_ref[...], SWIGLU_LIMIT)
        up = jnp.clip(_dot(x, wu_ref[...]) + bu_ref[...], -SWIGLU_LIMIT, SWIGLU_LIMIT)
        act = (up + 1.0) * gate * _sigmoid(SWIGLU_ALPHA * gate)
        y = prev_ref[...] + _dot(act.astype(BF16), wd_ref[...])
        y_ref[...] = _pack_pairs(y) if last else y


def _expert_pass(c, block_e, nb_used, x_sorted, gu_w, gu_b, down_w, prev):
    P = x_sorted.shape[0]
    nc = D_FF // MOE_TF
    first, last = c == 0, c == nc - 1
    rowblk = lambda i, be, nbu: (jnp.minimum(i, nbu[0] - 1), 0)
    prev_spec = (pl.BlockSpec((None, 1, D_MODEL), lambda i, be, nbu: (be[i], 0, 0)) if first
                 else pl.BlockSpec((MOE_TM, D_MODEL), rowblk))
    grid_spec = pltpu.PrefetchScalarGridSpec(
        num_scalar_prefetch=2,
        grid=(P // MOE_TM,),
        in_specs=[
            pl.BlockSpec((MOE_TM, PACK_W), rowblk),
            pl.BlockSpec((None, D_MODEL, MOE_TF), lambda i, be, nbu: (be[i], 0, c)),
            pl.BlockSpec((None, D_MODEL, MOE_TF), lambda i, be, nbu: (be[i], 0, nc + c)),
            pl.BlockSpec((None, 1, MOE_TF), lambda i, be, nbu: (be[i], 0, c)),
            pl.BlockSpec((None, 1, MOE_TF), lambda i, be, nbu: (be[i], 0, nc + c)),
            pl.BlockSpec((None, MOE_TF, D_MODEL), lambda i, be, nbu: (be[i], c, 0)),
            prev_spec,
        ],
        out_specs=pl.BlockSpec((MOE_TM, PACK_W if last else D_MODEL), rowblk),
    )
    return pl.pallas_call(
        functools.partial(_expert_kernel, last=last),
        grid_spec=grid_spec,
        out_shape=jax.ShapeDtypeStruct((P, PACK_W), U32) if last else jax.ShapeDtypeStruct((P, D_MODEL), F32),
        compiler_params=_params(("arbitrary",)),
        name="experts_%d" % c,
    )(block_e, nb_used, x_sorted, gu_w, gu_w, gu_b, gu_b, down_w, prev)


def _combine_kernel(x_ref, y_ref, gate_ref, g_ref, o_ref):
    acc = x_ref[...]
    gates = gate_ref[...]
    for kk in range(TOP_K):
        acc = acc + gates[:, kk:kk + 1] * _unpack_pairs(y_ref[kk])
    o_ref[...] = _rms(acc, g_ref[...])


def _combine(x2, y_g, gates, g):
    T = x2.shape[0]
    return pl.pallas_call(
        _combine_kernel,
        grid=(T // CB_TM,),
        in_specs=[
            pl.BlockSpec((CB_TM, D_MODEL), lambda i: (i, 0)),
            pl.BlockSpec((TOP_K, CB_TM, PACK_W), lambda i: (0, i, 0)),
            pl.BlockSpec((CB_TM, LANES), lambda i: (i, 0)),
            pl.BlockSpec((1, D_MODEL), lambda i: (0, 0)),
        ],
        out_specs=pl.BlockSpec((CB_TM, D_MODEL), lambda i: (i, 0)),
        out_shape=jax.ShapeDtypeStruct((T, D_MODEL), F32),
        compiler_params=_params(("arbitrary",)),
        name="combine",
    )(x2, y_g, gates, g)


def _row(v):
    return v.reshape(1, -1).astype(F32)


def _layer(x, mem, norm_mix_g, w_in, gla_gate_w, gla_gate_b, gla_norm_g, gla_out_w, conv_w, conv_b,
           conv_out_w, mix_out_w, norm_xattn_g, norm_mem_g, xq_w, xk_w, xv_w, xo_w, norm_moe_g,
           router_w, router_b, gu_w, gu_b, down_w, down_b):
    T = x.shape[0]
    a0 = 3 * D_MODEL
    w_main = jnp.concatenate([w_in[:, :a0], w_in[:, a0 + GLA_GATE_RANK:]], axis=1).astype(BF16)
    w_alow = jnp.pad(w_in[:, a0:a0 + GLA_GATE_RANK], ((0, 0), (0, LANES - GLA_GATE_RANK))).astype(BF16)
    gate_w_pad = jnp.pad(gla_gate_w, ((0, LANES - GLA_GATE_RANK), (0, 0)))
    conv_w8 = jnp.pad(conv_w, ((0, 8 - CONV_WIDTH), (0, 0)))

    p, a_low = _in_proj(x, _row(norm_mix_g), w_main, w_alow)
    og = _gla(p, a_low, gate_w_pad, _row(gla_gate_b), _row(gla_norm_g))
    merged = _merge(og, p, conv_w8, _row(conv_b), gla_out_w.astype(BF16), conv_out_w.astype(BF16))
    x1 = _mix_out(merged, mix_out_w.astype(BF16), x)

    kmem, vmem = _mem_kv(mem, _row(norm_mem_g), xk_w.astype(BF16), xv_w.astype(BF16))
    x2 = _xattn(x1, _row(norm_xattn_g), xq_w.astype(BF16), kmem, vmem, xo_w.astype(BF16))

    rw = jnp.pad(router_w, ((0, 0), (0, LANES - N_EXPERTS))).astype(BF16)
    rb = jnp.pad(router_b, (0, LANES - N_EXPERTS)).reshape(1, LANES)
    hm, idx, gates, rank, counts = _router(x2, _row(norm_moe_g), rw, rb)

    cnt = counts[0, :N_EXPERTS].astype(I32)
    padded = (cnt + MOE_TM - 1) // MOE_TM * MOE_TM
    pend = jnp.cumsum(padded)
    pstart_row = jnp.pad(pend - padded, (0, LANES - N_EXPERTS)).reshape(1, LANES)
    nb = (T * TOP_K) // MOE_TM + N_EXPERTS
    blk_start = jnp.arange(nb, dtype=I32) * MOE_TM
    block_e = jnp.minimum(jnp.sum(pend[None, :] <= blk_start[:, None], axis=1), N_EXPERTS - 1).astype(I32)
    nb_used = (pend[-1] // MOE_TM).astype(I32).reshape(1)

    dest = _dispatch(idx, rank, pstart_row)
    dest_flat = dest[:, :TOP_K].T.reshape(-1)
    x_sorted = _sc_dispatch_gather(hm, dest_flat, nb * MOE_TM)

    gu_wb, down_wb = _cast_bf16(gu_w), _cast_bf16(down_w)
    gu_b3 = gu_b.reshape(N_EXPERTS, 1, 2 * D_FF)
    y = down_b.reshape(N_EXPERTS, 1, D_MODEL)
    for c in range(D_FF // MOE_TF):
        y = _expert_pass(c, block_e, nb_used, x_sorted, gu_wb, gu_b3, down_wb, y)
    y_g = _sc_combine_gather(y, dest_flat).reshape(TOP_K, T, PACK_W)
    return x2, y_g, gates


def kernel(x, mem, norm_mix_g, w_in, gla_gate_w, gla_gate_b, gla_norm_g, gla_out_w, conv_w, conv_b, conv_out_w, mix_out_w, norm_xattn_g, norm_mem_g, xq_w, xk_w, xv_w, xo_w, norm_moe_g, router_w, router_b, expert_gu_w, expert_gu_b, expert_down_w, expert_down_b, norm_final_g):
    assert x.shape[0] == 1 and mem.shape[0] == 1 and w_in.shape[0] == 1
    x2, y_g, gates = _layer(
        x[0], mem[0], norm_mix_g[0], w_in[0], gla_gate_w[0], gla_gate_b[0], gla_norm_g[0], gla_out_w[0],
        conv_w[0], conv_b[0], conv_out_w[0], mix_out_w[0], norm_xattn_g[0], norm_mem_g[0], xq_w[0], xk_w[0],
        xv_w[0], xo_w[0], norm_moe_g[0], router_w[0], router_b[0], expert_gu_w[0], expert_gu_b[0],
        expert_down_w[0], expert_down_b[0])
    out = _combine(x2, y_g, gates, _row(norm_final_g))
    return out[None]
```

```python
import functools

import jax
import jax.numpy as jnp
from jax import lax
from jax.experimental import pallas as pl
from jax.experimental.pallas import tpu as pltpu
from jax.experimental.pallas import tpu_sc as plsc

F32 = jnp.float32
BF16 = jnp.bfloat16
I32 = jnp.int32
U32 = jnp.uint32

D_MODEL = 2048
MEM_LEN = 256
GLA_HEADS = 4
GLA_DK = 256
GLA_DV = 512
GLA_GATE_RANK = 16
GLA_GATE_TEMP = 16.0
CONV_WIDTH = 3
XATTN_HEADS = 4
XATTN_HEAD_DIM = 128
XATTN_DIM = XATTN_HEADS * XATTN_HEAD_DIM
N_EXPERTS = 32
TOP_K = 4
D_FF = D_MODEL
SWIGLU_LIMIT = 7.0
SWIGLU_ALPHA = 1.702
NORM_EPS = 1e-5

LANES = 128
SC_CORES, SC_SUBCORES, SC_LANES = 2, 16, 16
PACK_W = D_MODEL // 2
P_COLS = 8 * D_MODEL
VMEM_LIMIT = 56 * 1024 * 1024

IN_TM, IN_TN = 1024, 1024
GLA_CH = 128
GLA_RB = 512
MG_TM = 256
XA_TM = 512
RT_TM = 512
MOE_TM = 256
MOE_TF = 1024
CB_TM = 256
SC_G = 32
SC_CH = 4096


def _params(sem):
    return pltpu.CompilerParams(dimension_semantics=sem, vmem_limit_bytes=VMEM_LIMIT)


def _rms(x, g):
    return x * lax.rsqrt(jnp.mean(x * x, axis=-1, keepdims=True) + NORM_EPS) * g


def _dot(a, b):
    return jnp.dot(a, b, preferred_element_type=F32)


def _dot_nt(a, b):
    return lax.dot_general(a, b, (((1,), (1,)), ((), ())), preferred_element_type=F32)


def _dot_tn(a, b):
    return lax.dot_general(a, b, (((0,), (0,)), ((), ())), preferred_element_type=F32)


def _split2(x):
    hi = x.astype(BF16)
    lo = (x - hi.astype(F32)).astype(BF16)
    return hi, lo


def _split3(x):
    hi = x.astype(BF16)
    r = x - hi.astype(F32)
    mid = r.astype(BF16)
    lo = (r - mid.astype(F32)).astype(BF16)
    return hi, mid, lo


def _sigmoid(x):
    return 1.0 / (1.0 + jnp.exp(-x))


def _pack_pairs(x):
    w = x.shape[1] // 2
    u = lax.bitcast_convert_type(x.astype(BF16).astype(F32), U32)
    return (u[:, :w] >> 16) | u[:, w:]


def _unpack_pairs(u):
    lo = lax.bitcast_convert_type(u << 16, F32)
    hi = lax.bitcast_convert_type(u & jnp.uint32(0xFFFF0000), F32)
    return jnp.concatenate([lo, hi], axis=1)


def _side_cast_spec(w2d, n_steps, step_of):
    rows, cols = w2d.shape
    assert rows % n_steps == 0
    return pl.BlockSpec((rows // n_steps, cols), lambda *ids: (step_of(*ids), 0))


def _inproj_kernel(x_ref, g_ref, w_ref, wa_ref, cin_ref, p_ref, a_ref, cout_ref, h_scr):
    @pl.when(pl.program_id(1) == 0)
    def _():
        hb = _rms(x_ref[...], g_ref[...]).astype(BF16)
        h_scr[...] = hb
        a_ref[...] = _dot(hb, wa_ref[...])

    p_ref[...] = _dot(h_scr[...], w_ref[...]).astype(BF16)
    cout_ref[...] = cin_ref[...].astype(BF16)


def _in_proj(x, g, w_main, w_alow, w_cast):
    T = x.shape[0]
    ni, nj = T // IN_TM, P_COLS // IN_TN
    cast_spec = _side_cast_spec(w_cast, ni * nj, lambda i, j: i * nj + j)
    return pl.pallas_call(
        _inproj_kernel,
        grid=(ni, nj),
        in_specs=[
            pl.BlockSpec((IN_TM, D_MODEL), lambda i, j: (i, 0)),
            pl.BlockSpec((1, D_MODEL), lambda i, j: (0, 0)),
            pl.BlockSpec((D_MODEL, IN_TN), lambda i, j: (0, j)),
            pl.BlockSpec((D_MODEL, LANES), lambda i, j: (0, 0)),
            cast_spec,
        ],
        out_specs=[
            pl.BlockSpec((IN_TM, IN_TN), lambda i, j: (i, j)),
            pl.BlockSpec((IN_TM, LANES), lambda i, j: (i, 0)),
            cast_spec,
        ],
        out_shape=[
            jax.ShapeDtypeStruct((T, P_COLS), BF16),
            jax.ShapeDtypeStruct((T, LANES), F32),
            jax.ShapeDtypeStruct(w_cast.shape, BF16),
        ],
        scratch_shapes=[pltpu.VMEM((IN_TM, D_MODEL), BF16)],
        compiler_params=_params(("arbitrary", "arbitrary")),
        name="in_proj",
    )(x, g, w_main, w_alow, w_cast)


def _gla_level_weights(b, la, row):
    ch, dk = b.shape
    out = []
    s = ch // 2
    while s >= 4:
        bb = b.reshape(ch // (2 * s), 2 * s, dk)
        d = (bb - bb[:, s - 1:s, :]).reshape(ch, dk)
        out.append((s, jnp.exp(-jnp.abs(d))))
        s //= 2
    la_prev = pltpu.roll(la, 1, 0)
    la_next = pltpu.roll(la, ch - 1, 0)
    r4 = row & 3
    d2 = jnp.where(r4 == 0, la_next, jnp.where(r4 == 1, 0.0, jnp.where(r4 == 2, la, la + la_prev)))
    out.append((2, jnp.exp(-jnp.abs(d2))))
    d1 = jnp.where((row & 1) == 1, la, 0.0)
    out.append((1, jnp.exp(-jnp.abs(d1))))
    return out


def _gla_kernel(q_ref, k_ref, v_ref, r_ref, a_ref, gw_ref, gb_ref, ng_ref, cin_ref, o_ref, cout_ref, st_scr):
    ch = GLA_CH
    cout_ref[...] = cin_ref[...].astype(BF16)

    @pl.when(pl.program_id(1) == 0)
    def _():
        st_scr[...] = jnp.zeros_like(st_scr)

    row = lax.broadcasted_iota(jnp.int32, (ch, 1), 0)
    col = lax.broadcasted_iota(jnp.int32, (1, ch), 1)
    tril = (row >= col).astype(BF16)
    gw_hi, gw_lo = _split2(gw_ref[...])
    gate_b = gb_ref[...]
    norm_g = ng_ref[...]

    def chunk(c, carry):
        sl = pl.ds(pl.multiple_of(c * ch, ch), ch)
        q = q_ref[sl, :].astype(F32) * (GLA_DK ** -0.5)
        k = k_ref[sl, :].astype(F32)
        v = v_ref[sl, :]
        a_hi, a_lo = _split2(a_ref[sl, :])
        z = _dot(a_hi, gw_hi) + _dot(a_lo, gw_hi) + _dot(a_hi, gw_lo) + gate_b
        la = (jnp.minimum(z, 0.0) - jnp.log(1.0 + jnp.exp(-jnp.abs(z)))) * (1.0 / GLA_GATE_TEMP)
        l_hi, l_mid, l_lo = _split3(la)
        b = _dot(tril, l_hi) + _dot(tril, l_mid) + _dot(tril, l_lo)

        scores = jnp.where(row == col, _dot_nt(q.astype(BF16), k.astype(BF16)), 0.0)
        for s, w in _gla_level_weights(b, la, row):
            in_right = (row & (2 * s - 1)) >= s
            ql = jnp.where(in_right, q * w, 0.0).astype(BF16)
            kl = jnp.where(in_right, 0.0, k * w).astype(BF16)
            part = _dot_nt(ql, kl)
            if 2 * s < ch:
                part = jnp.where((row ^ col) < 2 * s, part, 0.0)
            scores = scores + part
        o = _dot(scores.astype(BF16), v)

        st = st_scr[...]
        o = o + _dot_nt((q * jnp.exp(b)).astype(BF16), st.astype(BF16))
        b_last = b[ch - 1:ch, :]
        kd = (k * jnp.exp(b_last - b)).astype(BF16)
        st_scr[...] = st * jnp.exp(b_last) + _dot_tn(v, kd)

        o = _rms(o, norm_g)
        r = r_ref[sl, :].astype(F32)
        o_ref[sl, :] = (o * (r * _sigmoid(r))).astype(BF16)
        return carry

    lax.fori_loop(0, GLA_RB // ch, chunk, 0)


def _gla(p, a_low, gate_w_pad, gate_b, norm_g, w_cast):
    T = p.shape[0]
    ni = T // GLA_RB
    cast_spec = _side_cast_spec(w_cast, GLA_HEADS * ni, lambda h, i: h * ni + i)
    return pl.pallas_call(
        _gla_kernel,
        grid=(GLA_HEADS, ni),
        in_specs=[
            pl.BlockSpec((GLA_RB, GLA_DK), lambda h, i: (i, h)),
            pl.BlockSpec((GLA_RB, GLA_DK), lambda h, i: (i, GLA_HEADS + h)),
            pl.BlockSpec((GLA_RB, GLA_DV), lambda h, i: (i, GLA_HEADS + h)),
            pl.BlockSpec((GLA_RB, GLA_DV), lambda h, i: (i, 2 * GLA_HEADS + h)),
            pl.BlockSpec((GLA_RB, LANES), lambda h, i: (i, 0)),
            pl.BlockSpec((LANES, GLA_DK), lambda h, i: (0, h)),
            pl.BlockSpec((1, GLA_DK), lambda h, i: (0, h)),
            pl.BlockSpec((1, GLA_DV), lambda h, i: (0, 0)),
            cast_spec,
        ],
        out_specs=[pl.BlockSpec((GLA_RB, GLA_DV), lambda h, i: (i, h)), cast_spec],
        out_shape=[jax.ShapeDtypeStruct((T, GLA_HEADS * GLA_DV), BF16),
                   jax.ShapeDtypeStruct(w_cast.shape, BF16)],
        scratch_shapes=[pltpu.VMEM((GLA_DV, GLA_DK), F32)],
        compiler_params=_params(("arbitrary", "arbitrary")),
        name="gla",
    )(p, p, p, p, a_low, gate_w_pad, gate_b, norm_g, w_cast)


def _merge_kernel(og_ref, cb_ref, cc_ref, ch_ref, ccp_ref, chp_ref, cw_ref, cbias_ref,
                  ga_ref, gb_ref, wa_ref, wb_ref, wm_ref, x_ref, o_ref):
    tm = cc_ref.shape[0]
    u = cc_ref[...].astype(F32) * ch_ref[...].astype(F32)
    up = ccp_ref[...].astype(F32) * chp_ref[...].astype(F32)
    up = jnp.where(pl.program_id(0) == 0, 0.0, up)
    hp = up.shape[0]
    row = lax.broadcasted_iota(jnp.int32, (tm, 1), 0)
    u1 = jnp.where(row == 0, up[hp - 1:hp, :], pltpu.roll(u, 1, 0))
    u2 = jnp.where(row == 0, up[hp - 2:hp - 1, :],
                   jnp.where(row == 1, up[hp - 1:hp, :], pltpu.roll(u, 2, 0)))
    cw = cw_ref[...]
    uc = cw[0:1, :] * u2 + cw[1:2, :] * u1 + cw[2:3, :] * u + cbias_ref[...]
    zb = (cb_ref[...].astype(F32) * uc).astype(BF16)

    ya = _dot(og_ref[...], wa_ref[...])
    yb = _dot(zb, wb_ref[...])
    merged = (_sigmoid(ga_ref[...].astype(F32)) * ya + _sigmoid(gb_ref[...].astype(F32)) * yb).astype(BF16)
    o_ref[...] = x_ref[...] + _dot(merged, wm_ref[...])


def _merge(og, p, conv_w8, conv_b, gla_out_w, conv_out_w, mix_out_w, x):
    T = og.shape[0]
    halo = 16
    prev = lambda blk: (lambda i: (jnp.maximum(i * (MG_TM // halo) - 1, 0), blk))
    pblk = lambda blk: pl.BlockSpec((MG_TM, D_MODEL), lambda i: (i, blk))
    const = lambda shape: pl.BlockSpec(shape, lambda i: (0, 0), pipeline_mode=pl.Buffered(1))
    return pl.pallas_call(
        _merge_kernel,
        grid=(T // MG_TM,),
        in_specs=[
            pblk(0),
            pblk(3), pblk(4), pblk(5),
            pl.BlockSpec((halo, D_MODEL), prev(4)),
            pl.BlockSpec((halo, D_MODEL), prev(5)),
            const((8, D_MODEL)),
            const((1, D_MODEL)),
            pblk(6), pblk(7),
            const((D_MODEL, D_MODEL)), const((D_MODEL, D_MODEL)), const((D_MODEL, D_MODEL)),
            pblk(0),
        ],
        out_specs=pblk(0),
        out_shape=jax.ShapeDtypeStruct((T, D_MODEL), F32),
        compiler_params=_params(("arbitrary",)),
        name="merge",
    )(og, p, p, p, p, p, conv_w8, conv_b, p, p, gla_out_w, conv_out_w, mix_out_w, x)


def _memkv_kernel(m_ref, g_ref, wk_ref, wv_ref, k_ref, v_ref):
    mn = _rms(m_ref[...], g_ref[...]).astype(BF16)
    k_ref[...] = _dot(mn, wk_ref[...]).astype(BF16)
    v_ref[...] = _dot(mn, wv_ref[...]).astype(BF16)


def _mem_kv(mem, g, wk, wv):
    return pl.pallas_call(
        _memkv_kernel,
        out_shape=[jax.ShapeDtypeStruct((MEM_LEN, XATTN_DIM), BF16)] * 2,
        compiler_params=pltpu.CompilerParams(vmem_limit_bytes=VMEM_LIMIT),
        name="mem_kv",
    )(mem, g, wk, wv)


def _xattn_kernel(x_ref, g_ref, wq_ref, k_ref, v_ref, wo_ref, o_ref):
    x = x_ref[...]
    hq = _rms(x, g_ref[...]).astype(BF16)
    q = (_dot(hq, wq_ref[...]) * (XATTN_HEAD_DIM ** -0.5)).astype(BF16)
    outs = []
    for h in range(XATTN_HEADS):
        cs = slice(h * XATTN_HEAD_DIM, (h + 1) * XATTN_HEAD_DIM)
        s = _dot_nt(q[:, cs], k_ref[:, cs])
        s = s - jnp.max(s, axis=-1, keepdims=True)
        e = jnp.exp(s)
        p = e / jnp.sum(e, axis=-1, keepdims=True)
        outs.append(_dot(p.astype(BF16), v_ref[:, cs]))
    o = jnp.concatenate(outs, axis=-1).astype(BF16)
    o_ref[...] = x + _dot(o, wo_ref[...])


def _xattn(x1, g, wq, kmem, vmem, wo):
    T = x1.shape[0]
    full = lambda shape: pl.BlockSpec(shape, lambda i: (0, 0))
    return pl.pallas_call(
        _xattn_kernel,
        grid=(T // XA_TM,),
        in_specs=[
            pl.BlockSpec((XA_TM, D_MODEL), lambda i: (i, 0)),
            full((1, D_MODEL)),
            full((D_MODEL, XATTN_DIM)),
            full((MEM_LEN, XATTN_DIM)),
            full((MEM_LEN, XATTN_DIM)),
            full((XATTN_DIM, D_MODEL)),
        ],
        out_specs=pl.BlockSpec((XA_TM, D_MODEL), lambda i: (i, 0)),
        out_shape=jax.ShapeDtypeStruct((T, D_MODEL), F32),
        compiler_params=_params(("arbitrary",)),
        name="xattn",
    )(x1, g, wq, kmem, vmem, wo)


def _router_kernel(x_ref, g_ref, w_ref, b_ref, hm_ref, idx_ref, gate_ref, rank_ref, cnt_ref, carry_scr):
    tm = x_ref.shape[0]

    @pl.when(pl.program_id(0) == 0)
    def _():
        carry_scr[...] = jnp.zeros_like(carry_scr)

    h = _rms(x_ref[...], g_ref[...])
    hb = h.astype(BF16)
    hm_ref[...] = _pack_pairs(h)
    lane = lax.broadcasted_iota(jnp.int32, (tm, LANES), 1)
    logits = jnp.where(lane < N_EXPERTS, _dot(hb, w_ref[...]) + b_ref[...], -jnp.inf)

    vals, idxs = [], []
    work = logits
    for _ in range(TOP_K):
        m = jnp.max(work, axis=-1, keepdims=True)
        am = jnp.min(jnp.where(work == m, lane, LANES), axis=-1, keepdims=True)
        vals.append(m)
        idxs.append(am)
        work = jnp.where(lane == am, -jnp.inf, work)
    es = [jnp.exp(v - vals[0]) for v in vals]
    inv = 1.0 / (es[0] + es[1] + es[2] + es[3])

    onehot = (work != logits).astype(BF16)
    r_ = lax.broadcasted_iota(jnp.int32, (tm, tm), 0)
    c_ = lax.broadcasted_iota(jnp.int32, (tm, tm), 1)
    before = _dot((r_ > c_).astype(BF16), onehot) + carry_scr[...]
    carry_scr[...] = carry_scr[...] + jnp.sum(onehot.astype(F32), axis=0, keepdims=True)
    cnt_ref[...] = carry_scr[...]

    idx_out = jnp.zeros((tm, LANES), jnp.int32)
    gate_out = jnp.zeros((tm, LANES), F32)
    rank_out = jnp.zeros((tm, LANES), jnp.int32)
    for kk in range(TOP_K):
        rk = jnp.sum(jnp.where(lane == idxs[kk], before, 0.0), axis=-1, keepdims=True)
        idx_out = jnp.where(lane == kk, idxs[kk], idx_out)
        gate_out = jnp.where(lane == kk, es[kk] * inv, gate_out)
        rank_out = jnp.where(lane == kk, rk.astype(jnp.int32), rank_out)
    idx_ref[...] = idx_out
    gate_ref[...] = gate_out
    rank_ref[...] = rank_out


def _router(x2, g, w_pad, b_pad):
    T = x2.shape[0]
    row_blk = pl.BlockSpec((RT_TM, LANES), lambda i: (i, 0))
    return pl.pallas_call(
        _router_kernel,
        grid=(T // RT_TM,),
        in_specs=[
            pl.BlockSpec((RT_TM, D_MODEL), lambda i: (i, 0)),
            pl.BlockSpec((1, D_MODEL), lambda i: (0, 0)),
            pl.BlockSpec((D_MODEL, LANES), lambda i: (0, 0)),
            pl.BlockSpec((1, LANES), lambda i: (0, 0)),
        ],
        out_specs=[
            pl.BlockSpec((RT_TM, PACK_W), lambda i: (i, 0)),
            row_blk, row_blk, row_blk,
            pl.BlockSpec((1, LANES), lambda i: (0, 0)),
        ],
        out_shape=[
            jax.ShapeDtypeStruct((T, PACK_W), U32),
            jax.ShapeDtypeStruct((T, LANES), jnp.int32),
            jax.ShapeDtypeStruct((T, LANES), F32),
            jax.ShapeDtypeStruct((T, LANES), jnp.int32),
            jax.ShapeDtypeStruct((1, LANES), F32),
        ],
        scratch_shapes=[pltpu.VMEM((1, LANES), F32)],
        compiler_params=_params(("arbitrary",)),
        name="router",
    )(x2, g, w_pad, b_pad)


def _dispatch_kernel(idx_ref, rank_ref, pstart_ref, dest_ref):
    tm = idx_ref.shape[0]
    lane = lax.broadcasted_iota(I32, (tm, LANES), 1)
    idx = idx_ref[...]
    pstart = pstart_ref[...]
    dest = rank_ref[...]
    for kk in range(TOP_K):
        e = jnp.sum(jnp.where(lane == kk, idx, 0), axis=-1, keepdims=True)
        start = jnp.sum(jnp.where(lane == e, pstart, 0), axis=-1, keepdims=True)
        dest = dest + jnp.where(lane == kk, start, 0)
    dest_ref[...] = dest


def _dispatch(idx, rank, pstart_row):
    T = idx.shape[0]
    blk = pl.BlockSpec((RT_TM, LANES), lambda i: (i, 0))
    return pl.pallas_call(
        _dispatch_kernel,
        grid=(T // RT_TM,),
        in_specs=[blk, blk, pl.BlockSpec((1, LANES), lambda i: (0, 0))],
        out_specs=blk,
        out_shape=jax.ShapeDtypeStruct((T, LANES), I32),
        compiler_params=_params(("arbitrary",)),
        name="dispatch",
    )(idx, rank, pstart_row)


def _sc_mesh():
    return plsc.VectorSubcoreMesh(core_axis_name="c", subcore_axis_name="s",
                                  num_cores=SC_CORES, num_subcores=SC_SUBCORES)


def _sc_gather_rows(table_hbm, idx_v, out_hbm, out_base, n_steps, rows_v, gsem, wsem):
    def gather(s, b):
        return pltpu.make_async_copy(table_hbm.at[idx_v.at[pl.ds(s * SC_G, SC_G)]], rows_v.at[b], gsem.at[b])

    def write(s, b):
        return pltpu.make_async_copy(rows_v.at[b], out_hbm.at[pl.ds(out_base + s * SC_G, SC_G)], wsem.at[b])

    gather(0, 0).start()

    @pl.loop(0, n_steps, step=2)
    def _(s0):
        for b in range(2):
            s = s0 + b
            gather(s, b).wait()

            @pl.when(s + 1 < n_steps)
            def _():
                @pl.when(s >= 1)
                def _():
                    write(s - 1, 1 - b).wait()
                gather(s + 1, 1 - b).start()

            write(s, b).start()

    write(n_steps - 2, 0).wait()
    write(n_steps - 1, 1).wait()


def _sc_dispatch_gather(table, dest_flat, n_slots):
    T, width = table.shape
    n_workers = SC_CORES * SC_SUBCORES
    per_w = n_slots // n_workers
    n_steps = per_w // SC_G
    assert n_slots % n_workers == 0 and per_w % (2 * SC_G) == 0
    assert dest_flat.shape[0] % SC_CH == 0 and T & (T - 1) == 0

    @functools.partial(
        pl.kernel, mesh=_sc_mesh(), compiler_params=pltpu.CompilerParams(needs_layout_passes=False),
        out_type=jax.ShapeDtypeStruct((n_slots, width), table.dtype),
        scratch_types=[
            pltpu.VMEM((per_w,), I32),
            pltpu.VMEM((SC_CH,), I32),
            pltpu.VMEM((2, SC_G, width), table.dtype),
            pltpu.SemaphoreType.DMA((2,)),
            pltpu.SemaphoreType.DMA((2,)),
        ],
        name="sc_dispatch_gather",
    )
    def k(table_hbm, dest_hbm, out_hbm, inv_v, dst_v, rows_v, gsem, wsem):
        wid = lax.axis_index("s") * SC_CORES + lax.axis_index("c")
        lo = wid * per_w
        lane = lax.iota(I32, SC_LANES)

        @pl.loop(0, per_w // SC_LANES)
        def _(j):
            inv_v[pl.ds(j * SC_LANES, SC_LANES)] = jnp.zeros((SC_LANES,), I32)

        @pl.loop(0, dest_flat.shape[0] // SC_CH)
        def _(c):
            pltpu.sync_copy(dest_hbm.at[pl.ds(c * SC_CH, SC_CH)], dst_v)

            @pl.loop(0, SC_CH // SC_LANES)
            def _(j):
                local = dst_v[pl.ds(j * SC_LANES, SC_LANES)] - lo
                mine = (local >= 0) & (local < per_w)
                tok = (c * SC_CH + j * SC_LANES + lane) & (T - 1)
                plsc.store_scatter(inv_v, [jnp.clip(local, 0, per_w - 1)], tok, mask=mine)

        _sc_gather_rows(table_hbm, inv_v, out_hbm, lo, n_steps, rows_v, gsem, wsem)

    return k(table, dest_flat)


def _sc_combine_gather(table, dest_flat):
    n_rows, width = dest_flat.shape[0], table.shape[1]
    n_workers = SC_CORES * SC_SUBCORES
    per_w = n_rows // n_workers
    n_steps = per_w // SC_G
    assert n_rows % n_workers == 0 and per_w % (2 * SC_G) == 0

    @functools.partial(
        pl.kernel, mesh=_sc_mesh(),
        out_type=jax.ShapeDtypeStruct((n_rows, width), table.dtype),
        scratch_types=[
            pltpu.VMEM((per_w,), I32),
            pltpu.VMEM((2, SC_G, width), table.dtype),
            pltpu.SemaphoreType.DMA((2,)),
            pltpu.SemaphoreType.DMA((2,)),
        ],
        name="sc_combine_gather",
    )
    def k(table_hbm, dest_hbm, out_hbm, dst_v, rows_v, gsem, wsem):
        wid = lax.axis_index("s") * SC_CORES + lax.axis_index("c")
        lo = wid * per_w
        pltpu.sync_copy(dest_hbm.at[pl.ds(lo, per_w)], dst_v)
        _sc_gather_rows(table_hbm, dst_v, out_hbm, lo, n_steps, rows_v, gsem, wsem)

    return k(table, dest_flat)


def _expert_kernel(be_ref, nb_ref, x_ref, wg_ref, wu_ref, bg_ref, bu_ref, wd_ref, prev_ref, y_ref, *, last):
    @pl.when(pl.program_id(0) < nb_ref[0])
    def _():
        x = _unpack_pairs(x_ref[...]).astype(BF16)
        gate = jnp.minimum(_dot(x, wg_ref[...]) + bg_ref[...], SWIGLU_LIMIT)
        up = jnp.clip(_dot(x, wu_ref[...]) + bu_ref[...], -SWIGLU_LIMIT, SWIGLU_LIMIT)
        act = (up + 1.0) * gate * _sigmoid(SWIGLU_ALPHA * gate)
        y = prev_ref[...] + _dot(act.astype(BF16), wd_ref[...])
        y_ref[...] = _pack_pairs(y) if last else y


def _expert_pass(c, block_e, nb_used, x_sorted, gu_w, gu_b, down_w, prev):
    P = x_sorted.shape[0]
    nc = D_FF // MOE_TF
    first, last = c == 0, c == nc - 1
    rowblk = lambda i, be, nbu: (jnp.minimum(i, nbu[0] - 1), 0)
    prev_spec = (pl.BlockSpec((None, 1, D_MODEL), lambda i, be, nbu: (be[i], 0, 0)) if first
                 else pl.BlockSpec((MOE_TM, D_MODEL), rowblk))
    grid_spec = pltpu.PrefetchScalarGridSpec(
        num_scalar_prefetch=2,
        grid=(P // MOE_TM,),
        in_specs=[
            pl.BlockSpec((MOE_TM, PACK_W), rowblk),
            pl.BlockSpec((None, D_MODEL, MOE_TF), lambda i, be, nbu: (be[i], 0, c)),
            pl.BlockSpec((None, D_MODEL, MOE_TF), lambda i, be, nbu: (be[i], 0, nc + c)),
            pl.BlockSpec((None, 1, MOE_TF), lambda i, be, nbu: (be[i], 0, c)),
            pl.BlockSpec((None, 1, MOE_TF), lambda i, be, nbu: (be[i], 0, nc + c)),
            pl.BlockSpec((None, MOE_TF, D_MODEL), lambda i, be, nbu: (be[i], c, 0)),
            prev_spec,
        ],
        out_specs=pl.BlockSpec((MOE_TM, PACK_W if last else D_MODEL), rowblk),
    )
    return pl.pallas_call(
        functools.partial(_expert_kernel, last=last),
        grid_spec=grid_spec,
        out_shape=jax.ShapeDtypeStruct((P, PACK_W), U32) if last else jax.ShapeDtypeStruct((P, D_MODEL), F32),
        compiler_params=_params(("arbitrary",)),
        name="experts_%d" % c,
    )(block_e, nb_used, x_sorted, gu_w, gu_w, gu_b, gu_b, down_w, prev)


def _combine_kernel(x_ref, y_ref, gate_ref, g_ref, o_ref):
    acc = x_ref[...]
    gates = gate_ref[...]
    for kk in range(TOP_K):
        acc = acc + gates[:, kk:kk + 1] * _unpack_pairs(y_ref[kk])
    o_ref[...] = _rms(acc, g_ref[...])


def _combine(x2, y_g, gates, g):
    T = x2.shape[0]
    return pl.pallas_call(
        _combine_kernel,
        grid=(T // CB_TM,),
        in_specs=[
            pl.BlockSpec((CB_TM, D_MODEL), lambda i: (i, 0)),
            pl.BlockSpec((TOP_K, CB_TM, PACK_W), lambda i: (0, i, 0)),
            pl.BlockSpec((CB_TM, LANES), lambda i: (i, 0)),
            pl.BlockSpec((1, D_MODEL), lambda i: (0, 0)),
        ],
        out_specs=pl.BlockSpec((CB_TM, D_MODEL), lambda i: (i, 0)),
        out_shape=jax.ShapeDtypeStruct((T, D_MODEL), F32),
        compiler_params=_params(("arbitrary",)),
        name="combine",
    )(x2, y_g, gates, g)


def _row(v):
    return v.reshape(1, -1).astype(F32)


def _layer(x, mem, norm_mix_g, w_in, gla_gate_w, gla_gate_b, gla_norm_g, gla_out_w, conv_w, conv_b,
           conv_out_w, mix_out_w, norm_xattn_g, norm_mem_g, xq_w, xk_w, xv_w, xo_w, norm_moe_g,
           router_w, router_b, gu_w, gu_b, down_w, down_b):
    T = x.shape[0]
    a0 = 3 * D_MODEL
    w_main = jnp.concatenate([w_in[:, :a0], w_in[:, a0 + GLA_GATE_RANK:]], axis=1).astype(BF16)
    w_alow = jnp.pad(w_in[:, a0:a0 + GLA_GATE_RANK], ((0, 0), (0, LANES - GLA_GATE_RANK))).astype(BF16)
    gate_w_pad = jnp.pad(gla_gate_w, ((0, LANES - GLA_GATE_RANK), (0, 0)))
    conv_w8 = jnp.pad(conv_w, ((0, 8 - CONV_WIDTH), (0, 0)))

    p, a_low, gu_wb = _in_proj(x, _row(norm_mix_g), w_main, w_alow, gu_w.reshape(-1, 2 * D_FF))
    og, down_wb = _gla(p, a_low, gate_w_pad, _row(gla_gate_b), _row(gla_norm_g), down_w.reshape(-1, D_MODEL))
    gu_wb = gu_wb.reshape(gu_w.shape)
    down_wb = down_wb.reshape(down_w.shape)
    x1 = _merge(og, p, conv_w8, _row(conv_b), gla_out_w.astype(BF16), conv_out_w.astype(BF16),
                mix_out_w.astype(BF16), x)

    kmem, vmem = _mem_kv(mem, _row(norm_mem_g), xk_w.astype(BF16), xv_w.astype(BF16))
    x2 = _xattn(x1, _row(norm_xattn_g), xq_w.astype(BF16), kmem, vmem, xo_w.astype(BF16))

    rw = jnp.pad(router_w, ((0, 0), (0, LANES - N_EXPERTS))).astype(BF16)
    rb = jnp.pad(router_b, (0, LANES - N_EXPERTS)).reshape(1, LANES)
    hm, idx, gates, rank, counts = _router(x2, _row(norm_moe_g), rw, rb)

    cnt = counts[0, :N_EXPERTS].astype(I32)
    padded = (cnt + MOE_TM - 1) // MOE_TM * MOE_TM
    pend = jnp.cumsum(padded)
    pstart_row = jnp.pad(pend - padded, (0, LANES - N_EXPERTS)).reshape(1, LANES)
    nb = (T * TOP_K) // MOE_TM + N_EXPERTS
    blk_start = jnp.arange(nb, dtype=I32) * MOE_TM
    block_e = jnp.minimum(jnp.sum(pend[None, :] <= blk_start[:, None], axis=1), N_EXPERTS - 1).astype(I32)
    nb_used = (pend[-1] // MOE_TM).astype(I32).reshape(1)

    dest = _dispatch(idx, rank, pstart_row)
    dest_flat = dest[:, :TOP_K].T.reshape(-1)
    x_sorted = _sc_dispatch_gather(hm, dest_flat, nb * MOE_TM)

    gu_b3 =gu_b.reshape(N_EXPERTS, 1, 2 * D_FF)
    y = down_b.reshape(N_EXPERTS, 1, D_MODEL)
    for c in range(D_FF // MOE_TF):
        y = _expert_pass(c, block_e, nb_used, x_sorted, gu_wb, gu_b3, down_wb, y)
    y_g = _sc_combine_gather(y, dest_flat).reshape(TOP_K, T, PACK_W)
    return x2, y_g, gates


def kernel(x, mem, norm_mix_g, w_in, gla_gate_w, gla_gate_b, gla_norm_g, gla_out_w, conv_w, conv_b, conv_out_w, mix_out_w, norm_xattn_g, norm_mem_g, xq_w, xk_w, xv_w, xo_w, norm_moe_g, router_w, router_b, expert_gu_w, expert_gu_b, expert_down_w, expert_down_b, norm_final_g):
    assert x.shape[0] == 1 and mem.shape[0] == 1 and w_in.shape[0] == 1
    x2, y_g, gates = _layer(
        x[0], mem[0], norm_mix_g[0], w_in[0], gla_gate_w[0], gla_gate_b[0], gla_norm_g[0], gla_out_w[0],
        conv_w[0], conv_b[0], conv_out_w[0], mix_out_w[0], norm_xattn_g[0], norm_mem_g[0], xq_w[0], xk_w[0],
        xv_w[0], xo_w[0], norm_moe_g[0], router_w[0], router_b[0], expert_gu_w[0], expert_gu_b[0],
        expert_down_w[0], expert_down_b[0])
    out = _combine(x2, y_g, gates, _row(norm_final_g))
    return out[None]
```

```python
import functools

import jax
import jax.numpy as jnp
from jax import lax
from jax.experimental import pallas as pl
from jax.experimental.pallas import tpu as pltpu
from jax.experimental.pallas import tpu_sc as plsc

F32 = jnp.float32
BF16 = jnp.bfloat16
I32 = jnp.int32
U32 = jnp.uint32

D_MODEL = 2048
MEM_LEN = 256
GLA_HEADS = 4
GLA_DK = 256
GLA_DV = 512
GLA_GATE_RANK = 16
GLA_GATE_TEMP = 16.0
CONV_WIDTH = 3
XATTN_HEADS = 4
XATTN_HEAD_DIM = 128
XATTN_DIM = XATTN_HEADS * XATTN_HEAD_DIM
N_EXPERTS = 32
TOP_K = 4
D_FF = D_MODEL
SWIGLU_LIMIT = 7.0
SWIGLU_ALPHA = 1.702
NORM_EPS = 1e-5

LANES = 128
SC_CORES, SC_SUBCORES, SC_LANES = 2, 16, 16
PACK_W = D_MODEL // 2
P_COLS = 8 * D_MODEL
VMEM_LIMIT = 56 * 1024 * 1024

IN_TM, IN_TN = 1024, 1024
GLA_CH = 128
GLA_RB = 512
MG_TM = 256
XA_TM = 512
RT_TM = 512
MOE_TM = 256
MOE_TF = 1024
CB_TM = 256
SC_G = 32


def _params(sem):
    return pltpu.CompilerParams(dimension_semantics=sem, vmem_limit_bytes=VMEM_LIMIT)


def _rms(x, g):
    return x * lax.rsqrt(jnp.mean(x * x, axis=-1, keepdims=True) + NORM_EPS) * g


def _dot(a, b):
    return jnp.dot(a, b, preferred_element_type=F32)


def _dot_nt(a, b):
    return lax.dot_general(a, b, (((1,), (1,)), ((), ())), preferred_element_type=F32)


def _dot_tn(a, b):
    return lax.dot_general(a, b, (((0,), (0,)), ((), ())), preferred_element_type=F32)


def _split2(x):
    hi = x.astype(BF16)
    lo = (x - hi.astype(F32)).astype(BF16)
    return hi, lo


def _split3(x):
    hi = x.astype(BF16)
    r = x - hi.astype(F32)
    mid = r.astype(BF16)
    lo = (r - mid.astype(F32)).astype(BF16)
    return hi, mid, lo


def _sigmoid(x):
    return 1.0 / (1.0 + jnp.exp(-x))


def _pack_pairs(x):
    w = x.shape[1] // 2
    u = lax.bitcast_convert_type(x.astype(BF16).astype(F32), U32)
    return (u[:, :w] >> 16) | u[:, w:]


def _unpack_pairs(u):
    lo = lax.bitcast_convert_type(u << 16, F32)
    hi = lax.bitcast_convert_type(u & jnp.uint32(0xFFFF0000), F32)
    return jnp.concatenate([lo, hi], axis=1)


def _side_cast_spec(w2d, n_steps, step_of):
    rows, cols = w2d.shape
    assert rows % n_steps == 0
    return pl.BlockSpec((rows // n_steps, cols), lambda *ids: (step_of(*ids), 0))


def _inproj_kernel(x_ref, g_ref, w_ref, wa_ref, cin_ref, p_ref, a_ref, cout_ref, h_scr):
    @pl.when(pl.program_id(1) == 0)
    def _():
        hb = _rms(x_ref[...], g_ref[...]).astype(BF16)
        h_scr[...] = hb
        a_ref[...] = _dot(hb, wa_ref[...])

    p_ref[...] = _dot(h_scr[...], w_ref[...]).astype(BF16)
    cout_ref[...] = cin_ref[...].astype(BF16)


def _in_proj(x, g, w_main, w_alow, w_cast):
    T = x.shape[0]
    ni, nj = T // IN_TM, P_COLS // IN_TN
    cast_spec = _side_cast_spec(w_cast, ni * nj, lambda i, j: i * nj + j)
    return pl.pallas_call(
        _inproj_kernel,
        grid=(ni, nj),
        in_specs=[
            pl.BlockSpec((IN_TM, D_MODEL), lambda i, j: (i, 0)),
            pl.BlockSpec((1, D_MODEL), lambda i, j: (0, 0)),
            pl.BlockSpec((D_MODEL, IN_TN), lambda i, j: (0, j)),
            pl.BlockSpec((D_MODEL, LANES), lambda i, j: (0, 0)),
            cast_spec,
        ],
        out_specs=[
            pl.BlockSpec((IN_TM, IN_TN), lambda i, j: (i, j)),
            pl.BlockSpec((IN_TM, LANES), lambda i, j: (i, 0)),
            cast_spec,
        ],
        out_shape=[
            jax.ShapeDtypeStruct((T, P_COLS), BF16),
            jax.ShapeDtypeStruct((T, LANES), F32),
            jax.ShapeDtypeStruct(w_cast.shape, BF16),
        ],
        scratch_shapes=[pltpu.VMEM((IN_TM, D_MODEL), BF16)],
        compiler_params=_params(("arbitrary", "arbitrary")),
        name="in_proj",
    )(x, g, w_main, w_alow, w_cast)


def _gla_level_weights(b, la, row):
    ch, dk = b.shape
    out = []
    s = ch // 2
    while s >= 4:
        bb = b.reshape(ch // (2 * s), 2 * s, dk)
        d = (bb - bb[:, s - 1:s, :]).reshape(ch, dk)
        out.append((s, jnp.exp(-jnp.abs(d))))
        s //= 2
    la_prev = pltpu.roll(la, 1, 0)
    la_next = pltpu.roll(la, ch - 1, 0)
    r4 = row & 3
    d2 = jnp.where(r4 == 0, la_next, jnp.where(r4 == 1, 0.0, jnp.where(r4 == 2, la, la + la_prev)))
    out.append((2, jnp.exp(-jnp.abs(d2))))
    d1 = jnp.where((row & 1) == 1, la, 0.0)
    out.append((1, jnp.exp(-jnp.abs(d1))))
    return out


def _gla_kernel(q_ref, k_ref, v_ref, r_ref, a_ref, gw_ref, gb_ref, ng_ref, cin_ref, o_ref, cout_ref, st_scr):
    ch = GLA_CH
    cout_ref[...] = cin_ref[...].astype(BF16)

    @pl.when(pl.program_id(1) == 0)
    def _():
        st_scr[...] = jnp.zeros_like(st_scr)

    row = lax.broadcasted_iota(jnp.int32, (ch, 1), 0)
    col = lax.broadcasted_iota(jnp.int32, (1, ch), 1)
    tril = (row >= col).astype(BF16)
    gw_hi, gw_lo = _split2(gw_ref[...])
    gate_b = gb_ref[...]
    norm_g = ng_ref[...]

    def chunk(c, carry):
        sl = pl.ds(pl.multiple_of(c * ch, ch), ch)
        q = q_ref[sl, :].astype(F32) * (GLA_DK ** -0.5)
        k = k_ref[sl, :].astype(F32)
        v = v_ref[sl, :]
        a_hi, a_lo = _split2(a_ref[sl, :])
        z = _dot(a_hi, gw_hi) + _dot(a_lo, gw_hi) + _dot(a_hi, gw_lo) + gate_b
        la = (jnp.minimum(z, 0.0) - jnp.log(1.0 + jnp.exp(-jnp.abs(z)))) * (1.0 / GLA_GATE_TEMP)
        l_hi, l_mid, l_lo = _split3(la)
        b = _dot(tril, l_hi) + _dot(tril, l_mid) + _dot(tril, l_lo)

        scores = jnp.where(row == col, _dot_nt(q.astype(BF16), k.astype(BF16)), 0.0)
        for s, w in _gla_level_weights(b, la, row):
            in_right = (row & (2 * s - 1)) >= s
            ql = jnp.where(in_right, q * w, 0.0).astype(BF16)
            kl = jnp.where(in_right, 0.0, k * w).astype(BF16)
            part = _dot_nt(ql, kl)
            if 2 * s < ch:
                part = jnp.where((row ^ col) < 2 * s, part, 0.0)
            scores = scores + part
        o = _dot(scores.astype(BF16), v)

        st = st_scr[...]
        o = o + _dot_nt((q * jnp.exp(b)).astype(BF16), st.astype(BF16))
        b_last = b[ch - 1:ch, :]
        kd = (k * jnp.exp(b_last - b)).astype(BF16)
        st_scr[...] = st * jnp.exp(b_last) + _dot_tn(v, kd)

        o = _rms(o, norm_g)
        r = r_ref[sl, :].astype(F32)
        o_ref[sl, :] = (o * (r * _sigmoid(r))).astype(BF16)
        return carry

    lax.fori_loop(0, GLA_RB // ch, chunk, 0)


def _gla(p, a_low, gate_w_pad, gate_b, norm_g, w_cast):
    T = p.shape[0]
    ni = T // GLA_RB
    cast_spec = _side_cast_spec(w_cast, GLA_HEADS * ni, lambda h, i: h * ni + i)
    return pl.pallas_call(
        _gla_kernel,
        grid=(GLA_HEADS, ni),
        in_specs=[
            pl.BlockSpec((GLA_RB, GLA_DK), lambda h, i: (i, h)),
            pl.BlockSpec((GLA_RB, GLA_DK), lambda h, i: (i, GLA_HEADS + h)),
            pl.BlockSpec((GLA_RB, GLA_DV), lambda h, i: (i, GLA_HEADS + h)),
            pl.BlockSpec((GLA_RB, GLA_DV), lambda h, i: (i, 2 * GLA_HEADS + h)),
            pl.BlockSpec((GLA_RB, LANES), lambda h, i: (i, 0)),
            pl.BlockSpec((LANES, GLA_DK), lambda h, i: (0, h)),
            pl.BlockSpec((1, GLA_DK), lambda h, i: (0, h)),
            pl.BlockSpec((1, GLA_DV), lambda h, i: (0, 0)),
            cast_spec,
        ],
        out_specs=[pl.BlockSpec((GLA_RB, GLA_DV), lambda h, i: (i, h)), cast_spec],
        out_shape=[jax.ShapeDtypeStruct((T, GLA_HEADS * GLA_DV), BF16),
                   jax.ShapeDtypeStruct(w_cast.shape, BF16)],
        scratch_shapes=[pltpu.VMEM((GLA_DV, GLA_DK), F32)],
        compiler_params=_params(("arbitrary", "arbitrary")),
        name="gla",
    )(p, p, p, p, a_low, gate_w_pad, gate_b, norm_g, w_cast)


def _merge_kernel(og_ref, cb_ref, cc_ref, ch_ref, ccp_ref, chp_ref, cw_ref, cbias_ref,
                  ga_ref, gb_ref, wa_ref, wb_ref, wm_ref, x_ref, o_ref):
    tm = cc_ref.shape[0]
    u = cc_ref[...].astype(F32) * ch_ref[...].astype(F32)
    up = ccp_ref[...].astype(F32) * chp_ref[...].astype(F32)
    up = jnp.where(pl.program_id(0) == 0, 0.0, up)
    hp = up.shape[0]
    row = lax.broadcasted_iota(jnp.int32, (tm, 1), 0)
    u1 = jnp.where(row == 0, up[hp - 1:hp, :], pltpu.roll(u, 1, 0))
    u2 = jnp.where(row == 0, up[hp - 2:hp - 1, :],
                   jnp.where(row == 1, up[hp - 1:hp, :], pltpu.roll(u, 2, 0)))
    cw = cw_ref[...]
    uc = cw[0:1, :] * u2 + cw[1:2, :] * u1 + cw[2:3, :] * u + cbias_ref[...]
    zb = (cb_ref[...].astype(F32) * uc).astype(BF16)

    ya = _dot(og_ref[...], wa_ref[...])
    yb = _dot(zb, wb_ref[...])
    merged = (_sigmoid(ga_ref[...].astype(F32)) * ya + _sigmoid(gb_ref[...].astype(F32)) * yb).astype(BF16)
    o_ref[...] = x_ref[...] + _dot(merged, wm_ref[...])


def _merge(og, p, conv_w8, conv_b, gla_out_w, conv_out_w, mix_out_w, x):
    T = og.shape[0]
    halo = 16
    prev = lambda blk: (lambda i: (jnp.maximum(i * (MG_TM // halo) - 1, 0), blk))
    pblk = lambda blk: pl.BlockSpec((MG_TM, D_MODEL), lambda i: (i, blk))
    const = lambda shape: pl.BlockSpec(shape, lambda i: (0, 0), pipeline_mode=pl.Buffered(1))
    return pl.pallas_call(
        _merge_kernel,
        grid=(T // MG_TM,),
        in_specs=[
            pblk(0),
            pblk(3), pblk(4), pblk(5),
            pl.BlockSpec((halo, D_MODEL), prev(4)),
            pl.BlockSpec((halo, D_MODEL), prev(5)),
            const((8, D_MODEL)),
            const((1, D_MODEL)),
            pblk(6), pblk(7),
            const((D_MODEL, D_MODEL)), const((D_MODEL, D_MODEL)), const((D_MODEL, D_MODEL)),
            pblk(0),
        ],
        out_specs=pblk(0),
        out_shape=jax.ShapeDtypeStruct((T, D_MODEL), F32),
        compiler_params=_params(("arbitrary",)),
        name="merge",
    )(og, p, p, p, p, p, conv_w8, conv_b, p, p, gla_out_w, conv_out_w, mix_out_w, x)


def _memkv_kernel(m_ref, g_ref, wk_ref, wv_ref, k_ref, v_ref):
    mn = _rms(m_ref[...], g_ref[...]).astype(BF16)
    k_ref[...] = _dot(mn, wk_ref[...]).astype(BF16)
    v_ref[...] = _dot(mn, wv_ref[...]).astype(BF16)


def _mem_kv(mem, g, wk, wv):
    return pl.pallas_call(
        _memkv_kernel,
        out_shape=[jax.ShapeDtypeStruct((MEM_LEN, XATTN_DIM), BF16)] * 2,
        compiler_params=pltpu.CompilerParams(vmem_limit_bytes=VMEM_LIMIT),
        name="mem_kv",
    )(mem, g, wk, wv)


def _xattn_kernel(x_ref, g_ref, wq_ref, k_ref, v_ref, wo_ref, o_ref):
    x = x_ref[...]
    hq = _rms(x, g_ref[...]).astype(BF16)
    q = (_dot(hq, wq_ref[...]) * (XATTN_HEAD_DIM ** -0.5)).astype(BF16)
    outs = []
    for h in range(XATTN_HEADS):
        cs = slice(h * XATTN_HEAD_DIM, (h + 1) * XATTN_HEAD_DIM)
        s = _dot_nt(q[:, cs], k_ref[:, cs])
        s = s - jnp.max(s, axis=-1, keepdims=True)
        e = jnp.exp(s)
        p = e / jnp.sum(e, axis=-1, keepdims=True)
        outs.append(_dot(p.astype(BF16), v_ref[:, cs]))
    o = jnp.concatenate(outs, axis=-1).astype(BF16)
    o_ref[...] = x + _dot(o, wo_ref[...])


def _xattn(x1, g, wq, kmem, vmem, wo):
    T = x1.shape[0]
    full = lambda shape: pl.BlockSpec(shape, lambda i: (0, 0))
    return pl.pallas_call(
        _xattn_kernel,
        grid=(T // XA_TM,),
        in_specs=[
            pl.BlockSpec((XA_TM, D_MODEL), lambda i: (i, 0)),
            full((1, D_MODEL)),
            full((D_MODEL, XATTN_DIM)),
            full((MEM_LEN, XATTN_DIM)),
            full((MEM_LEN, XATTN_DIM)),
            full((XATTN_DIM, D_MODEL)),
        ],
        out_specs=pl.BlockSpec((XA_TM, D_MODEL), lambda i: (i, 0)),
        out_shape=jax.ShapeDtypeStruct((T, D_MODEL), F32),
        compiler_params=_params(("arbitrary",)),
        name="xattn",
    )(x1, g, wq, kmem, vmem, wo)


def _router_kernel(x_ref, g_ref, w_ref, b_ref, hm_ref, idx_ref, gate_ref, rank_ref, cnt_ref, carry_scr):
    tm = x_ref.shape[0]

    @pl.when(pl.program_id(0) == 0)
    def _():
        carry_scr[...] = jnp.zeros_like(carry_scr)

    h = _rms(x_ref[...], g_ref[...])
    hb = h.astype(BF16)
    hm_ref[...] = _pack_pairs(h)
    lane = lax.broadcasted_iota(jnp.int32, (tm, LANES), 1)
    logits = jnp.where(lane < N_EXPERTS, _dot(hb, w_ref[...]) + b_ref[...], -jnp.inf)

    vals, idxs = [], []
    work = logits
    for _ in range(TOP_K):
        m = jnp.max(work, axis=-1, keepdims=True)
        am = jnp.min(jnp.where(work == m, lane, LANES), axis=-1, keepdims=True)
        vals.append(m)
        idxs.append(am)
        work = jnp.where(lane == am, -jnp.inf, work)
    es = [jnp.exp(v - vals[0]) for v in vals]
    inv = 1.0 / (es[0] + es[1] + es[2] + es[3])

    onehot = (work != logits).astype(BF16)
    r_ = lax.broadcasted_iota(jnp.int32, (tm, tm), 0)
    c_ = lax.broadcasted_iota(jnp.int32, (tm, tm), 1)
    before = _dot((r_ > c_).astype(BF16), onehot) + carry_scr[...]
    carry_scr[...] = carry_scr[...] + jnp.sum(onehot.astype(F32), axis=0, keepdims=True)
    cnt_ref[...] = carry_scr[...]

    idx_out = jnp.zeros((tm, LANES), jnp.int32)
    gate_out = jnp.zeros((tm, LANES), F32)
    rank_out = jnp.zeros((tm, LANES), jnp.int32)
    for kk in range(TOP_K):
        rk = jnp.sum(jnp.where(lane == idxs[kk], before, 0.0), axis=-1, keepdims=True)
        idx_out = jnp.where(lane == kk, idxs[kk], idx_out)
        gate_out = jnp.where(lane == kk, es[kk] * inv, gate_out)
        rank_out = jnp.where(lane == kk, rk.astype(jnp.int32), rank_out)
    idx_ref[...] = idx_out
    gate_ref[...] = gate_out
    rank_ref[...] = rank_out


def _router(x2, g, w_pad, b_pad):
    T = x2.shape[0]
    row_blk = pl.BlockSpec((RT_TM, LANES), lambda i: (i, 0))
    return pl.pallas_call(
        _router_kernel,
        grid=(T // RT_TM,),
        in_specs=[
            pl.BlockSpec((RT_TM, D_MODEL), lambda i: (i, 0)),
            pl.BlockSpec((1, D_MODEL), lambda i: (0, 0)),
            pl.BlockSpec((D_MODEL, LANES), lambda i: (0, 0)),
            pl.BlockSpec((1, LANES), lambda i: (0, 0)),
        ],
        out_specs=[
            pl.BlockSpec((RT_TM, PACK_W), lambda i: (i, 0)),
            row_blk, row_blk, row_blk,
            pl.BlockSpec((1, LANES), lambda i: (0, 0)),
        ],
        out_shape=[
            jax.ShapeDtypeStruct((T, PACK_W), U32),
            jax.ShapeDtypeStruct((T, LANES), jnp.int32),
            jax.ShapeDtypeStruct((T, LANES), F32),
            jax.ShapeDtypeStruct((T, LANES), jnp.int32),
            jax.ShapeDtypeStruct((1, LANES), F32),
        ],
        scratch_shapes=[pltpu.VMEM((1, LANES), F32)],
        compiler_params=_params(("arbitrary",)),
        name="router",
    )(x2, g, w_pad, b_pad)


def _dispatch_kernel(idx_ref, rank_ref, pstart_ref, dest_ref):
    tm = idx_ref.shape[0]
    lane = lax.broadcasted_iota(I32, (tm, LANES), 1)
    idx = idx_ref[...]
    pstart = pstart_ref[...]
    dest = rank_ref[...]
    for kk in range(TOP_K):
        e = jnp.sum(jnp.where(lane == kk, idx, 0), axis=-1, keepdims=True)
        start = jnp.sum(jnp.where(lane == e, pstart, 0), axis=-1, keepdims=True)
        dest = dest + jnp.where(lane == kk, start, 0)
    dest_ref[...] = dest


def _dispatch(idx, rank, pstart_row):
    T = idx.shape[0]
    blk = pl.BlockSpec((RT_TM, LANES), lambda i: (i, 0))
    return pl.pallas_call(
        _dispatch_kernel,
        grid=(T // RT_TM,),
        in_specs=[blk, blk, pl.BlockSpec((1, LANES), lambda i: (0, 0))],
        out_specs=blk,
        out_shape=jax.ShapeDtypeStruct((T, LANES), I32),
        compiler_params=_params(("arbitrary",)),
        name="dispatch",
    )(idx, rank, pstart_row)


def _sc_mesh():
    return plsc.VectorSubcoreMesh(core_axis_name="c", subcore_axis_name="s",
                                  num_cores=SC_CORES, num_subcores=SC_SUBCORES)


def _sc_gather_rows(table_hbm, idx_v, out_hbm, out_base, n_steps, rows_v, gsem, wsem):
    def gather(s, b):
        return pltpu.make_async_copy(table_hbm.at[idx_v.at[pl.ds(s * SC_G, SC_G)]], rows_v.at[b], gsem.at[b])

    def write(s, b):
        return pltpu.make_async_copy(rows_v.at[b], out_hbm.at[pl.ds(out_base + s * SC_G, SC_G)], wsem.at[b])

    gather(0, 0).start()

    @pl.loop(0, n_steps, step=2)
    def _(s0):
        for b in range(2):
            s = s0 + b
            gather(s, b).wait()

            @pl.when(s + 1 < n_steps)
            def _():
                @pl.when(s >= 1)
                def _():
                    write(s - 1, 1 - b).wait()
                gather(s + 1, 1 - b).start()

            write(s, b).start()

    write(n_steps - 2, 0).wait()
    write(n_steps - 1, 1).wait()


def _sc_dispatch_scatter(table, dest_chunks, n_slots):
    T, width = table.shape
    n_workers = SC_CORES * SC_SUBCORES
    per_w = T // n_workers
    n_steps = per_w // SC_G
    assert T % n_workers == 0 and per_w % (2 * SC_G) == 0
    assert dest_chunks.shape == (T // SC_G, TOP_K, SC_G)

    @functools.partial(
        pl.kernel, mesh=_sc_mesh(),
        out_type=jax.ShapeDtypeStruct((n_slots, width), table.dtype),
        scratch_types=[
            pltpu.VMEM((n_steps, TOP_K, SC_G), I32),
            pltpu.VMEM((2, SC_G, width), table.dtype),
            pltpu.SemaphoreType.DMA((2,)),
            pltpu.SemaphoreType.DMA((2,)),
        ],
        name="sc_dispatch_scatter",
    )
    def k(table_hbm, dest_hbm, out_hbm, dst_v, rows_v, rsem, wsem):
        wid = lax.axis_index("s") * SC_CORES + lax.axis_index("c")
        pltpu.sync_copy(dest_hbm.at[pl.ds(wid * n_steps, n_steps)], dst_v)

        def read(s, b):
            return pltpu.make_async_copy(table_hbm.at[pl.ds(wid * per_w + s * SC_G, SC_G)], rows_v.at[b], rsem.at[b])

        def scatter(s, kk, b):
            return pltpu.make_async_copy(rows_v.at[b], out_hbm.at[dst_v.at[s, kk]], wsem.at[b])

        read(0, 0).start()

        @pl.loop(0, n_steps, step=2)
        def _(s0):
            for b in range(2):
                s = s0 + b
                read(s, b).wait()

                @pl.when(s + 1 < n_steps)
                def _():
                    @pl.when(s >= 1)
                    def _():
                        for kk in range(TOP_K):
                            scatter(s - 1, kk, 1 - b).wait()
                    read(s + 1, 1 - b).start()

                for kk in range(TOP_K):
                    scatter(s, kk, b).start()

        for kk in range(TOP_K):
            scatter(n_steps - 2, kk, 0).wait()
            scatter(n_steps - 1, kk, 1).wait()

    return k(table, dest_chunks)


def _sc_combine_gather(table, dest_flat):
    n_rows, width = dest_flat.shape[0], table.shape[1]
    n_workers = SC_CORES * SC_SUBCORES
    per_w = n_rows // n_workers
    n_steps = per_w // SC_G
    assert n_rows % n_workers == 0 and per_w % (2 * SC_G) == 0

    @functools.partial(
        pl.kernel, mesh=_sc_mesh(),
        out_type=jax.ShapeDtypeStruct((n_rows, width), table.dtype),
        scratch_types=[
            pltpu.VMEM((per_w,), I32),
            pltpu.VMEM((2, SC_G, width), table.dtype),
            pltpu.SemaphoreType.DMA((2,)),
            pltpu.SemaphoreType.DMA((2,)),
        ],
        name="sc_combine_gather",
    )
    def k(table_hbm, dest_hbm, out_hbm, dst_v, rows_v, gsem, wsem):
        wid = lax.axis_index("s") * SC_CORES + lax.axis_index("c")
        lo = wid * per_w
        pltpu.sync_copy(dest_hbm.at[pl.ds(lo, per_w)], dst_v)
        _sc_gather_rows(table_hbm, dst_v, out_hbm, lo, n_steps, rows_v, gsem, wsem)

    return k(table, dest_flat)


def _expert_kernel(be_ref, nb_ref, nv_ref, x_ref, wg_ref, wu_ref, bg_ref, bu_ref, wd_ref, prev_ref, y_ref, *, last):
    i = pl.program_id(0)

    @pl.when(i < nb_ref[0])
    def _():
        row = lax.broadcasted_iota(I32, (x_ref.shape[0], 1), 0)
        x = _unpack_pairs(jnp.where(row < nv_ref[i], x_ref[...], jnp.uint32(0))).astype(BF16)
        gate = jnp.minimum(_dot(x, wg_ref[...]) + bg_ref[...], SWIGLU_LIMIT)
        up = jnp.clip(_dot(x, wu_ref[...]) + bu_ref[...], -SWIGLU_LIMIT, SWIGLU_LIMIT)
        act = (up + 1.0) * gate * _sigmoid(SWIGLU_ALPHA * gate)
        y = prev_ref[...] + _dot(act.astype(BF16), wd_ref[...])
        y_ref[...] = _pack_pairs(y) if last else y


def _expert_pass(c, block_e, nb_used, n_valid, x_sorted, gu_w, gu_b, down_w, prev):
    P = x_sorted.shape[0]
    nc = D_FF // MOE_TF
    first, last = c == 0, c == nc - 1
    rowblk = lambda i, be, nbu, nv: (jnp.minimum(i, nbu[0] - 1), 0)
    prev_spec = (pl.BlockSpec((None, 1, D_MODEL), lambda i, be, nbu, nv: (be[i], 0, 0)) if first
                 else pl.BlockSpec((MOE_TM, D_MODEL), rowblk))
    grid_spec = pltpu.PrefetchScalarGridSpec(
        num_scalar_prefetch=3,
        grid=(P // MOE_TM,),
        in_specs=[
            pl.BlockSpec((MOE_TM, PACK_W), rowblk),
            pl.BlockSpec((None, D_MODEL, MOE_TF), lambda i, be, nbu, nv: (be[i], 0, c)),
            pl.BlockSpec((None, D_MODEL, MOE_TF), lambda i, be, nbu, nv: (be[i], 0, nc + c)),
            pl.BlockSpec((None, 1, MOE_TF), lambda i, be, nbu, nv: (be[i], 0, c)),
            pl.BlockSpec((None, 1, MOE_TF), lambda i, be, nbu, nv: (be[i], 0, nc + c)),
            pl.BlockSpec((None, MOE_TF, D_MODEL), lambda i, be, nbu, nv: (be[i], c, 0)),
            prev_spec,
        ],
        out_specs=pl.BlockSpec((MOE_TM, PACK_W if last else D_MODEL), rowblk),
    )
    return pl.pallas_call(
        functools.partial(_expert_kernel, last=last),
        grid_spec=grid_spec,
        out_shape=jax.ShapeDtypeStruct((P, PACK_W), U32) if last else jax.ShapeDtypeStruct((P, D_MODEL), F32),
        compiler_params=_params(("arbitrary",)),
        name="experts_%d" % c,
    )(block_e, nb_used, n_valid, x_sorted, gu_w, gu_w, gu_b, gu_b, down_w, prev)


def _combine_kernel(x_ref, y_ref, gate_ref, g_ref, o_ref):
    acc = x_ref[...]
    gates = gate_ref[...]
    for kk in range(TOP_K):
        acc = acc + gates[:, kk:kk + 1] * _unpack_pairs(y_ref[kk])
    o_ref[...] = _rms(acc, g_ref[...])


def _combine(x2, y_g, gates, g):
    T = x2.shape[0]
    return pl.pallas_call(
        _combine_kernel,
        grid=(T // CB_TM,),
        in_specs=[
            pl.BlockSpec((CB_TM, D_MODEL), lambda i: (i, 0)),
            pl.BlockSpec((TOP_K, CB_TM, PACK_W), lambda i: (0, i, 0)),
            pl.BlockSpec((CB_TM, LANES), lambda i: (i, 0)),
            pl.BlockSpec((1, D_MODEL), lambda i: (0, 0)),
        ],
        out_specs=pl.BlockSpec((CB_TM, D_MODEL), lambda i: (i, 0)),
        out_shape=jax.ShapeDtypeStruct((T, D_MODEL), F32),
        compiler_params=_params(("arbitrary",)),
        name="combine",
    )(x2, y_g, gates, g)


def _row(v):
    return v.reshape(1, -1).astype(F32)


def _layer(x, mem, norm_mix_g, w_in, gla_gate_w, gla_gate_b, gla_norm_g, gla_out_w, conv_w, conv_b,
           conv_out_w, mix_out_w, norm_xattn_g, norm_mem_g, xq_w, xk_w, xv_w, xo_w, norm_moe_g,
           router_w, router_b, gu_w, gu_b, down_w, down_b):
    T = x.shape[0]
    a0 = 3 * D_MODEL
    w_main = jnp.concatenate([w_in[:, :a0], w_in[:, a0 + GLA_GATE_RANK:]], axis=1).astype(BF16)
    w_alow = jnp.pad(w_in[:, a0:a0 + GLA_GATE_RANK], ((0, 0), (0, LANES - GLA_GATE_RANK))).astype(BF16)
    gate_w_pad = jnp.pad(gla_gate_w, ((0, LANES - GLA_GATE_RANK), (0, 0)))
    conv_w8 = jnp.pad(conv_w, ((0, 8 - CONV_WIDTH), (0, 0)))

    p, a_low, gu_wb = _in_proj(x, _row(norm_mix_g), w_main, w_alow, gu_w.reshape(-1, 2 * D_FF))
    og, down_wb = _gla(p, a_low, gate_w_pad, _row(gla_gate_b), _row(gla_norm_g), down_w.reshape(-1, D_MODEL))
    gu_wb = gu_wb.reshape(gu_w.shape)
    down_wb = down_wb.reshape(down_w.shape)
    x1 = _merge(og, p, conv_w8, _row(conv_b), gla_out_w.astype(BF16), conv_out_w.astype(BF16),
                mix_out_w.astype(BF16), x)

    kmem, vmem = _mem_kv(mem, _row(norm_mem_g), xk_w.astype(BF16), xv_w.astype(BF16))
    x2 = _xattn(x1, _row(norm_xattn_g), xq_w.astype(BF16), kmem, vmem, xo_w.astype(BF16))

    rw = jnp.pad(router_w, ((0, 0), (0, LANES - N_EXPERTS))).astype(BF16)
    rb = jnp.pad(router_b, (0, LANES - N_EXPERTS)).reshape(1, LANES)
    hm, idx, gates, rank, counts = _router(x2, _row(norm_moe_g), rw, rb)

    cnt = counts[0, :N_EXPERTS].astype(I32)
    padded = (cnt + MOE_TM - 1) // MOE_TM * MOE_TM
    pend = jnp.cumsum(padded)
    pstart = pend - padded
    pstart_row = jnp.pad(pstart, (0, LANES - N_EXPERTS)).reshape(1, LANES)
    nb = (T * TOP_K) // MOE_TM + N_EXPERTS
    blk_start = jnp.arange(nb, dtype=I32) * MOE_TM
    block_e = jnp.minimum(jnp.sum(pend[None, :] <= blk_start[:, None], axis=1), N_EXPERTS - 1).astype(I32)
    nb_used = (pend[-1] // MOE_TM).astype(I32).reshape(1)
    n_valid = jnp.clip(pstart[block_e] + cnt[block_e] - blk_start, 0, MOE_TM).astype(I32)

    dest = _dispatch(idx, rank, pstart_row)[:, :TOP_K]
    dest_chunks = dest.reshape(T // SC_G, SC_G, TOP_K).transpose(0, 2, 1)
    x_sorted = _sc_dispatch_scatter(hm, dest_chunks, nb * MOE_TM)

    gu_b3 = gu_b.reshape(N_EXPERTS, 1, 2 * D_FF)
    y = down_b.reshape(N_EXPERTS, 1, D_MODEL)
    for c in range(D_FF // MOE_TF):
        y = _expert_pass(c, block_e, nb_used, n_valid, x_sorted, gu_wb, gu_b3, down_wb, y)
    y_g = _sc_combine_gather(y, dest.T.reshape(-1)).reshape(TOP_K, T, PACK_W)
    return x2, y_g, gates


def kernel(x, mem, norm_mix_g, w_in, gla_gate_w, gla_gate_b, gla_norm_g, gla_out_w, conv_w, conv_b, conv_out_w, mix_out_w, norm_xattn_g, norm_mem_g, xq_w, xk_w, xv_w, xo_w, norm_moe_g, router_w, router_b, expert_gu_w, expert_gu_b, expert_down_w, expert_down_b, norm_final_g):
    assert x.shape[0] == 1 and mem.shape[0] == 1 and w_in.shape[0] == 1
    x2, y_g, gates = _layer(
        x[0], mem[0], norm_mix_g[0], w_in[0], gla_gate_w[0], gla_gate_b[0], gla_norm_g[0], gla_out_w[0],
        conv_w[0], conv_b[0], conv_out_w[0], mix_out_w[0], norm_xattn_g[0], norm_mem_g[0], xq_w[0], xk_w[0],
        xv_w[0], xo_w[0], norm_moe_g[0], router_w[0], router_b[0], expert_gu_w[0], expert_gu_b[0],
        expert_down_w[0], expert_down_b[0])
    out = _combine(x2, y_g, gates, _row(norm_final_g))
    return out[None]
```

```python
import functools

import jax
import jax.numpy as jnp
from jax import lax
from jax.experimental import pallas as pl
from jax.experimental.pallas import tpu as pltpu
from jax.experimental.pallas import tpu_sc as plsc

F32 = jnp.float32
BF16 = jnp.bfloat16
I32 = jnp.int32
U32 = jnp.uint32

D_MODEL = 2048
MEM_LEN = 256
GLA_HEADS = 4
GLA_DK = 256
GLA_DV = 512
GLA_GATE_RANK = 16
GLA_GATE_TEMP = 16.0
CONV_WIDTH = 3
XATTN_HEADS = 4
XATTN_HEAD_DIM = 128
XATTN_DIM = XATTN_HEADS * XATTN_HEAD_DIM
N_EXPERTS = 32
TOP_K = 4
D_FF = D_MODEL
SWIGLU_LIMIT = 7.0
SWIGLU_ALPHA = 1.702
NORM_EPS = 1e-5

LANES = 128
SC_CORES, SC_SUBCORES, SC_LANES = 2, 16, 16
PACK_W = D_MODEL // 2
P_COLS = 8 * D_MODEL
VMEM_LIMIT = 56 * 1024 * 1024

IN_TM, IN_TN = 1024, 1024
GLA_CH = 128
GLA_RB = 512
GLA_HG = 2
MG_TM = 256
XA_TM = 512
RT_TM = 512
MOE_TM = 256
MOE_TF = 1024
CB_TM = 256
SC_G = 32
SC_GG, SC_NBUF = 16, 4


def _params(sem):
    return pltpu.CompilerParams(dimension_semantics=sem, vmem_limit_bytes=VMEM_LIMIT)


def _rms(x, g):
    return x * lax.rsqrt(jnp.mean(x * x, axis=-1, keepdims=True) + NORM_EPS) * g


def _dot(a, b):
    return jnp.dot(a, b, preferred_element_type=F32)


def _dot_nt(a, b):
    return lax.dot_general(a, b, (((1,), (1,)), ((), ())), preferred_element_type=F32)


def _dot_tn(a, b):
    return lax.dot_general(a, b, (((0,), (0,)), ((), ())), preferred_element_type=F32)


def _split2(x):
    hi = x.astype(BF16)
    lo = (x - hi.astype(F32)).astype(BF16)
    return hi, lo


def _split3(x):
    hi = x.astype(BF16)
    r = x - hi.astype(F32)
    mid = r.astype(BF16)
    lo = (r - mid.astype(F32)).astype(BF16)
    return hi, mid, lo


def _sigmoid(x):
    return 1.0 / (1.0 + jnp.exp(-x))


def _pack_pairs(x):
    w = x.shape[1] // 2
    u = lax.bitcast_convert_type(x.astype(BF16).astype(F32), U32)
    return (u[:, :w] >> 16) | u[:, w:]


def _unpack_pairs(u):
    lo = lax.bitcast_convert_type(u << 16, F32)
    hi = lax.bitcast_convert_type(u & jnp.uint32(0xFFFF0000), F32)
    return jnp.concatenate([lo, hi], axis=1)


def _side_cast_spec(w2d, n_steps, step_of):
    rows, cols = w2d.shape
    assert rows % n_steps == 0
    return pl.BlockSpec((rows // n_steps, cols), lambda *ids: (step_of(*ids), 0))


def _inproj_kernel(x_ref, g_ref, w_ref, wa_ref, cin_ref, p_ref, a_ref, cout_ref, h_scr):
    @pl.when(pl.program_id(1) == 0)
    def _():
        hb = _rms(x_ref[...], g_ref[...]).astype(BF16)
        h_scr[...] = hb
        a_ref[...] = _dot(hb, wa_ref[...])

    p_ref[...] = _dot(h_scr[...], w_ref[...]).astype(BF16)
    cout_ref[...] = cin_ref[...].astype(BF16)


def _in_proj(x, g, w_main, w_alow, w_cast):
    T = x.shape[0]
    ni, nj = T // IN_TM, P_COLS // IN_TN
    cast_spec = _side_cast_spec(w_cast, ni * nj, lambda i, j: i * nj + j)
    return pl.pallas_call(
        _inproj_kernel,
        grid=(ni, nj),
        in_specs=[
            pl.BlockSpec((IN_TM, D_MODEL), lambda i, j: (i, 0)),
            pl.BlockSpec((1, D_MODEL), lambda i, j: (0, 0)),
            pl.BlockSpec((D_MODEL, IN_TN), lambda i, j: (0, j)),
            pl.BlockSpec((D_MODEL, LANES), lambda i, j: (0, 0)),
            cast_spec,
        ],
        out_specs=[
            pl.BlockSpec((IN_TM, IN_TN), lambda i, j: (i, j)),
            pl.BlockSpec((IN_TM, LANES), lambda i, j: (i, 0)),
            cast_spec,
        ],
        out_shape=[
            jax.ShapeDtypeStruct((T, P_COLS), BF16),
            jax.ShapeDtypeStruct((T, LANES), F32),
            jax.ShapeDtypeStruct(w_cast.shape, BF16),
        ],
        scratch_shapes=[pltpu.VMEM((IN_TM, D_MODEL), BF16)],
        compiler_params=_params(("arbitrary", "arbitrary")),
        name="in_proj",
    )(x, g, w_main, w_alow, w_cast)


def _gla_level_weights(b, la, row):
    ch, dk = b.shape
    out = []
    s = ch // 2
    while s >= 4:
        bb = b.reshape(ch // (2 * s), 2 * s, dk)
        d = (bb - bb[:, s - 1:s, :]).reshape(ch, dk)
        out.append((s, jnp.exp(-jnp.abs(d))))
        s //= 2
    la_prev = pltpu.roll(la, 1, 0)
    la_next = pltpu.roll(la, ch - 1, 0)
    r4 = row & 3
    d2 = jnp.where(r4 == 0, la_next, jnp.where(r4 == 1, 0.0, jnp.where(r4 == 2, la, la + la_prev)))
    out.append((2, jnp.exp(-jnp.abs(d2))))
    d1 = jnp.where((row & 1) == 1, la, 0.0)
    out.append((1, jnp.exp(-jnp.abs(d1))))
    return out


def _gla_kernel(q_ref, k_ref, v_ref, r_ref, a_ref, gw_ref, gb_ref, ng_ref, cin_ref, o_ref, cout_ref, st_scr):
    ch = GLA_CH
    cout_ref[...] = cin_ref[...].astype(BF16)

    @pl.when(pl.program_id(1) == 0)
    def _():
        st_scr[...] = jnp.zeros_like(st_scr)

    row = lax.broadcasted_iota(jnp.int32, (ch, 1), 0)
    col = lax.broadcasted_iota(jnp.int32, (1, ch), 1)
    tril = (row >= col).astype(BF16)
    levels = []
    s = ch // 2
    while s >= 1:
        levels.append((s, ((row ^ col) < 2 * s) & ((row & (2 * s - 1)) >= s) & ((col & (2 * s - 1)) < s)))
        s //= 2
    gw_hi, gw_lo = _split2(gw_ref[...])
    gate_b = gb_ref[...]
    norm_g = ng_ref[...]
    q_scale = jnp.asarray(GLA_DK ** -0.5, BF16)

    def head(sl, hh, a_hi, a_lo):
        ks = slice(hh * GLA_DK, (hh + 1) * GLA_DK)
        vs = slice(hh * GLA_DV, (hh + 1) * GLA_DV)
        qb = q_ref[sl, ks] * q_scale
        kb = k_ref[sl, ks]
        q, k = qb.astype(F32), kb.astype(F32)
        v = v_ref[sl, vs]
        z = _dot(a_hi, gw_hi[:, ks]) + _dot(a_lo, gw_hi[:, ks]) + _dot(a_hi, gw_lo[:, ks]) + gate_b[:, ks]
        la = (jnp.minimum(z, 0.0) - jnp.log(1.0 + jnp.exp(-jnp.abs(z)))) * (1.0 / GLA_GATE_TEMP)
        l_hi, l_mid, l_lo = _split3(la)
        b = _dot(tril, l_hi) + _dot(tril, l_mid) + _dot(tril, l_lo)

        scores = jnp.where(row == col, _dot_nt(qb, kb), 0.0)
        for (s, mask), (_, w) in zip(levels, _gla_level_weights(b, la, row)):
            scores = jnp.where(mask, _dot_nt((q * w).astype(BF16), (k * w).astype(BF16)), scores)
        o = _dot(scores.astype(BF16), v)

        st = st_scr[hh]
        o = o + _dot_nt((q * jnp.exp(b)).astype(BF16), st.astype(BF16))
        b_last = b[ch - 1:ch, :]
        kd = (k * jnp.exp(b_last - b)).astype(BF16)
        st_scr[hh] = st * jnp.exp(b_last) + _dot_tn(v, kd)

        o = _rms(o, norm_g)
        r = r_ref[sl, vs].astype(F32)
        o_ref[sl, vs] = (o * (r * _sigmoid(r))).astype(BF16)

    def chunk(c, carry):
        sl = pl.ds(pl.multiple_of(c * ch, ch), ch)
        a_hi, a_lo = _split2(a_ref[sl, :])
        for hh in range(GLA_HG):
            head(sl, hh, a_hi, a_lo)
        return carry

    lax.fori_loop(0, GLA_RB // ch, chunk, 0)


def _gla(p, a_low, gate_w_pad, gate_b, norm_g, w_cast):
    T = p.shape[0]
    ni = T // GLA_RB
    ng = GLA_HEADS // GLA_HG
    kw, vw = GLA_HG * GLA_DK, GLA_HG * GLA_DV
    cast_spec = _side_cast_spec(w_cast, ng * ni, lambda h, i: h * ni + i)
    return pl.pallas_call(
        _gla_kernel,
        grid=(ng, ni),
        in_specs=[
            pl.BlockSpec((GLA_RB, kw), lambda h, i: (i, h)),
            pl.BlockSpec((GLA_RB, kw), lambda h, i: (i, ng + h)),
            pl.BlockSpec((GLA_RB, vw), lambda h, i: (i, ng + h)),
            pl.BlockSpec((GLA_RB, vw), lambda h, i: (i, 2 * ng + h)),
            pl.BlockSpec((GLA_RB, LANES), lambda h, i: (i, 0)),
            pl.BlockSpec((LANES, kw), lambda h, i: (0, h)),
            pl.BlockSpec((1, kw), lambda h, i: (0, h)),
            pl.BlockSpec((1, GLA_DV), lambda h, i: (0, 0)),
            cast_spec,
        ],
        out_specs=[pl.BlockSpec((GLA_RB, vw), lambda h, i: (i, h)), cast_spec],
        out_shape=[jax.ShapeDtypeStruct((T, GLA_HEADS * GLA_DV), BF16),
                   jax.ShapeDtypeStruct(w_cast.shape, BF16)],
        scratch_shapes=[pltpu.VMEM((GLA_HG, GLA_DV, GLA_DK), F32)],
        compiler_params=_params(("arbitrary", "arbitrary")),
        name="gla",
    )(p, p, p, p, a_low, gate_w_pad, gate_b, norm_g, w_cast)


def _merge_kernel(og_ref, cb_ref, cc_ref, ch_ref, ccp_ref, chp_ref, cw_ref, cbias_ref,
                  ga_ref, gb_ref, wa_ref, wb_ref, wm_ref, x_ref, o_ref):
    tm = cc_ref.shape[0]
    u = cc_ref[...].astype(F32) * ch_ref[...].astype(F32)
    up = ccp_ref[...].astype(F32) * chp_ref[...].astype(F32)
    up = jnp.where(pl.program_id(0) == 0, 0.0, up)
    hp = up.shape[0]
    row = lax.broadcasted_iota(jnp.int32, (tm, 1), 0)
    u1 = jnp.where(row == 0, up[hp - 1:hp, :], pltpu.roll(u, 1, 0))
    u2 = jnp.where(row == 0, up[hp - 2:hp - 1, :],
                   jnp.where(row == 1, up[hp - 1:hp, :], pltpu.roll(u, 2, 0)))
    cw = cw_ref[...]
    uc = cw[0:1, :] * u2 + cw[1:2, :] * u1 + cw[2:3, :] * u + cbias_ref[...]
    zb = (cb_ref[...].astype(F32) * uc).astype(BF16)

    ya = _dot(og_ref[...], wa_ref[...])
    yb = _dot(zb, wb_ref[...])
    merged = (_sigmoid(ga_ref[...].astype(F32)) * ya + _sigmoid(gb_ref[...].astype(F32)) * yb).astype(BF16)
    o_ref[...] = x_ref[...] + _dot(merged, wm_ref[...])


def _merge(og, p, conv_w8, conv_b, gla_out_w, conv_out_w, mix_out_w, x):
    T = og.shape[0]
    halo = 16
    prev = lambda blk: (lambda i: (jnp.maximum(i * (MG_TM // halo) - 1, 0), blk))
    pblk = lambda blk: pl.BlockSpec((MG_TM, D_MODEL), lambda i: (i, blk))
    const = lambda shape: pl.BlockSpec(shape, lambda i: (0, 0), pipeline_mode=pl.Buffered(1))
    return pl.pallas_call(
        _merge_kernel,
        grid=(T // MG_TM,),
        in_specs=[
            pblk(0),
            pblk(3), pblk(4), pblk(5),
            pl.BlockSpec((halo, D_MODEL), prev(4)),
            pl.BlockSpec((halo, D_MODEL), prev(5)),
            const((8, D_MODEL)),
            const((1, D_MODEL)),
            pblk(6), pblk(7),
            const((D_MODEL, D_MODEL)), const((D_MODEL, D_MODEL)), const((D_MODEL, D_MODEL)),
            pblk(0),
        ],
        out_specs=pblk(0),
        out_shape=jax.ShapeDtypeStruct((T, D_MODEL), F32),
        compiler_params=_params(("arbitrary",)),
        name="merge",
    )(og, p, p, p, p, p, conv_w8, conv_b, p, p, gla_out_w, conv_out_w, mix_out_w, x)


def _memkv_kernel(m_ref, g_ref, wk_ref, wv_ref, k_ref, v_ref):
    mn = _rms(m_ref[...], g_ref[...]).astype(BF16)
    k_ref[...] = _dot(mn, wk_ref[...]).astype(BF16)
    v_ref[...] = _dot(mn, wv_ref[...]).astype(BF16)


def _mem_kv(mem, g, wk, wv):
    return pl.pallas_call(
        _memkv_kernel,
        out_shape=[jax.ShapeDtypeStruct((MEM_LEN, XATTN_DIM), BF16)] * 2,
        compiler_params=pltpu.CompilerParams(vmem_limit_bytes=VMEM_LIMIT),
        name="mem_kv",
    )(mem, g, wk, wv)


def _xattn_kernel(x_ref, g_ref, wq_ref, k_ref, v_ref, wo_ref, o_ref):
    x = x_ref[...]
    hq = _rms(x, g_ref[...]).astype(BF16)
    q = (_dot(hq, wq_ref[...]) * (XATTN_HEAD_DIM ** -0.5)).astype(BF16)
    outs = []
    for h in range(XATTN_HEADS):
        cs = slice(h * XATTN_HEAD_DIM, (h + 1) * XATTN_HEAD_DIM)
        s = _dot_nt(q[:, cs], k_ref[:, cs])
        s = s - jnp.max(s, axis=-1, keepdims=True)
        e = jnp.exp(s)
        p = e / jnp.sum(e, axis=-1, keepdims=True)
        outs.append(_dot(p.astype(BF16), v_ref[:, cs]))
    o = jnp.concatenate(outs, axis=-1).astype(BF16)
    o_ref[...] = x + _dot(o, wo_ref[...])


def _xattn(x1, g, wq, kmem, vmem, wo):
    T = x1.shape[0]
    full = lambda shape: pl.BlockSpec(shape, lambda i: (0, 0))
    return pl.pallas_call(
        _xattn_kernel,
        grid=(T // XA_TM,),
        in_specs=[
            pl.BlockSpec((XA_TM, D_MODEL), lambda i: (i, 0)),
            full((1, D_MODEL)),
            full((D_MODEL, XATTN_DIM)),
            full((MEM_LEN, XATTN_DIM)),
            full((MEM_LEN, XATTN_DIM)),
            full((XATTN_DIM, D_MODEL)),
        ],
        out_specs=pl.BlockSpec((XA_TM, D_MODEL), lambda i: (i, 0)),
        out_shape=jax.ShapeDtypeStruct((T, D_MODEL), F32),
        compiler_params=_params(("arbitrary",)),
        name="xattn",
    )(x1, g, wq, kmem, vmem, wo)


def _router_kernel(x_ref, g_ref, w_ref, b_ref, hm_ref, idx_ref, gate_ref, rank_ref, cnt_ref, carry_scr):
    tm = x_ref.shape[0]

    @pl.when(pl.program_id(0) == 0)
    def _():
        carry_scr[...] = jnp.zeros_like(carry_scr)

    h = _rms(x_ref[...], g_ref[...])
    hb = h.astype(BF16)
    hm_ref[...] = _pack_pairs(h)
    lane = lax.broadcasted_iota(jnp.int32, (tm, LANES), 1)
    logits = jnp.where(lane < N_EXPERTS, _dot(hb, w_ref[...]) + b_ref[...], -jnp.inf)

    vals, idxs = [], []
    work = logits
    for _ in range(TOP_K):
        m = jnp.max(work, axis=-1, keepdims=True)
        am = jnp.min(jnp.where(work == m, lane, LANES), axis=-1, keepdims=True)
        vals.append(m)
        idxs.append(am)
        work = jnp.where(lane == am, -jnp.inf, work)
    es = [jnp.exp(v - vals[0]) for v in vals]
    inv = 1.0 / (es[0] + es[1] + es[2] + es[3])

    onehot = (work != logits).astype(BF16)
    r_ = lax.broadcasted_iota(jnp.int32, (tm, tm), 0)
    c_ = lax.broadcasted_iota(jnp.int32, (tm, tm), 1)
    before = _dot((r_ > c_).astype(BF16), onehot) + carry_scr[...]
    carry_scr[...] = carry_scr[...] + jnp.sum(onehot.astype(F32), axis=0, keepdims=True)
    cnt_ref[...] = carry_scr[...]

    idx_out = jnp.zeros((tm, LANES), jnp.int32)
    gate_out = jnp.zeros((tm, LANES), F32)
    rank_out = jnp.zeros((tm, LANES), jnp.int32)
    for kk in range(TOP_K):
        rk = jnp.sum(jnp.where(lane == idxs[kk], before, 0.0), axis=-1, keepdims=True)
        idx_out = jnp.where(lane == kk, idxs[kk], idx_out)
        gate_out = jnp.where(lane == kk, es[kk] * inv, gate_out)
        rank_out = jnp.where(lane == kk, rk.astype(jnp.int32), rank_out)
    idx_ref[...] = idx_out
    gate_ref[...] = gate_out
    rank_ref[...] = rank_out


def _router(x2, g, w_pad, b_pad):
    T = x2.shape[0]
    row_blk = pl.BlockSpec((RT_TM, LANES), lambda i: (i, 0))
    return pl.pallas_call(
        _router_kernel,
        grid=(T // RT_TM,),
        in_specs=[
            pl.BlockSpec((RT_TM, D_MODEL), lambda i: (i, 0)),
            pl.BlockSpec((1, D_MODEL), lambda i: (0, 0)),
            pl.BlockSpec((D_MODEL, LANES), lambda i: (0, 0)),
            pl.BlockSpec((1, LANES), lambda i: (0, 0)),
        ],
        out_specs=[
            pl.BlockSpec((RT_TM, PACK_W), lambda i: (i, 0)),
            row_blk, row_blk, row_blk,
            pl.BlockSpec((1, LANES), lambda i: (0, 0)),
        ],
        out_shape=[
            jax.ShapeDtypeStruct((T, PACK_W), U32),
            jax.ShapeDtypeStruct((T, LANES), jnp.int32),
            jax.ShapeDtypeStruct((T, LANES), F32),
            jax.ShapeDtypeStruct((T, LANES), jnp.int32),
            jax.ShapeDtypeStruct((1, LANES), F32),
        ],
        scratch_shapes=[pltpu.VMEM((1, LANES), F32)],
        compiler_params=_params(("arbitrary",)),
        name="router",
    )(x2, g, w_pad, b_pad)


def _dispatch_kernel(idx_ref, rank_ref, pstart_ref, dest_ref):
    tm = idx_ref.shape[0]
    lane = lax.broadcasted_iota(I32, (tm, LANES), 1)
    idx = idx_ref[...]
    pstart = pstart_ref[...]
    dest = rank_ref[...]
    for kk in range(TOP_K):
        e = jnp.sum(jnp.where(lane == kk, idx, 0), axis=-1, keepdims=True)
        start = jnp.sum(jnp.where(lane == e, pstart, 0), axis=-1, keepdims=True)
        dest = dest + jnp.where(lane == kk, start, 0)
    dest_ref[...] = dest


def _dispatch(idx, rank, pstart_row):
    T = idx.shape[0]
    blk = pl.BlockSpec((RT_TM, LANES), lambda i: (i, 0))
    return pl.pallas_call(
        _dispatch_kernel,
        grid=(T // RT_TM,),
        in_specs=[blk, blk, pl.BlockSpec((1, LANES), lambda i: (0, 0))],
        out_specs=blk,
        out_shape=jax.ShapeDtypeStruct((T, LANES), I32),
        compiler_params=_params(("arbitrary",)),
        name="dispatch",
    )(idx, rank, pstart_row)


def _sc_mesh():
    return plsc.VectorSubcoreMesh(core_axis_name="c", subcore_axis_name="s",
                                  num_cores=SC_CORES, num_subcores=SC_SUBCORES)


def _sc_gather_rows(table_hbm, idx_v, out_hbm, out_base, n_steps, rows_v, gsem, wsem):
    nbuf = SC_NBUF

    def gather(s, b):
        return pltpu.make_async_copy(table_hbm.at[idx_v.at[pl.ds(s * SC_GG, SC_GG)]], rows_v.at[b], gsem.at[b])

    def write(s, b):
        return pltpu.make_async_copy(rows_v.at[b], out_hbm.at[pl.ds(out_base + s * SC_GG, SC_GG)], wsem.at[b])

    for j in range(nbuf - 1):
        gather(j, j).start()

    @pl.loop(0, n_steps, step=nbuf)
    def _(s0):
        for b in range(nbuf):
            s = s0 + b
            gather(s, b).wait()
            write(s, b).start()
            refill = (b + nbuf - 1) % nbuf

            @pl.when(s + nbuf - 1 < n_steps)
            def _():
                @pl.when(s >= 1)
                def _():
                    write(s - 1, refill).wait()
                gather(s + nbuf - 1, refill).start()

    for j in range(nbuf):
        write(n_steps - nbuf + j, j).wait()


def _sc_dispatch_scatter(table, dest_chunks, n_slots):
    T, width = table.shape
    n_workers = SC_CORES * SC_SUBCORES
    per_w = T // n_workers
    n_steps = per_w // SC_G
    assert T % n_workers == 0 and per_w % (2 * SC_G) == 0
    assert dest_chunks.shape == (T // SC_G, TOP_K, SC_G)

    @functools.partial(
        pl.kernel, mesh=_sc_mesh(),
        out_type=jax.ShapeDtypeStruct((n_slots, width), table.dtype),
        scratch_types=[
            pltpu.VMEM((n_steps, TOP_K, SC_G), I32),
            pltpu.VMEM((2, SC_G, width), table.dtype),
            pltpu.SemaphoreType.DMA((2,)),
            pltpu.SemaphoreType.DMA((2,)),
        ],
        name="sc_dispatch_scatter",
    )
    def k(table_hbm, dest_hbm, out_hbm, dst_v, rows_v, rsem, wsem):
        wid = lax.axis_index("s") * SC_CORES + lax.axis_index("c")
        pltpu.sync_copy(dest_hbm.at[pl.ds(wid * n_steps, n_steps)], dst_v)

        def read(s, b):
            return pltpu.make_async_copy(table_hbm.at[pl.ds(wid * per_w + s * SC_G, SC_G)], rows_v.at[b], rsem.at[b])

        def scatter(s, kk, b):
            return pltpu.make_async_copy(rows_v.at[b], out_hbm.at[dst_v.at[s, kk]], wsem.at[b])

        read(0, 0).start()

        @pl.loop(0, n_steps, step=2)
        def _(s0):
            for b in range(2):
                s = s0 + b
                read(s, b).wait()

                @pl.when(s + 1 < n_steps)
                def _():
                    @pl.when(s >= 1)
                    def _():
                        for kk in range(TOP_K):
                            scatter(s - 1, kk, 1 - b).wait()
                    read(s + 1, 1 - b).start()

                for kk in range(TOP_K):
                    scatter(s, kk, b).start()

        for kk in range(TOP_K):
            scatter(n_steps - 2, kk, 0).wait()
            scatter(n_steps - 1, kk, 1).wait()

    return k(table, dest_chunks)


def _sc_combine_gather(table, dest_flat):
    n_rows, width = dest_flat.shape[0], table.shape[1]
    n_workers = SC_CORES * SC_SUBCORES
    per_w = n_rows // n_workers
    n_steps = per_w // SC_GG
    assert n_rows % n_workers == 0 and per_w % (SC_NBUF * SC_GG) == 0

    @functools.partial(
        pl.kernel, mesh=_sc_mesh(),
        out_type=jax.ShapeDtypeStruct((n_rows, width), table.dtype),
        scratch_types=[
            pltpu.VMEM((per_w,), I32),
            pltpu.VMEM((SC_NBUF, SC_GG, width), table.dtype),
            pltpu.SemaphoreType.DMA((SC_NBUF,)),
            pltpu.SemaphoreType.DMA((SC_NBUF,)),
        ],
        name="sc_combine_gather",
    )
    def k(table_hbm, dest_hbm, out_hbm, dst_v, rows_v, gsem, wsem):
        wid = lax.axis_index("s") * SC_CORES + lax.axis_index("c")
        lo = wid * per_w
        pltpu.sync_copy(dest_hbm.at[pl.ds(lo, per_w)], dst_v)
        _sc_gather_rows(table_hbm, dst_v, out_hbm, lo, n_steps, rows_v, gsem, wsem)

    return k(table, dest_flat)


def _expert_kernel(be_ref, nb_ref, nv_ref, x_ref, wg_ref, wu_ref, bg_ref, bu_ref, wd_ref, prev_ref, y_ref, *, last):
    i = pl.program_id(0)

    @pl.when(i < nb_ref[0])
    def _():
        row = lax.broadcasted_iota(I32, (x_ref.shape[0], 1), 0)
        x = _unpack_pairs(jnp.where(row < nv_ref[i], x_ref[...], jnp.uint32(0))).astype(BF16)
        gate = jnp.minimum(_dot(x, wg_ref[...]) + bg_ref[...], SWIGLU_LIMIT)
        up = jnp.clip(_dot(x, wu_ref[...]) + bu_ref[...], -SWIGLU_LIMIT, SWIGLU_LIMIT)
        act = (up + 1.0) * gate * _sigmoid(SWIGLU_ALPHA * gate)
        y = prev_ref[...] + _dot(act.astype(BF16), wd_ref[...])
        y_ref[...] = _pack_pairs(y) if last else y


def _expert_pass(c, block_e, nb_used, n_valid, x_sorted, gu_w, gu_b, down_w, prev):
    P = x_sorted.shape[0]
    nc = D_FF // MOE_TF
    first, last = c == 0, c == nc - 1
    rowblk = lambda i, be, nbu, nv: (jnp.minimum(i, nbu[0] - 1), 0)
    prev_spec = (pl.BlockSpec((None, 1, D_MODEL), lambda i, be, nbu, nv: (be[i], 0, 0)) if first
                 else pl.BlockSpec((MOE_TM, D_MODEL), rowblk))
    grid_spec = pltpu.PrefetchScalarGridSpec(
        num_scalar_prefetch=3,
        grid=(P // MOE_TM,),
        in_specs=[
            pl.BlockSpec((MOE_TM, PACK_W), rowblk),
            pl.BlockSpec((None, D_MODEL, MOE_TF), lambda i, be, nbu, nv: (be[i], 0, c)),
            pl.BlockSpec((None, D_MODEL, MOE_TF), lambda i, be, nbu, nv: (be[i], 0, nc + c)),
            pl.BlockSpec((None, 1, MOE_TF), lambda i, be, nbu, nv: (be[i], 0, c)),
            pl.BlockSpec((None, 1, MOE_TF), lambda i, be, nbu, nv: (be[i], 0, nc + c)),
            pl.BlockSpec((None, MOE_TF, D_MODEL), lambda i, be, nbu, nv: (be[i], c, 0)),
            prev_spec,
        ],
        out_specs=pl.BlockSpec((MOE_TM, PACK_W if last else D_MODEL), rowblk),
    )
    return pl.pallas_call(
        functools.partial(_expert_kernel, last=last),
        grid_spec=grid_spec,
        out_shape=jax.ShapeDtypeStruct((P, PACK_W), U32) if last else jax.ShapeDtypeStruct((P, D_MODEL), F32),
        compiler_params=_params(("arbitrary",)),
        name="experts_%d" % c,
    )(block_e, nb_used, n_valid, x_sorted, gu_w, gu_w, gu_b, gu_b, down_w, prev)


def _combine_kernel(x_ref, y_ref, gate_ref, g_ref, o_ref):
    acc = x_ref[...]
    gates = gate_ref[...]
    for kk in range(TOP_K):
        acc = acc + gates[:, kk:kk + 1] * _unpack_pairs(y_ref[kk])
    o_ref[...] = _rms(acc, g_ref[...])


def _combine(x2, y_g, gates, g):
    T = x2.shape[0]
    return pl.pallas_call(
        _combine_kernel,
        grid=(T // CB_TM,),
        in_specs=[
            pl.BlockSpec((CB_TM, D_MODEL), lambda i: (i, 0)),
            pl.BlockSpec((TOP_K, CB_TM, PACK_W), lambda i: (0, i, 0)),
            pl.BlockSpec((CB_TM, LANES), lambda i: (i, 0)),
            pl.BlockSpec((1, D_MODEL), lambda i: (0, 0)),
        ],
        out_specs=pl.BlockSpec((CB_TM, D_MODEL), lambda i: (i, 0)),
        out_shape=jax.ShapeDtypeStruct((T, D_MODEL), F32),
        compiler_params=_params(("arbitrary",)),
        name="combine",
    )(x2, y_g, gates, g)


def _row(v):
    return v.reshape(1, -1).astype(F32)


def _layer(x, mem, norm_mix_g, w_in, gla_gate_w, gla_gate_b, gla_norm_g, gla_out_w, conv_w, conv_b,
           conv_out_w, mix_out_w, norm_xattn_g, norm_mem_g, xq_w, xk_w, xv_w, xo_w, norm_moe_g,
           router_w, router_b, gu_w, gu_b, down_w, down_b):
    T = x.shape[0]
    a0 = 3 * D_MODEL
    w_main = jnp.concatenate([w_in[:, :a0].astype(BF16), w_in[:, a0 + GLA_GATE_RANK:].astype(BF16)], axis=1)
    w_alow = jnp.pad(w_in[:, a0:a0 + GLA_GATE_RANK], ((0, 0), (0, LANES - GLA_GATE_RANK))).astype(BF16)
    gate_w_pad = jnp.pad(gla_gate_w, ((0, LANES - GLA_GATE_RANK), (0, 0)))
    conv_w8 = jnp.pad(conv_w, ((0, 8 - CONV_WIDTH), (0, 0)))

    p, a_low, gu_wb = _in_proj(x, _row(norm_mix_g), w_main, w_alow, gu_w.reshape(-1, 2 * D_FF))
    og, down_wb = _gla(p, a_low, gate_w_pad, _row(gla_gate_b), _row(gla_norm_g), down_w.reshape(-1, D_MODEL))
    gu_wb = gu_wb.reshape(gu_w.shape)
    down_wb = down_wb.reshape(down_w.shape)
    x1 = _merge(og, p, conv_w8, _row(conv_b), gla_out_w.astype(BF16), conv_out_w.astype(BF16),
                mix_out_w.astype(BF16), x)

    kmem, vmem = _mem_kv(mem, _row(norm_mem_g), xk_w.astype(BF16), xv_w.astype(BF16))
    x2 = _xattn(x1, _row(norm_xattn_g), xq_w.astype(BF16), kmem, vmem, xo_w.astype(BF16))

    rw = jnp.pad(router_w, ((0, 0), (0, LANES - N_EXPERTS))).astype(BF16)
    rb = jnp.pad(router_b, (0, LANES - N_EXPERTS)).reshape(1, LANES)
    hm, idx, gates, rank, counts = _router(x2, _row(norm_moe_g), rw, rb)

    cnt = counts[0, :N_EXPERTS].astype(I32)
    padded = (cnt + MOE_TM - 1) // MOE_TM * MOE_TM
    pend = jnp.cumsum(padded)
    pstart = pend - padded
    pstart_row = jnp.pad(pstart, (0, LANES - N_EXPERTS)).reshape(1, LANES)
    nb = (T * TOP_K) // MOE_TM + N_EXPERTS
    blk_start = jnp.arange(nb, dtype=I32) * MOE_TM
    block_e = jnp.minimum(jnp.sum(pend[None, :] <= blk_start[:, None], axis=1), N_EXPERTS - 1).astype(I32)
    nb_used = (pend[-1] // MOE_TM).astype(I32).reshape(1)
    n_valid = jnp.clip(pstart[block_e] + cnt[block_e] - blk_start, 0, MOE_TM).astype(I32)

    dest = _dispatch(idx, rank, pstart_row)[:, :TOP_K]
    dest_chunks = dest.reshape(T // SC_G, SC_G, TOP_K).transpose(0, 2, 1)
    x_sorted = _sc_dispatch_scatter(hm, dest_chunks, nb * MOE_TM)

    gu_b3 = gu_b.reshape(N_EXPERTS, 1, 2 * D_FF)
    y = down_b.reshape(N_EXPERTS, 1, D_MODEL)
    for c in range(D_FF // MOE_TF):
        y = _expert_pass(c, block_e, nb_used, n_valid, x_sorted, gu_wb, gu_b3, down_wb, y)
    y_g = _sc_combine_gather(y, dest.T.reshape(-1)).reshape(TOP_K, T, PACK_W)
    return x2, y_g, gates


def kernel(x, mem, norm_mix_g, w_in, gla_gate_w, gla_gate_b, gla_norm_g, gla_out_w, conv_w, conv_b, conv_out_w, mix_out_w, norm_xattn_g, norm_mem_g, xq_w, xk_w, xv_w, xo_w, norm_moe_g, router_w, router_b, expert_gu_w, expert_gu_b, expert_down_w, expert_down_b, norm_final_g):
    assert x.shape[0] == 1 and mem.shape[0] == 1 and w_in.shape[0] == 1
    x2, y_g, gates = _layer(
        x[0], mem[0], norm_mix_g[0], w_in[0], gla_gate_w[0], gla_gate_b[0], gla_norm_g[0], gla_out_w[0],
        conv_w[0], conv_b[0], conv_out_w[0], mix_out_w[0], norm_xattn_g[0], norm_mem_g[0], xq_w[0], xk_w[0],
        xv_w[0], xo_w[0], norm_moe_g[0], router_w[0], router_b[0], expert_gu_w[0], expert_gu_b[0],
        expert_down_w[0], expert_down_b[0])
    out = _combine(x2, y_g, gates, _row(norm_final_g))
    return out[None]
```

```python
import functools

import jax
import jax.numpy as jnp
from jax import lax
from jax.experimental import pallas as pl
from jax.experimental.pallas import tpu as pltpu
from jax.experimental.pallas import tpu_sc as plsc

F32 = jnp.float32
BF16 = jnp.bfloat16
I32 = jnp.int32
U32 = jnp.uint32

D_MODEL = 2048
MEM_LEN = 256
GLA_HEADS = 4
GLA_DK = 256
GLA_DV = 512
GLA_GATE_RANK = 16
GLA_GATE_TEMP = 16.0
CONV_WIDTH = 3
XATTN_HEADS = 4
XATTN_HEAD_DIM = 128
XATTN_DIM = XATTN_HEADS * XATTN_HEAD_DIM
N_EXPERTS = 32
TOP_K = 4
D_FF = D_MODEL
SWIGLU_LIMIT = 7.0
SWIGLU_ALPHA = 1.702
NORM_EPS = 1e-5

LANES = 128
SC_CORES, SC_SUBCORES, SC_LANES = 2, 16, 16
PACK_W = D_MODEL // 2
P_COLS = 8 * D_MODEL
VMEM_LIMIT = 56 * 1024 * 1024

IN_TM, IN_TN = 1024, 1024
GLA_CH = 128
GLA_RB = 512
GLA_HG = 2
MG_TM = 256
XA_TM = 512
DP_TM = 2048
MOE_TM = 256
MOE_TF = 1024
CB_TM = 256
SC_G = 32
SC_GG, SC_NBUF = 16, 4


def _params(sem):
    return pltpu.CompilerParams(dimension_semantics=sem, vmem_limit_bytes=VMEM_LIMIT)


def _rms(x, g):
    return x * lax.rsqrt(jnp.mean(x * x, axis=-1, keepdims=True) + NORM_EPS) * g


def _dot(a, b):
    return jnp.dot(a, b, preferred_element_type=F32)


def _dot_nt(a, b):
    return lax.dot_general(a, b, (((1,), (1,)), ((), ())), preferred_element_type=F32)


def _dot_tn(a, b):
    return lax.dot_general(a, b, (((0,), (0,)), ((), ())), preferred_element_type=F32)


def _split2(x):
    hi = x.astype(BF16)
    lo = (x - hi.astype(F32)).astype(BF16)
    return hi, lo


def _split3(x):
    hi = x.astype(BF16)
    r = x - hi.astype(F32)
    mid = r.astype(BF16)
    lo = (r - mid.astype(F32)).astype(BF16)
    return hi, mid, lo


def _sigmoid(x):
    return 1.0 / (1.0 + jnp.exp(-x))


def _pack_pairs(x):
    w = x.shape[1] // 2
    u = lax.bitcast_convert_type(x.astype(BF16).astype(F32), U32)
    return (u[:, :w] >> 16) | u[:, w:]


def _unpack_pairs(u):
    lo = lax.bitcast_convert_type(u << 16, F32)
    hi = lax.bitcast_convert_type(u & jnp.uint32(0xFFFF0000), F32)
    return jnp.concatenate([lo, hi], axis=1)


def _side_cast_spec(w2d, n_steps, step_of):
    rows, cols = w2d.shape
    assert rows % n_steps == 0
    return pl.BlockSpec((rows // n_steps, cols), lambda *ids: (step_of(*ids), 0))


def _inproj_kernel(x_ref, g_ref, wl_ref, wr_ref, wa_ref, cin_ref, p_ref, a_ref, cout_ref, h_scr, *, n_left):
    j = pl.program_id(1)

    @pl.when(j == 0)
    def _():
        hb = _rms(x_ref[...], g_ref[...]).astype(BF16)
        h_scr[...] = hb
        a_ref[...] = _dot(hb, wa_ref[...])

    @pl.when(j < n_left)
    def _():
        p_ref[...] = _dot(h_scr[...], wl_ref[...]).astype(BF16)

    @pl.when(j >= n_left)
    def _():
        p_ref[...] = _dot(h_scr[...], wr_ref[...]).astype(BF16)

    cout_ref[...] = cin_ref[...].astype(BF16)


def _in_proj(x, g, w_left, w_right, w_alow, w_cast):
    T = x.shape[0]
    n_left = w_left.shape[1] // IN_TN
    ni, nj = T // IN_TM, P_COLS // IN_TN
    assert w_left.shape[1] % IN_TN == 0 and w_left.shape[1] + w_right.shape[1] == P_COLS
    cast_spec = _side_cast_spec(w_cast, ni * nj, lambda i, j: i * nj + j)
    return pl.pallas_call(
        functools.partial(_inproj_kernel, n_left=n_left),
        grid=(ni, nj),
        in_specs=[
            pl.BlockSpec((IN_TM, D_MODEL), lambda i, j: (i, 0), pipeline_mode=pl.Buffered(1)),
            pl.BlockSpec((1, D_MODEL), lambda i, j: (0, 0)),
            pl.BlockSpec((D_MODEL, IN_TN), lambda i, j: (0, jnp.minimum(j, n_left - 1))),
            pl.BlockSpec((D_MODEL, IN_TN), lambda i, j: (0, jnp.maximum(j - n_left, 0))),
            pl.BlockSpec((D_MODEL, LANES), lambda i, j: (0, 0)),
            cast_spec,
        ],
        out_specs=[
            pl.BlockSpec((IN_TM, IN_TN), lambda i, j: (i, j)),
            pl.BlockSpec((IN_TM, LANES), lambda i, j: (i, 0)),
            cast_spec,
        ],
        out_shape=[
            jax.ShapeDtypeStruct((T, P_COLS), BF16),
            jax.ShapeDtypeStruct((T, LANES), F32),
            jax.ShapeDtypeStruct(w_cast.shape, BF16),
        ],
        scratch_shapes=[pltpu.VMEM((IN_TM, D_MODEL), BF16)],
        compiler_params=_params(("arbitrary", "arbitrary")),
        name="in_proj",
    )(x, g, w_left, w_right, w_alow, w_cast)


def _gla_level_weights(b, la, row):
    ch, dk = b.shape
    out = []
    s = ch // 2
    while s >= 4:
        bb = b.reshape(ch // (2 * s), 2 * s, dk)
        d = (bb - bb[:, s - 1:s, :]).reshape(ch, dk)
        out.append((s, jnp.exp(-jnp.abs(d))))
        s //= 2
    la_prev = pltpu.roll(la, 1, 0)
    la_next = pltpu.roll(la, ch - 1, 0)
    r4 = row & 3
    d2 = jnp.where(r4 == 0, la_next, jnp.where(r4 == 1, 0.0, jnp.where(r4 == 2, la, la + la_prev)))
    out.append((2, jnp.exp(-jnp.abs(d2))))
    d1 = jnp.where((row & 1) == 1, la, 0.0)
    out.append((1, jnp.exp(-jnp.abs(d1))))
    return out


def _gla_kernel(q_ref, k_ref, v_ref, r_ref, a_ref, gw_ref, gb_ref, ng_ref, cin_ref, o_ref, cout_ref, st_scr):
    ch = GLA_CH
    cout_ref[...] = cin_ref[...].astype(BF16)

    @pl.when(pl.program_id(1) == 0)
    def _():
        st_scr[...] = jnp.zeros_like(st_scr)

    row = lax.broadcasted_iota(jnp.int32, (ch, 1), 0)
    col = lax.broadcasted_iota(jnp.int32, (1, ch), 1)
    tril = (row >= col).astype(BF16)
    levels = []
    s = ch // 2
    while s >= 1:
        levels.append((s, ((row ^ col) < 2 * s) & ((row & (2 * s - 1)) >= s) & ((col & (2 * s - 1)) < s)))
        s //= 2
    gw_hi, gw_lo = _split2(gw_ref[...])
    gate_b = gb_ref[...]
    norm_g = ng_ref[...]
    q_scale = jnp.asarray(GLA_DK ** -0.5, BF16)

    def head(sl, hh, a_hi, a_lo):
        ks = slice(hh * GLA_DK, (hh + 1) * GLA_DK)
        vs = slice(hh * GLA_DV, (hh + 1) * GLA_DV)
        qb = q_ref[sl, ks] * q_scale
        kb = k_ref[sl, ks]
        q, k = qb.astype(F32), kb.astype(F32)
        v = v_ref[sl, vs]
        z = _dot(a_hi, gw_hi[:, ks]) + _dot(a_lo, gw_hi[:, ks]) + _dot(a_hi, gw_lo[:, ks]) + gate_b[:, ks]
        la = (jnp.minimum(z, 0.0) - jnp.log(1.0 + jnp.exp(-jnp.abs(z)))) * (1.0 / GLA_GATE_TEMP)
        l_hi, l_mid, l_lo = _split3(la)
        b = _dot(tril, l_hi) + _dot(tril, l_mid) + _dot(tril, l_lo)

        scores = jnp.where(row == col, _dot_nt(qb, kb), 0.0)
        for (s, mask), (_, w) in zip(levels, _gla_level_weights(b, la, row)):
            scores = jnp.where(mask, _dot_nt((q * w).astype(BF16), (k * w).astype(BF16)), scores)
        o = _dot(scores.astype(BF16), v)

        st = st_scr[hh]
        o = o + _dot_nt((q * jnp.exp(b)).astype(BF16), st.astype(BF16))
        b_last = b[ch - 1:ch, :]
        kd = (k * jnp.exp(b_last - b)).astype(BF16)
        st_scr[hh] = st * jnp.exp(b_last) + _dot_tn(v, kd)

        o = _rms(o, norm_g)
        r = r_ref[sl, vs].astype(F32)
        o_ref[sl, vs] = (o * (r * _sigmoid(r))).astype(BF16)

    def chunk(c, carry):
        sl = pl.ds(pl.multiple_of(c * ch, ch), ch)
        a_hi, a_lo = _split2(a_ref[sl, :])
        for hh in range(GLA_HG):
            head(sl, hh, a_hi, a_lo)
        return carry

    lax.fori_loop(0, GLA_RB // ch, chunk, 0)


def _gla(p, a_low, gate_w_pad, gate_b, norm_g, w_cast):
    T = p.shape[0]
    ni = T // GLA_RB
    ng = GLA_HEADS // GLA_HG
    kw, vw = GLA_HG * GLA_DK, GLA_HG * GLA_DV
    cast_spec = _side_cast_spec(w_cast, ng * ni, lambda h, i: h * ni + i)
    return pl.pallas_call(
        _gla_kernel,
        grid=(ng, ni),
        in_specs=[
            pl.BlockSpec((GLA_RB, kw), lambda h, i: (i, h)),
            pl.BlockSpec((GLA_RB, kw), lambda h, i: (i, ng + h)),
            pl.BlockSpec((GLA_RB, vw), lambda h, i: (i, ng + h)),
            pl.BlockSpec((GLA_RB, vw), lambda h, i: (i, 2 * ng + h)),
            pl.BlockSpec((GLA_RB, LANES), lambda h, i: (i, 0)),
            pl.BlockSpec((LANES, kw), lambda h, i: (0, h)),
            pl.BlockSpec((1, kw), lambda h, i: (0, h)),
            pl.BlockSpec((1, GLA_DV), lambda h, i: (0, 0)),
            cast_spec,
        ],
        out_specs=[pl.BlockSpec((GLA_RB, vw), lambda h, i: (i, h)), cast_spec],
        out_shape=[jax.ShapeDtypeStruct((T, GLA_HEADS * GLA_DV), BF16),
                   jax.ShapeDtypeStruct(w_cast.shape, BF16)],
        scratch_shapes=[pltpu.VMEM((GLA_HG, GLA_DV, GLA_DK), F32)],
        compiler_params=_params(("arbitrary", "arbitrary")),
        name="gla",
    )(p, p, p, p, a_low, gate_w_pad, gate_b, norm_g, w_cast)


def _merge_kernel(og_ref, cb_ref, cc_ref, ch_ref, ccp_ref, chp_ref, cw_ref, cbias_ref,
                  ga_ref, gb_ref, wa_ref, wb_ref, wm_ref, x_ref, o_ref):
    tm = cc_ref.shape[0]
    u = cc_ref[...].astype(F32) * ch_ref[...].astype(F32)
    up = ccp_ref[...].astype(F32) * chp_ref[...].astype(F32)
    up = jnp.where(pl.program_id(0) == 0, 0.0, up)
    hp = up.shape[0]
    row = lax.broadcasted_iota(jnp.int32, (tm, 1), 0)
    u1 = jnp.where(row == 0, up[hp - 1:hp, :], pltpu.roll(u, 1, 0))
    u2 = jnp.where(row == 0, up[hp - 2:hp - 1, :],
                   jnp.where(row == 1, up[hp - 1:hp, :], pltpu.roll(u, 2, 0)))
    cw = cw_ref[...]
    uc = cw[0:1, :] * u2 + cw[1:2, :] * u1 + cw[2:3, :] * u + cbias_ref[...]
    zb = (cb_ref[...].astype(F32) * uc).astype(BF16)

    ya = _dot(og_ref[...], wa_ref[...])
    yb = _dot(zb, wb_ref[...])
    merged = (_sigmoid(ga_ref[...].astype(F32)) * ya + _sigmoid(gb_ref[...].astype(F32)) * yb).astype(BF16)
    o_ref[...] = x_ref[...] + _dot(merged, wm_ref[...])


def _merge(og, p, conv_w8, conv_b, gla_out_w, conv_out_w, mix_out_w, x):
    T = og.shape[0]
    halo = 16
    prev = lambda blk: (lambda i: (jnp.maximum(i * (MG_TM // halo) - 1, 0), blk))
    pblk = lambda blk: pl.BlockSpec((MG_TM, D_MODEL), lambda i: (i, blk))
    const = lambda shape: pl.BlockSpec(shape, lambda i: (0, 0), pipeline_mode=pl.Buffered(1))
    return pl.pallas_call(
        _merge_kernel,
        grid=(T // MG_TM,),
        in_specs=[
            pblk(0),
            pblk(3), pblk(4), pblk(5),
            pl.BlockSpec((halo, D_MODEL), prev(4)),
            pl.BlockSpec((halo, D_MODEL), prev(5)),
            const((8, D_MODEL)),
            const((1, D_MODEL)),
            pblk(6), pblk(7),
            const((D_MODEL, D_MODEL)), const((D_MODEL, D_MODEL)), const((D_MODEL, D_MODEL)),
            pblk(0),
        ],
        out_specs=pblk(0),
        out_shape=jax.ShapeDtypeStruct((T, D_MODEL), F32),
        compiler_params=_params(("arbitrary",)),
        name="merge",
    )(og, p, p, p, p, p, conv_w8, conv_b, p, p, gla_out_w, conv_out_w, mix_out_w, x)


def _memkv_kernel(m_ref, g_ref, wk_ref, wv_ref, k_ref, v_ref):
    mn = _rms(m_ref[...], g_ref[...]).astype(BF16)
    k_ref[...] = _dot(mn, wk_ref[...]).astype(BF16)
    v_ref[...] = _dot(mn, wv_ref[...]).astype(BF16)


def _mem_kv(mem, g, wk, wv):
    return pl.pallas_call(
        _memkv_kernel,
        out_shape=[jax.ShapeDtypeStruct((MEM_LEN, XATTN_DIM), BF16)] * 2,
        compiler_params=pltpu.CompilerParams(vmem_limit_bytes=VMEM_LIMIT),
        name="mem_kv",
    )(mem, g, wk, wv)


def _xattn_tile(x_ref, g_ref, wq_ref, k_ref, v_ref, wo_ref):
    x = x_ref[...]
    hq = _rms(x, g_ref[...]).astype(BF16)
    q = (_dot(hq, wq_ref[...]) * (XATTN_HEAD_DIM ** -0.5)).astype(BF16)
    outs = []
    for h in range(XATTN_HEADS):
        cs = slice(h * XATTN_HEAD_DIM, (h + 1) * XATTN_HEAD_DIM)
        s = _dot_nt(q[:, cs], k_ref[:, cs])
        s = s - jnp.max(s, axis=-1, keepdims=True)
        e = jnp.exp(s)
        p = e / jnp.sum(e, axis=-1, keepdims=True)
        outs.append(_dot(p.astype(BF16), v_ref[:, cs]))
    o = jnp.concatenate(outs, axis=-1).astype(BF16)
    return x + _dot(o, wo_ref[...])


def _xattn_router_kernel(x_ref, gx_ref, wq_ref, k_ref, v_ref, wo_ref, gm_ref, rw_ref, rb_ref,
                         x2_ref, hm_ref, idx_ref, gate_ref, rank_ref, cnt_ref, carry_scr):
    x2 = _xattn_tile(x_ref, gx_ref, wq_ref, k_ref, v_ref, wo_ref)
    x2_ref[...] = x2
    _route_tile(x2, gm_ref, rw_ref, rb_ref, hm_ref, idx_ref, gate_ref, rank_ref, cnt_ref, carry_scr)


def _route_tile(x2, g_ref, w_ref, b_ref, hm_ref, idx_ref, gate_ref, rank_ref, cnt_ref, carry_scr):
    tm = x2.shape[0]

    @pl.when(pl.program_id(0) == 0)
    def _():
        carry_scr[...] = jnp.zeros_like(carry_scr)

    h = _rms(x2, g_ref[...])
    hb = h.astype(BF16)
    hm_ref[...] = _pack_pairs(h)
    lane = lax.broadcasted_iota(jnp.int32, (tm, LANES), 1)
    logits = jnp.where(lane < N_EXPERTS, _dot(hb, w_ref[...]) + b_ref[...], -jnp.inf)

    vals, idxs = [], []
    work = logits
    for _ in range(TOP_K):
        m = jnp.max(work, axis=-1, keepdims=True)
        am = jnp.min(jnp.where(work == m, lane, LANES), axis=-1, keepdims=True)
        vals.append(m)
        idxs.append(am)
        work = jnp.where(lane == am, -jnp.inf, work)
    es = [jnp.exp(v - vals[0]) for v in vals]
    inv = 1.0 / (es[0] + es[1] + es[2] + es[3])

    onehot = (work != logits).astype(BF16)
    r_ = lax.broadcasted_iota(jnp.int32, (tm, tm), 0)
    c_ = lax.broadcasted_iota(jnp.int32, (tm, tm), 1)
    before = _dot((r_ > c_).astype(BF16), onehot) + carry_scr[...]
    carry_scr[...] = carry_scr[...] + jnp.sum(onehot.astype(F32), axis=0, keepdims=True)
    cnt_ref[...] = carry_scr[...]

    idx_out = jnp.zeros((tm, LANES), jnp.int32)
    gate_out = jnp.zeros((tm, LANES), F32)
    rank_out = jnp.zeros((tm, LANES), jnp.int32)
    for kk in range(TOP_K):
        rk = jnp.sum(jnp.where(lane == idxs[kk], before, 0.0), axis=-1, keepdims=True)
        idx_out = jnp.where(lane == kk, idxs[kk], idx_out)
        gate_out = jnp.where(lane == kk, es[kk] * inv, gate_out)
        rank_out = jnp.where(lane == kk, rk.astype(jnp.int32), rank_out)
    idx_ref[...] = idx_out
    gate_ref[...] = gate_out
    rank_ref[...] = rank_out


def _xattn_router(x1, gx, wq, kmem, vmem, wo, gm, rw_pad, rb_pad):
    T = x1.shape[0]
    full = lambda shape: pl.BlockSpec(shape, lambda i: (0, 0))
    row_blk = pl.BlockSpec((XA_TM, LANES), lambda i: (i, 0))
    return pl.pallas_call(
        _xattn_router_kernel,
        grid=(T // XA_TM,),
        in_specs=[
            pl.BlockSpec((XA_TM, D_MODEL), lambda i: (i, 0)),
            full((1, D_MODEL)),
            full((D_MODEL, XATTN_DIM)),
            full((MEM_LEN, XATTN_DIM)),
            full((MEM_LEN, XATTN_DIM)),
            full((XATTN_DIM, D_MODEL)),
            full((1, D_MODEL)),
            full((D_MODEL, LANES)),
            full((1, LANES)),
        ],
        out_specs=[
            pl.BlockSpec((XA_TM, D_MODEL), lambda i: (i, 0)),
            pl.BlockSpec((XA_TM, PACK_W), lambda i: (i, 0)),
            row_blk, row_blk, row_blk,
            full((1, LANES)),
        ],
        out_shape=[
            jax.ShapeDtypeStruct((T, D_MODEL), F32),
            jax.ShapeDtypeStruct((T, PACK_W), U32),
            jax.ShapeDtypeStruct((T, LANES), jnp.int32),
            jax.ShapeDtypeStruct((T, LANES), F32),
            jax.ShapeDtypeStruct((T, LANES), jnp.int32),
            jax.ShapeDtypeStruct((1, LANES), F32),
        ],
        scratch_shapes=[pltpu.VMEM((1, LANES), F32)],
        compiler_params=_params(("arbitrary",)),
        name="xattn_router",
    )(x1, gx, wq, kmem, vmem, wo, gm, rw_pad, rb_pad)


def _dispatch_kernel(idx_ref, rank_ref, pstart_ref, dest_ref):
    tm = idx_ref.shape[0]
    lane = lax.broadcasted_iota(I32, (tm, LANES), 1)
    idx = idx_ref[...]
    pstart = pstart_ref[...]
    dest = rank_ref[...]
    for kk in range(TOP_K):
        e = jnp.sum(jnp.where(lane == kk, idx, 0), axis=-1, keepdims=True)
        start = jnp.sum(jnp.where(lane == e, pstart, 0), axis=-1, keepdims=True)
        dest = dest + jnp.where(lane == kk, start, 0)
    dest_ref[...] = dest


def _dispatch(idx, rank, pstart_row):
    T = idx.shape[0]
    blk = pl.BlockSpec((DP_TM, LANES), lambda i: (i, 0))
    return pl.pallas_call(
        _dispatch_kernel,
        grid=(T // DP_TM,),
        in_specs=[blk, blk, pl.BlockSpec((1, LANES), lambda i: (0, 0))],
        out_specs=blk,
        out_shape=jax.ShapeDtypeStruct((T, LANES), I32),
        compiler_params=_params(("arbitrary",)),
        name="dispatch",
    )(idx, rank, pstart_row)


def _sc_mesh():
    return plsc.VectorSubcoreMesh(core_axis_name="c", subcore_axis_name="s",
                                  num_cores=SC_CORES, num_subcores=SC_SUBCORES)


def _sc_gather_rows(table_hbm, idx_v, out_hbm, out_base, n_steps, rows_v, gsem, wsem):
    nbuf = SC_NBUF

    def gather(s, b):
        return pltpu.make_async_copy(table_hbm.at[idx_v.at[pl.ds(s * SC_GG, SC_GG)]], rows_v.at[b], gsem.at[b])

    def write(s, b):
        return pltpu.make_async_copy(rows_v.at[b], out_hbm.at[pl.ds(out_base + s * SC_GG, SC_GG)], wsem.at[b])

    for j in range(nbuf - 1):
        gather(j, j).start()

    @pl.loop(0, n_steps, step=nbuf)
    def _(s0):
        for b in range(nbuf):
            s = s0 + b
            gather(s, b).wait()
            write(s, b).start()
            refill = (b + nbuf - 1) % nbuf

            @pl.when(s + nbuf - 1 < n_steps)
            def _():
                @pl.when(s >= 1)
                def _():
                    write(s - 1, refill).wait()
                gather(s + nbuf - 1, refill).start()

    for j in range(nbuf):
        write(n_steps - nbuf + j, j).wait()


def _sc_dispatch_scatter(table, dest_chunks, n_slots):
    T, width = table.shape
    n_workers = SC_CORES * SC_SUBCORES
    per_w = T // n_workers
    n_steps = per_w // SC_G
    assert T % n_workers == 0 and per_w % (2 * SC_G) == 0
    assert dest_chunks.shape == (T // SC_G, TOP_K, SC_G)

    @functools.partial(
        pl.kernel, mesh=_sc_mesh(),
        out_type=jax.ShapeDtypeStruct((n_slots, width), table.dtype),
        scratch_types=[
            pltpu.VMEM((n_steps, TOP_K, SC_G), I32),
            pltpu.VMEM((2, SC_G, width), table.dtype),
            pltpu.SemaphoreType.DMA((2,)),
            pltpu.SemaphoreType.DMA((2,)),
        ],
        name="sc_dispatch_scatter",
    )
    def k(table_hbm, dest_hbm, out_hbm, dst_v, rows_v, rsem, wsem):
        wid = lax.axis_index("s") * SC_CORES + lax.axis_index("c")
        pltpu.sync_copy(dest_hbm.at[pl.ds(wid * n_steps, n_steps)], dst_v)

        def read(s, b):
            return pltpu.make_async_copy(table_hbm.at[pl.ds(wid * per_w + s * SC_G, SC_G)], rows_v.at[b], rsem.at[b])

        def scatter(s, kk, b):
            return pltpu.make_async_copy(rows_v.at[b], out_hbm.at[dst_v.at[s, kk]], wsem.at[b])

        read(0, 0).start()

        @pl.loop(0, n_steps, step=2)
        def _(s0):
            for b in range(2):
                s = s0 + b
                read(s, b).wait()

                @pl.when(s + 1 < n_steps)
                def _():
                    @pl.when(s >= 1)
                    def _():
                        for kk in range(TOP_K):
                            scatter(s - 1, kk, 1 - b).wait()
                    read(s + 1, 1 - b).start()

                for kk in range(TOP_K):
                    scatter(s, kk, b).start()

        for kk in range(TOP_K):
            scatter(n_steps - 2, kk, 0).wait()
            scatter(n_steps - 1, kk, 1).wait()

    return k(table, dest_chunks)


def _sc_combine_gather(table, dest_flat):
    n_rows, width = dest_flat.shape[0], table.shape[1]
    n_workers = SC_CORES * SC_SUBCORES
    per_w = n_rows // n_workers
    n_steps = per_w // SC_GG
    assert n_rows % n_workers == 0 and per_w % (SC_NBUF * SC_GG) == 0

    @functools.partial(
        pl.kernel, mesh=_sc_mesh(),
        out_type=jax.ShapeDtypeStruct((n_rows, width), table.dtype),
        scratch_types=[
            pltpu.VMEM((per_w,), I32),
            pltpu.VMEM((SC_NBUF, SC_GG, width), table.dtype),
            pltpu.SemaphoreType.DMA((SC_NBUF,)),
            pltpu.SemaphoreType.DMA((SC_NBUF,)),
        ],
        name="sc_combine_gather",
    )
    def k(table_hbm, dest_hbm, out_hbm, dst_v, rows_v, gsem, wsem):
        wid = lax.axis_index("s") * SC_CORES + lax.axis_index("c")
        lo = wid * per_w
        pltpu.sync_copy(dest_hbm.at[pl.ds(lo, per_w)], dst_v)
        _sc_gather_rows(table_hbm, dst_v, out_hbm, lo, n_steps, rows_v, gsem, wsem)

    return k(table, dest_flat)


def _expert_kernel(be_ref, nb_ref, nv_ref, x_ref, wg_ref, wu_ref, bg_ref, bu_ref, wd_ref, prev_ref, y_ref, *, last):
    i = pl.program_id(0)

    @pl.when(i < nb_ref[0])
    def _():
        row = lax.broadcasted_iota(I32, (x_ref.shape[0], 1), 0)
        x = _unpack_pairs(jnp.where(row < nv_ref[i], x_ref[...], jnp.uint32(0))).astype(BF16)
        gate = jnp.minimum(_dot(x, wg_ref[...]) + bg_ref[...], SWIGLU_LIMIT)
        up = jnp.clip(_dot(x, wu_ref[...]) + bu_ref[...], -SWIGLU_LIMIT, SWIGLU_LIMIT)
        act = (up + 1.0) * gate * _sigmoid(SWIGLU_ALPHA * gate)
        y = prev_ref[...] + _dot(act.astype(BF16), wd_ref[...])
        y_ref[...] = _pack_pairs(y) if last else y


def _expert_pass(c, block_e, nb_used, n_valid, x_sorted, gu_w, gu_b, down_w, prev):
    P = x_sorted.shape[0]
    nc = D_FF // MOE_TF
    first, last = c == 0, c == nc - 1
    rowblk = lambda i, be, nbu, nv: (jnp.minimum(i, nbu[0] - 1), 0)
    prev_spec = (pl.BlockSpec((None, 1, D_MODEL), lambda i, be, nbu, nv: (be[i], 0, 0)) if first
                 else pl.BlockSpec((MOE_TM, D_MODEL), rowblk))
    grid_spec = pltpu.PrefetchScalarGridSpec(
        num_scalar_prefetch=3,
        grid=(P // MOE_TM,),
        in_specs=[
            pl.BlockSpec((MOE_TM, PACK_W), rowblk),
            pl.BlockSpec((None, D_MODEL, MOE_TF), lambda i, be, nbu, nv: (be[i], 0, c)),
            pl.BlockSpec((None, D_MODEL, MOE_TF), lambda i, be, nbu, nv: (be[i], 0, nc + c)),
            pl.BlockSpec((None, 1, MOE_TF), lambda i, be, nbu, nv: (be[i], 0, c)),
            pl.BlockSpec((None, 1, MOE_TF), lambda i, be, nbu, nv: (be[i], 0, nc + c)),
            pl.BlockSpec((None, MOE_TF, D_MODEL), lambda i, be, nbu, nv: (be[i], c, 0)),
            prev_spec,
        ],
        out_specs=pl.BlockSpec((MOE_TM, PACK_W if last else D_MODEL), rowblk),
    )
    return pl.pallas_call(
        functools.partial(_expert_kernel, last=last),
        grid_spec=grid_spec,
        out_shape=jax.ShapeDtypeStruct((P, PACK_W), U32) if last else jax.ShapeDtypeStruct((P, D_MODEL), F32),
        compiler_params=_params(("arbitrary",)),
        name="experts_%d" % c,
    )(block_e, nb_used, n_valid, x_sorted, gu_w, gu_w, gu_b, gu_b, down_w, prev)


def _combine_kernel(x_ref, y_ref, gate_ref, g_ref, o_ref):
    acc = x_ref[...]
    gates = gate_ref[...]
    for kk in range(TOP_K):
        acc = acc + gates[:, kk:kk + 1] * _unpack_pairs(y_ref[kk])
    o_ref[...] = _rms(acc, g_ref[...])


def _combine(x2, y_g, gates, g):
    T = x2.shape[0]
    return pl.pallas_call(
        _combine_kernel,
        grid=(T // CB_TM,),
        in_specs=[
            pl.BlockSpec((CB_TM, D_MODEL), lambda i: (i, 0)),
            pl.BlockSpec((TOP_K, CB_TM, PACK_W), lambda i: (0, i, 0)),
            pl.BlockSpec((CB_TM, LANES), lambda i: (i, 0)),
            pl.BlockSpec((1, D_MODEL), lambda i: (0, 0)),
        ],
        out_specs=pl.BlockSpec((CB_TM, D_MODEL), lambda i: (i, 0)),
        out_shape=jax.ShapeDtypeStruct((T, D_MODEL), F32),
        compiler_params=_params(("arbitrary",)),
        name="combine",
    )(x2, y_g, gates, g)


def _row(v):
    return v.reshape(1, -1).astype(F32)


def _layer(x, mem, norm_mix_g, w_in, gla_gate_w, gla_gate_b, gla_norm_g, gla_out_w, conv_w, conv_b,
           conv_out_w, mix_out_w, norm_xattn_g, norm_mem_g, xq_w, xk_w, xv_w, xo_w, norm_moe_g,
           router_w, router_b, gu_w, gu_b, down_w, down_b):
    T = x.shape[0]
    a0 = 3 * D_MODEL
    w_left = w_in[:, :a0].astype(BF16)
    w_right = w_in[:, a0 + GLA_GATE_RANK:].astype(BF16)
    w_alow = jnp.pad(w_in[:, a0:a0 + GLA_GATE_RANK], ((0, 0), (0, LANES - GLA_GATE_RANK))).astype(BF16)
    gate_w_pad = jnp.pad(gla_gate_w, ((0, LANES - GLA_GATE_RANK), (0, 0)))
    conv_w8 = jnp.pad(conv_w, ((0, 8 - CONV_WIDTH), (0, 0)))

    p, a_low, gu_wb = _in_proj(x, _row(norm_mix_g), w_left, w_right, w_alow, gu_w.reshape(-1, 2 * D_FF))
    og, down_wb = _gla(p, a_low, gate_w_pad, _row(gla_gate_b), _row(gla_norm_g), down_w.reshape(-1, D_MODEL))
    gu_wb = gu_wb.reshape(gu_w.shape)
    down_wb = down_wb.reshape(down_w.shape)
    x1 = _merge(og, p, conv_w8, _row(conv_b), gla_out_w.astype(BF16), conv_out_w.astype(BF16),
                mix_out_w.astype(BF16), x)

    kmem, vmem = _mem_kv(mem, _row(norm_mem_g), xk_w.astype(BF16), xv_w.astype(BF16))
    rw = jnp.pad(router_w, ((0, 0), (0, LANES - N_EXPERTS))).astype(BF16)
    rb = jnp.pad(router_b, (0, LANES - N_EXPERTS)).reshape(1, LANES)
    x2, hm, idx, gates, rank, counts = _xattn_router(
        x1, _row(norm_xattn_g), xq_w.astype(BF16), kmem, vmem, xo_w.astype(BF16), _row(norm_moe_g), rw, rb)

    cnt = counts[0, :N_EXPERTS].astype(I32)
    padded = (cnt + MOE_TM - 1) // MOE_TM * MOE_TM
    pend = jnp.cumsum(padded)
    pstart = pend - padded
    pstart_row = jnp.pad(pstart, (0, LANES - N_EXPERTS)).reshape(1, LANES)
    nb = (T * TOP_K) // MOE_TM + N_EXPERTS
    blk_start = jnp.arange(nb, dtype=I32) * MOE_TM
    block_e = jnp.minimum(jnp.sum(pend[None, :] <= blk_start[:, None], axis=1), N_EXPERTS - 1).astype(I32)
    nb_used = (pend[-1] // MOE_TM).astype(I32).reshape(1)
    n_valid = jnp.clip(pstart[block_e] + cnt[block_e] - blk_start, 0, MOE_TM).astype(I32)

    dest = _dispatch(idx, rank, pstart_row)[:, :TOP_K]
    dest_chunks = dest.reshape(T // SC_G, SC_G, TOP_K).transpose(0, 2, 1)
    x_sorted = _sc_dispatch_scatter(hm, dest_chunks, nb * MOE_TM)

    gu_b3 = gu_b.reshape(N_EXPERTS, 1, 2 * D_FF)
    y = down_b.reshape(N_EXPERTS, 1, D_MODEL)
    for c in range(D_FF // MOE_TF):
        y = _expert_pass(c, block_e, nb_used, n_valid, x_sorted, gu_wb, gu_b3, down_wb, y)
    y_g = _sc_combine_gather(y, dest.T.reshape(-1)).reshape(TOP_K, T, PACK_W)
    return x2, y_g, gates


def kernel(x, mem, norm_mix_g, w_in, gla_gate_w, gla_gate_b, gla_norm_g, gla_out_w, conv_w, conv_b, conv_out_w, mix_out_w, norm_xattn_g, norm_mem_g, xq_w, xk_w, xv_w, xo_w, norm_moe_g, router_w, router_b, expert_gu_w, expert_gu_b, expert_down_w, expert_down_b, norm_final_g):
    assert x.shape[0] == 1 and mem.shape[0] == 1 and w_in.shape[0] == 1
    x2, y_g, gates = _layer(
        x[0], mem[0], norm_mix_g[0], w_in[0], gla_gate_w[0], gla_gate_b[0], gla_norm_g[0], gla_out_w[0],
        conv_w[0], conv_b[0], conv_out_w[0], mix_out_w[0], norm_xattn_g[0], norm_mem_g[0], xq_w[0], xk_w[0],
        xv_w[0], xo_w[0], norm_moe_g[0], router_w[0], router_b[0], expert_gu_w[0], expert_gu_b[0],
        expert_down_w[0], expert_down_b[0])
    out = _combine(x2, y_g, gates, _row(norm_final_g))
    return out[None]
```

```python
import functools

import jax
import jax.numpy as jnp
from jax import lax
from jax.experimental import pallas as pl
from jax.experimental.pallas import tpu as pltpu
from jax.experimental.pallas import tpu_sc as plsc

F32 = jnp.float32
BF16 = jnp.bfloat16
I32 = jnp.int32
U32 = jnp.uint32

D_MODEL = 2048
MEM_LEN = 256
GLA_HEADS = 4
GLA_DK = 256
GLA_DV = 512
GLA_GATE_RANK = 16
GLA_GATE_TEMP = 16.0
CONV_WIDTH = 3
XATTN_HEADS = 4
XATTN_HEAD_DIM = 128
XATTN_DIM = XATTN_HEADS * XATTN_HEAD_DIM
N_EXPERTS = 32
TOP_K = 4
D_FF = D_MODEL
SWIGLU_LIMIT = 7.0
SWIGLU_ALPHA = 1.702
NORM_EPS = 1e-5

LANES = 128
SC_CORES, SC_SUBCORES, SC_LANES = 2, 16, 16
PACK_W = D_MODEL // 2
P_COLS = 8 * D_MODEL
VMEM_LIMIT = 56 * 1024 * 1024

WP_ROWS = 128
IN_TM, IN_TN = 1024, 1024
GLA_CH = 128
GLA_RB = 512
GLA_HG = 2
MG_TM = 256
XA_TM = 512
DP_TM = 2048
MOE_TM = 256
MOE_TF = 1024
CB_TM = 256
SC_G = 32
SC_GG, SC_NBUF = 16, 4


def _params(sem):
    return pltpu.CompilerParams(dimension_semantics=sem, vmem_limit_bytes=VMEM_LIMIT)


def _rms(x, g):
    return x * lax.rsqrt(jnp.mean(x * x, axis=-1, keepdims=True) + NORM_EPS) * g


def _dot(a, b):
    return jnp.dot(a, b, preferred_element_type=F32)


def _dot_nt(a, b):
    return lax.dot_general(a, b, (((1,), (1,)), ((), ())), preferred_element_type=F32)


def _dot_tn(a, b):
    return lax.dot_general(a, b, (((0,), (0,)), ((), ())), preferred_element_type=F32)


def _split2(x):
    hi = x.astype(BF16)
    lo = (x - hi.astype(F32)).astype(BF16)
    return hi, lo


def _split3(x):
    hi = x.astype(BF16)
    r = x - hi.astype(F32)
    mid = r.astype(BF16)
    lo = (r - mid.astype(F32)).astype(BF16)
    return hi, mid, lo


def _sigmoid(x):
    return 1.0 / (1.0 + jnp.exp(-x))


def _pack_pairs(x):
    w = x.shape[1] // 2
    u = lax.bitcast_convert_type(x.astype(BF16).astype(F32), U32)
    return (u[:, :w] >> 16) | u[:, w:]


def _unpack_pairs(u):
    lo = lax.bitcast_convert_type(u << 16, F32)
    hi = lax.bitcast_convert_type(u & jnp.uint32(0xFFFF0000), F32)
    return jnp.concatenate([lo, hi], axis=1)


def _side_cast_spec(w2d, n_steps, step_of):
    rows, cols = w2d.shape
    assert rows % n_steps == 0
    return pl.BlockSpec((rows // n_steps, cols), lambda *ids: (step_of(*ids), 0))


def _winprep_kernel(w_ref, main_ref, alow_ref):
    a0 = 3 * D_MODEL
    a1 = a0 + GLA_GATE_RANK
    main_ref[:, :a0] = w_ref[:, :a0].astype(BF16)
    main_ref[:, a0:] = w_ref[:, a1:].astype(BF16)
    lane = lax.broadcasted_iota(I32, (w_ref.shape[0], LANES), 1)
    alow_ref[...] = jnp.where(lane < GLA_GATE_RANK, w_ref[:, a0:a0 + LANES], 0.0).astype(BF16)


def _w_in_prep(w_in):
    rows, cols = w_in.shape
    assert cols == P_COLS + GLA_GATE_RANK
    return pl.pallas_call(
        _winprep_kernel,
        grid=(rows // WP_ROWS,),
        in_specs=[pl.BlockSpec((WP_ROWS, cols), lambda i: (i, 0))],
        out_specs=[pl.BlockSpec((WP_ROWS, P_COLS), lambda i: (i, 0)),
                   pl.BlockSpec((WP_ROWS, LANES), lambda i: (i, 0))],
        out_shape=[jax.ShapeDtypeStruct((rows, P_COLS), BF16), jax.ShapeDtypeStruct((rows, LANES), BF16)],
        compiler_params=_params(("arbitrary",)),
        name="w_in_prep",
    )(w_in)


def _inproj_kernel(x_ref, g_ref, w_ref, wa_ref, cin_ref, p_ref, a_ref, cout_ref, h_scr):
    @pl.when(pl.program_id(1) == 0)
    def _():
        hb = _rms(x_ref[...], g_ref[...]).astype(BF16)
        h_scr[...] = hb
        a_ref[...] = _dot(hb, wa_ref[...])

    p_ref[...] = _dot(h_scr[...], w_ref[...]).astype(BF16)
    cout_ref[...] = cin_ref[...].astype(BF16)


def _in_proj(x, g, w_main, w_alow, w_cast):
    T = x.shape[0]
    ni, nj = T // IN_TM, P_COLS // IN_TN
    cast_spec = _side_cast_spec(w_cast, ni * nj, lambda i, j: i * nj + j)
    return pl.pallas_call(
        _inproj_kernel,
        grid=(ni, nj),
        in_specs=[
            pl.BlockSpec((IN_TM, D_MODEL), lambda i, j: (i, 0)),
            pl.BlockSpec((1, D_MODEL), lambda i, j: (0, 0)),
            pl.BlockSpec((D_MODEL, IN_TN), lambda i, j: (0, j)),
            pl.BlockSpec((D_MODEL, LANES), lambda i, j: (0, 0)),
            cast_spec,
        ],
        out_specs=[
            pl.BlockSpec((IN_TM, IN_TN), lambda i, j: (i, j)),
            pl.BlockSpec((IN_TM, LANES), lambda i, j: (i, 0)),
            cast_spec,
        ],
        out_shape=[
            jax.ShapeDtypeStruct((T, P_COLS), BF16),
            jax.ShapeDtypeStruct((T, LANES), F32),
            jax.ShapeDtypeStruct(w_cast.shape, BF16),
        ],
        scratch_shapes=[pltpu.VMEM((IN_TM, D_MODEL), BF16)],
        compiler_params=_params(("arbitrary", "arbitrary")),
        name="in_proj",
    )(x, g, w_main, w_alow, w_cast)


def _gla_level_weights(b, la, row):
    ch, dk = b.shape
    out = []
    s = ch // 2
    while s >= 4:
        bb = b.reshape(ch // (2 * s), 2 * s, dk)
        d = (bb - bb[:, s - 1:s, :]).reshape(ch, dk)
        out.append((s, jnp.exp(-jnp.abs(d))))
        s //= 2
    la_prev = pltpu.roll(la, 1, 0)
    la_next = pltpu.roll(la, ch - 1, 0)
    r4 = row & 3
    d2 = jnp.where(r4 == 0, la_next, jnp.where(r4 == 1, 0.0, jnp.where(r4 == 2, la, la + la_prev)))
    out.append((2, jnp.exp(-jnp.abs(d2))))
    d1 = jnp.where((row & 1) == 1, la, 0.0)
    out.append((1, jnp.exp(-jnp.abs(d1))))
    return out


def _gla_kernel(q_ref, k_ref, v_ref, r_ref, a_ref, gw_ref, gb_ref, ng_ref, cin_ref, o_ref, cout_ref, st_scr):
    ch = GLA_CH
    cout_ref[...] = cin_ref[...].astype(BF16)

    @pl.when(pl.program_id(1) == 0)
    def _():
        st_scr[...] = jnp.zeros_like(st_scr)

    row = lax.broadcasted_iota(jnp.int32, (ch, 1), 0)
    col = lax.broadcasted_iota(jnp.int32, (1, ch), 1)
    tril = (row >= col).astype(BF16)
    levels = []
    s = ch // 2
    while s >= 1:
        levels.append((s, ((row ^ col) < 2 * s) & ((row & (2 * s - 1)) >= s) & ((col & (2 * s - 1)) < s)))
        s //= 2
    gw_hi, gw_lo = _split2(gw_ref[...])
    gate_b = gb_ref[...]
    norm_g = ng_ref[...]
    q_scale = jnp.asarray(GLA_DK ** -0.5, BF16)

    def head(sl, hh, a_hi, a_lo):
        ks = slice(hh * GLA_DK, (hh + 1) * GLA_DK)
        vs = slice(hh * GLA_DV, (hh + 1) * GLA_DV)
        qb = q_ref[sl, ks] * q_scale
        kb = k_ref[sl, ks]
        q, k = qb.astype(F32), kb.astype(F32)
        v = v_ref[sl, vs]
        z = _dot(a_hi, gw_hi[:, ks]) + _dot(a_lo, gw_hi[:, ks]) + _dot(a_hi, gw_lo[:, ks]) + gate_b[:, ks]
        la = (jnp.minimum(z, 0.0) - jnp.log(1.0 + jnp.exp(-jnp.abs(z)))) * (1.0 / GLA_GATE_TEMP)
        l_hi, l_mid, l_lo = _split3(la)
        b = _dot(tril, l_hi) + _dot(tril, l_mid) + _dot(tril, l_lo)

        scores = jnp.where(row == col, _dot_nt(qb, kb), 0.0)
        for (s, mask), (_, w) in zip(levels, _gla_level_weights(b, la, row)):
            scores = jnp.where(mask, _dot_nt((q * w).astype(BF16), (k * w).astype(BF16)), scores)
        o = _dot(scores.astype(BF16), v)

        st = st_scr[hh]
        o = o + _dot_nt((q * jnp.exp(b)).astype(BF16), st.astype(BF16))
        b_last = b[ch - 1:ch, :]
        kd = (k * jnp.exp(b_last - b)).astype(BF16)
        st_scr[hh] = st * jnp.exp(b_last) + _dot_tn(v, kd)

        o = _rms(o, norm_g)
        r = r_ref[sl, vs].astype(F32)
        o_ref[sl, vs] = (o * (r * _sigmoid(r))).astype(BF16)

    def chunk(c, carry):
        sl = pl.ds(pl.multiple_of(c * ch, ch), ch)
        a_hi, a_lo = _split2(a_ref[sl, :])
        for hh in range(GLA_HG):
            head(sl, hh, a_hi, a_lo)
        return carry

    lax.fori_loop(0, GLA_RB // ch, chunk, 0)


def _gla(p, a_low, gate_w_pad, gate_b, norm_g, w_cast):
    T = p.shape[0]
    ni = T // GLA_RB
    ng = GLA_HEADS // GLA_HG
    kw, vw = GLA_HG * GLA_DK, GLA_HG * GLA_DV
    cast_spec = _side_cast_spec(w_cast, ng * ni, lambda h, i: h * ni + i)
    return pl.pallas_call(
        _gla_kernel,
        grid=(ng, ni),
        in_specs=[
            pl.BlockSpec((GLA_RB, kw), lambda h, i: (i, h)),
            pl.BlockSpec((GLA_RB, kw), lambda h, i: (i, ng + h)),
            pl.BlockSpec((GLA_RB, vw), lambda h, i: (i, ng + h)),
            pl.BlockSpec((GLA_RB, vw), lambda h, i: (i, 2 * ng + h)),
            pl.BlockSpec((GLA_RB, LANES), lambda h, i: (i, 0)),
            pl.BlockSpec((LANES, kw), lambda h, i: (0, h)),
            pl.BlockSpec((1, kw), lambda h, i: (0, h)),
            pl.BlockSpec((1, GLA_DV), lambda h, i: (0, 0)),
            cast_spec,
        ],
        out_specs=[pl.BlockSpec((GLA_RB, vw), lambda h, i: (i, h)), cast_spec],
        out_shape=[jax.ShapeDtypeStruct((T, GLA_HEADS * GLA_DV), BF16),
                   jax.ShapeDtypeStruct(w_cast.shape, BF16)],
        scratch_shapes=[pltpu.VMEM((GLA_HG, GLA_DV, GLA_DK), F32)],
        compiler_params=_params(("arbitrary", "arbitrary")),
        name="gla",
    )(p, p, p, p, a_low, gate_w_pad, gate_b, norm_g, w_cast)


def _merge_kernel(og_ref, cb_ref, cc_ref, ch_ref, ccp_ref, chp_ref, cw_ref, cbias_ref,
                  ga_ref, gb_ref, wa_ref, wb_ref, wm_ref, x_ref, o_ref):
    tm = cc_ref.shape[0]
    u = cc_ref[...].astype(F32) * ch_ref[...].astype(F32)
    up = ccp_ref[...].astype(F32) * chp_ref[...].astype(F32)
    up = jnp.where(pl.program_id(0) == 0, 0.0, up)
    hp = up.shape[0]
    row = lax.broadcasted_iota(jnp.int32, (tm, 1), 0)
    u1 = jnp.where(row == 0, up[hp - 1:hp, :], pltpu.roll(u, 1, 0))
    u2 = jnp.where(row == 0, up[hp - 2:hp - 1, :],
                   jnp.where(row == 1, up[hp - 1:hp, :], pltpu.roll(u, 2, 0)))
    cw = cw_ref[...]
    uc = cw[0:1, :] * u2 + cw[1:2, :] * u1 + cw[2:3, :] * u + cbias_ref[...]
    zb = (cb_ref[...].astype(F32) * uc).astype(BF16)

    ya = _dot(og_ref[...], wa_ref[...])
    yb = _dot(zb, wb_ref[...])
    merged = (_sigmoid(ga_ref[...].astype(F32)) * ya + _sigmoid(gb_ref[...].astype(F32)) * yb).astype(BF16)
    o_ref[...] = x_ref[...] + _dot(merged, wm_ref[...])


def _merge(og, p, conv_w8, conv_b, gla_out_w, conv_out_w, mix_out_w, x):
    T = og.shape[0]
    halo = 16
    prev = lambda blk: (lambda i: (jnp.maximum(i * (MG_TM // halo) - 1, 0), blk))
    pblk = lambda blk: pl.BlockSpec((MG_TM, D_MODEL), lambda i: (i, blk))
    const = lambda shape: pl.BlockSpec(shape, lambda i: (0, 0), pipeline_mode=pl.Buffered(1))
    return pl.pallas_call(
        _merge_kernel,
        grid=(T // MG_TM,),
        in_specs=[
            pblk(0),
            pblk(3), pblk(4), pblk(5),
            pl.BlockSpec((halo, D_MODEL), prev(4)),
            pl.BlockSpec((halo, D_MODEL), prev(5)),
            const((8, D_MODEL)),
            const((1, D_MODEL)),
            pblk(6), pblk(7),
            const((D_MODEL, D_MODEL)), const((D_MODEL, D_MODEL)), const((D_MODEL, D_MODEL)),
            pblk(0),
        ],
        out_specs=pblk(0),
        out_shape=jax.ShapeDtypeStruct((T, D_MODEL), F32),
        compiler_params=_params(("arbitrary",)),
        name="merge",
    )(og, p, p, p, p, p, conv_w8, conv_b, p, p, gla_out_w, conv_out_w, mix_out_w, x)


def _memkv_kernel(m_ref, g_ref, wk_ref, wv_ref, k_ref, v_ref):
    mn = _rms(m_ref[...], g_ref[...]).astype(BF16)
    k_ref[...] = _dot(mn, wk_ref[...]).astype(BF16)
    v_ref[...] = _dot(mn, wv_ref[...]).astype(BF16)


def _mem_kv(mem, g, wk, wv):
    return pl.pallas_call(
        _memkv_kernel,
        out_shape=[jax.ShapeDtypeStruct((MEM_LEN, XATTN_DIM), BF16)] * 2,
        compiler_params=pltpu.CompilerParams(vmem_limit_bytes=VMEM_LIMIT),
        name="mem_kv",
    )(mem, g, wk, wv)


def _xattn_tile(x_ref, g_ref, wq_ref, k_ref, v_ref, wo_ref):
    x = x_ref[...]
    hq = _rms(x, g_ref[...]).astype(BF16)
    q = (_dot(hq, wq_ref[...]) * (XATTN_HEAD_DIM ** -0.5)).astype(BF16)
    outs = []
    for h in range(XATTN_HEADS):
        cs = slice(h * XATTN_HEAD_DIM, (h + 1) * XATTN_HEAD_DIM)
        s = _dot_nt(q[:, cs], k_ref[:, cs])
        s = s - jnp.max(s, axis=-1, keepdims=True)
        e = jnp.exp(s)
        p = e / jnp.sum(e, axis=-1, keepdims=True)
        outs.append(_dot(p.astype(BF16), v_ref[:, cs]))
    o = jnp.concatenate(outs, axis=-1).astype(BF16)
    return x + _dot(o, wo_ref[...])


def _xattn_router_kernel(x_ref, gx_ref, wq_ref, k_ref, v_ref, wo_ref, gm_ref, rw_ref, rb_ref,
                         x2_ref, hm_ref, idx_ref, gate_ref, rank_ref, cnt_ref, carry_scr):
    x2 = _xattn_tile(x_ref, gx_ref, wq_ref, k_ref, v_ref, wo_ref)
    x2_ref[...] = x2
    _route_tile(x2, gm_ref, rw_ref, rb_ref, hm_ref, idx_ref, gate_ref, rank_ref, cnt_ref, carry_scr)


def _route_tile(x2, g_ref, w_ref, b_ref, hm_ref, idx_ref, gate_ref, rank_ref, cnt_ref, carry_scr):
    tm = x2.shape[0]

    @pl.when(pl.program_id(0) == 0)
    def _():
        carry_scr[...] = jnp.zeros_like(carry_scr)

    h = _rms(x2, g_ref[...])
    hb = h.astype(BF16)
    hm_ref[...] = _pack_pairs(h)
    lane = lax.broadcasted_iota(jnp.int32, (tm, LANES), 1)
    logits = jnp.where(lane < N_EXPERTS, _dot(hb, w_ref[...]) + b_ref[...], -jnp.inf)

    vals, idxs = [], []
    work = logits
    for _ in range(TOP_K):
        m = jnp.max(work, axis=-1, keepdims=True)
        am = jnp.min(jnp.where(work == m, lane, LANES), axis=-1, keepdims=True)
        vals.append(m)
        idxs.append(am)
        work = jnp.where(lane == am, -jnp.inf, work)
    es = [jnp.exp(v - vals[0]) for v in vals]
    inv = 1.0 / (es[0] + es[1] + es[2] + es[3])

    onehot = (work != logits).astype(BF16)
    r_ = lax.broadcasted_iota(jnp.int32, (tm, tm), 0)
    c_ = lax.broadcasted_iota(jnp.int32, (tm, tm), 1)
    before = _dot((r_ > c_).astype(BF16), onehot) + carry_scr[...]
    carry_scr[...] = carry_scr[...] + jnp.sum(onehot.astype(F32), axis=0, keepdims=True)
    cnt_ref[...] = carry_scr[...]

    idx_out = jnp.zeros((tm, LANES), jnp.int32)
    gate_out = jnp.zeros((tm, LANES), F32)
    rank_out = jnp.zeros((tm, LANES), jnp.int32)
    for kk in range(TOP_K):
        rk = jnp.sum(jnp.where(lane == idxs[kk], before, 0.0), axis=-1, keepdims=True)
        idx_out = jnp.where(lane == kk, idxs[kk], idx_out)
        gate_out = jnp.where(lane == kk, es[kk] * inv, gate_out)
        rank_out = jnp.where(lane == kk, rk.astype(jnp.int32), rank_out)
    idx_ref[...] = idx_out
    gate_ref[...] = gate_out
    rank_ref[...] = rank_out


def _xattn_router(x1, gx, wq, kmem, vmem, wo, gm, rw_pad, rb_pad):
    T = x1.shape[0]
    full = lambda shape: pl.BlockSpec(shape, lambda i: (0, 0))
    row_blk = pl.BlockSpec((XA_TM, LANES), lambda i: (i, 0))
    return pl.pallas_call(
        _xattn_router_kernel,
        grid=(T // XA_TM,),
        in_specs=[
            pl.BlockSpec((XA_TM, D_MODEL), lambda i: (i, 0)),
            full((1, D_MODEL)),
            full((D_MODEL, XATTN_DIM)),
            full((MEM_LEN, XATTN_DIM)),
            full((MEM_LEN, XATTN_DIM)),
            full((XATTN_DIM, D_MODEL)),
            full((1, D_MODEL)),
            full((D_MODEL, LANES)),
            full((1, LANES)),
        ],
        out_specs=[
            pl.BlockSpec((XA_TM, D_MODEL), lambda i: (i, 0)),
            pl.BlockSpec((XA_TM, PACK_W), lambda i: (i, 0)),
            row_blk, row_blk, row_blk,
            full((1, LANES)),
        ],
        out_shape=[
            jax.ShapeDtypeStruct((T, D_MODEL), F32),
            jax.ShapeDtypeStruct((T, PACK_W), U32),
            jax.ShapeDtypeStruct((T, LANES), jnp.int32),
            jax.ShapeDtypeStruct((T, LANES), F32),
            jax.ShapeDtypeStruct((T, LANES), jnp.int32),
            jax.ShapeDtypeStruct((1, LANES), F32),
        ],
        scratch_shapes=[pltpu.VMEM((1, LANES), F32)],
        compiler_params=_params(("arbitrary",)),
        name="xattn_router",
    )(x1, gx, wq, kmem, vmem, wo, gm, rw_pad, rb_pad)


def _dispatch_kernel(idx_ref, rank_ref, pstart_ref, dest_ref):
    tm = idx_ref.shape[0]
    lane = lax.broadcasted_iota(I32, (tm, LANES), 1)
    idx = idx_ref[...]
    pstart = pstart_ref[...]
    dest = rank_ref[...]
    for kk in range(TOP_K):
        start = jnp.sum(jnp.where(lane == idx[:, kk:kk + 1], pstart, 0.0), axis=-1, keepdims=True)
        dest = dest + jnp.where(lane == kk, start.astype(I32), 0)
    dest_ref[...] = dest


def _dispatch(idx, rank, pstart_row):
    T = idx.shape[0]
    blk = pl.BlockSpec((DP_TM, LANES), lambda i: (i, 0))
    return pl.pallas_call(
        _dispatch_kernel,
        grid=(T // DP_TM,),
        in_specs=[blk, blk, pl.BlockSpec((1, LANES), lambda i: (0, 0))],
        out_specs=blk,
        out_shape=jax.ShapeDtypeStruct((T, LANES), I32),
        compiler_params=_params(("arbitrary",)),
        name="dispatch",
    )(idx, rank, pstart_row)


def _sc_mesh():
    return plsc.VectorSubcoreMesh(core_axis_name="c", subcore_axis_name="s",
                                  num_cores=SC_CORES, num_subcores=SC_SUBCORES)


def _sc_gather_rows(table_hbm, idx_v, out_hbm, out_base, n_steps, rows_v, gsem, wsem):
    nbuf = SC_NBUF

    def gather(s, b):
        return pltpu.make_async_copy(table_hbm.at[idx_v.at[pl.ds(s * SC_GG, SC_GG)]], rows_v.at[b], gsem.at[b])

    def write(s, b):
        return pltpu.make_async_copy(rows_v.at[b], out_hbm.at[pl.ds(out_base + s * SC_GG, SC_GG)], wsem.at[b])

    for j in range(nbuf - 1):
        gather(j, j).start()

    @pl.loop(0, n_steps, step=nbuf)
    def _(s0):
        for b in range(nbuf):
            s = s0 + b
            gather(s, b).wait()
            write(s, b).start()
            refill = (b + nbuf - 1) % nbuf

            @pl.when(s + nbuf - 1 < n_steps)
            def _():
                @pl.when(s >= 1)
                def _():
                    write(s - 1, refill).wait()
                gather(s + nbuf - 1, refill).start()

    for j in range(nbuf):
        write(n_steps - nbuf + j, j).wait()


def _sc_dispatch_scatter(table, dest_chunks, n_slots):
    T, width = table.shape
    n_workers = SC_CORES * SC_SUBCORES
    per_w = T // n_workers
    n_steps = per_w // SC_G
    assert T % n_workers == 0 and per_w % (2 * SC_G) == 0
    assert dest_chunks.shape == (T // SC_G, TOP_K, SC_G)

    @functools.partial(
        pl.kernel, mesh=_sc_mesh(),
        out_type=jax.ShapeDtypeStruct((n_slots, width), table.dtype),
        scratch_types=[
            pltpu.VMEM((n_steps, TOP_K, SC_G), I32),
            pltpu.VMEM((2, SC_G, width), table.dtype),
            pltpu.SemaphoreType.DMA((2,)),
            pltpu.SemaphoreType.DMA((2,)),
        ],
        name="sc_dispatch_scatter",
    )
    def k(table_hbm, dest_hbm, out_hbm, dst_v, rows_v, rsem, wsem):
        wid = lax.axis_index("s") * SC_CORES + lax.axis_index("c")
        pltpu.sync_copy(dest_hbm.at[pl.ds(wid * n_steps, n_steps)], dst_v)

        def read(s, b):
            return pltpu.make_async_copy(table_hbm.at[pl.ds(wid * per_w + s * SC_G, SC_G)], rows_v.at[b], rsem.at[b])

        def scatter(s, kk, b):
            return pltpu.make_async_copy(rows_v.at[b], out_hbm.at[dst_v.at[s, kk]], wsem.at[b])

        read(0, 0).start()

        @pl.loop(0, n_steps, step=2)
        def _(s0):
            for b in range(2):
                s = s0 + b
                read(s, b).wait()

                @pl.when(s + 1 < n_steps)
                def _():
                    @pl.when(s >= 1)
                    def _():
                        for kk in range(TOP_K):
                            scatter(s - 1, kk, 1 - b).wait()
                    read(s + 1, 1 - b).start()

                for kk in range(TOP_K):
                    scatter(s, kk, b).start()

        for kk in range(TOP_K):
            scatter(n_steps - 2, kk, 0).wait()
            scatter(n_steps - 1, kk, 1).wait()

    return k(table, dest_chunks)


def _sc_combine_gather(table, dest_flat):
    n_rows, width = dest_flat.shape[0], table.shape[1]
    n_workers = SC_CORES * SC_SUBCORES
    per_w = n_rows // n_workers
    n_steps = per_w // SC_GG
    assert n_rows % n_workers == 0 and per_w % (SC_NBUF * SC_GG) == 0

    @functools.partial(
        pl.kernel, mesh=_sc_mesh(),
        out_type=jax.ShapeDtypeStruct((n_rows, width), table.dtype),
        scratch_types=[
            pltpu.VMEM((per_w,), I32),
            pltpu.VMEM((SC_NBUF, SC_GG, width), table.dtype),
            pltpu.SemaphoreType.DMA((SC_NBUF,)),
            pltpu.SemaphoreType.DMA((SC_NBUF,)),
        ],
        name="sc_combine_gather",
    )
    def k(table_hbm, dest_hbm, out_hbm, dst_v, rows_v, gsem, wsem):
        wid = lax.axis_index("s") * SC_CORES + lax.axis_index("c")
        lo = wid * per_w
        pltpu.sync_copy(dest_hbm.at[pl.ds(lo, per_w)], dst_v)
        _sc_gather_rows(table_hbm, dst_v, out_hbm, lo, n_steps, rows_v, gsem, wsem)

    return k(table, dest_flat)


def _expert_kernel(be_ref, nb_ref, nv_ref, x_ref, wg_ref, wu_ref, bg_ref, bu_ref, wd_ref, prev_ref, y_ref, *, last):
    i = pl.program_id(0)

    @pl.when(i < nb_ref[0])
    def _():
        row = lax.broadcasted_iota(I32, (x_ref.shape[0], 1), 0)
        x = _unpack_pairs(jnp.where(row < nv_ref[i], x_ref[...], jnp.uint32(0))).astype(BF16)
        gate = jnp.minimum(_dot(x, wg_ref[...]) + bg_ref[...], SWIGLU_LIMIT)
        up = jnp.clip(_dot(x, wu_ref[...]) + bu_ref[...], -SWIGLU_LIMIT, SWIGLU_LIMIT)
        act = (up + 1.0) * gate * _sigmoid(SWIGLU_ALPHA * gate)
        y = prev_ref[...] + _dot(act.astype(BF16), wd_ref[...])
        y_ref[...] = _pack_pairs(y) if last else y


def _expert_pass(c, block_e, nb_used, n_valid, x_sorted, gu_w, gu_b, down_w, prev):
    P = x_sorted.shape[0]
    nc = D_FF // MOE_TF
    first, last = c == 0, c == nc - 1
    rowblk = lambda i, be, nbu, nv: (jnp.minimum(i, nbu[0] - 1), 0)
    prev_spec = (pl.BlockSpec((None, 1, D_MODEL), lambda i, be, nbu, nv: (be[i], 0, 0)) if first
                 else pl.BlockSpec((MOE_TM, D_MODEL), rowblk))
    grid_spec = pltpu.PrefetchScalarGridSpec(
        num_scalar_prefetch=3,
        grid=(P // MOE_TM,),
        in_specs=[
            pl.BlockSpec((MOE_TM, PACK_W), rowblk),
            pl.BlockSpec((None, D_MODEL, MOE_TF), lambda i, be, nbu, nv: (be[i], 0, c)),
            pl.BlockSpec((None, D_MODEL, MOE_TF), lambda i, be, nbu, nv: (be[i], 0, nc + c)),
            pl.BlockSpec((None, 1, MOE_TF), lambda i, be, nbu, nv: (be[i], 0, c)),
            pl.BlockSpec((None, 1, MOE_TF), lambda i, be, nbu, nv: (be[i], 0, nc + c)),
            pl.BlockSpec((None, MOE_TF, D_MODEL), lambda i, be, nbu, nv: (be[i], c, 0)),
            prev_spec,
        ],
        out_specs=pl.BlockSpec((MOE_TM, PACK_W if last else D_MODEL), rowblk),
    )
    return pl.pallas_call(
        functools.partial(_expert_kernel, last=last),
        grid_spec=grid_spec,
        out_shape=jax.ShapeDtypeStruct((P, PACK_W), U32) if last else jax.ShapeDtypeStruct((P, D_MODEL), F32),
        compiler_params=_params(("arbitrary",)),
        name="experts_%d" % c,
    )(block_e, nb_used, n_valid, x_sorted, gu_w, gu_w, gu_b, gu_b, down_w, prev)


def _combine_kernel(x_ref, y_ref, gate_ref, g_ref, o_ref):
    acc = x_ref[...]
    gates = gate_ref[...]
    for kk in range(TOP_K):
        acc = acc + gates[:, kk:kk + 1] * _unpack_pairs(y_ref[kk])
    o_ref[...] = _rms(acc, g_ref[...])


def _combine(x2, y_g, gates, g):
    T = x2.shape[0]
    return pl.pallas_call(
        _combine_kernel,
        grid=(T // CB_TM,),
        in_specs=[
            pl.BlockSpec((CB_TM, D_MODEL), lambda i: (i, 0)),
            pl.BlockSpec((TOP_K, CB_TM, PACK_W), lambda i: (0, i, 0)),
            pl.BlockSpec((CB_TM, LANES), lambda i: (i, 0)),
            pl.BlockSpec((1, D_MODEL), lambda i: (0, 0)),
        ],
        out_specs=pl.BlockSpec((CB_TM, D_MODEL), lambda i: (i, 0)),
        out_shape=jax.ShapeDtypeStruct((T, D_MODEL), F32),
        compiler_params=_params(("arbitrary",)),
        name="combine",
    )(x2, y_g, gates, g)


def _row(v):
    return v.reshape(1, -1).astype(F32)


def _layer(x, mem, norm_mix_g, w_in, gla_gate_w, gla_gate_b, gla_norm_g, gla_out_w, conv_w, conv_b,
           conv_out_w, mix_out_w, norm_xattn_g, norm_mem_g, xq_w, xk_w, xv_w, xo_w, norm_moe_g,
           router_w, router_b, gu_w, gu_b, down_w, down_b):
    T = x.shape[0]
    w_main, w_alow = _w_in_prep(w_in)
    gate_w_pad = jnp.pad(gla_gate_w, ((0, LANES - GLA_GATE_RANK), (0, 0)))
    conv_w8 = jnp.pad(conv_w, ((0, 8 - CONV_WIDTH), (0, 0)))

    p, a_low, gu_wb = _in_proj(x, _row(norm_mix_g), w_main, w_alow, gu_w.reshape(-1, 2 * D_FF))
    og, down_wb = _gla(p, a_low, gate_w_pad, _row(gla_gate_b), _row(gla_norm_g), down_w.reshape(-1, D_MODEL))
    gu_wb = gu_wb.reshape(gu_w.shape)
    down_wb = down_wb.reshape(down_w.shape)
    x1 = _merge(og, p, conv_w8, _row(conv_b), gla_out_w.astype(BF16), conv_out_w.astype(BF16),
                mix_out_w.astype(BF16), x)

    kmem, vmem = _mem_kv(mem, _row(norm_mem_g), xk_w.astype(BF16), xv_w.astype(BF16))
    rw = jnp.pad(router_w, ((0, 0), (0, LANES - N_EXPERTS))).astype(BF16)
    rb = jnp.pad(router_b, (0, LANES - N_EXPERTS)).reshape(1, LANES)
    x2, hm, idx, gates, rank, counts = _xattn_router(
        x1, _row(norm_xattn_g), xq_w.astype(BF16), kmem, vmem, xo_w.astype(BF16), _row(norm_moe_g), rw, rb)

    cnt = counts[0, :N_EXPERTS].astype(I32)
    padded = (cnt + MOE_TM - 1) // MOE_TM * MOE_TM
    pend = jnp.cumsum(padded)
    pstart = pend - padded
    pstart_row = jnp.pad(pstart, (0, LANES - N_EXPERTS)).reshape(1, LANES).astype(F32)
    nb = (T * TOP_K) // MOE_TM + N_EXPERTS
    blk_start = (jnp.arange(nb, dtype=I32) * MOE_TM)[:, None]
    block_e = jnp.minimum(jnp.sum(pend[None, :] <= blk_start, axis=1), N_EXPERTS - 1).astype(I32)
    nb_used = (pend[-1] // MOE_TM).astype(I32).reshape(1)
    in_group = (pstart[None, :] <= blk_start) & (blk_start < pend[None, :])
    n_valid = jnp.sum(jnp.where(in_group, jnp.clip((pstart + cnt)[None, :] - blk_start, 0, MOE_TM), 0),
                      axis=1).astype(I32)

    dest = _dispatch(idx, rank, pstart_row)[:, :TOP_K]
    dest_chunks = dest.reshape(T // SC_G, SC_G, TOP_K).transpose(0, 2, 1)
    x_sorted = _sc_dispatch_scatter(hm, dest_chunks, nb * MOE_TM)

    gu_b3 = gu_b.reshape(N_EXPERTS, 1, 2 * D_FF)
    y = down_b.reshape(N_EXPERTS, 1, D_MODEL)
    for c in range(D_FF // MOE_TF):
        y = _expert_pass(c, block_e, nb_used, n_valid, x_sorted, gu_wb, gu_b3, down_wb, y)
    y_g = _sc_combine_gather(y, dest.T.reshape(-1)).reshape(TOP_K, T, PACK_W)
    return x2, y_g, gates


def kernel(x, mem, norm_mix_g, w_in, gla_gate_w, gla_gate_b, gla_norm_g, gla_out_w, conv_w, conv_b, conv_out_w, mix_out_w, norm_xattn_g, norm_mem_g, xq_w, xk_w, xv_w, xo_w, norm_moe_g, router_w, router_b, expert_gu_w, expert_gu_b, expert_down_w, expert_down_b, norm_final_g):
    assert x.shape[0] == 1 and mem.shape[0] == 1 and w_in.shape[0] == 1
    x2, y_g, gates = _layer(
        x[0], mem[0], norm_mix_g[0], w_in[0], gla_gate_w[0], gla_gate_b[0], gla_norm_g[0], gla_out_w[0],
        conv_w[0], conv_b[0], conv_out_w[0], mix_out_w[0], norm_xattn_g[0], norm_mem_g[0], xq_w[0], xk_w[0],
        xv_w[0], xo_w[0], norm_moe_g[0], router_w[0], router_b[0], expert_gu_w[0], expert_gu_b[0],
        expert_down_w[0], expert_down_b[0])
    out = _combine(x2, y_g, gates, _row(norm_final_g))
    return out[None]
```

```python
import functools

import jax
import jax.numpy as jnp
from jax import lax
from jax.experimental import pallas as pl
from jax.experimental.pallas import tpu as pltpu
from jax.experimental.pallas import tpu_sc as plsc

F32 = jnp.float32
BF16 = jnp.bfloat16
I32 = jnp.int32
U32 = jnp.uint32

D_MODEL = 2048
MEM_LEN = 256
GLA_HEADS = 4
GLA_DK = 256
GLA_DV = 512
GLA_GATE_RANK = 16
GLA_GATE_TEMP = 16.0
CONV_WIDTH = 3
XATTN_HEADS = 4
XATTN_HEAD_DIM = 128
XATTN_DIM = XATTN_HEADS * XATTN_HEAD_DIM
N_EXPERTS = 32
TOP_K = 4
D_FF = D_MODEL
SWIGLU_LIMIT = 7.0
SWIGLU_ALPHA = 1.702
NORM_EPS = 1e-5

LANES = 128
SC_CORES, SC_SUBCORES, SC_LANES = 2, 16, 16
PACK_W = D_MODEL // 2
P_COLS = 8 * D_MODEL
VMEM_LIMIT = 56 * 1024 * 1024

IN_TM, IN_TN = 1024, 1024
GLA_CH = 128
GLA_RB = 512
GLA_HG = 2
MG_TM = 256
XA_TM = 512
DP_TM = 2048
MOE_TM = 256
MOE_TF = 1024
CB_TM = 256
SC_G = 32
SC_GG, SC_NBUF = 16, 4


def _params(sem):
    return pltpu.CompilerParams(dimension_semantics=sem, vmem_limit_bytes=VMEM_LIMIT)


def _rms(x, g):
    return x * lax.rsqrt(jnp.mean(x * x, axis=-1, keepdims=True) + NORM_EPS) * g


def _dot(a, b):
    return jnp.dot(a, b, preferred_element_type=F32)


def _dot_nt(a, b):
    return lax.dot_general(a, b, (((1,), (1,)), ((), ())), preferred_element_type=F32)


def _dot_tn(a, b):
    return lax.dot_general(a, b, (((0,), (0,)), ((), ())), preferred_element_type=F32)


def _split2(x):
    hi = x.astype(BF16)
    lo = (x - hi.astype(F32)).astype(BF16)
    return hi, lo


def _split3(x):
    hi = x.astype(BF16)
    r = x - hi.astype(F32)
    mid = r.astype(BF16)
    lo = (r - mid.astype(F32)).astype(BF16)
    return hi, mid, lo


def _sigmoid(x):
    return 1.0 / (1.0 + jnp.exp(-x))


def _pack_pairs(x):
    w = x.shape[1] // 2
    u = lax.bitcast_convert_type(x.astype(BF16).astype(F32), U32)
    return (u[:, :w] >> 16) | u[:, w:]


def _unpack_pairs(u):
    lo = lax.bitcast_convert_type(u << 16, F32)
    hi = lax.bitcast_convert_type(u & jnp.uint32(0xFFFF0000), F32)
    return jnp.concatenate([lo, hi], axis=1)


def _side_cast_spec(w2d, n_steps, step_of):
    rows, cols = w2d.shape
    assert rows % n_steps == 0
    return pl.BlockSpec((rows // n_steps, cols), lambda *ids: (step_of(*ids), 0))


def _winprep_kernel(wt_ref, main_ref):
    main_ref[...] = wt_ref[...].T.astype(BF16)


def _w_in_prep(w_in_t):
    rows, d = w_in_t.shape
    a0 = 3 * D_MODEL
    assert rows == P_COLS + GLA_GATE_RANK and a0 % IN_TN == 0
    n_left = a0 // IN_TN
    assert IN_TN % GLA_GATE_RANK == 0
    row0 = lambda j: ((j * (IN_TN // GLA_GATE_RANK) + (j >= n_left).astype(I32)) * GLA_GATE_RANK, 0)
    return pl.pallas_call(
        _winprep_kernel,
        grid=(P_COLS // IN_TN,),
        in_specs=[pl.BlockSpec((pl.Element(IN_TN), pl.Element(d)), row0)],
        out_specs=pl.BlockSpec((d, IN_TN), lambda j: (0, j)),
        out_shape=jax.ShapeDtypeStruct((d, P_COLS), BF16),
        compiler_params=_params(("arbitrary",)),
        name="w_in_prep",
    )(w_in_t)


def _inproj_kernel(x_ref, g_ref, w_ref, wa_ref, cin_ref, p_ref, a_ref, cout_ref, h_scr):
    @pl.when(pl.program_id(1) == 0)
    def _():
        hb = _rms(x_ref[...], g_ref[...]).astype(BF16)
        h_scr[...] = hb
        a_ref[...] = _dot_nt(hb, wa_ref[...].astype(BF16))

    p_ref[...] = _dot(h_scr[...], w_ref[...]).astype(BF16)
    cout_ref[...] = cin_ref[...].astype(BF16)


def _in_proj(x, g, w_main, w_alow_t, w_cast):
    T = x.shape[0]
    ni, nj = T // IN_TM, P_COLS // IN_TN
    cast_spec = _side_cast_spec(w_cast, ni * nj, lambda i, j: i * nj + j)
    return pl.pallas_call(
        _inproj_kernel,
        grid=(ni, nj),
        in_specs=[
            pl.BlockSpec((IN_TM, D_MODEL), lambda i, j: (i, 0)),
            pl.BlockSpec((1, D_MODEL), lambda i, j: (0, 0)),
            pl.BlockSpec((D_MODEL, IN_TN), lambda i, j: (0, j)),
            pl.BlockSpec((LANES, D_MODEL), lambda i, j: (0, 0)),
            cast_spec,
        ],
        out_specs=[
            pl.BlockSpec((IN_TM, IN_TN), lambda i, j: (i, j)),
            pl.BlockSpec((IN_TM, LANES), lambda i, j: (i, 0)),
            cast_spec,
        ],
        out_shape=[
            jax.ShapeDtypeStruct((T, P_COLS), BF16),
            jax.ShapeDtypeStruct((T, LANES), F32),
            jax.ShapeDtypeStruct(w_cast.shape, BF16),
        ],
        scratch_shapes=[pltpu.VMEM((IN_TM, D_MODEL), BF16)],
        compiler_params=_params(("arbitrary", "arbitrary")),
        name="in_proj",
    )(x, g, w_main, w_alow_t, w_cast)


def _gla_level_weights(b, la, row):
    ch, dk = b.shape
    out = []
    s = ch // 2
    while s >= 4:
        bb = b.reshape(ch // (2 * s), 2 * s, dk)
        d = (bb - bb[:, s - 1:s, :]).reshape(ch, dk)
        out.append((s, jnp.exp(-jnp.abs(d))))
        s //= 2
    la_prev = pltpu.roll(la, 1, 0)
    la_next = pltpu.roll(la, ch - 1, 0)
    r4 = row & 3
    d2 = jnp.where(r4 == 0, la_next, jnp.where(r4 == 1, 0.0, jnp.where(r4 == 2, la, la + la_prev)))
    out.append((2, jnp.exp(-jnp.abs(d2))))
    d1 = jnp.where((row & 1) == 1, la, 0.0)
    out.append((1, jnp.exp(-jnp.abs(d1))))
    return out


def _gla_kernel(q_ref, k_ref, v_ref, r_ref, a_ref, gw_ref, gb_ref, ng_ref, cin_ref, o_ref, cout_ref, st_scr):
    ch = GLA_CH
    cout_ref[...] = cin_ref[...].astype(BF16)

    @pl.when(pl.program_id(1) == 0)
    def _():
        st_scr[...] = jnp.zeros_like(st_scr)

    row = lax.broadcasted_iota(jnp.int32, (ch, 1), 0)
    col = lax.broadcasted_iota(jnp.int32, (1, ch), 1)
    tril = (row >= col).astype(BF16)
    levels = []
    s = ch // 2
    while s >= 1:
        levels.append((s, ((row ^ col) < 2 * s) & ((row & (2 * s - 1)) >= s) & ((col & (2 * s - 1)) < s)))
        s //= 2
    gw_hi, gw_lo = _split2(gw_ref[...])
    gate_b = gb_ref[...]
    norm_g = ng_ref[...]
    q_scale = jnp.asarray(GLA_DK ** -0.5, BF16)

    def head(sl, hh, a_hi, a_lo):
        ks = slice(hh * GLA_DK, (hh + 1) * GLA_DK)
        vs = slice(hh * GLA_DV, (hh + 1) * GLA_DV)
        qb = q_ref[sl, ks] * q_scale
        kb = k_ref[sl, ks]
        q, k = qb.astype(F32), kb.astype(F32)
        v = v_ref[sl, vs]
        z = _dot(a_hi, gw_hi[:, ks]) + _dot(a_lo, gw_hi[:, ks]) + _dot(a_hi, gw_lo[:, ks]) + gate_b[:, ks]
        la = (jnp.minimum(z, 0.0) - jnp.log(1.0 + jnp.exp(-jnp.abs(z)))) * (1.0 / GLA_GATE_TEMP)
        l_hi, l_mid, l_lo = _split3(la)
        b = _dot(tril, l_hi) + _dot(tril, l_mid) + _dot(tril, l_lo)

        scores = jnp.where(row == col, _dot_nt(qb, kb), 0.0)
        for (s, mask), (_, w) in zip(levels, _gla_level_weights(b, la, row)):
            scores = jnp.where(mask, _dot_nt((q * w).astype(BF16), (k * w).astype(BF16)), scores)
        o = _dot(scores.astype(BF16), v)

        st = st_scr[hh]
        o = o + _dot_nt((q * jnp.exp(b)).astype(BF16), st.astype(BF16))
        b_last = b[ch - 1:ch, :]
        kd = (k * jnp.exp(b_last - b)).astype(BF16)
        st_scr[hh] = st * jnp.exp(b_last) + _dot_tn(v, kd)

        o = _rms(o, norm_g)
        r = r_ref[sl, vs].astype(F32)
        o_ref[sl, vs] = (o * (r * _sigmoid(r))).astype(BF16)

    def chunk(c, carry):
        sl = pl.ds(pl.multiple_of(c * ch, ch), ch)
        a_hi, a_lo = _split2(a_ref[sl, :])
        for hh in range(GLA_HG):
            head(sl, hh, a_hi, a_lo)
        return carry

    lax.fori_loop(0, GLA_RB // ch, chunk, 0)


def _gla(p, a_low, gate_w_pad, gate_b, norm_g, w_cast):
    T = p.shape[0]
    ni = T // GLA_RB
    ng = GLA_HEADS // GLA_HG
    kw, vw = GLA_HG * GLA_DK, GLA_HG * GLA_DV
    cast_spec = _side_cast_spec(w_cast, ng * ni, lambda h, i: h * ni + i)
    return pl.pallas_call(
        _gla_kernel,
        grid=(ng, ni),
        in_specs=[
            pl.BlockSpec((GLA_RB, kw), lambda h, i: (i, h)),
            pl.BlockSpec((GLA_RB, kw), lambda h, i: (i, ng + h)),
            pl.BlockSpec((GLA_RB, vw), lambda h, i: (i, ng + h)),
            pl.BlockSpec((GLA_RB, vw), lambda h, i: (i, 2 * ng + h)),
            pl.BlockSpec((GLA_RB, LANES), lambda h, i: (i, 0)),
            pl.BlockSpec((LANES, kw), lambda h, i: (0, h)),
            pl.BlockSpec((1, kw), lambda h, i: (0, h)),
            pl.BlockSpec((1, GLA_DV), lambda h, i: (0, 0)),
            cast_spec,
        ],
        out_specs=[pl.BlockSpec((GLA_RB, vw), lambda h, i: (i, h)), cast_spec],
        out_shape=[jax.ShapeDtypeStruct((T, GLA_HEADS * GLA_DV), BF16),
                   jax.ShapeDtypeStruct(w_cast.shape, BF16)],
        scratch_shapes=[pltpu.VMEM((GLA_HG, GLA_DV, GLA_DK), F32)],
        compiler_params=_params(("arbitrary", "arbitrary")),
        name="gla",
    )(p, p, p, p, a_low, gate_w_pad, gate_b, norm_g, w_cast)


def _merge_kernel(og_ref, cb_ref, cc_ref, ch_ref, ccp_ref, chp_ref, cw_ref, cbias_ref,
                  ga_ref, gb_ref, wa_ref, wb_ref, wm_ref, x_ref, o_ref):
    tm = cc_ref.shape[0]
    u = cc_ref[...].astype(F32) * ch_ref[...].astype(F32)
    up = ccp_ref[...].astype(F32) * chp_ref[...].astype(F32)
    up = jnp.where(pl.program_id(0) == 0, 0.0, up)
    hp = up.shape[0]
    row = lax.broadcasted_iota(jnp.int32, (tm, 1), 0)
    u1 = jnp.where(row == 0, up[hp - 1:hp, :], pltpu.roll(u, 1, 0))
    u2 = jnp.where(row == 0, up[hp - 2:hp - 1, :],
                   jnp.where(row == 1, up[hp - 1:hp, :], pltpu.roll(u, 2, 0)))
    cw = cw_ref[...]
    uc = cw[0:1, :] * u2 + cw[1:2, :] * u1 + cw[2:3, :] * u + cbias_ref[...]
    zb = (cb_ref[...].astype(F32) * uc).astype(BF16)

    ya = _dot(og_ref[...], wa_ref[...])
    yb = _dot(zb, wb_ref[...])
    merged = (_sigmoid(ga_ref[...].astype(F32)) * ya + _sigmoid(gb_ref[...].astype(F32)) * yb).astype(BF16)
    o_ref[...] = x_ref[...] + _dot(merged, wm_ref[...])


def _merge(og, p, conv_w8, conv_b, gla_out_w, conv_out_w, mix_out_w, x):
    T = og.shape[0]
    halo = 16
    prev = lambda blk: (lambda i: (jnp.maximum(i * (MG_TM // halo) - 1, 0), blk))
    pblk = lambda blk: pl.BlockSpec((MG_TM, D_MODEL), lambda i: (i, blk))
    const = lambda shape: pl.BlockSpec(shape, lambda i: (0, 0), pipeline_mode=pl.Buffered(1))
    return pl.pallas_call(
        _merge_kernel,
        grid=(T // MG_TM,),
        in_specs=[
            pblk(0),
            pblk(3), pblk(4), pblk(5),
            pl.BlockSpec((halo, D_MODEL), prev(4)),
            pl.BlockSpec((halo, D_MODEL), prev(5)),
            const((8, D_MODEL)),
            const((1, D_MODEL)),
            pblk(6), pblk(7),
            const((D_MODEL, D_MODEL)), const((D_MODEL, D_MODEL)), const((D_MODEL, D_MODEL)),
            pblk(0),
        ],
        out_specs=pblk(0),
        out_shape=jax.ShapeDtypeStruct((T, D_MODEL), F32),
        compiler_params=_params(("arbitrary",)),
        name="merge",
    )(og, p, p, p, p, p, conv_w8, conv_b, p, p, gla_out_w, conv_out_w, mix_out_w, x)


def _memkv_kernel(m_ref, g_ref, wk_ref, wv_ref, k_ref, v_ref):
    mn = _rms(m_ref[...], g_ref[...]).astype(BF16)
    k_ref[...] = _dot(mn, wk_ref[...]).astype(BF16)
    v_ref[...] = _dot(mn, wv_ref[...]).astype(BF16)


def _mem_kv(mem, g, wk, wv):
    return pl.pallas_call(
        _memkv_kernel,
        out_shape=[jax.ShapeDtypeStruct((MEM_LEN, XATTN_DIM), BF16)] * 2,
        compiler_params=pltpu.CompilerParams(vmem_limit_bytes=VMEM_LIMIT),
        name="mem_kv",
    )(mem, g, wk, wv)


def _xattn_tile(x_ref, g_ref, wq_ref, k_ref, v_ref, wo_ref):
    x = x_ref[...]
    hq = _rms(x, g_ref[...]).astype(BF16)
    q = (_dot(hq, wq_ref[...]) * (XATTN_HEAD_DIM ** -0.5)).astype(BF16)
    outs = []
    for h in range(XATTN_HEADS):
        cs = slice(h * XATTN_HEAD_DIM, (h + 1) * XATTN_HEAD_DIM)
        s = _dot_nt(q[:, cs], k_ref[:, cs])
        s = s - jnp.max(s, axis=-1, keepdims=True)
        e = jnp.exp(s)
        p = e / jnp.sum(e, axis=-1, keepdims=True)
        outs.append(_dot(p.astype(BF16), v_ref[:, cs]))
    o = jnp.concatenate(outs, axis=-1).astype(BF16)
    return x + _dot(o, wo_ref[...])


def _xattn_router_kernel(x_ref, gx_ref, wq_ref, k_ref, v_ref, wo_ref, gm_ref, rw_ref, rb_ref,
                         x2_ref, hm_ref, idx_ref, gate_ref, rank_ref, cnt_ref, carry_scr):
    x2 = _xattn_tile(x_ref, gx_ref, wq_ref, k_ref, v_ref, wo_ref)
    x2_ref[...] = x2
    _route_tile(x2, gm_ref, rw_ref, rb_ref, hm_ref, idx_ref, gate_ref, rank_ref, cnt_ref, carry_scr)


def _route_tile(x2, g_ref, w_ref, b_ref, hm_ref, idx_ref, gate_ref, rank_ref, cnt_ref, carry_scr):
    tm = x2.shape[0]

    @pl.when(pl.program_id(0) == 0)
    def _():
        carry_scr[...] = jnp.zeros_like(carry_scr)

    h = _rms(x2, g_ref[...])
    hb = h.astype(BF16)
    hm_ref[...] = _pack_pairs(h)
    lane = lax.broadcasted_iota(jnp.int32, (tm, LANES), 1)
    logits = jnp.where(lane < N_EXPERTS, _dot(hb, w_ref[...]) + b_ref[...], -jnp.inf)

    vals, idxs = [], []
    work = logits
    for _ in range(TOP_K):
        m = jnp.max(work, axis=-1, keepdims=True)
        am = jnp.min(jnp.where(work == m, lane, LANES), axis=-1, keepdims=True)
        vals.append(m)
        idxs.append(am)
        work = jnp.where(lane == am, -jnp.inf, work)
    es = [jnp.exp(v - vals[0]) for v in vals]
    inv = 1.0 / (es[0] + es[1] + es[2] + es[3])

    onehot = (work != logits).astype(BF16)
    r_ = lax.broadcasted_iota(jnp.int32, (tm, tm), 0)
    c_ = lax.broadcasted_iota(jnp.int32, (tm, tm), 1)
    before = _dot((r_ > c_).astype(BF16), onehot) + carry_scr[...]
    carry_scr[...] = carry_scr[...] + jnp.sum(onehot.astype(F32), axis=0, keepdims=True)
    cnt_ref[...] = carry_scr[...]

    idx_out = jnp.zeros((tm, LANES), jnp.int32)
    gate_out = jnp.zeros((tm, LANES), F32)
    rank_out = jnp.zeros((tm, LANES), jnp.int32)
    for kk in range(TOP_K):
        rk = jnp.sum(jnp.where(lane == idxs[kk], before, 0.0), axis=-1, keepdims=True)
        idx_out = jnp.where(lane == kk, idxs[kk], idx_out)
        gate_out = jnp.where(lane == kk, es[kk] * inv, gate_out)
        rank_out = jnp.where(lane == kk, rk.astype(jnp.int32), rank_out)
    idx_ref[...] = idx_out
    gate_ref[...] = gate_out
    rank_ref[...] = rank_out


def _xattn_router(x1, gx, wq, kmem, vmem, wo, gm, rw_pad, rb_pad):
    T = x1.shape[0]
    full = lambda shape: pl.BlockSpec(shape, lambda i: (0, 0))
    row_blk = pl.BlockSpec((XA_TM, LANES), lambda i: (i, 0))
    return pl.pallas_call(
        _xattn_router_kernel,
        grid=(T // XA_TM,),
        in_specs=[
            pl.BlockSpec((XA_TM, D_MODEL), lambda i: (i, 0)),
            full((1, D_MODEL)),
            full((D_MODEL, XATTN_DIM)),
            full((MEM_LEN, XATTN_DIM)),
            full((MEM_LEN, XATTN_DIM)),
            full((XATTN_DIM, D_MODEL)),
            full((1, D_MODEL)),
            full((D_MODEL, LANES)),
            full((1, LANES)),
        ],
        out_specs=[
            pl.BlockSpec((XA_TM, D_MODEL), lambda i: (i, 0)),
            pl.BlockSpec((XA_TM, PACK_W), lambda i: (i, 0)),
            row_blk, row_blk, row_blk,
            full((1, LANES)),
        ],
        out_shape=[
            jax.ShapeDtypeStruct((T, D_MODEL), F32),
            jax.ShapeDtypeStruct((T, PACK_W), U32),
            jax.ShapeDtypeStruct((T, LANES), jnp.int32),
            jax.ShapeDtypeStruct((T, LANES), F32),
            jax.ShapeDtypeStruct((T, LANES), jnp.int32),
            jax.ShapeDtypeStruct((1, LANES), F32),
        ],
        scratch_shapes=[pltpu.VMEM((1, LANES), F32)],
        compiler_params=_params(("arbitrary",)),
        name="xattn_router",
    )(x1, gx, wq, kmem, vmem, wo, gm, rw_pad, rb_pad)


def _dispatch_kernel(idx_ref, rank_ref, pstart_ref, dest_ref):
    tm = idx_ref.shape[0]
    lane = lax.broadcasted_iota(I32, (tm, LANES), 1)
    idx = idx_ref[...]
    pstart = pstart_ref[...]
    dest = rank_ref[...]
    for kk in range(TOP_K):
        start = jnp.sum(jnp.where(lane == idx[:, kk:kk + 1], pstart, 0.0), axis=-1, keepdims=True)
        dest = dest + jnp.where(lane == kk, start.astype(I32), 0)
    dest_ref[...] = dest


def _dispatch(idx, rank, pstart_row):
    T = idx.shape[0]
    blk = pl.BlockSpec((DP_TM, LANES), lambda i: (i, 0))
    return pl.pallas_call(
        _dispatch_kernel,
        grid=(T // DP_TM,),
        in_specs=[blk, blk, pl.BlockSpec((1, LANES), lambda i: (0, 0))],
        out_specs=blk,
        out_shape=jax.ShapeDtypeStruct((T, LANES), I32),
        compiler_params=_params(("arbitrary",)),
        name="dispatch",
    )(idx, rank, pstart_row)


def _sc_mesh():
    return plsc.VectorSubcoreMesh(core_axis_name="c", subcore_axis_name="s",
                                  num_cores=SC_CORES, num_subcores=SC_SUBCORES)


def _sc_gather_rows(table_hbm, idx_v, out_hbm, out_base, n_steps, rows_v, gsem, wsem):
    nbuf = SC_NBUF

    def gather(s, b):
        return pltpu.make_async_copy(table_hbm.at[idx_v.at[pl.ds(s * SC_GG, SC_GG)]], rows_v.at[b], gsem.at[b])

    def write(s, b):
        return pltpu.make_async_copy(rows_v.at[b], out_hbm.at[pl.ds(out_base + s * SC_GG, SC_GG)], wsem.at[b])

    for j in range(nbuf - 1):
        gather(j, j).start()

    @pl.loop(0, n_steps, step=nbuf)
    def _(s0):
        for b in range(nbuf):
            s = s0 + b
            gather(s, b).wait()
            write(s, b).start()
            refill = (b + nbuf - 1) % nbuf

            @pl.when(s + nbuf - 1 < n_steps)
            def _():
                @pl.when(s >= 1)
                def _():
                    write(s - 1, refill).wait()
                gather(s + nbuf - 1, refill).start()

    for j in range(nbuf):
        write(n_steps - nbuf + j, j).wait()


def _sc_dispatch_scatter(table, dest_chunks, n_slots):
    T, width = table.shape
    n_workers = SC_CORES * SC_SUBCORES
    per_w = T // n_workers
    n_steps = per_w // SC_G
    assert T % n_workers == 0 and per_w % (2 * SC_G) == 0
    assert dest_chunks.shape == (T // SC_G, TOP_K, SC_G)

    @functools.partial(
        pl.kernel, mesh=_sc_mesh(),
        out_type=jax.ShapeDtypeStruct((n_slots, width), table.dtype),
        scratch_types=[
            pltpu.VMEM((n_steps, TOP_K, SC_G), I32),
            pltpu.VMEM((2, SC_G, width), table.dtype),
            pltpu.SemaphoreType.DMA((2,)),
            pltpu.SemaphoreType.DMA((2,)),
        ],
        name="sc_dispatch_scatter",
    )
    def k(table_hbm, dest_hbm, out_hbm, dst_v, rows_v, rsem, wsem):
        wid = lax.axis_index("s") * SC_CORES + lax.axis_index("c")
        pltpu.sync_copy(dest_hbm.at[pl.ds(wid * n_steps, n_steps)], dst_v)

        def read(s, b):
            return pltpu.make_async_copy(table_hbm.at[pl.ds(wid * per_w + s * SC_G, SC_G)], rows_v.at[b], rsem.at[b])

        def scatter(s, kk, b):
            return pltpu.make_async_copy(rows_v.at[b], out_hbm.at[dst_v.at[s, kk]], wsem.at[b])

        read(0, 0).start()

        @pl.loop(0, n_steps, step=2)
        def _(s0):
            for b in range(2):
                s = s0 + b
                read(s, b).wait()

                @pl.when(s + 1 < n_steps)
                def _():
                    @pl.when(s >= 1)
                    def _():
                        for kk in range(TOP_K):
                            scatter(s - 1, kk, 1 - b).wait()
                    read(s + 1, 1 - b).start()

                for kk in range(TOP_K):
                    scatter(s, kk, b).start()

        for kk in range(TOP_K):
            scatter(n_steps - 2, kk, 0).wait()
            scatter(n_steps - 1, kk, 1).wait()

    return k(table, dest_chunks)


def _sc_combine_gather(table, dest_flat):
    n_rows, width = dest_flat.shape[0], table.shape[1]
    n_workers = SC_CORES * SC_SUBCORES
    per_w = n_rows // n_workers
    n_steps = per_w // SC_GG
    assert n_rows % n_workers == 0 and per_w % (SC_NBUF * SC_GG) == 0

    @functools.partial(
        pl.kernel, mesh=_sc_mesh(),
        out_type=jax.ShapeDtypeStruct((n_rows, width), table.dtype),
        scratch_types=[
            pltpu.VMEM((per_w,), I32),
            pltpu.VMEM((SC_NBUF, SC_GG, width), table.dtype),
            pltpu.SemaphoreType.DMA((SC_NBUF,)),
            pltpu.SemaphoreType.DMA((SC_NBUF,)),
        ],
        name="sc_combine_gather",
    )
    def k(table_hbm, dest_hbm, out_hbm, dst_v, rows_v, gsem, wsem):
        wid = lax.axis_index("s") * SC_CORES + lax.axis_index("c")
        lo = wid * per_w
        pltpu.sync_copy(dest_hbm.at[pl.ds(lo, per_w)], dst_v)
        _sc_gather_rows(table_hbm, dst_v, out_hbm, lo, n_steps, rows_v, gsem, wsem)

    return k(table, dest_flat)


def _expert_kernel(be_ref, nb_ref, nv_ref, x_ref, wg_ref, wu_ref, bg_ref, bu_ref, wd_ref, prev_ref, y_ref, *, last):
    i = pl.program_id(0)

    @pl.when(i < nb_ref[0])
    def _():
        row = lax.broadcasted_iota(I32, (x_ref.shape[0], 1), 0)
        x = _unpack_pairs(jnp.where(row < nv_ref[i], x_ref[...], jnp.uint32(0))).astype(BF16)
        gate = jnp.minimum(_dot(x, wg_ref[...]) + bg_ref[...], SWIGLU_LIMIT)
        up = jnp.clip(_dot(x, wu_ref[...]) + bu_ref[...], -SWIGLU_LIMIT, SWIGLU_LIMIT)
        act = (up + 1.0) * gate * _sigmoid(SWIGLU_ALPHA * gate)
        y = prev_ref[...] + _dot(act.astype(BF16), wd_ref[...])
        y_ref[...] = _pack_pairs(y) if last else y


def _expert_pass(c, block_e, nb_used, n_valid, x_sorted, gu_w, gu_b, down_w, prev):
    P = x_sorted.shape[0]
    nc = D_FF // MOE_TF
    first, last = c == 0, c == nc - 1
    rowblk = lambda i, be, nbu, nv: (jnp.minimum(i, nbu[0] - 1), 0)
    prev_spec = (pl.BlockSpec((None, 1, D_MODEL), lambda i, be, nbu, nv: (be[i], 0, 0)) if first
                 else pl.BlockSpec((MOE_TM, D_MODEL), rowblk))
    grid_spec = pltpu.PrefetchScalarGridSpec(
        num_scalar_prefetch=3,
        grid=(P // MOE_TM,),
        in_specs=[
            pl.BlockSpec((MOE_TM, PACK_W), rowblk),
            pl.BlockSpec((None, D_MODEL, MOE_TF), lambda i, be, nbu, nv: (be[i], 0, c)),
            pl.BlockSpec((None, D_MODEL, MOE_TF), lambda i, be, nbu, nv: (be[i], 0, nc + c)),
            pl.BlockSpec((None, 1, MOE_TF), lambda i, be, nbu, nv: (be[i], 0, c)),
            pl.BlockSpec((None, 1, MOE_TF), lambda i, be, nbu, nv: (be[i], 0, nc + c)),
            pl.BlockSpec((None, MOE_TF, D_MODEL), lambda i, be, nbu, nv: (be[i], c, 0)),
            prev_spec,
        ],
        out_specs=pl.BlockSpec((MOE_TM, PACK_W if last else D_MODEL), rowblk),
    )
    return pl.pallas_call(
        functools.partial(_expert_kernel, last=last),
        grid_spec=grid_spec,
        out_shape=jax.ShapeDtypeStruct((P, PACK_W), U32) if last else jax.ShapeDtypeStruct((P, D_MODEL), F32),
        compiler_params=_params(("arbitrary",)),
        name="experts_%d" % c,
    )(block_e, nb_used, n_valid, x_sorted, gu_w, gu_w, gu_b, gu_b, down_w, prev)


def _combine_kernel(x_ref, y_ref, gate_ref, g_ref, o_ref):
    acc = x_ref[...]
    gates = gate_ref[...]
    for kk in range(TOP_K):
        acc = acc + gates[:, kk:kk + 1] * _unpack_pairs(y_ref[kk])
    o_ref[...] = _rms(acc, g_ref[...])


def _combine(x2, y_g, gates, g):
    T = x2.shape[0]
    return pl.pallas_call(
        _combine_kernel,
        grid=(T // CB_TM,),
        in_specs=[
            pl.BlockSpec((CB_TM, D_MODEL), lambda i: (i, 0)),
            pl.BlockSpec((TOP_K, CB_TM, PACK_W), lambda i: (0, i, 0)),
            pl.BlockSpec((CB_TM, LANES), lambda i: (i, 0)),
            pl.BlockSpec((1, D_MODEL), lambda i: (0, 0)),
        ],
        out_specs=pl.BlockSpec((CB_TM, D_MODEL), lambda i: (i, 0)),
        out_shape=jax.ShapeDtypeStruct((T, D_MODEL), F32),
        compiler_params=_params(("arbitrary",)),
        name="combine",
    )(x2, y_g, gates, g)


def _row(v):
    return v.reshape(1, -1).astype(F32)


def _layer(x, mem, norm_mix_g, w_in, gla_gate_w, gla_gate_b, gla_norm_g, gla_out_w, conv_w, conv_b,
           conv_out_w, mix_out_w, norm_xattn_g, norm_mem_g, xq_w, xk_w, xv_w, xo_w, norm_moe_g,
           router_w, router_b, gu_w, gu_b, down_w, down_b):
    T = x.shape[0]
    w_in_t = w_in.T
    w_main = _w_in_prep(w_in_t)
    a0 = 3 * D_MODEL
    w_alow_t = jnp.pad(w_in_t[a0:a0 + GLA_GATE_RANK], ((0, LANES - GLA_GATE_RANK), (0, 0)))
    gate_w_pad = jnp.pad(gla_gate_w, ((0, LANES - GLA_GATE_RANK), (0, 0)))
    conv_w8 = jnp.pad(conv_w, ((0, 8 - CONV_WIDTH), (0, 0)))

    p, a_low, gu_wb = _in_proj(x, _row(norm_mix_g), w_main, w_alow_t, gu_w.reshape(-1, 2 * D_FF))
    og, down_wb = _gla(p, a_low, gate_w_pad, _row(gla_gate_b), _row(gla_norm_g), down_w.reshape(-1, D_MODEL))
    gu_wb = gu_wb.reshape(gu_w.shape)
    down_wb = down_wb.reshape(down_w.shape)
    x1 = _merge(og, p, conv_w8, _row(conv_b), gla_out_w.astype(BF16), conv_out_w.astype(BF16),
                mix_out_w.astype(BF16), x)

    kmem, vmem = _mem_kv(mem, _row(norm_mem_g), xk_w.astype(BF16), xv_w.astype(BF16))
    rw = jnp.pad(router_w, ((0, 0), (0, LANES - N_EXPERTS))).astype(BF16)
    rb = jnp.pad(router_b, (0, LANES - N_EXPERTS)).reshape(1, LANES)
    x2, hm, idx, gates, rank, counts = _xattn_router(
        x1, _row(norm_xattn_g), xq_w.astype(BF16), kmem, vmem, xo_w.astype(BF16), _row(norm_moe_g), rw, rb)

    cnt = counts[0, :N_EXPERTS].astype(I32)
    padded = (cnt + MOE_TM - 1) // MOE_TM * MOE_TM
    pend = jnp.cumsum(padded)
    pstart = pend - padded
    pstart_row = jnp.pad(pstart, (0, LANES - N_EXPERTS)).reshape(1, LANES).astype(F32)
    nb = (T * TOP_K) // MOE_TM + N_EXPERTS
    blk_start = (jnp.arange(nb, dtype=I32) * MOE_TM)[:, None]
    block_e = jnp.minimum(jnp.sum(pend[None, :] <= blk_start, axis=1), N_EXPERTS - 1).astype(I32)
    nb_used = (pend[-1] // MOE_TM).astype(I32).reshape(1)
    in_group = (pstart[None, :] <= blk_start) & (blk_start < pend[None, :])
    n_valid = jnp.sum(jnp.where(in_group, jnp.clip((pstart + cnt)[None, :] - blk_start, 0, MOE_TM), 0),
                      axis=1).astype(I32)

    dest = _dispatch(idx, rank, pstart_row)[:, :TOP_K]
    dest_chunks = dest.reshape(T // SC_G, SC_G, TOP_K).transpose(0, 2, 1)
    x_sorted = _sc_dispatch_scatter(hm, dest_chunks, nb * MOE_TM)

    gu_b3 = gu_b.reshape(N_EXPERTS, 1, 2 * D_FF)
    y = down_b.reshape(N_EXPERTS, 1, D_MODEL)
    for c in range(D_FF // MOE_TF):
        y = _expert_pass(c, block_e, nb_used, n_valid, x_sorted, gu_wb, gu_b3, down_wb, y)
    y_g = _sc_combine_gather(y, dest.T.reshape(-1)).reshape(TOP_K, T, PACK_W)
    return x2, y_g, gates


def kernel(x, mem, norm_mix_g, w_in, gla_gate_w, gla_gate_b, gla_norm_g, gla_out_w, conv_w, conv_b, conv_out_w, mix_out_w, norm_xattn_g, norm_mem_g, xq_w, xk_w, xv_w, xo_w, norm_moe_g, router_w, router_b, expert_gu_w, expert_gu_b, expert_down_w, expert_down_b, norm_final_g):
    assert x.shape[0] == 1 and mem.shape[0] == 1 and w_in.shape[0] == 1
    x2, y_g, gates = _layer(
        x[0], mem[0], norm_mix_g[0], w_in[0], gla_gate_w[0], gla_gate_b[0], gla_norm_g[0], gla_out_w[0],
        conv_w[0], conv_b[0], conv_out_w[0], mix_out_w[0], norm_xattn_g[0], norm_mem_g[0], xq_w[0], xk_w[0],
        xv_w[0], xo_w[0], norm_moe_g[0], router_w[0], router_b[0], expert_gu_w[0], expert_gu_b[0],
        expert_down_w[0], expert_down_b[0])
    out = _combine(x2, y_g, gates, _row(norm_final_g))
    return out[None]
```

```python
import functools

import jax
import jax.numpy as jnp
from jax import lax
from jax.experimental import pallas as pl
from jax.experimental.pallas import tpu as pltpu
from jax.experimental.pallas import tpu_sc as plsc

F32 = jnp.float32
BF16 = jnp.bfloat16
I32 = jnp.int32
U32 = jnp.uint32

D_MODEL = 2048
MEM_LEN = 256
GLA_HEADS = 4
GLA_DK = 256
GLA_DV = 512
GLA_GATE_RANK = 16
GLA_GATE_TEMP = 16.0
CONV_WIDTH = 3
XATTN_HEADS = 4
XATTN_HEAD_DIM = 128
XATTN_DIM = XATTN_HEADS * XATTN_HEAD_DIM
N_EXPERTS = 32
TOP_K = 4
D_FF = D_MODEL
SWIGLU_LIMIT = 7.0
SWIGLU_ALPHA = 1.702
NORM_EPS = 1e-5

LANES = 128
SC_CORES, SC_SUBCORES, SC_LANES = 2, 16, 16
PACK_W = D_MODEL // 2
P_COLS = 8 * D_MODEL
VMEM_LIMIT = 56 * 1024 * 1024

WP_TN = 1024
IN_TM, IN_TN = 512, 2048
GLA_CH = 128
GLA_RB = 512
GLA_HG = 2
MG_TM = 256
XA_TM = 512
DP_TM = 2048
MOE_TM = 256
MOE_TF = 1024
CB_TM = 256
SC_G = 32
SC_GG, SC_NBUF = 16, 4


def _params(sem):
    return pltpu.CompilerParams(dimension_semantics=sem, vmem_limit_bytes=VMEM_LIMIT)


def _rms(x, g):
    return x * lax.rsqrt(jnp.mean(x * x, axis=-1, keepdims=True) + NORM_EPS) * g


def _dot(a, b):
    return jnp.dot(a, b, preferred_element_type=F32)


def _dot_nt(a, b):
    return lax.dot_general(a, b, (((1,), (1,)), ((), ())), preferred_element_type=F32)


def _dot_tn(a, b):
    return lax.dot_general(a, b, (((0,), (0,)), ((), ())), preferred_element_type=F32)


def _split2(x):
    hi = x.astype(BF16)
    lo = (x - hi.astype(F32)).astype(BF16)
    return hi, lo


def _split3(x):
    hi = x.astype(BF16)
    r = x - hi.astype(F32)
    mid = r.astype(BF16)
    lo = (r - mid.astype(F32)).astype(BF16)
    return hi, mid, lo


def _sigmoid(x):
    return 1.0 / (1.0 + jnp.exp(-x))


def _pack_pairs(x):
    w = x.shape[1] // 2
    u = lax.bitcast_convert_type(x.astype(BF16).astype(F32), U32)
    return (u[:, :w] >> 16) | u[:, w:]


def _unpack_pairs(u):
    lo = lax.bitcast_convert_type(u << 16, F32)
    hi = lax.bitcast_convert_type(u & jnp.uint32(0xFFFF0000), F32)
    return jnp.concatenate([lo, hi], axis=1)


def _side_cast_spec(w2d, n_steps, step_of):
    rows, cols = w2d.shape
    assert rows % n_steps == 0
    return pl.BlockSpec((rows // n_steps, cols), lambda *ids: (step_of(*ids), 0))


def _winprep_kernel(wt_ref, main_ref):
    main_ref[...] = wt_ref[...].T.astype(BF16)


def _w_in_prep(w_in_t):
    rows, d = w_in_t.shape
    a0 = 3 * D_MODEL
    assert rows == P_COLS + GLA_GATE_RANK and a0 % WP_TN == 0
    n_left = a0 // WP_TN
    assert WP_TN % GLA_GATE_RANK == 0
    row0 = lambda j: ((j * (WP_TN // GLA_GATE_RANK) + (j >= n_left).astype(I32)) * GLA_GATE_RANK, 0)
    return pl.pallas_call(
        _winprep_kernel,
        grid=(P_COLS // WP_TN,),
        in_specs=[pl.BlockSpec((pl.Element(WP_TN), pl.Element(d)), row0)],
        out_specs=pl.BlockSpec((d, WP_TN), lambda j: (0, j)),
        out_shape=jax.ShapeDtypeStruct((d, P_COLS), BF16),
        compiler_params=_params(("arbitrary",)),
        name="w_in_prep",
    )(w_in_t)


def _inproj_kernel(x_ref, g_ref, w_ref, wa_ref, cin_ref, p_ref, a_ref, cout_ref, h_scr):
    @pl.when(pl.program_id(1) == 0)
    def _():
        hb = _rms(x_ref[...], g_ref[...]).astype(BF16)
        h_scr[...] = hb
        a_ref[...] = _dot_nt(hb, wa_ref[...].astype(BF16))

    p_ref[...] = _dot(h_scr[...], w_ref[...]).astype(BF16)
    cout_ref[...] = cin_ref[...].astype(BF16)


def _in_proj(x, g, w_main, w_alow_t, w_cast):
    T = x.shape[0]
    ni, nj = T // IN_TM, P_COLS // IN_TN
    cast_spec = _side_cast_spec(w_cast, ni * nj, lambda i, j: i * nj + j)
    return pl.pallas_call(
        _inproj_kernel,
        grid=(ni, nj),
        in_specs=[
            pl.BlockSpec((IN_TM, D_MODEL), lambda i, j: (i, 0)),
            pl.BlockSpec((1, D_MODEL), lambda i, j: (0, 0)),
            pl.BlockSpec((D_MODEL, IN_TN), lambda i, j: (0, j)),
            pl.BlockSpec((LANES, D_MODEL), lambda i, j: (0, 0)),
            cast_spec,
        ],
        out_specs=[
            pl.BlockSpec((IN_TM, IN_TN), lambda i, j: (i, j)),
            pl.BlockSpec((IN_TM, LANES), lambda i, j: (i, 0)),
            cast_spec,
        ],
        out_shape=[
            jax.ShapeDtypeStruct((T, P_COLS), BF16),
            jax.ShapeDtypeStruct((T, LANES), F32),
            jax.ShapeDtypeStruct(w_cast.shape, BF16),
        ],
        scratch_shapes=[pltpu.VMEM((IN_TM, D_MODEL), BF16)],
        compiler_params=_params(("arbitrary", "arbitrary")),
        name="in_proj",
    )(x, g, w_main, w_alow_t, w_cast)


def _gla_level_weights(b, la, row):
    ch, dk = b.shape
    out = []
    s = ch // 2
    while s >= 4:
        bb = b.reshape(ch // (2 * s), 2 * s, dk)
        d = (bb - bb[:, s - 1:s, :]).reshape(ch, dk)
        out.append((s, jnp.exp(-jnp.abs(d))))
        s //= 2
    la_prev = pltpu.roll(la, 1, 0)
    la_next = pltpu.roll(la, ch - 1, 0)
    r4 = row & 3
    d2 = jnp.where(r4 == 0, la_next, jnp.where(r4 == 1, 0.0, jnp.where(r4 == 2, la, la + la_prev)))
    out.append((2, jnp.exp(-jnp.abs(d2))))
    d1 = jnp.where((row & 1) == 1, la, 0.0)
    out.append((1, jnp.exp(-jnp.abs(d1))))
    return out


def _gla_kernel(q_ref, k_ref, v_ref, r_ref, a_ref, gw_ref, gb_ref, ng_ref, cin_ref, o_ref, cout_ref, st_scr):
    ch = GLA_CH
    cout_ref[...] = cin_ref[...].astype(BF16)

    @pl.when(pl.program_id(1) == 0)
    def _():
        st_scr[...] = jnp.zeros_like(st_scr)

    row = lax.broadcasted_iota(jnp.int32, (ch, 1), 0)
    col = lax.broadcasted_iota(jnp.int32, (1, ch), 1)
    tril = (row >= col).astype(BF16)
    levels = []
    s = ch // 2
    while s >= 1:
        levels.append((s, ((row ^ col) < 2 * s) & ((row & (2 * s - 1)) >= s) & ((col & (2 * s - 1)) < s)))
        s //= 2
    gw_hi, gw_lo = _split2(gw_ref[...])
    gate_b = gb_ref[...]
    norm_g = ng_ref[...]
    q_scale = jnp.asarray(GLA_DK ** -0.5, BF16)

    def head(sl, hh, a_hi, a_lo):
        ks = slice(hh * GLA_DK, (hh + 1) * GLA_DK)
        vs = slice(hh * GLA_DV, (hh + 1) * GLA_DV)
        qb = q_ref[sl, ks] * q_scale
        kb = k_ref[sl, ks]
        q, k = qb.astype(F32), kb.astype(F32)
        v = v_ref[sl, vs]
        z = _dot(a_hi, gw_hi[:, ks]) + _dot(a_lo, gw_hi[:, ks]) + _dot(a_hi, gw_lo[:, ks]) + gate_b[:, ks]
        la = (jnp.minimum(z, 0.0) - jnp.log(1.0 + jnp.exp(-jnp.abs(z)))) * (1.0 / GLA_GATE_TEMP)
        l_hi, l_mid, l_lo = _split3(la)
        b = _dot(tril, l_hi) + _dot(tril, l_mid) + _dot(tril, l_lo)

        scores = jnp.where(row == col, _dot_nt(qb, kb), 0.0)
        for (s, mask), (_, w) in zip(levels, _gla_level_weights(b, la, row)):
            scores = jnp.where(mask, _dot_nt((q * w).astype(BF16), (k * w).astype(BF16)), scores)
        o = _dot(scores.astype(BF16), v)

        st = st_scr[hh]
        o = o + _dot_nt((q * jnp.exp(b)).astype(BF16), st.astype(BF16))
        b_last = b[ch - 1:ch, :]
        kd = (k * jnp.exp(b_last - b)).astype(BF16)
        st_scr[hh] = st * jnp.exp(b_last) + _dot_tn(v, kd)

        o = _rms(o, norm_g)
        r = r_ref[sl, vs].astype(F32)
        o_ref[sl, vs] = (o * (r * _sigmoid(r))).astype(BF16)

    def chunk(c, carry):
        sl = pl.ds(pl.multiple_of(c * ch, ch), ch)
        a_hi, a_lo = _split2(a_ref[sl, :])
        for hh in range(GLA_HG):
            head(sl, hh, a_hi, a_lo)
        return carry

    lax.fori_loop(0, GLA_RB // ch, chunk, 0)


def _gla(p, a_low, gate_w_pad, gate_b, norm_g, w_cast):
    T = p.shape[0]
    ni = T // GLA_RB
    ng = GLA_HEADS // GLA_HG
    kw, vw = GLA_HG * GLA_DK, GLA_HG * GLA_DV
    cast_spec = _side_cast_spec(w_cast, ng * ni, lambda h, i: h * ni + i)
    return pl.pallas_call(
        _gla_kernel,
        grid=(ng, ni),
        in_specs=[
            pl.BlockSpec((GLA_RB, kw), lambda h, i: (i, h)),
            pl.BlockSpec((GLA_RB, kw), lambda h, i: (i, ng + h)),
            pl.BlockSpec((GLA_RB, vw), lambda h, i: (i, ng + h)),
            pl.BlockSpec((GLA_RB, vw), lambda h, i: (i, 2 * ng + h)),
            pl.BlockSpec((GLA_RB, LANES), lambda h, i: (i, 0)),
            pl.BlockSpec((LANES, kw), lambda h, i: (0, h)),
            pl.BlockSpec((1, kw), lambda h, i: (0, h)),
            pl.BlockSpec((1, GLA_DV), lambda h, i: (0, 0)),
            cast_spec,
        ],
        out_specs=[pl.BlockSpec((GLA_RB, vw), lambda h, i: (i, h)), cast_spec],
        out_shape=[jax.ShapeDtypeStruct((T, GLA_HEADS * GLA_DV), BF16),
                   jax.ShapeDtypeStruct(w_cast.shape, BF16)],
        scratch_shapes=[pltpu.VMEM((GLA_HG, GLA_DV, GLA_DK), F32)],
        compiler_params=_params(("arbitrary", "arbitrary")),
        name="gla",
    )(p, p, p, p, a_low, gate_w_pad, gate_b, norm_g, w_cast)


def _merge_kernel(og_ref, cb_ref, cc_ref, ch_ref, ccp_ref, chp_ref, cw_ref, cbias_ref,
                  ga_ref, gb_ref, wa_ref, wb_ref, wm_ref, x_ref, o_ref):
    tm = cc_ref.shape[0]
    u = cc_ref[...].astype(F32) * ch_ref[...].astype(F32)
    up = ccp_ref[...].astype(F32) * chp_ref[...].astype(F32)
    up = jnp.where(pl.program_id(0) == 0, 0.0, up)
    hp = up.shape[0]
    row = lax.broadcasted_iota(jnp.int32, (tm, 1), 0)
    u1 = jnp.where(row == 0, up[hp - 1:hp, :], pltpu.roll(u, 1, 0))
    u2 = jnp.where(row == 0, up[hp - 2:hp - 1, :],
                   jnp.where(row == 1, up[hp - 1:hp, :], pltpu.roll(u, 2, 0)))
    cw = cw_ref[...]
    uc = cw[0:1, :] * u2 + cw[1:2, :] * u1 + cw[2:3, :] * u + cbias_ref[...]
    zb = (cb_ref[...].astype(F32) * uc).astype(BF16)

    ya = _dot(og_ref[...], wa_ref[...])
    yb = _dot(zb, wb_ref[...])
    merged = (_sigmoid(ga_ref[...].astype(F32)) * ya + _sigmoid(gb_ref[...].astype(F32)) * yb).astype(BF16)
    o_ref[...] = x_ref[...] + _dot(merged, wm_ref[...])


def _merge(og, p, conv_w8, conv_b, gla_out_w, conv_out_w, mix_out_w, x):
    T = og.shape[0]
    halo = 16
    prev = lambda blk: (lambda i: (jnp.maximum(i * (MG_TM // halo) - 1, 0), blk))
    pblk = lambda blk: pl.BlockSpec((MG_TM, D_MODEL), lambda i: (i, blk))
    const = lambda shape: pl.BlockSpec(shape, lambda i: (0, 0), pipeline_mode=pl.Buffered(1))
    return pl.pallas_call(
        _merge_kernel,
        grid=(T // MG_TM,),
        in_specs=[
            pblk(0),
            pblk(3), pblk(4), pblk(5),
            pl.BlockSpec((halo, D_MODEL), prev(4)),
            pl.BlockSpec((halo, D_MODEL), prev(5)),
            const((8, D_MODEL)),
            const((1, D_MODEL)),
            pblk(6), pblk(7),
            const((D_MODEL, D_MODEL)), const((D_MODEL, D_MODEL)), const((D_MODEL, D_MODEL)),
            pblk(0),
        ],
        out_specs=pblk(0),
        out_shape=jax.ShapeDtypeStruct((T, D_MODEL), F32),
        compiler_params=_params(("arbitrary",)),
        name="merge",
    )(og, p, p, p, p, p, conv_w8, conv_b, p, p, gla_out_w, conv_out_w, mix_out_w, x)


def _memkv_kernel(m_ref, g_ref, wk_ref, wv_ref, k_ref, v_ref):
    mn = _rms(m_ref[...], g_ref[...]).astype(BF16)
    k_ref[...] = _dot(mn, wk_ref[...]).astype(BF16)
    v_ref[...] = _dot(mn, wv_ref[...]).astype(BF16)


def _mem_kv(mem, g, wk, wv):
    return pl.pallas_call(
        _memkv_kernel,
        out_shape=[jax.ShapeDtypeStruct((MEM_LEN, XATTN_DIM), BF16)] * 2,
        compiler_params=pltpu.CompilerParams(vmem_limit_bytes=VMEM_LIMIT),
        name="mem_kv",
    )(mem, g, wk, wv)


def _xattn_rows(x, g_ref, wq_ref, k_ref, v_ref, wo_ref):
    hq = _rms(x, g_ref[...]).astype(BF16)
    q = (_dot(hq, wq_ref[...]) * (XATTN_HEAD_DIM ** -0.5)).astype(BF16)
    outs = []
    for h in range(XATTN_HEADS):
        cs = slice(h * XATTN_HEAD_DIM, (h + 1) * XATTN_HEAD_DIM)
        s = _dot_nt(q[:, cs], k_ref[:, cs])
        s = s - jnp.max(s, axis=-1, keepdims=True)
        e = jnp.exp(s)
        p = e / jnp.sum(e, axis=-1, keepdims=True)
        outs.append(_dot(p.astype(BF16), v_ref[:, cs]))
    o = jnp.concatenate(outs, axis=-1).astype(BF16)
    return x + _dot(o, wo_ref[...])


def _xattn_router_kernel(x_ref, gx_ref, wq_ref, k_ref, v_ref, wo_ref, gm_ref, rw_ref, rb_ref,
                         x2_ref, hm_ref, idx_ref, gate_ref, rank_ref, cnt_ref, carry_scr):
    x2 = _xattn_rows(x_ref[...], gx_ref, wq_ref, k_ref, v_ref, wo_ref)
    x2_ref[...] = x2
    _route_tile(x2, gm_ref, rw_ref, rb_ref, hm_ref, idx_ref, gate_ref, rank_ref, cnt_ref, carry_scr)


def _route_tile(x2, g_ref, w_ref, b_ref, hm_ref, idx_ref, gate_ref, rank_ref, cnt_ref, carry_scr):
    tm = x2.shape[0]

    @pl.when(pl.program_id(0) == 0)
    def _():
        carry_scr[...] = jnp.zeros_like(carry_scr)

    h = _rms(x2, g_ref[...])
    hb = h.astype(BF16)
    hm_ref[...] = _pack_pairs(h)
    lane = lax.broadcasted_iota(jnp.int32, (tm, LANES), 1)
    logits = jnp.where(lane < N_EXPERTS, _dot(hb, w_ref[...]) + b_ref[...], -jnp.inf)

    vals, idxs = [], []
    work = logits
    for _ in range(TOP_K):
        m = jnp.max(work, axis=-1, keepdims=True)
        am = jnp.min(jnp.where(work == m, lane, LANES), axis=-1, keepdims=True)
        vals.append(m)
        idxs.append(am)
        work = jnp.where(lane == am, -jnp.inf, work)
    es = [jnp.exp(v - vals[0]) for v in vals]
    inv = 1.0 / (es[0] + es[1] + es[2] + es[3])

    onehot = (work != logits).astype(BF16)
    r_ = lax.broadcasted_iota(jnp.int32, (tm, tm), 0)
    c_ = lax.broadcasted_iota(jnp.int32, (tm, tm), 1)
    before = _dot((r_ > c_).astype(BF16), onehot) + carry_scr[...]
    carry_scr[...] = carry_scr[...] + jnp.sum(onehot.astype(F32), axis=0, keepdims=True)
    cnt_ref[...] = carry_scr[...]

    idx_out = jnp.zeros((tm, LANES), jnp.int32)
    gate_out = jnp.zeros((tm, LANES), F32)
    rank_out = jnp.zeros((tm, LANES), jnp.int32)
    for kk in range(TOP_K):
        rk = jnp.sum(jnp.where(lane == idxs[kk], before, 0.0), axis=-1, keepdims=True)
        idx_out = jnp.where(lane == kk, idxs[kk], idx_out)
        gate_out = jnp.where(lane == kk, es[kk] * inv, gate_out)
        rank_out = jnp.where(lane == kk, rk.astype(jnp.int32), rank_out)
    idx_ref[...] = idx_out
    gate_ref[...] = gate_out
    rank_ref[...] = rank_out


def _xattn_router(x1, gx, wq, kmem, vmem, wo, gm, rw_pad, rb_pad):
    T = x1.shape[0]
    full = lambda shape: pl.BlockSpec(shape, lambda i: (0, 0))
    row_blk = pl.BlockSpec((XA_TM, LANES), lambda i: (i, 0))
    return pl.pallas_call(
        _xattn_router_kernel,
        grid=(T // XA_TM,),
        in_specs=[
            pl.BlockSpec((XA_TM, D_MODEL), lambda i: (i, 0)),
            full((1, D_MODEL)),
            full((D_MODEL, XATTN_DIM)),
            full((MEM_LEN, XATTN_DIM)),
            full((MEM_LEN, XATTN_DIM)),
            full((XATTN_DIM, D_MODEL)),
            full((1, D_MODEL)),
            full((D_MODEL, LANES)),
            full((1, LANES)),
        ],
        out_specs=[
            pl.BlockSpec((XA_TM, D_MODEL), lambda i: (i, 0)),
            pl.BlockSpec((XA_TM, PACK_W), lambda i: (i, 0)),
            row_blk, row_blk, row_blk,
            full((1, LANES)),
        ],
        out_shape=[
            jax.ShapeDtypeStruct((T, D_MODEL), F32),
            jax.ShapeDtypeStruct((T, PACK_W), U32),
            jax.ShapeDtypeStruct((T, LANES), jnp.int32),
            jax.ShapeDtypeStruct((T, LANES), F32),
            jax.ShapeDtypeStruct((T, LANES), jnp.int32),
            jax.ShapeDtypeStruct((1, LANES), F32),
        ],
        scratch_shapes=[pltpu.VMEM((1, LANES), F32)],
        compiler_params=_params(("arbitrary",)),
        name="xattn_router",
    )(x1, gx, wq, kmem, vmem, wo, gm, rw_pad, rb_pad)


def _dispatch_kernel(idx_ref, rank_ref, pstart_ref, dest_ref):
    tm = idx_ref.shape[0]
    lane = lax.broadcasted_iota(I32, (tm, LANES), 1)
    idx = idx_ref[...]
    pstart = pstart_ref[...]
    dest = rank_ref[...]
    for kk in range(TOP_K):
        start = jnp.sum(jnp.where(lane == idx[:, kk:kk + 1], pstart, 0.0), axis=-1, keepdims=True)
        dest = dest + jnp.where(lane == kk, start.astype(I32), 0)
    dest_ref[...] = dest


def _dispatch(idx, rank, pstart_row):
    T = idx.shape[0]
    blk = pl.BlockSpec((DP_TM, LANES), lambda i: (i, 0))
    return pl.pallas_call(
        _dispatch_kernel,
        grid=(T // DP_TM,),
        in_specs=[blk, blk, pl.BlockSpec((1, LANES), lambda i: (0, 0))],
        out_specs=blk,
        out_shape=jax.ShapeDtypeStruct((T, LANES), I32),
        compiler_params=_params(("arbitrary",)),
        name="dispatch",
    )(idx, rank, pstart_row)


def _sc_mesh():
    return plsc.VectorSubcoreMesh(core_axis_name="c", subcore_axis_name="s",
                                  num_cores=SC_CORES, num_subcores=SC_SUBCORES)


def _sc_gather_rows(table_hbm, idx_v, out_hbm, out_base, n_steps, rows_v, gsem, wsem):
    nbuf = SC_NBUF

    def gather(s, b):
        return pltpu.make_async_copy(table_hbm.at[idx_v.at[pl.ds(s * SC_GG, SC_GG)]], rows_v.at[b], gsem.at[b])

    def write(s, b):
        return pltpu.make_async_copy(rows_v.at[b], out_hbm.at[pl.ds(out_base + s * SC_GG, SC_GG)], wsem.at[b])

    for j in range(nbuf - 1):
        gather(j, j).start()

    @pl.loop(0, n_steps, step=nbuf)
    def _(s0):
        for b in range(nbuf):
            s = s0 + b
            gather(s, b).wait()
            write(s, b).start()
            refill = (b + nbuf - 1) % nbuf

            @pl.when(s + nbuf - 1 < n_steps)
            def _():
                @pl.when(s >= 1)
                def _():
                    write(s - 1, refill).wait()
                gather(s + nbuf - 1, refill).start()

    for j in range(nbuf):
        write(n_steps - nbuf + j, j).wait()


def _sc_dispatch_scatter(table, dest_chunks, n_slots):
    T, width = table.shape
    n_workers = SC_CORES * SC_SUBCORES
    per_w = T // n_workers
    n_steps = per_w // SC_G
    assert T % n_workers == 0 and per_w % (2 * SC_G) == 0
    assert dest_chunks.shape == (T // SC_G, TOP_K, SC_G)

    @functools.partial(
        pl.kernel, mesh=_sc_mesh(),
        out_type=jax.ShapeDtypeStruct((n_slots, width), table.dtype),
        scratch_types=[
            pltpu.VMEM((n_steps, TOP_K, SC_G), I32),
            pltpu.VMEM((2, SC_G, width), table.dtype),
            pltpu.SemaphoreType.DMA((2,)),
            pltpu.SemaphoreType.DMA((2,)),
        ],
        name="sc_dispatch_scatter",
    )
    def k(table_hbm, dest_hbm, out_hbm, dst_v, rows_v, rsem, wsem):
        wid = lax.axis_index("s") * SC_CORES + lax.axis_index("c")
        pltpu.sync_copy(dest_hbm.at[pl.ds(wid * n_steps, n_steps)], dst_v)

        def read(s, b):
            return pltpu.make_async_copy(table_hbm.at[pl.ds(wid * per_w + s * SC_G, SC_G)], rows_v.at[b], rsem.at[b])

        def scatter(s, kk, b):
            return pltpu.make_async_copy(rows_v.at[b], out_hbm.at[dst_v.at[s, kk]], wsem.at[b])

        read(0, 0).start()

        @pl.loop(0, n_steps, step=2)
        def _(s0):
            for b in range(2):
                s = s0 + b
                read(s, b).wait()

                @pl.when(s + 1 < n_steps)
                def _():
                    @pl.when(s >= 1)
                    def _():
                        for kk in range(TOP_K):
                            scatter(s - 1, kk, 1 - b).wait()
                    read(s + 1, 1 - b).start()

                for kk in range(TOP_K):
                    scatter(s, kk, b).start()

        for kk in range(TOP_K):
            scatter(n_steps - 2, kk, 0).wait()
            scatter(n_steps - 1, kk, 1).wait()

    return k(table, dest_chunks)


def _sc_combine_gather(table, dest_flat):
    n_rows, width = dest_flat.shape[0], table.shape[1]
    n_workers = SC_CORES * SC_SUBCORES
    per_w = n_rows // n_workers
    n_steps = per_w // SC_GG
    assert n_rows % n_workers == 0 and per_w % (SC_NBUF * SC_GG) == 0

    @functools.partial(
        pl.kernel, mesh=_sc_mesh(),
        out_type=jax.ShapeDtypeStruct((n_rows, width), table.dtype),
        scratch_types=[
            pltpu.VMEM((per_w,), I32),
            pltpu.VMEM((SC_NBUF, SC_GG, width), table.dtype),
            pltpu.SemaphoreType.DMA((SC_NBUF,)),
            pltpu.SemaphoreType.DMA((SC_NBUF,)),
        ],
        name="sc_combine_gather",
    )
    def k(table_hbm, dest_hbm, out_hbm, dst_v, rows_v, gsem, wsem):
        wid = lax.axis_index("s") * SC_CORES + lax.axis_index("c")
        lo = wid * per_w
        pltpu.sync_copy(dest_hbm.at[pl.ds(lo, per_w)], dst_v)
        _sc_gather_rows(table_hbm, dst_v, out_hbm, lo, n_steps, rows_v, gsem, wsem)

    return k(table, dest_flat)


def _expert_kernel(be_ref, nb_ref, nv_ref, x_ref, wg_ref, wu_ref, bg_ref, bu_ref, wd_ref, prev_ref, y_ref, *, last):
    i = pl.program_id(0)

    @pl.when(i < nb_ref[0])
    def _():
        row = lax.broadcasted_iota(I32, (x_ref.shape[0], 1), 0)
        x = _unpack_pairs(jnp.where(row < nv_ref[i], x_ref[...], jnp.uint32(0))).astype(BF16)
        gate = jnp.minimum(_dot(x, wg_ref[...]) + bg_ref[...], SWIGLU_LIMIT)
        up = jnp.clip(_dot(x, wu_ref[...]) + bu_ref[...], -SWIGLU_LIMIT, SWIGLU_LIMIT)
        act = (up + 1.0) * gate * _sigmoid(SWIGLU_ALPHA * gate)
        y = prev_ref[...] + _dot(act.astype(BF16), wd_ref[...])
        y_ref[...] = _pack_pairs(y) if last else y


def _expert_pass(c, block_e, nb_used, n_valid, x_sorted, gu_w, gu_b, down_w, prev):
    P = x_sorted.shape[0]
    nc = D_FF // MOE_TF
    first, last = c == 0, c == nc - 1
    rowblk = lambda i, be, nbu, nv: (jnp.minimum(i, nbu[0] - 1), 0)
    prev_spec = (pl.BlockSpec((None, 1, D_MODEL), lambda i, be, nbu, nv: (be[i], 0, 0)) if first
                 else pl.BlockSpec((MOE_TM, D_MODEL), rowblk))
    grid_spec = pltpu.PrefetchScalarGridSpec(
        num_scalar_prefetch=3,
        grid=(P // MOE_TM,),
        in_specs=[
            pl.BlockSpec((MOE_TM, PACK_W), rowblk),
            pl.BlockSpec((None, D_MODEL, MOE_TF), lambda i, be, nbu, nv: (be[i], 0, c)),
            pl.BlockSpec((None, D_MODEL, MOE_TF), lambda i, be, nbu, nv: (be[i], 0, nc + c)),
            pl.BlockSpec((None, 1, MOE_TF), lambda i, be, nbu, nv: (be[i], 0, c)),
            pl.BlockSpec((None, 1, MOE_TF), lambda i, be, nbu, nv: (be[i], 0, nc + c)),
            pl.BlockSpec((None, MOE_TF, D_MODEL), lambda i, be, nbu, nv: (be[i], c, 0)),
            prev_spec,
        ],
        out_specs=pl.BlockSpec((MOE_TM, PACK_W if last else D_MODEL), rowblk),
    )
    return pl.pallas_call(
        functools.partial(_expert_kernel, last=last),
        grid_spec=grid_spec,
        out_shape=jax.ShapeDtypeStruct((P, PACK_W), U32) if last else jax.ShapeDtypeStruct((P, D_MODEL), F32),
        compiler_params=_params(("arbitrary",)),
        name="experts_%d" % c,
    )(block_e, nb_used, n_valid, x_sorted, gu_w, gu_w, gu_b, gu_b, down_w, prev)


def _combine_kernel(x_ref, y_ref, gate_ref, g_ref, o_ref):
    acc = x_ref[...]
    gates = gate_ref[...]
    for kk in range(TOP_K):
        acc = acc + gates[:, kk:kk + 1] * _unpack_pairs(y_ref[kk])
    o_ref[...] = _rms(acc, g_ref[...])


def _combine(x2, y_g, gates, g):
    T = x2.shape[0]
    return pl.pallas_call(
        _combine_kernel,
        grid=(T // CB_TM,),
        in_specs=[
            pl.BlockSpec((CB_TM, D_MODEL), lambda i: (i, 0)),
            pl.BlockSpec((TOP_K, CB_TM, PACK_W), lambda i: (0, i, 0)),
            pl.BlockSpec((CB_TM, LANES), lambda i: (i, 0)),
            pl.BlockSpec((1, D_MODEL), lambda i: (0, 0)),
        ],
        out_specs=pl.BlockSpec((CB_TM, D_MODEL), lambda i: (i, 0)),
        out_shape=jax.ShapeDtypeStruct((T, D_MODEL), F32),
        compiler_params=_params(("arbitrary",)),
        name="combine",
    )(x2, y_g, gates, g)


def _row(v):
    return v.reshape(1, -1).astype(F32)


def _layer(x, mem, norm_mix_g, w_in, gla_gate_w, gla_gate_b, gla_norm_g, gla_out_w, conv_w, conv_b,
           conv_out_w, mix_out_w, norm_xattn_g, norm_mem_g, xq_w, xk_w, xv_w, xo_w, norm_moe_g,
           router_w, router_b, gu_w, gu_b, down_w, down_b):
    T = x.shape[0]
    w_in_t = w_in.T
    w_main = _w_in_prep(w_in_t)
    a0 = 3 * D_MODEL
    w_alow_t = jnp.pad(w_in_t[a0:a0 + GLA_GATE_RANK], ((0, LANES - GLA_GATE_RANK), (0, 0)))
    gate_w_pad = jnp.pad(gla_gate_w, ((0, LANES - GLA_GATE_RANK), (0, 0)))
    conv_w8 = jnp.pad(conv_w, ((0, 8 - CONV_WIDTH), (0, 0)))

    p, a_low, gu_wb = _in_proj(x, _row(norm_mix_g), w_main, w_alow_t, gu_w.reshape(-1, 2 * D_FF))
    og, down_wb = _gla(p, a_low, gate_w_pad, _row(gla_gate_b), _row(gla_norm_g), down_w.reshape(-1, D_MODEL))
    gu_wb = gu_wb.reshape(gu_w.shape)
    down_wb = down_wb.reshape(down_w.shape)
    x1 = _merge(og, p, conv_w8, _row(conv_b), gla_out_w.astype(BF16), conv_out_w.astype(BF16),
                mix_out_w.astype(BF16), x)

    kmem, vmem = _mem_kv(mem, _row(norm_mem_g), xk_w.astype(BF16), xv_w.astype(BF16))
    rw = jnp.pad(router_w, ((0, 0), (0, LANES - N_EXPERTS))).astype(BF16)
    rb = jnp.pad(router_b, (0, LANES - N_EXPERTS)).reshape(1, LANES)
    x2, hm, idx, gates, rank, counts = _xattn_router(
        x1, _row(norm_xattn_g), xq_w.astype(BF16), kmem, vmem, xo_w.astype(BF16), _row(norm_moe_g), rw, rb)

    cnt = counts[0, :N_EXPERTS].astype(I32)
    padded = (cnt + MOE_TM - 1) // MOE_TM * MOE_TM
    pend = jnp.cumsum(padded)
    pstart = pend - padded
    pstart_row = jnp.pad(pstart, (0, LANES - N_EXPERTS)).reshape(1, LANES).astype(F32)
    nb = (T * TOP_K) // MOE_TM + N_EXPERTS
    blk_start = (jnp.arange(nb, dtype=I32) * MOE_TM)[:, None]
    block_e = jnp.minimum(jnp.sum(pend[None, :] <= blk_start, axis=1), N_EXPERTS - 1).astype(I32)
    nb_used = (pend[-1] // MOE_TM).astype(I32).reshape(1)
    in_group = (pstart[None, :] <= blk_start) & (blk_start < pend[None, :])
    n_valid = jnp.sum(jnp.where(in_group, jnp.clip((pstart + cnt)[None, :] - blk_start, 0, MOE_TM), 0),
                      axis=1).astype(I32)

    dest = _dispatch(idx, rank, pstart_row)[:, :TOP_K]
    dest_chunks = dest.reshape(T // SC_G, SC_G, TOP_K).transpose(0, 2, 1)
    x_sorted = _sc_dispatch_scatter(hm, dest_chunks, nb * MOE_TM)

    gu_b3 = gu_b.reshape(N_EXPERTS, 1, 2 * D_FF)
    y = down_b.reshape(N_EXPERTS, 1, D_MODEL)
    for c in range(D_FF // MOE_TF):
        y = _expert_pass(c, block_e, nb_used, n_valid, x_sorted, gu_wb, gu_b3, down_wb, y)
    y_g = _sc_combine_gather(y, dest.T.reshape(-1)).reshape(TOP_K, T, PACK_W)
    return x2, y_g, gates


def kernel(x, mem, norm_mix_g, w_in, gla_gate_w, gla_gate_b, gla_norm_g, gla_out_w, conv_w, conv_b, conv_out_w, mix_out_w, norm_xattn_g, norm_mem_g, xq_w, xk_w, xv_w, xo_w, norm_moe_g, router_w, router_b, expert_gu_w, expert_gu_b, expert_down_w, expert_down_b, norm_final_g):
    assert x.shape[0] == 1 and mem.shape[0] == 1 and w_in.shape[0] == 1
    x2, y_g, gates = _layer(
        x[0], mem[0], norm_mix_g[0], w_in[0], gla_gate_w[0], gla_gate_b[0], gla_norm_g[0], gla_out_w[0],
        conv_w[0], conv_b[0], conv_out_w[0], mix_out_w[0], norm_xattn_g[0], norm_mem_g[0], xq_w[0], xk_w[0],
        xv_w[0], xo_w[0], norm_moe_g[0], router_w[0], router_b[0], expert_gu_w[0], expert_gu_b[0],
        expert_down_w[0], expert_down_b[0])
    out = _combine(x2, y_g, gates, _row(norm_final_g))
    return out[None]
```

```python
import functools

import jax
import jax.numpy as jnp
from jax import lax
from jax.experimental import pallas as pl
from jax.experimental.pallas import tpu as pltpu
from jax.experimental.pallas import tpu_sc as plsc

F32 = jnp.float32
BF16 = jnp.bfloat16
I32 = jnp.int32
U32 = jnp.uint32

D_MODEL = 2048
MEM_LEN = 256
GLA_HEADS = 4
GLA_DK = 256
GLA_DV = 512
GLA_GATE_RANK = 16
GLA_GATE_TEMP = 16.0
CONV_WIDTH = 3
XATTN_HEADS = 4
XATTN_HEAD_DIM = 128
XATTN_DIM = XATTN_HEADS * XATTN_HEAD_DIM
N_EXPERTS = 32
TOP_K = 4
D_FF = D_MODEL
SWIGLU_LIMIT = 7.0
SWIGLU_ALPHA = 1.702
NORM_EPS = 1e-5

LANES = 128
SC_CORES, SC_SUBCORES, SC_LANES = 2, 16, 16
PACK_W = D_MODEL // 2
P_COLS = 8 * D_MODEL
VMEM_LIMIT = 56 * 1024 * 1024

IN_TM, IN_TN = 1024, 1024
GLA_CH = 128
GLA_RB = 512
GLA_HG = 2
MG_TM = 256
XA_TM = 512
DP_TM = 2048
MOE_TM = 512
MOE_TF = 1024
CB_TM = 512
SC_G = 32
SC_GG, SC_NBUF = 16, 4


def _params(sem):
    return pltpu.CompilerParams(dimension_semantics=sem, vmem_limit_bytes=VMEM_LIMIT)


def _rms(x, g):
    return x * lax.rsqrt(jnp.mean(x * x, axis=-1, keepdims=True) + NORM_EPS) * g


def _dot(a, b):
    return jnp.dot(a, b, preferred_element_type=F32)


def _dot_nt(a, b):
    return lax.dot_general(a, b, (((1,), (1,)), ((), ())), preferred_element_type=F32)


def _dot_tn(a, b):
    return lax.dot_general(a, b, (((0,), (0,)), ((), ())), preferred_element_type=F32)


def _split2(x):
    hi = x.astype(BF16)
    lo = (x - hi.astype(F32)).astype(BF16)
    return hi, lo


def _split3(x):
    hi = x.astype(BF16)
    r = x - hi.astype(F32)
    mid = r.astype(BF16)
    lo = (r - mid.astype(F32)).astype(BF16)
    return hi, mid, lo


def _sigmoid(x):
    return 1.0 / (1.0 + jnp.exp(-x))


def _pack_pairs(x):
    w = x.shape[1] // 2
    u = lax.bitcast_convert_type(x.astype(BF16).astype(F32), U32)
    return (u[:, :w] >> 16) | u[:, w:]


def _unpack_pairs(u):
    lo = lax.bitcast_convert_type(u << 16, F32)
    hi = lax.bitcast_convert_type(u & jnp.uint32(0xFFFF0000), F32)
    return jnp.concatenate([lo, hi], axis=1)


def _side_cast_spec(w2d, n_steps, step_of):
    rows, cols = w2d.shape
    assert rows % n_steps == 0
    return pl.BlockSpec((rows // n_steps, cols), lambda *ids: (step_of(*ids), 0))


def _winprep_kernel(wt_ref, main_ref):
    main_ref[...] = wt_ref[...].T.astype(BF16)


def _w_in_prep(w_in_t):
    rows, d = w_in_t.shape
    a0 = 3 * D_MODEL
    assert rows == P_COLS + GLA_GATE_RANK and a0 % IN_TN == 0
    n_left = a0 // IN_TN
    assert IN_TN % GLA_GATE_RANK == 0
    row0 = lambda j: ((j * (IN_TN // GLA_GATE_RANK) + (j >= n_left).astype(I32)) * GLA_GATE_RANK, 0)
    return pl.pallas_call(
        _winprep_kernel,
        grid=(P_COLS // IN_TN,),
        in_specs=[pl.BlockSpec((pl.Element(IN_TN), pl.Element(d)), row0)],
        out_specs=pl.BlockSpec((d, IN_TN), lambda j: (0, j)),
        out_shape=jax.ShapeDtypeStruct((d, P_COLS), BF16),
        compiler_params=_params(("arbitrary",)),
        name="w_in_prep",
    )(w_in_t)


def _inproj_kernel(x_ref, g_ref, w_ref, wa_ref, cin_ref, p_ref, a_ref, cout_ref, h_scr):
    @pl.when(pl.program_id(1) == 0)
    def _():
        hb = _rms(x_ref[...], g_ref[...]).astype(BF16)
        h_scr[...] = hb
        a_ref[...] = _dot_nt(hb, wa_ref[...].astype(BF16))

    p_ref[...] = _dot(h_scr[...], w_ref[...]).astype(BF16)
    cout_ref[...] = cin_ref[...].astype(BF16)


def _in_proj(x, g, w_main, w_alow_t, w_cast):
    T = x.shape[0]
    ni, nj = T // IN_TM, P_COLS // IN_TN
    cast_spec = _side_cast_spec(w_cast, ni * nj, lambda i, j: i * nj + j)
    return pl.pallas_call(
        _inproj_kernel,
        grid=(ni, nj),
        in_specs=[
            pl.BlockSpec((IN_TM, D_MODEL), lambda i, j: (i, 0)),
            pl.BlockSpec((1, D_MODEL), lambda i, j: (0, 0)),
            pl.BlockSpec((D_MODEL, IN_TN), lambda i, j: (0, j)),
            pl.BlockSpec((LANES, D_MODEL), lambda i, j: (0, 0)),
            cast_spec,
        ],
        out_specs=[
            pl.BlockSpec((IN_TM, IN_TN), lambda i, j: (i, j)),
            pl.BlockSpec((IN_TM, LANES), lambda i, j: (i, 0)),
            cast_spec,
        ],
        out_shape=[
            jax.ShapeDtypeStruct((T, P_COLS), BF16),
            jax.ShapeDtypeStruct((T, LANES), F32),
            jax.ShapeDtypeStruct(w_cast.shape, BF16),
        ],
        scratch_shapes=[pltpu.VMEM((IN_TM, D_MODEL), BF16)],
        compiler_params=_params(("arbitrary", "arbitrary")),
        name="in_proj",
    )(x, g, w_main, w_alow_t, w_cast)


def _gla_level_weights(b, la, row):
    ch, dk = b.shape
    out = []
    s = ch // 2
    while s >= 4:
        bb = b.reshape(ch // (2 * s), 2 * s, dk)
        d = (bb - bb[:, s - 1:s, :]).reshape(ch, dk)
        out.append((s, jnp.exp(-jnp.abs(d))))
        s //= 2
    la_prev = pltpu.roll(la, 1, 0)
    la_next = pltpu.roll(la, ch - 1, 0)
    r4 = row & 3
    d2 = jnp.where(r4 == 0, la_next, jnp.where(r4 == 1, 0.0, jnp.where(r4 == 2, la, la + la_prev)))
    out.append((2, jnp.exp(-jnp.abs(d2))))
    d1 = jnp.where((row & 1) == 1, la, 0.0)
    out.append((1, jnp.exp(-jnp.abs(d1))))
    return out


def _gla_kernel(q_ref, k_ref, v_ref, r_ref, a_ref, gw_ref, gb_ref, ng_ref, cin_ref, o_ref, cout_ref, st_scr):
    ch = GLA_CH
    cout_ref[...] = cin_ref[...].astype(BF16)

    @pl.when(pl.program_id(1) == 0)
    def _():
        st_scr[...] = jnp.zeros_like(st_scr)

    row = lax.broadcasted_iota(jnp.int32, (ch, 1), 0)
    col = lax.broadcasted_iota(jnp.int32, (1, ch), 1)
    tril = (row >= col).astype(BF16)
    levels = []
    s = ch // 2
    while s >= 1:
        levels.append((s, ((row ^ col) < 2 * s) & ((row & (2 * s - 1)) >= s) & ((col & (2 * s - 1)) < s)))
        s //= 2
    gw_hi, gw_lo = _split2(gw_ref[...])
    gate_b = gb_ref[...]
    norm_g = ng_ref[...]
    q_scale = jnp.asarray(GLA_DK ** -0.5, BF16)

    def head(sl, hh, a_hi, a_lo):
        ks = slice(hh * GLA_DK, (hh + 1) * GLA_DK)
        vs = slice(hh * GLA_DV, (hh + 1) * GLA_DV)
        qb = q_ref[sl, ks] * q_scale
        kb = k_ref[sl, ks]
        q, k = qb.astype(F32), kb.astype(F32)
        v = v_ref[sl, vs]
        z = _dot(a_hi, gw_hi[:, ks]) + _dot(a_lo, gw_hi[:, ks]) + _dot(a_hi, gw_lo[:, ks]) + gate_b[:, ks]
        la = (jnp.minimum(z, 0.0) - jnp.log(1.0 + jnp.exp(-jnp.abs(z)))) * (1.0 / GLA_GATE_TEMP)
        l_hi, l_mid, l_lo = _split3(la)
        b = _dot(tril, l_hi) + _dot(tril, l_mid) + _dot(tril, l_lo)

        scores = jnp.where(row == col, _dot_nt(qb, kb), 0.0)
        for (s, mask), (_, w) in zip(levels, _gla_level_weights(b, la, row)):
            scores = jnp.where(mask, _dot_nt((q * w).astype(BF16), (k * w).astype(BF16)), scores)
        o = _dot(scores.astype(BF16), v)

        st = st_scr[hh]
        o = o + _dot_nt((q * jnp.exp(b)).astype(BF16), st.astype(BF16))
        b_last = b[ch - 1:ch, :]
        kd = (k * jnp.exp(b_last - b)).astype(BF16)
        st_scr[hh] = st * jnp.exp(b_last) + _dot_tn(v, kd)

        o = _rms(o, norm_g)
        r = r_ref[sl, vs].astype(F32)
        o_ref[sl, vs] = (o * (r * _sigmoid(r))).astype(BF16)

    def chunk(c, carry):
        sl = pl.ds(pl.multiple_of(c * ch, ch), ch)
        a_hi, a_lo = _split2(a_ref[sl, :])
        for hh in range(GLA_HG):
            head(sl, hh, a_hi, a_lo)
        return carry

    lax.fori_loop(0, GLA_RB // ch, chunk, 0)


def _gla(p, a_low, gate_w_pad, gate_b, norm_g, w_cast):
    T = p.shape[0]
    ni = T // GLA_RB
    ng = GLA_HEADS // GLA_HG
    kw, vw = GLA_HG * GLA_DK, GLA_HG * GLA_DV
    cast_spec = _side_cast_spec(w_cast, ng * ni, lambda h, i: h * ni + i)
    return pl.pallas_call(
        _gla_kernel,
        grid=(ng, ni),
        in_specs=[
            pl.BlockSpec((GLA_RB, kw), lambda h, i: (i, h)),
            pl.BlockSpec((GLA_RB, kw), lambda h, i: (i, ng + h)),
            pl.BlockSpec((GLA_RB, vw), lambda h, i: (i, ng + h)),
            pl.BlockSpec((GLA_RB, vw), lambda h, i: (i, 2 * ng + h)),
            pl.BlockSpec((GLA_RB, LANES), lambda h, i: (i, 0)),
            pl.BlockSpec((LANES, kw), lambda h, i: (0, h)),
            pl.BlockSpec((1, kw), lambda h, i: (0, h)),
            pl.BlockSpec((1, GLA_DV), lambda h, i: (0, 0)),
            cast_spec,
        ],
        out_specs=[pl.BlockSpec((GLA_RB, vw), lambda h, i: (i, h)), cast_spec],
        out_shape=[jax.ShapeDtypeStruct((T, GLA_HEADS * GLA_DV), BF16),
                   jax.ShapeDtypeStruct(w_cast.shape, BF16)],
        scratch_shapes=[pltpu.VMEM((GLA_HG, GLA_DV, GLA_DK), F32)],
        compiler_params=_params(("arbitrary", "arbitrary")),
        name="gla",
    )(p, p, p, p, a_low, gate_w_pad, gate_b, norm_g, w_cast)


def _merge_kernel(og_ref, cb_ref, cc_ref, ch_ref, ccp_ref, chp_ref, cw_ref, cbias_ref,
                  ga_ref, gb_ref, wa_ref, wb_ref, wm_ref, x_ref, o_ref):
    tm = cc_ref.shape[0]
    u = cc_ref[...].astype(F32) * ch_ref[...].astype(F32)
    up = ccp_ref[...].astype(F32) * chp_ref[...].astype(F32)
    up = jnp.where(pl.program_id(0) == 0, 0.0, up)
    hp = up.shape[0]
    row = lax.broadcasted_iota(jnp.int32, (tm, 1), 0)
    u1 = jnp.where(row == 0, up[hp - 1:hp, :], pltpu.roll(u, 1, 0))
    u2 = jnp.where(row == 0, up[hp - 2:hp - 1, :],
                   jnp.where(row == 1, up[hp - 1:hp, :], pltpu.roll(u, 2, 0)))
    cw = cw_ref[...]
    uc = cw[0:1, :] * u2 + cw[1:2, :] * u1 + cw[2:3, :] * u + cbias_ref[...]
    zb = (cb_ref[...].astype(F32) * uc).astype(BF16)

    ya = _dot(og_ref[...], wa_ref[...])
    yb = _dot(zb, wb_ref[...])
    merged = (_sigmoid(ga_ref[...].astype(F32)) * ya + _sigmoid(gb_ref[...].astype(F32)) * yb).astype(BF16)
    o_ref[...] = x_ref[...] + _dot(merged, wm_ref[...])


def _merge(og, p, conv_w8, conv_b, gla_out_w, conv_out_w, mix_out_w, x):
    T = og.shape[0]
    halo = 16
    prev = lambda blk: (lambda i: (jnp.maximum(i * (MG_TM // halo) - 1, 0), blk))
    pblk = lambda blk: pl.BlockSpec((MG_TM, D_MODEL), lambda i: (i, blk))
    const = lambda shape: pl.BlockSpec(shape, lambda i: (0, 0), pipeline_mode=pl.Buffered(1))
    return pl.pallas_call(
        _merge_kernel,
        grid=(T // MG_TM,),
        in_specs=[
            pblk(0),
            pblk(3), pblk(4), pblk(5),
            pl.BlockSpec((halo, D_MODEL), prev(4)),
            pl.BlockSpec((halo, D_MODEL), prev(5)),
            const((8, D_MODEL)),
            const((1, D_MODEL)),
            pblk(6), pblk(7),
            const((D_MODEL, D_MODEL)), const((D_MODEL, D_MODEL)), const((D_MODEL, D_MODEL)),
            pblk(0),
        ],
        out_specs=pblk(0),
        out_shape=jax.ShapeDtypeStruct((T, D_MODEL), F32),
        compiler_params=_params(("arbitrary",)),
        name="merge",
    )(og, p, p, p, p, p, conv_w8, conv_b, p, p, gla_out_w, conv_out_w, mix_out_w, x)


def _memkv_kernel(m_ref, g_ref, wk_ref, wv_ref, k_ref, v_ref):
    mn = _rms(m_ref[...], g_ref[...]).astype(BF16)
    k_ref[...] = _dot(mn, wk_ref[...]).astype(BF16)
    v_ref[...] = _dot(mn, wv_ref[...]).astype(BF16)


def _mem_kv(mem, g, wk, wv):
    return pl.pallas_call(
        _memkv_kernel,
        out_shape=[jax.ShapeDtypeStruct((MEM_LEN, XATTN_DIM), BF16)] * 2,
        compiler_params=pltpu.CompilerParams(vmem_limit_bytes=VMEM_LIMIT),
        name="mem_kv",
    )(mem, g, wk, wv)


def _xattn_tile(x_ref, g_ref, wq_ref, k_ref, v_ref, wo_ref):
    x = x_ref[...]
    hq = _rms(x, g_ref[...]).astype(BF16)
    q = (_dot(hq, wq_ref[...]) * (XATTN_HEAD_DIM ** -0.5)).astype(BF16)
    outs = []
    for h in range(XATTN_HEADS):
        cs = slice(h * XATTN_HEAD_DIM, (h + 1) * XATTN_HEAD_DIM)
        s = _dot_nt(q[:, cs], k_ref[:, cs])
        s = s - jnp.max(s, axis=-1, keepdims=True)
        e = jnp.exp(s)
        p = e / jnp.sum(e, axis=-1, keepdims=True)
        outs.append(_dot(p.astype(BF16), v_ref[:, cs]))
    o = jnp.concatenate(outs, axis=-1).astype(BF16)
    return x + _dot(o, wo_ref[...])


def _xattn_router_kernel(x_ref, gx_ref, wq_ref, k_ref, v_ref, wo_ref, gm_ref, rw_ref, rb_ref,
                         x2_ref, hm_ref, idx_ref, gate_ref, rank_ref, cnt_ref, carry_scr):
    x2 = _xattn_tile(x_ref, gx_ref, wq_ref, k_ref, v_ref, wo_ref)
    x2_ref[...] = x2
    _route_tile(x2, gm_ref, rw_ref, rb_ref, hm_ref, idx_ref, gate_ref, rank_ref, cnt_ref, carry_scr)


def _route_tile(x2, g_ref, w_ref, b_ref, hm_ref, idx_ref, gate_ref, rank_ref, cnt_ref, carry_scr):
    tm = x2.shape[0]

    @pl.when(pl.program_id(0) == 0)
    def _():
        carry_scr[...] = jnp.zeros_like(carry_scr)

    h = _rms(x2, g_ref[...])
    hb = h.astype(BF16)
    hm_ref[...] = _pack_pairs(h)
    lane = lax.broadcasted_iota(jnp.int32, (tm, LANES), 1)
    logits = jnp.where(lane < N_EXPERTS, _dot(hb, w_ref[...]) + b_ref[...], -jnp.inf)

    vals, idxs = [], []
    work = logits
    lane_f = lane.astype(F32)
    for _ in range(TOP_K):
        m = jnp.max(work, axis=-1, keepdims=True)
        am = jnp.min(jnp.where(work == m, lane_f, float(LANES)), axis=-1, keepdims=True).astype(I32)
        vals.append(m)
        idxs.append(am)
        work = jnp.where(lane == am, -jnp.inf, work)
    es = [jnp.exp(v - vals[0]) for v in vals]
    inv = 1.0 / (es[0] + es[1] + es[2] + es[3])

    onehot = (work != logits).astype(BF16)
    r_ = lax.broadcasted_iota(jnp.int32, (tm, tm), 0)
    c_ = lax.broadcasted_iota(jnp.int32, (tm, tm), 1)
    before = _dot((r_ > c_).astype(BF16), onehot) + carry_scr[...]
    carry_scr[...] = carry_scr[...] + jnp.sum(onehot.astype(F32), axis=0, keepdims=True)
    cnt_ref[...] = carry_scr[...]

    idx_out = jnp.zeros((tm, LANES), jnp.int32)
    gate_out = jnp.zeros((tm, LANES), F32)
    rank_out = jnp.zeros((tm, LANES), jnp.int32)
    for kk in range(TOP_K):
        rk = jnp.sum(jnp.where(lane == idxs[kk], before, 0.0), axis=-1, keepdims=True)
        idx_out = jnp.where(lane == kk, idxs[kk], idx_out)
        gate_out = jnp.where(lane == kk, es[kk] * inv, gate_out)
        rank_out = jnp.where(lane == kk, rk.astype(jnp.int32), rank_out)
    idx_ref[...] = idx_out
    gate_ref[...] = gate_out
    rank_ref[...] = rank_out


def _xattn_router(x1, gx, wq, kmem, vmem, wo, gm, rw_pad, rb_pad):
    T = x1.shape[0]
    full = lambda shape: pl.BlockSpec(shape, lambda i: (0, 0))
    row_blk = pl.BlockSpec((XA_TM, LANES), lambda i: (i, 0))
    return pl.pallas_call(
        _xattn_router_kernel,
        grid=(T // XA_TM,),
        in_specs=[
            pl.BlockSpec((XA_TM, D_MODEL), lambda i: (i, 0)),
            full((1, D_MODEL)),
            full((D_MODEL, XATTN_DIM)),
            full((MEM_LEN, XATTN_DIM)),
            full((MEM_LEN, XATTN_DIM)),
            full((XATTN_DIM, D_MODEL)),
            full((1, D_MODEL)),
            full((D_MODEL, LANES)),
            full((1, LANES)),
        ],
        out_specs=[
            pl.BlockSpec((XA_TM, D_MODEL), lambda i: (i, 0)),
            pl.BlockSpec((XA_TM, PACK_W), lambda i: (i, 0)),
            row_blk, row_blk, row_blk,
            full((1, LANES)),
        ],
        out_shape=[
            jax.ShapeDtypeStruct((T, D_MODEL), F32),
            jax.ShapeDtypeStruct((T, PACK_W), U32),
            jax.ShapeDtypeStruct((T, LANES), jnp.int32),
            jax.ShapeDtypeStruct((T, LANES), F32),
            jax.ShapeDtypeStruct((T, LANES), jnp.int32),
            jax.ShapeDtypeStruct((1, LANES), F32),
        ],
        scratch_shapes=[pltpu.VMEM((1, LANES), F32)],
        compiler_params=_params(("arbitrary",)),
        name="xattn_router",
    )(x1, gx, wq, kmem, vmem, wo, gm, rw_pad, rb_pad)


def _dispatch_kernel(idx_ref, rank_ref, pstart_ref, dest_ref):
    tm = idx_ref.shape[0]
    lane = lax.broadcasted_iota(I32, (tm, LANES), 1)
    idx = idx_ref[...]
    pstart = pstart_ref[...]
    dest = rank_ref[...]
    for kk in range(TOP_K):
        start = jnp.sum(jnp.where(lane == idx[:, kk:kk + 1], pstart, 0.0), axis=-1, keepdims=True)
        dest = dest + jnp.where(lane == kk, start.astype(I32), 0)
    dest_ref[...] = dest


def _dispatch(idx, rank, pstart_row):
    T = idx.shape[0]
    blk = pl.BlockSpec((DP_TM, LANES), lambda i: (i, 0))
    return pl.pallas_call(
        _dispatch_kernel,
        grid=(T // DP_TM,),
        in_specs=[blk, blk, pl.BlockSpec((1, LANES), lambda i: (0, 0))],
        out_specs=blk,
        out_shape=jax.ShapeDtypeStruct((T, LANES), I32),
        compiler_params=_params(("arbitrary",)),
        name="dispatch",
    )(idx, rank, pstart_row)


def _sc_mesh():
    return plsc.VectorSubcoreMesh(core_axis_name="c", subcore_axis_name="s",
                                  num_cores=SC_CORES, num_subcores=SC_SUBCORES)


def _sc_gather_rows(table_hbm, idx_v, out_hbm, out_base, n_steps, rows_v, gsem, wsem):
    nbuf = SC_NBUF

    def gather(s, b):
        return pltpu.make_async_copy(table_hbm.at[idx_v.at[pl.ds(s * SC_GG, SC_GG)]], rows_v.at[b], gsem.at[b])

    def write(s, b):
        return pltpu.make_async_copy(rows_v.at[b], out_hbm.at[pl.ds(out_base + s * SC_GG, SC_GG)], wsem.at[b])

    for j in range(nbuf - 1):
        gather(j, j).start()

    @pl.loop(0, n_steps, step=nbuf)
    def _(s0):
        for b in range(nbuf):
            s = s0 + b
            gather(s, b).wait()
            write(s, b).start()
            refill = (b + nbuf - 1) % nbuf

            @pl.when(s + nbuf - 1 < n_steps)
            def _():
                @pl.when(s >= 1)
                def _():
                    write(s - 1, refill).wait()
                gather(s + nbuf - 1, refill).start()

    for j in range(nbuf):
        write(n_steps - nbuf + j, j).wait()


def _sc_dispatch_scatter(table, dest_chunks, n_slots):
    T, width = table.shape
    n_workers = SC_CORES * SC_SUBCORES
    per_w = T // n_workers
    n_steps = per_w // SC_G
    assert T % n_workers == 0 and per_w % (2 * SC_G) == 0
    assert dest_chunks.shape == (T // SC_G, TOP_K, SC_G)

    @functools.partial(
        pl.kernel, mesh=_sc_mesh(),
        out_type=jax.ShapeDtypeStruct((n_slots, width), table.dtype),
        scratch_types=[
            pltpu.VMEM((n_steps, TOP_K, SC_G), I32),
            pltpu.VMEM((2, SC_G, width), table.dtype),
            pltpu.SemaphoreType.DMA((2,)),
            pltpu.SemaphoreType.DMA((2,)),
        ],
        name="sc_dispatch_scatter",
    )
    def k(table_hbm, dest_hbm, out_hbm, dst_v, rows_v, rsem, wsem):
        wid = lax.axis_index("s") * SC_CORES + lax.axis_index("c")
        pltpu.sync_copy(dest_hbm.at[pl.ds(wid * n_steps, n_steps)], dst_v)

        def read(s, b):
            return pltpu.make_async_copy(table_hbm.at[pl.ds(wid * per_w + s * SC_G, SC_G)], rows_v.at[b], rsem.at[b])

        def scatter(s, kk, b):
            return pltpu.make_async_copy(rows_v.at[b], out_hbm.at[dst_v.at[s, kk]], wsem.at[b])

        read(0, 0).start()

        @pl.loop(0, n_steps, step=2)
        def _(s0):
            for b in range(2):
                s = s0 + b
                read(s, b).wait()

                @pl.when(s + 1 < n_steps)
                def _():
                    @pl.when(s >= 1)
                    def _():
                        for kk in range(TOP_K):
                            scatter(s - 1, kk, 1 - b).wait()
                    read(s + 1, 1 - b).start()

                for kk in range(TOP_K):
                    scatter(s, kk, b).start()

        for kk in range(TOP_K):
            scatter(n_steps - 2, kk, 0).wait()
            scatter(n_steps - 1, kk, 1).wait()

    return k(table, dest_chunks)


def _sc_combine_gather(table, dest_flat):
    n_rows, width = dest_flat.shape[0], table.shape[1]
    n_workers = SC_CORES * SC_SUBCORES
    per_w = n_rows // n_workers
    n_steps = per_w // SC_GG
    assert n_rows % n_workers == 0 and per_w % (SC_NBUF * SC_GG) == 0

    @functools.partial(
        pl.kernel, mesh=_sc_mesh(),
        out_type=jax.ShapeDtypeStruct((n_rows, width), table.dtype),
        scratch_types=[
            pltpu.VMEM((per_w,), I32),
            pltpu.VMEM((SC_NBUF, SC_GG, width), table.dtype),
            pltpu.SemaphoreType.DMA((SC_NBUF,)),
            pltpu.SemaphoreType.DMA((SC_NBUF,)),
        ],
        name="sc_combine_gather",
    )
    def k(table_hbm, dest_hbm, out_hbm, dst_v, rows_v, gsem, wsem):
        wid = lax.axis_index("s") * SC_CORES + lax.axis_index("c")
        lo = wid * per_w
        pltpu.sync_copy(dest_hbm.at[pl.ds(lo, per_w)], dst_v)
        _sc_gather_rows(table_hbm, dst_v, out_hbm, lo, n_steps, rows_v, gsem, wsem)

    return k(table, dest_flat)


def _expert_kernel(be_ref, nb_ref, nv_ref, x_ref, wg_ref, wu_ref, bg_ref, bu_ref, wd_ref, prev_ref, y_ref, *, last):
    i = pl.program_id(0)

    @pl.when(i < nb_ref[0])
    def _():
        row = lax.broadcasted_iota(I32, (x_ref.shape[0], 1), 0)
        x = _unpack_pairs(jnp.where(row < nv_ref[i], x_ref[...], jnp.uint32(0))).astype(BF16)
        gate = jnp.minimum(_dot(x, wg_ref[...]) + bg_ref[...], SWIGLU_LIMIT)
        up = jnp.clip(_dot(x, wu_ref[...]) + bu_ref[...], -SWIGLU_LIMIT, SWIGLU_LIMIT)
        act = (up + 1.0) * gate * _sigmoid(SWIGLU_ALPHA * gate)
        y = prev_ref[...] + _dot(act.astype(BF16), wd_ref[...])
        y_ref[...] = _pack_pairs(y) if last else y


def _expert_pass(c, block_e, nb_used, n_valid, x_sorted, gu_w, gu_b, down_w, prev):
    P = x_sorted.shape[0]
    nc = D_FF // MOE_TF
    first, last = c == 0, c == nc - 1
    rowblk = lambda i, be, nbu, nv: (jnp.minimum(i, nbu[0] - 1), 0)
    prev_spec = (pl.BlockSpec((None, 1, D_MODEL), lambda i, be, nbu, nv: (be[i], 0, 0)) if first
                 else pl.BlockSpec((MOE_TM, D_MODEL), rowblk))
    grid_spec = pltpu.PrefetchScalarGridSpec(
        num_scalar_prefetch=3,
        grid=(P // MOE_TM,),
        in_specs=[
            pl.BlockSpec((MOE_TM, PACK_W), rowblk),
            pl.BlockSpec((None, D_MODEL, MOE_TF), lambda i, be, nbu, nv: (be[i], 0, c)),
            pl.BlockSpec((None, D_MODEL, MOE_TF), lambda i, be, nbu, nv: (be[i], 0, nc + c)),
            pl.BlockSpec((None, 1, MOE_TF), lambda i, be, nbu, nv: (be[i], 0, c)),
            pl.BlockSpec((None, 1, MOE_TF), lambda i, be, nbu, nv: (be[i], 0, nc + c)),
            pl.BlockSpec((None, MOE_TF, D_MODEL), lambda i, be, nbu, nv: (be[i], c, 0)),
            prev_spec,
        ],
        out_specs=pl.BlockSpec((MOE_TM, PACK_W if last else D_MODEL), rowblk),
    )
    return pl.pallas_call(
        functools.partial(_expert_kernel, last=last),
        grid_spec=grid_spec,
        out_shape=jax.ShapeDtypeStruct((P, PACK_W), U32) if last else jax.ShapeDtypeStruct((P, D_MODEL), F32),
        compiler_params=_params(("arbitrary",)),
        name="experts_%d" % c,
    )(block_e, nb_used, n_valid, x_sorted, gu_w, gu_w, gu_b, gu_b, down_w, prev)


def _combine_kernel(x_ref, y_ref, gate_ref, g_ref, o_ref):
    acc = x_ref[...]
    gates = gate_ref[...]
    for kk in range(TOP_K):
        acc = acc + gates[:, kk:kk + 1] * _unpack_pairs(y_ref[kk])
    o_ref[...] = _rms(acc, g_ref[...])


def _combine(x2, y_g, gates, g):
    T = x2.shape[0]
    return pl.pallas_call(
        _combine_kernel,
        grid=(T // CB_TM,),
        in_specs=[
            pl.BlockSpec((CB_TM, D_MODEL), lambda i: (i, 0)),
            pl.BlockSpec((TOP_K, CB_TM, PACK_W), lambda i: (0, i, 0)),
            pl.BlockSpec((CB_TM, LANES), lambda i: (i, 0)),
            pl.BlockSpec((1, D_MODEL), lambda i: (0, 0)),
        ],
        out_specs=pl.BlockSpec((CB_TM, D_MODEL), lambda i: (i, 0)),
        out_shape=jax.ShapeDtypeStruct((T, D_MODEL), F32),
        compiler_params=_params(("arbitrary",)),
        name="combine",
    )(x2, y_g, gates, g)


def _row(v):
    return v.reshape(1, -1).astype(F32)


def _layer(x, mem, norm_mix_g, w_in, gla_gate_w, gla_gate_b, gla_norm_g, gla_out_w, conv_w, conv_b,
           conv_out_w, mix_out_w, norm_xattn_g, norm_mem_g, xq_w, xk_w, xv_w, xo_w, norm_moe_g,
           router_w, router_b, gu_w, gu_b, down_w, down_b):
    T = x.shape[0]
    w_in_t = w_in.T
    w_main = _w_in_prep(w_in_t)
    a0 = 3 * D_MODEL
    w_alow_t = jnp.pad(w_in_t[a0:a0 + GLA_GATE_RANK], ((0, LANES - GLA_GATE_RANK), (0, 0)))
    gate_w_pad = jnp.pad(gla_gate_w, ((0, LANES - GLA_GATE_RANK), (0, 0)))
    conv_w8 = jnp.pad(conv_w, ((0, 8 - CONV_WIDTH), (0, 0)))

    p, a_low, gu_wb = _in_proj(x, _row(norm_mix_g), w_main, w_alow_t, gu_w.reshape(-1, 2 * D_FF))
    og, down_wb = _gla(p, a_low, gate_w_pad, _row(gla_gate_b), _row(gla_norm_g), down_w.reshape(-1, D_MODEL))
    gu_wb = gu_wb.reshape(gu_w.shape)
    down_wb = down_wb.reshape(down_w.shape)
    x1 = _merge(og, p, conv_w8, _row(conv_b), gla_out_w.astype(BF16), conv_out_w.astype(BF16),
                mix_out_w.astype(BF16), x)

    kmem, vmem = _mem_kv(mem, _row(norm_mem_g), xk_w.astype(BF16), xv_w.astype(BF16))
    rw = jnp.pad(router_w, ((0, 0), (0, LANES - N_EXPERTS))).astype(BF16)
    rb = jnp.pad(router_b, (0, LANES - N_EXPERTS)).reshape(1, LANES)
    x2, hm, idx, gates, rank, counts = _xattn_router(
        x1, _row(norm_xattn_g), xq_w.astype(BF16), kmem, vmem, xo_w.astype(BF16), _row(norm_moe_g), rw, rb)

    cnt = counts[0, :N_EXPERTS].astype(I32)
    padded = (cnt + MOE_TM - 1) // MOE_TM * MOE_TM
    pend = jnp.cumsum(padded)
    pstart = pend - padded
    pstart_row = jnp.pad(pstart, (0, LANES - N_EXPERTS)).reshape(1, LANES).astype(F32)
    nb = (T * TOP_K) // MOE_TM + N_EXPERTS
    blk_start = (jnp.arange(nb, dtype=I32) * MOE_TM)[:, None]
    block_e = jnp.minimum(jnp.sum(pend[None, :] <= blk_start, axis=1), N_EXPERTS - 1).astype(I32)
    nb_used = (pend[-1] // MOE_TM).astype(I32).reshape(1)
    in_group = (pstart[None, :] <= blk_start) & (blk_start < pend[None, :])
    n_valid = jnp.sum(jnp.where(in_group, jnp.clip((pstart + cnt)[None, :] - blk_start, 0, MOE_TM), 0),
                      axis=1).astype(I32)

    dest = _dispatch(idx, rank, pstart_row)[:, :TOP_K]
    dest_chunks = dest.reshape(T // SC_G, SC_G, TOP_K).transpose(0, 2, 1)
    x_sorted = _sc_dispatch_scatter(hm, dest_chunks, nb * MOE_TM)

    gu_b3 = gu_b.reshape(N_EXPERTS, 1, 2 * D_FF)
    y = down_b.reshape(N_EXPERTS, 1, D_MODEL)
    for c in range(D_FF // MOE_TF):
        y = _expert_pass(c, block_e, nb_used, n_valid, x_sorted, gu_wb, gu_b3, down_wb, y)
    y_g = _sc_combine_gather(y, dest.T.reshape(-1)).reshape(TOP_K, T, PACK_W)
    return x2, y_g, gates


def kernel(x, mem, norm_mix_g, w_in, gla_gate_w, gla_gate_b, gla_norm_g, gla_out_w, conv_w, conv_b, conv_out_w, mix_out_w, norm_xattn_g, norm_mem_g, xq_w, xk_w, xv_w, xo_w, norm_moe_g, router_w, router_b, expert_gu_w, expert_gu_b, expert_down_w, expert_down_b, norm_final_g):
    assert x.shape[0] == 1 and mem.shape[0] == 1 and w_in.shape[0] == 1
    x2, y_g, gates = _layer(
        x[0], mem[0], norm_mix_g[0], w_in[0], gla_gate_w[0], gla_gate_b[0], gla_norm_g[0], gla_out_w[0],
        conv_w[0], conv_b[0], conv_out_w[0], mix_out_w[0], norm_xattn_g[0], norm_mem_g[0], xq_w[0], xk_w[0],
        xv_w[0], xo_w[0], norm_moe_g[0], router_w[0], router_b[0], expert_gu_w[0], expert_gu_b[0],
        expert_down_w[0], expert_down_b[0])
    out = _combine(x2, y_g, gates, _row(norm_final_g))
    return out[None]
```

```python
import functools

import jax
import jax.numpy as jnp
from jax import lax
from jax.experimental import pallas as pl
from jax.experimental.pallas import tpu as pltpu
from jax.experimental.pallas import tpu_sc as plsc

F32 = jnp.float32
BF16 = jnp.bfloat16
I32 = jnp.int32
U32 = jnp.uint32

D_MODEL = 2048
MEM_LEN = 256
GLA_HEADS = 4
GLA_DK = 256
GLA_DV = 512
GLA_GATE_RANK = 16
GLA_GATE_TEMP = 16.0
CONV_WIDTH = 3
XATTN_HEADS = 4
XATTN_HEAD_DIM = 128
XATTN_DIM = XATTN_HEADS * XATTN_HEAD_DIM
N_EXPERTS = 32
TOP_K = 4
D_FF = D_MODEL
SWIGLU_LIMIT = 7.0
SWIGLU_ALPHA = 1.702
NORM_EPS = 1e-5

LANES = 128
SC_CORES, SC_SUBCORES, SC_LANES = 2, 16, 16
PACK_W = D_MODEL // 2
P_COLS = 8 * D_MODEL
VMEM_LIMIT = 56 * 1024 * 1024

IN_TM, IN_TN = 1024, 1024
GLA_CH = 128
GLA_RB = 512
GLA_HG = 2
MG_TM = 256
XA_TM = 512
DP_TM = 2048
MOE_TM = 512
MOE_TS = 128
MOE_TF = 1024
CB_TM = 512
SC_G = 32
SC_GG, SC_NBUF = 16, 4


def _params(sem):
    return pltpu.CompilerParams(dimension_semantics=sem, vmem_limit_bytes=VMEM_LIMIT)


def _rms(x, g):
    return x * lax.rsqrt(jnp.mean(x * x, axis=-1, keepdims=True) + NORM_EPS) * g


def _dot(a, b):
    return jnp.dot(a, b, preferred_element_type=F32)


def _dot_nt(a, b):
    return lax.dot_general(a, b, (((1,), (1,)), ((), ())), preferred_element_type=F32)


def _dot_tn(a, b):
    return lax.dot_general(a, b, (((0,), (0,)), ((), ())), preferred_element_type=F32)


def _split2(x):
    hi = x.astype(BF16)
    lo = (x - hi.astype(F32)).astype(BF16)
    return hi, lo


def _split3(x):
    hi = x.astype(BF16)
    r = x - hi.astype(F32)
    mid = r.astype(BF16)
    lo = (r - mid.astype(F32)).astype(BF16)
    return hi, mid, lo


def _sigmoid(x):
    return 1.0 / (1.0 + jnp.exp(-x))


def _pack_pairs(x):
    w = x.shape[1] // 2
    u = lax.bitcast_convert_type(x.astype(BF16).astype(F32), U32)
    return (u[:, :w] >> 16) | u[:, w:]


def _unpack_pairs(u):
    lo = lax.bitcast_convert_type(u << 16, F32)
    hi = lax.bitcast_convert_type(u & jnp.uint32(0xFFFF0000), F32)
    return jnp.concatenate([lo, hi], axis=1)


def _side_cast_spec(w2d, n_steps, step_of):
    rows, cols = w2d.shape
    assert rows % n_steps == 0
    return pl.BlockSpec((rows // n_steps, cols), lambda *ids: (step_of(*ids), 0))


def _winprep_kernel(wt_ref, main_ref):
    main_ref[...] = wt_ref[...].T.astype(BF16)


def _w_in_prep(w_in_t):
    rows, d = w_in_t.shape
    a0 = 3 * D_MODEL
    assert rows == P_COLS + GLA_GATE_RANK and a0 % IN_TN == 0
    n_left = a0 // IN_TN
    assert IN_TN % GLA_GATE_RANK == 0
    row0 = lambda j: ((j * (IN_TN // GLA_GATE_RANK) + (j >= n_left).astype(I32)) * GLA_GATE_RANK, 0)
    return pl.pallas_call(
        _winprep_kernel,
        grid=(P_COLS // IN_TN,),
        in_specs=[pl.BlockSpec((pl.Element(IN_TN), pl.Element(d)), row0)],
        out_specs=pl.BlockSpec((d, IN_TN), lambda j: (0, j)),
        out_shape=jax.ShapeDtypeStruct((d, P_COLS), BF16),
        compiler_params=_params(("arbitrary",)),
        name="w_in_prep",
    )(w_in_t)


def _inproj_kernel(x_ref, g_ref, w_ref, wa_ref, cin_ref, p_ref, a_ref, cout_ref, h_scr):
    @pl.when(pl.program_id(1) == 0)
    def _():
        hb = _rms(x_ref[...], g_ref[...]).astype(BF16)
        h_scr[...] = hb
        a_ref[...] = _dot_nt(hb, wa_ref[...].astype(BF16))

    p_ref[...] = _dot(h_scr[...], w_ref[...]).astype(BF16)
    cout_ref[...] = cin_ref[...].astype(BF16)


def _in_proj(x, g, w_main, w_alow_t, w_cast):
    T = x.shape[0]
    ni, nj = T // IN_TM, P_COLS // IN_TN
    cast_spec = _side_cast_spec(w_cast, ni * nj, lambda i, j: i * nj + j)
    return pl.pallas_call(
        _inproj_kernel,
        grid=(ni, nj),
        in_specs=[
            pl.BlockSpec((IN_TM, D_MODEL), lambda i, j: (i, 0)),
            pl.BlockSpec((1, D_MODEL), lambda i, j: (0, 0)),
            pl.BlockSpec((D_MODEL, IN_TN), lambda i, j: (0, j)),
            pl.BlockSpec((LANES, D_MODEL), lambda i, j: (0, 0)),
            cast_spec,
        ],
        out_specs=[
            pl.BlockSpec((IN_TM, IN_TN), lambda i, j: (i, j)),
            pl.BlockSpec((IN_TM, LANES), lambda i, j: (i, 0)),
            cast_spec,
        ],
        out_shape=[
            jax.ShapeDtypeStruct((T, P_COLS), BF16),
            jax.ShapeDtypeStruct((T, LANES), F32),
            jax.ShapeDtypeStruct(w_cast.shape, BF16),
        ],
        scratch_shapes=[pltpu.VMEM((IN_TM, D_MODEL), BF16)],
        compiler_params=_params(("arbitrary", "arbitrary")),
        name="in_proj",
    )(x, g, w_main, w_alow_t, w_cast)


def _gla_level_weights(b, la, row):
    ch, dk = b.shape
    out = []
    s = ch // 2
    while s >= 4:
        bb = b.reshape(ch // (2 * s), 2 * s, dk)
        d = (bb - bb[:, s - 1:s, :]).reshape(ch, dk)
        out.append((s, jnp.exp(-jnp.abs(d))))
        s //= 2
    la_prev = pltpu.roll(la, 1, 0)
    la_next = pltpu.roll(la, ch - 1, 0)
    r4 = row & 3
    d2 = jnp.where(r4 == 0, la_next, jnp.where(r4 == 1, 0.0, jnp.where(r4 == 2, la, la + la_prev)))
    out.append((2, jnp.exp(-jnp.abs(d2))))
    d1 = jnp.where((row & 1) == 1, la, 0.0)
    out.append((1, jnp.exp(-jnp.abs(d1))))
    return out


def _gla_kernel(q_ref, k_ref, v_ref, r_ref, a_ref, gw_ref, gb_ref, ng_ref, cin_ref, o_ref, cout_ref, st_scr):
    ch = GLA_CH
    cout_ref[...] = cin_ref[...].astype(BF16)

    @pl.when(pl.program_id(1) == 0)
    def _():
        st_scr[...] = jnp.zeros_like(st_scr)

    row = lax.broadcasted_iota(jnp.int32, (ch, 1), 0)
    col = lax.broadcasted_iota(jnp.int32, (1, ch), 1)
    tril = (row >= col).astype(BF16)
    levels = []
    s = ch // 2
    while s >= 1:
        levels.append((s, ((row ^ col) < 2 * s) & ((row & (2 * s - 1)) >= s) & ((col & (2 * s - 1)) < s)))
        s //= 2
    gw_hi, gw_lo = _split2(gw_ref[...])
    gate_b = gb_ref[...]
    norm_g = ng_ref[...]
    q_scale = jnp.asarray(GLA_DK ** -0.5, BF16)

    def head(sl, hh, a_hi, a_lo):
        ks = slice(hh * GLA_DK, (hh + 1) * GLA_DK)
        vs = slice(hh * GLA_DV, (hh + 1) * GLA_DV)
        qb = q_ref[sl, ks] * q_scale
        kb = k_ref[sl, ks]
        q, k = qb.astype(F32), kb.astype(F32)
        v = v_ref[sl, vs]
        z = _dot(a_hi, gw_hi[:, ks]) + _dot(a_lo, gw_hi[:, ks]) + _dot(a_hi, gw_lo[:, ks]) + gate_b[:, ks]
        la = (jnp.minimum(z, 0.0) - jnp.log(1.0 + jnp.exp(-jnp.abs(z)))) * (1.0 / GLA_GATE_TEMP)
        l_hi, l_mid, l_lo = _split3(la)
        b = _dot(tril, l_hi) + _dot(tril, l_mid) + _dot(tril, l_lo)

        scores = jnp.where(row == col, _dot_nt(qb, kb), 0.0)
        for (s, mask), (_, w) in zip(levels, _gla_level_weights(b, la, row)):
            scores = jnp.where(mask, _dot_nt((q * w).astype(BF16), (k * w).astype(BF16)), scores)
        o = _dot(scores.astype(BF16), v)

        st = st_scr[hh]
        o = o + _dot_nt((q * jnp.exp(b)).astype(BF16), st.astype(BF16))
        b_last = b[ch - 1:ch, :]
        kd = (k * jnp.exp(b_last - b)).astype(BF16)
        st_scr[hh] = st * jnp.exp(b_last) + _dot_tn(v, kd)

        o = _rms(o, norm_g)
        r = r_ref[sl, vs].astype(F32)
        o_ref[sl, vs] = (o * (r * _sigmoid(r))).astype(BF16)

    def chunk(c, carry):
        sl = pl.ds(pl.multiple_of(c * ch, ch), ch)
        a_hi, a_lo = _split2(a_ref[sl, :])
        for hh in range(GLA_HG):
            head(sl, hh, a_hi, a_lo)
        return carry

    lax.fori_loop(0, GLA_RB // ch, chunk, 0)


def _gla(p, a_low, gate_w_pad, gate_b, norm_g, w_cast):
    T = p.shape[0]
    ni = T // GLA_RB
    ng = GLA_HEADS // GLA_HG
    kw, vw = GLA_HG * GLA_DK, GLA_HG * GLA_DV
    cast_spec = _side_cast_spec(w_cast, ng * ni, lambda h, i: h * ni + i)
    return pl.pallas_call(
        _gla_kernel,
        grid=(ng, ni),
        in_specs=[
            pl.BlockSpec((GLA_RB, kw), lambda h, i: (i, h)),
            pl.BlockSpec((GLA_RB, kw), lambda h, i: (i, ng + h)),
            pl.BlockSpec((GLA_RB, vw), lambda h, i: (i, ng + h)),
            pl.BlockSpec((GLA_RB, vw), lambda h, i: (i, 2 * ng + h)),
            pl.BlockSpec((GLA_RB, LANES), lambda h, i: (i, 0)),
            pl.BlockSpec((LANES, kw), lambda h, i: (0, h)),
            pl.BlockSpec((1, kw), lambda h, i: (0, h)),
            pl.BlockSpec((1, GLA_DV), lambda h, i: (0, 0)),
            cast_spec,
        ],
        out_specs=[pl.BlockSpec((GLA_RB, vw), lambda h, i: (i, h)), cast_spec],
        out_shape=[jax.ShapeDtypeStruct((T, GLA_HEADS * GLA_DV), BF16),
                   jax.ShapeDtypeStruct(w_cast.shape, BF16)],
        scratch_shapes=[pltpu.VMEM((GLA_HG, GLA_DV, GLA_DK), F32)],
        compiler_params=_params(("arbitrary", "arbitrary")),
        name="gla",
    )(p, p, p, p, a_low, gate_w_pad, gate_b, norm_g, w_cast)


def _merge_kernel(og_ref, cb_ref, cc_ref, ch_ref, ccp_ref, chp_ref, cw_ref, cbias_ref,
                  ga_ref, gb_ref, wa_ref, wb_ref, wm_ref, x_ref, o_ref):
    tm = cc_ref.shape[0]
    u = cc_ref[...].astype(F32) * ch_ref[...].astype(F32)
    up = ccp_ref[...].astype(F32) * chp_ref[...].astype(F32)
    up = jnp.where(pl.program_id(0) == 0, 0.0, up)
    hp = up.shape[0]
    row = lax.broadcasted_iota(jnp.int32, (tm, 1), 0)
    u1 = jnp.where(row == 0, up[hp - 1:hp, :], pltpu.roll(u, 1, 0))
    u2 = jnp.where(row == 0, up[hp - 2:hp - 1, :],
                   jnp.where(row == 1, up[hp - 1:hp, :], pltpu.roll(u, 2, 0)))
    cw = cw_ref[...]
    uc = cw[0:1, :] * u2 + cw[1:2, :] * u1 + cw[2:3, :] * u + cbias_ref[...]
    zb = (cb_ref[...].astype(F32) * uc).astype(BF16)

    ya = _dot(og_ref[...], wa_ref[...])
    yb = _dot(zb, wb_ref[...])
    merged = (_sigmoid(ga_ref[...].astype(F32)) * ya + _sigmoid(gb_ref[...].astype(F32)) * yb).astype(BF16)
    o_ref[...] = x_ref[...] + _dot(merged, wm_ref[...])


def _merge(og, p, conv_w8, conv_b, gla_out_w, conv_out_w, mix_out_w, x):
    T = og.shape[0]
    halo = 16
    prev = lambda blk: (lambda i: (jnp.maximum(i * (MG_TM // halo) - 1, 0), blk))
    pblk = lambda blk: pl.BlockSpec((MG_TM, D_MODEL), lambda i: (i, blk))
    const = lambda shape: pl.BlockSpec(shape, lambda i: (0, 0), pipeline_mode=pl.Buffered(1))
    return pl.pallas_call(
        _merge_kernel,
        grid=(T // MG_TM,),
        in_specs=[
            pblk(0),
            pblk(3), pblk(4), pblk(5),
            pl.BlockSpec((halo, D_MODEL), prev(4)),
            pl.BlockSpec((halo, D_MODEL), prev(5)),
            const((8, D_MODEL)),
            const((1, D_MODEL)),
            pblk(6), pblk(7),
            const((D_MODEL, D_MODEL)), const((D_MODEL, D_MODEL)), const((D_MODEL, D_MODEL)),
            pblk(0),
        ],
        out_specs=pblk(0),
        out_shape=jax.ShapeDtypeStruct((T, D_MODEL), F32),
        compiler_params=_params(("arbitrary",)),
        name="merge",
    )(og, p, p, p, p, p, conv_w8, conv_b, p, p, gla_out_w, conv_out_w, mix_out_w, x)


def _memkv_kernel(m_ref, g_ref, wk_ref, wv_ref, k_ref, v_ref):
    mn = _rms(m_ref[...], g_ref[...]).astype(BF16)
    k_ref[...] = _dot(mn, wk_ref[...]).astype(BF16)
    v_ref[...] = _dot(mn, wv_ref[...]).astype(BF16)


def _mem_kv(mem, g, wk, wv):
    return pl.pallas_call(
        _memkv_kernel,
        out_shape=[jax.ShapeDtypeStruct((MEM_LEN, XATTN_DIM), BF16)] * 2,
        compiler_params=pltpu.CompilerParams(vmem_limit_bytes=VMEM_LIMIT),
        name="mem_kv",
    )(mem, g, wk, wv)


def _xattn_tile(x_ref, g_ref, wq_ref, k_ref, v_ref, wo_ref):
    x = x_ref[...]
    hq = _rms(x, g_ref[...]).astype(BF16)
    q = (_dot(hq, wq_ref[...]) * (XATTN_HEAD_DIM ** -0.5)).astype(BF16)
    outs = []
    for h in range(XATTN_HEADS):
        cs = slice(h * XATTN_HEAD_DIM, (h + 1) * XATTN_HEAD_DIM)
        s = _dot_nt(q[:, cs], k_ref[:, cs])
        s = s - jnp.max(s, axis=-1, keepdims=True)
        e = jnp.exp(s)
        p = e / jnp.sum(e, axis=-1, keepdims=True)
        outs.append(_dot(p.astype(BF16), v_ref[:, cs]))
    o = jnp.concatenate(outs, axis=-1).astype(BF16)
    return x + _dot(o, wo_ref[...])


def _xattn_router_kernel(x_ref, gx_ref, wq_ref, k_ref, v_ref, wo_ref, gm_ref, rw_ref, rb_ref,
                         x2_ref, hm_ref, idx_ref, gate_ref, rank_ref, cnt_ref, carry_scr):
    x2 = _xattn_tile(x_ref, gx_ref, wq_ref, k_ref, v_ref, wo_ref)
    x2_ref[...] = x2
    _route_tile(x2, gm_ref, rw_ref, rb_ref, hm_ref, idx_ref, gate_ref, rank_ref, cnt_ref, carry_scr)


def _route_tile(x2, g_ref, w_ref, b_ref, hm_ref, idx_ref, gate_ref, rank_ref, cnt_ref, carry_scr):
    tm = x2.shape[0]

    @pl.when(pl.program_id(0) == 0)
    def _():
        carry_scr[...] = jnp.zeros_like(carry_scr)

    h = _rms(x2, g_ref[...])
    hb = h.astype(BF16)
    hm_ref[...] = _pack_pairs(h)
    lane = lax.broadcasted_iota(jnp.int32, (tm, LANES), 1)
    logits = jnp.where(lane < N_EXPERTS, _dot(hb, w_ref[...]) + b_ref[...], -jnp.inf)

    vals, idxs = [], []
    work = logits
    lane_f = lane.astype(F32)
    for _ in range(TOP_K):
        m = jnp.max(work, axis=-1, keepdims=True)
        am = jnp.min(jnp.where(work == m, lane_f, float(LANES)), axis=-1, keepdims=True).astype(I32)
        vals.append(m)
        idxs.append(am)
        work = jnp.where(lane == am, -jnp.inf, work)
    es = [jnp.exp(v - vals[0]) for v in vals]
    inv = 1.0 / (es[0] + es[1] + es[2] + es[3])

    onehot = (work != logits).astype(BF16)
    r_ = lax.broadcasted_iota(jnp.int32, (tm, tm), 0)
    c_ = lax.broadcasted_iota(jnp.int32, (tm, tm), 1)
    before = _dot((r_ > c_).astype(BF16), onehot) + carry_scr[...]
    carry_scr[...] = carry_scr[...] + jnp.sum(onehot.astype(F32), axis=0, keepdims=True)
    cnt_ref[...] = carry_scr[...]

    idx_out = jnp.zeros((tm, LANES), jnp.int32)
    gate_out = jnp.zeros((tm, LANES), F32)
    rank_out = jnp.zeros((tm, LANES), jnp.int32)
    for kk in range(TOP_K):
        rk = jnp.sum(jnp.where(lane == idxs[kk], before, 0.0), axis=-1, keepdims=True)
        idx_out = jnp.where(lane == kk, idxs[kk], idx_out)
        gate_out = jnp.where(lane == kk, es[kk] * inv, gate_out)
        rank_out = jnp.where(lane == kk, rk.astype(jnp.int32), rank_out)
    idx_ref[...] = idx_out
    gate_ref[...] = gate_out
    rank_ref[...] = rank_out


def _xattn_router(x1, gx, wq, kmem, vmem, wo, gm, rw_pad, rb_pad):
    T = x1.shape[0]
    full = lambda shape: pl.BlockSpec(shape, lambda i: (0, 0))
    row_blk = pl.BlockSpec((XA_TM, LANES), lambda i: (i, 0))
    return pl.pallas_call(
        _xattn_router_kernel,
        grid=(T // XA_TM,),
        in_specs=[
            pl.BlockSpec((XA_TM, D_MODEL), lambda i: (i, 0)),
            full((1, D_MODEL)),
            full((D_MODEL, XATTN_DIM)),
            full((MEM_LEN, XATTN_DIM)),
            full((MEM_LEN, XATTN_DIM)),
            full((XATTN_DIM, D_MODEL)),
            full((1, D_MODEL)),
            full((D_MODEL, LANES)),
            full((1, LANES)),
        ],
        out_specs=[
            pl.BlockSpec((XA_TM, D_MODEL), lambda i: (i, 0)),
            pl.BlockSpec((XA_TM, PACK_W), lambda i: (i, 0)),
            row_blk, row_blk, row_blk,
            full((1, LANES)),
        ],
        out_shape=[
            jax.ShapeDtypeStruct((T, D_MODEL), F32),
            jax.ShapeDtypeStruct((T, PACK_W), U32),
            jax.ShapeDtypeStruct((T, LANES), jnp.int32),
            jax.ShapeDtypeStruct((T, LANES), F32),
            jax.ShapeDtypeStruct((T, LANES), jnp.int32),
            jax.ShapeDtypeStruct((1, LANES), F32),
        ],
        scratch_shapes=[pltpu.VMEM((1, LANES), F32)],
        compiler_params=_params(("arbitrary",)),
        name="xattn_router",
    )(x1, gx, wq, kmem, vmem, wo, gm, rw_pad, rb_pad)


def _dispatch_kernel(idx_ref, rank_ref, pstart_ref, dest_ref):
    tm = idx_ref.shape[0]
    lane = lax.broadcasted_iota(I32, (tm, LANES), 1)
    idx = idx_ref[...]
    pstart = pstart_ref[...]
    dest = rank_ref[...]
    for kk in range(TOP_K):
        start = jnp.sum(jnp.where(lane == idx[:, kk:kk + 1], pstart, 0.0), axis=-1, keepdims=True)
        dest = dest + jnp.where(lane == kk, start.astype(I32), 0)
    dest_ref[...] = dest


def _dispatch(idx, rank, pstart_row):
    T = idx.shape[0]
    blk = pl.BlockSpec((DP_TM, LANES), lambda i: (i, 0))
    return pl.pallas_call(
        _dispatch_kernel,
        grid=(T // DP_TM,),
        in_specs=[blk, blk, pl.BlockSpec((1, LANES), lambda i: (0, 0))],
        out_specs=blk,
        out_shape=jax.ShapeDtypeStruct((T, LANES), I32),
        compiler_params=_params(("arbitrary",)),
        name="dispatch",
    )(idx, rank, pstart_row)


def _sc_mesh():
    return plsc.VectorSubcoreMesh(core_axis_name="c", subcore_axis_name="s",
                                  num_cores=SC_CORES, num_subcores=SC_SUBCORES)


def _sc_gather_rows(table_hbm, idx_v, out_hbm, out_base, n_steps, rows_v, gsem, wsem):
    nbuf = SC_NBUF

    def gather(s, b):
        return pltpu.make_async_copy(table_hbm.at[idx_v.at[pl.ds(s * SC_GG, SC_GG)]], rows_v.at[b], gsem.at[b])

    def write(s, b):
        return pltpu.make_async_copy(rows_v.at[b], out_hbm.at[pl.ds(out_base + s * SC_GG, SC_GG)], wsem.at[b])

    for j in range(nbuf - 1):
        gather(j, j).start()

    @pl.loop(0, n_steps, step=nbuf)
    def _(s0):
        for b in range(nbuf):
            s = s0 + b
            gather(s, b).wait()
            write(s, b).start()
            refill = (b + nbuf - 1) % nbuf

            @pl.when(s + nbuf - 1 < n_steps)
            def _():
                @pl.when(s >= 1)
                def _():
                    write(s - 1, refill).wait()
                gather(s + nbuf - 1, refill).start()

    for j in range(nbuf):
        write(n_steps - nbuf + j, j).wait()


def _sc_dispatch_scatter(table, dest_chunks, n_slots):
    T, width = table.shape
    n_workers = SC_CORES * SC_SUBCORES
    per_w = T // n_workers
    n_steps = per_w // SC_G
    assert T % n_workers == 0 and per_w % (2 * SC_G) == 0
    assert dest_chunks.shape == (T // SC_G, TOP_K, SC_G)

    @functools.partial(
        pl.kernel, mesh=_sc_mesh(),
        out_type=jax.ShapeDtypeStruct((n_slots, width), table.dtype),
        scratch_types=[
            pltpu.VMEM((n_steps, TOP_K, SC_G), I32),
            pltpu.VMEM((2, SC_G, width), table.dtype),
            pltpu.SemaphoreType.DMA((2,)),
            pltpu.SemaphoreType.DMA((2,)),
        ],
        name="sc_dispatch_scatter",
    )
    def k(table_hbm, dest_hbm, out_hbm, dst_v, rows_v, rsem, wsem):
        wid = lax.axis_index("s") * SC_CORES + lax.axis_index("c")
        pltpu.sync_copy(dest_hbm.at[pl.ds(wid * n_steps, n_steps)], dst_v)

        def read(s, b):
            return pltpu.make_async_copy(table_hbm.at[pl.ds(wid * per_w + s * SC_G, SC_G)], rows_v.at[b], rsem.at[b])

        def scatter(s, kk, b):
            return pltpu.make_async_copy(rows_v.at[b], out_hbm.at[dst_v.at[s, kk]], wsem.at[b])

        read(0, 0).start()

        @pl.loop(0, n_steps, step=2)
        def _(s0):
            for b in range(2):
                s = s0 + b
                read(s, b).wait()

                @pl.when(s + 1 < n_steps)
                def _():
                    @pl.when(s >= 1)
                    def _():
                        for kk in range(TOP_K):
                            scatter(s - 1, kk, 1 - b).wait()
                    read(s + 1, 1 - b).start()

                for kk in range(TOP_K):
                    scatter(s, kk, b).start()

        for kk in range(TOP_K):
            scatter(n_steps - 2, kk, 0).wait()
            scatter(n_steps - 1, kk, 1).wait()

    return k(table, dest_chunks)


def _sc_combine_gather(table, dest_flat):
    n_rows, width = dest_flat.shape[0], table.shape[1]
    n_workers = SC_CORES * SC_SUBCORES
    per_w = n_rows // n_workers
    n_steps = per_w // SC_GG
    assert n_rows % n_workers == 0 and per_w % (SC_NBUF * SC_GG) == 0

    @functools.partial(
        pl.kernel, mesh=_sc_mesh(),
        out_type=jax.ShapeDtypeStruct((n_rows, width), table.dtype),
        scratch_types=[
            pltpu.VMEM((per_w,), I32),
            pltpu.VMEM((SC_NBUF, SC_GG, width), table.dtype),
            pltpu.SemaphoreType.DMA((SC_NBUF,)),
            pltpu.SemaphoreType.DMA((SC_NBUF,)),
        ],
        name="sc_combine_gather",
    )
    def k(table_hbm, dest_hbm, out_hbm, dst_v, rows_v, gsem, wsem):
        wid = lax.axis_index("s") * SC_CORES + lax.axis_index("c")
        lo = wid * per_w
        pltpu.sync_copy(dest_hbm.at[pl.ds(lo, per_w)], dst_v)
        _sc_gather_rows(table_hbm, dst_v, out_hbm, lo, n_steps, rows_v, gsem, wsem)

    return k(table, dest_flat)


def _expert_kernel(be_ref, nb_ref, nv_ref, x_ref, wg_ref, wu_ref, bg_ref, bu_ref, wd_ref, prev_ref, y_ref, *,
                   first, last):
    i = pl.program_id(0)
    nv = nv_ref[i]
    tm = x_ref.shape[0]

    def rows_path(m):
        row = lax.broadcasted_iota(I32, (m, 1), 0)
        x = _unpack_pairs(jnp.where(row < nv, x_ref[:m, :], jnp.uint32(0))).astype(BF16)
        gate = jnp.minimum(_dot(x, wg_ref[...]) + bg_ref[...], SWIGLU_LIMIT)
        up = jnp.clip(_dot(x, wu_ref[...]) + bu_ref[...], -SWIGLU_LIMIT, SWIGLU_LIMIT)
        act = (up + 1.0) * gate * _sigmoid(SWIGLU_ALPHA * gate)
        prev = prev_ref[...] if first else prev_ref[:m, :]
        y = prev + _dot(act.astype(BF16), wd_ref[...])
        y_ref[:m, :] = _pack_pairs(y) if last else y
        if m < tm:
            y_ref[m:, :] = jnp.zeros((tm - m, y_ref.shape[1]), y_ref.dtype)

    for m in range(MOE_TS, tm + 1, MOE_TS):
        @pl.when((i < nb_ref[0]) & (nv > m - MOE_TS) & (nv <= m))
        def _():
            rows_path(m)


def _expert_pass(c, block_e, nb_used, n_valid, x_sorted, gu_w, gu_b, down_w, prev):
    P = x_sorted.shape[0]
    nc = D_FF // MOE_TF
    first, last = c == 0, c == nc - 1
    rowblk = lambda i, be, nbu, nv: (jnp.minimum(i, nbu[0] - 1), 0)
    prev_spec = (pl.BlockSpec((None, 1, D_MODEL), lambda i, be, nbu, nv: (be[i], 0, 0)) if first
                 else pl.BlockSpec((MOE_TM, D_MODEL), rowblk))
    grid_spec = pltpu.PrefetchScalarGridSpec(
        num_scalar_prefetch=3,
        grid=(P // MOE_TM,),
        in_specs=[
            pl.BlockSpec((MOE_TM, PACK_W), rowblk),
            pl.BlockSpec((None, D_MODEL, MOE_TF), lambda i, be, nbu, nv: (be[i], 0, c)),
            pl.BlockSpec((None, D_MODEL, MOE_TF), lambda i, be, nbu, nv: (be[i], 0, nc + c)),
            pl.BlockSpec((None, 1, MOE_TF), lambda i, be, nbu, nv: (be[i], 0, c)),
            pl.BlockSpec((None, 1, MOE_TF), lambda i, be, nbu, nv: (be[i], 0, nc + c)),
            pl.BlockSpec((None, MOE_TF, D_MODEL), lambda i, be, nbu, nv: (be[i], c, 0)),
            prev_spec,
        ],
        out_specs=pl.BlockSpec((MOE_TM, PACK_W if last else D_MODEL), rowblk),
    )
    return pl.pallas_call(
        functools.partial(_expert_kernel, first=first, last=last),
        grid_spec=grid_spec,
        out_shape=jax.ShapeDtypeStruct((P, PACK_W), U32) if last else jax.ShapeDtypeStruct((P, D_MODEL), F32),
        compiler_params=_params(("arbitrary",)),
        name="experts_%d" % c,
    )(block_e, nb_used, n_valid, x_sorted, gu_w, gu_w, gu_b, gu_b, down_w, prev)


def _combine_kernel(x_ref, y_ref, gate_ref, g_ref, o_ref):
    acc = x_ref[...]
    gates = gate_ref[...]
    for kk in range(TOP_K):
        acc = acc + gates[:, kk:kk + 1] * _unpack_pairs(y_ref[kk])
    o_ref[...] = _rms(acc, g_ref[...])


def _combine(x2, y_g, gates, g):
    T = x2.shape[0]
    return pl.pallas_call(
        _combine_kernel,
        grid=(T // CB_TM,),
        in_specs=[
            pl.BlockSpec((CB_TM, D_MODEL), lambda i: (i, 0)),
            pl.BlockSpec((TOP_K, CB_TM, PACK_W), lambda i: (0, i, 0)),
            pl.BlockSpec((CB_TM, LANES), lambda i: (i, 0)),
            pl.BlockSpec((1, D_MODEL), lambda i: (0, 0)),
        ],
        out_specs=pl.BlockSpec((CB_TM, D_MODEL), lambda i: (i, 0)),
        out_shape=jax.ShapeDtypeStruct((T, D_MODEL), F32),
        compiler_params=_params(("arbitrary",)),
        name="combine",
    )(x2, y_g, gates, g)


def _row(v):
    return v.reshape(1, -1).astype(F32)


def _layer(x, mem, norm_mix_g, w_in, gla_gate_w, gla_gate_b, gla_norm_g, gla_out_w, conv_w, conv_b,
           conv_out_w, mix_out_w, norm_xattn_g, norm_mem_g, xq_w, xk_w, xv_w, xo_w, norm_moe_g,
           router_w, router_b, gu_w, gu_b, down_w, down_b):
    T = x.shape[0]
    w_in_t = w_in.T
    w_main = _w_in_prep(w_in_t)
    a0 = 3 * D_MODEL
    w_alow_t = jnp.pad(w_in_t[a0:a0 + GLA_GATE_RANK], ((0, LANES - GLA_GATE_RANK), (0, 0)))
    gate_w_pad = jnp.pad(gla_gate_w, ((0, LANES - GLA_GATE_RANK), (0, 0)))
    conv_w8 = jnp.pad(conv_w, ((0, 8 - CONV_WIDTH), (0, 0)))

    p, a_low, gu_wb = _in_proj(x, _row(norm_mix_g), w_main, w_alow_t, gu_w.reshape(-1, 2 * D_FF))
    og, down_wb = _gla(p, a_low, gate_w_pad, _row(gla_gate_b), _row(gla_norm_g), down_w.reshape(-1, D_MODEL))
    gu_wb = gu_wb.reshape(gu_w.shape)
    down_wb = down_wb.reshape(down_w.shape)
    x1 = _merge(og, p, conv_w8, _row(conv_b), gla_out_w.astype(BF16), conv_out_w.astype(BF16),
                mix_out_w.astype(BF16), x)

    kmem, vmem = _mem_kv(mem, _row(norm_mem_g), xk_w.astype(BF16), xv_w.astype(BF16))
    rw = jnp.pad(router_w, ((0, 0), (0, LANES - N_EXPERTS))).astype(BF16)
    rb = jnp.pad(router_b, (0, LANES - N_EXPERTS)).reshape(1, LANES)
    x2, hm, idx, gates, rank, counts = _xattn_router(
        x1, _row(norm_xattn_g), xq_w.astype(BF16), kmem, vmem, xo_w.astype(BF16), _row(norm_moe_g), rw, rb)

    cnt = counts[0, :N_EXPERTS].astype(I32)
    padded = (cnt + MOE_TM - 1) // MOE_TM * MOE_TM
    pend = jnp.cumsum(padded)
    pstart = pend - padded
    pstart_row = jnp.pad(pstart, (0, LANES - N_EXPERTS)).reshape(1, LANES).astype(F32)
    nb = (T * TOP_K) // MOE_TM + N_EXPERTS
    blk_start = (jnp.arange(nb, dtype=I32) * MOE_TM)[:, None]
    block_e = jnp.minimum(jnp.sum(pend[None, :] <= blk_start, axis=1), N_EXPERTS - 1).astype(I32)
    nb_used = (pend[-1] // MOE_TM).astype(I32).reshape(1)
    in_group = (pstart[None, :] <= blk_start) & (blk_start < pend[None, :])
    n_valid = jnp.sum(jnp.where(in_group, jnp.clip((pstart + cnt)[None, :] - blk_start, 0, MOE_TM), 0),
                      axis=1).astype(I32)

    dest = _dispatch(idx, rank, pstart_row)[:, :TOP_K]
    dest_chunks = dest.reshape(T // SC_G, SC_G, TOP_K).transpose(0, 2, 1)
    x_sorted = _sc_dispatch_scatter(hm, dest_chunks, nb * MOE_TM)

    gu_b3 = gu_b.reshape(N_EXPERTS, 1, 2 * D_FF)
    y = down_b.reshape(N_EXPERTS, 1, D_MODEL)
    for c in range(D_FF // MOE_TF):
        y = _expert_pass(c, block_e, nb_used, n_valid, x_sorted, gu_wb, gu_b3, down_wb, y)
    y_g = _sc_combine_gather(y, dest.T.reshape(-1)).reshape(TOP_K, T, PACK_W)
    return x2, y_g, gates


def kernel(x, mem, norm_mix_g, w_in, gla_gate_w, gla_gate_b, gla_norm_g, gla_out_w, conv_w, conv_b, conv_out_w, mix_out_w, norm_xattn_g, norm_mem_g, xq_w, xk_w, xv_w, xo_w, norm_moe_g, router_w, router_b, expert_gu_w, expert_gu_b, expert_down_w, expert_down_b, norm_final_g):
    assert x.shape[0] == 1 and mem.shape[0] == 1 and w_in.shape[0] == 1
    x2, y_g, gates = _layer(
        x[0], mem[0], norm_mix_g[0], w_in[0], gla_gate_w[0], gla_gate_b[0], gla_norm_g[0], gla_out_w[0],
        conv_w[0], conv_b[0], conv_out_w[0], mix_out_w[0], norm_xattn_g[0], norm_mem_g[0], xq_w[0], xk_w[0],
        xv_w[0], xo_w[0], norm_moe_g[0], router_w[0], router_b[0], expert_gu_w[0], expert_gu_b[0],
        expert_down_w[0], expert_down_b[0])
    out = _combine(x2, y_g, gates, _row(norm_final_g))
    return out[None]
```

```python
import functools

import jax
import jax.numpy as jnp
from jax import lax
from jax.experimental import pallas as pl
from jax.experimental.pallas import tpu as pltpu
from jax.experimental.pallas import tpu_sc as plsc

F32 = jnp.float32
BF16 = jnp.bfloat16
I32 = jnp.int32
U32 = jnp.uint32

D_MODEL = 2048
MEM_LEN = 256
GLA_HEADS = 4
GLA_DK = 256
GLA_DV = 512
GLA_GATE_RANK = 16
GLA_GATE_TEMP = 16.0
CONV_WIDTH = 3
XATTN_HEADS = 4
XATTN_HEAD_DIM = 128
XATTN_DIM = XATTN_HEADS * XATTN_HEAD_DIM
N_EXPERTS = 32
TOP_K = 4
D_FF = D_MODEL
SWIGLU_LIMIT = 7.0
SWIGLU_ALPHA = 1.702
NORM_EPS = 1e-5

LANES = 128
SC_CORES, SC_SUBCORES, SC_LANES = 2, 16, 16
PACK_W = D_MODEL // 2
P_COLS = 8 * D_MODEL
VMEM_LIMIT = 56 * 1024 * 1024

IN_TM, IN_TN = 1024, 1024
GLA_CH = 128
GLA_RB = 512
GLA_HG = 2
MG_TM = 512
XA_TM = 512
DP_TM = 2048
MOE_TM = 512
MOE_TF = 1024
CB_TM = 512
SC_G = 32
SC_GG, SC_NBUF = 16, 4


def _params(sem):
    return pltpu.CompilerParams(dimension_semantics=sem, vmem_limit_bytes=VMEM_LIMIT)


def _rms(x, g):
    return x * lax.rsqrt(jnp.mean(x * x, axis=-1, keepdims=True) + NORM_EPS) * g


def _dot(a, b):
    return jnp.dot(a, b, preferred_element_type=F32)


def _dot_nt(a, b):
    return lax.dot_general(a, b, (((1,), (1,)), ((), ())), preferred_element_type=F32)


def _dot_tn(a, b):
    return lax.dot_general(a, b, (((0,), (0,)), ((), ())), preferred_element_type=F32)


def _split2(x):
    hi = x.astype(BF16)
    lo = (x - hi.astype(F32)).astype(BF16)
    return hi, lo


def _split3(x):
    hi = x.astype(BF16)
    r = x - hi.astype(F32)
    mid = r.astype(BF16)
    lo = (r - mid.astype(F32)).astype(BF16)
    return hi, mid, lo


def _sigmoid(x):
    return 1.0 / (1.0 + jnp.exp(-x))


def _pack_pairs(x):
    w = x.shape[1] // 2
    u = lax.bitcast_convert_type(x.astype(BF16).astype(F32), U32)
    return (u[:, :w] >> 16) | u[:, w:]


def _unpack_pairs(u):
    lo = lax.bitcast_convert_type(u << 16, F32)
    hi = lax.bitcast_convert_type(u & jnp.uint32(0xFFFF0000), F32)
    return jnp.concatenate([lo, hi], axis=1)


def _side_cast_spec(w2d, n_steps, step_of):
    rows, cols = w2d.shape
    assert rows % n_steps == 0
    return pl.BlockSpec((rows // n_steps, cols), lambda *ids: (step_of(*ids), 0))


def _winprep_kernel(wt_ref, main_ref):
    main_ref[...] = wt_ref[...].T.astype(BF16)


def _w_in_prep(w_in_t):
    rows, d = w_in_t.shape
    a0 = 3 * D_MODEL
    assert rows == P_COLS + GLA_GATE_RANK and a0 % IN_TN == 0
    n_left = a0 // IN_TN
    assert IN_TN % GLA_GATE_RANK == 0
    row0 = lambda j: ((j * (IN_TN // GLA_GATE_RANK) + (j >= n_left).astype(I32)) * GLA_GATE_RANK, 0)
    return pl.pallas_call(
        _winprep_kernel,
        grid=(P_COLS // IN_TN,),
        in_specs=[pl.BlockSpec((pl.Element(IN_TN), pl.Element(d)), row0)],
        out_specs=pl.BlockSpec((d, IN_TN), lambda j: (0, j)),
        out_shape=jax.ShapeDtypeStruct((d, P_COLS), BF16),
        compiler_params=_params(("arbitrary",)),
        name="w_in_prep",
    )(w_in_t)


def _inproj_kernel(x_ref, g_ref, w_ref, wa_ref, cin_ref, p_ref, a_ref, cout_ref, h_scr):
    @pl.when(pl.program_id(1) == 0)
    def _():
        hb = _rms(x_ref[...], g_ref[...]).astype(BF16)
        h_scr[...] = hb
        a_ref[...] = _dot_nt(hb, wa_ref[...].astype(BF16))

    p_ref[...] = _dot(h_scr[...], w_ref[...]).astype(BF16)
    cout_ref[...] = cin_ref[...].astype(BF16)


def _in_proj(x, g, w_main, w_alow_t, w_cast):
    T = x.shape[0]
    ni, nj = T // IN_TM, P_COLS // IN_TN
    cast_spec = _side_cast_spec(w_cast, ni * nj, lambda i, j: i * nj + j)
    return pl.pallas_call(
        _inproj_kernel,
        grid=(ni, nj),
        in_specs=[
            pl.BlockSpec((IN_TM, D_MODEL), lambda i, j: (i, 0)),
            pl.BlockSpec((1, D_MODEL), lambda i, j: (0, 0)),
            pl.BlockSpec((D_MODEL, IN_TN), lambda i, j: (0, j)),
            pl.BlockSpec((LANES, D_MODEL), lambda i, j: (0, 0)),
            cast_spec,
        ],
        out_specs=[
            pl.BlockSpec((IN_TM, IN_TN), lambda i, j: (i, j)),
            pl.BlockSpec((IN_TM, LANES), lambda i, j: (i, 0)),
            cast_spec,
        ],
        out_shape=[
            jax.ShapeDtypeStruct((T, P_COLS), BF16),
            jax.ShapeDtypeStruct((T, LANES), F32),
            jax.ShapeDtypeStruct(w_cast.shape, BF16),
        ],
        scratch_shapes=[pltpu.VMEM((IN_TM, D_MODEL), BF16)],
        compiler_params=_params(("arbitrary", "arbitrary")),
        name="in_proj",
    )(x, g, w_main, w_alow_t, w_cast)


def _gla_level_weights(b, la, row):
    ch, dk = b.shape
    out = []
    s = ch // 2
    while s >= 4:
        bb = b.reshape(ch // (2 * s), 2 * s, dk)
        d = (bb - bb[:, s - 1:s, :]).reshape(ch, dk)
        out.append((s, jnp.exp(-jnp.abs(d))))
        s //= 2
    la_prev = pltpu.roll(la, 1, 0)
    la_next = pltpu.roll(la, ch - 1, 0)
    r4 = row & 3
    d2 = jnp.where(r4 == 0, la_next, jnp.where(r4 == 1, 0.0, jnp.where(r4 == 2, la, la + la_prev)))
    out.append((2, jnp.exp(-jnp.abs(d2))))
    d1 = jnp.where((row & 1) == 1, la, 0.0)
    out.append((1, jnp.exp(-jnp.abs(d1))))
    return out


def _gla_kernel(q_ref, k_ref, v_ref, r_ref, a_ref, gw_ref, gb_ref, ng_ref, cin_ref, o_ref, cout_ref, st_scr):
    ch = GLA_CH
    cout_ref[...] = cin_ref[...].astype(BF16)

    @pl.when(pl.program_id(1) == 0)
    def _():
        st_scr[...] = jnp.zeros_like(st_scr)

    row = lax.broadcasted_iota(jnp.int32, (ch, 1), 0)
    col = lax.broadcasted_iota(jnp.int32, (1, ch), 1)
    tril = (row >= col).astype(BF16)
    levels = []
    s = ch // 2
    while s >= 1:
        levels.append((s, ((row ^ col) < 2 * s) & ((row & (2 * s - 1)) >= s) & ((col & (2 * s - 1)) < s)))
        s //= 2
    gw_hi, gw_lo = _split2(gw_ref[...])
    gate_b = gb_ref[...]
    norm_g = ng_ref[...]
    q_scale = jnp.asarray(GLA_DK ** -0.5, BF16)

    def head(sl, hh, a_hi, a_lo):
        ks = slice(hh * GLA_DK, (hh + 1) * GLA_DK)
        vs = slice(hh * GLA_DV, (hh + 1) * GLA_DV)
        qb = q_ref[sl, ks] * q_scale
        kb = k_ref[sl, ks]
        q, k = qb.astype(F32), kb.astype(F32)
        v = v_ref[sl, vs]
        z = _dot(a_hi, gw_hi[:, ks]) + _dot(a_lo, gw_hi[:, ks]) + _dot(a_hi, gw_lo[:, ks]) + gate_b[:, ks]
        la = (jnp.minimum(z, 0.0) - jnp.log(1.0 + jnp.exp(-jnp.abs(z)))) * (1.0 / GLA_GATE_TEMP)
        l_hi, l_mid, l_lo = _split3(la)
        b = _dot(tril, l_hi) + _dot(tril, l_mid) + _dot(tril, l_lo)

        scores = jnp.where(row == col, _dot_nt(qb, kb), 0.0)
        for (s, mask), (_, w) in zip(levels, _gla_level_weights(b, la, row)):
            scores = jnp.where(mask, _dot_nt((q * w).astype(BF16), (k * w).astype(BF16)), scores)
        o = _dot(scores.astype(BF16), v)

        st = st_scr[hh]
        o = o + _dot_nt((q * jnp.exp(b)).astype(BF16), st.astype(BF16))
        b_last = b[ch - 1:ch, :]
        kd = (k * jnp.exp(b_last - b)).astype(BF16)
        st_scr[hh] = st * jnp.exp(b_last) + _dot_tn(v, kd)

        o = _rms(o, norm_g)
        r = r_ref[sl, vs].astype(F32)
        o_ref[sl, vs] = (o * (r * _sigmoid(r))).astype(BF16)

    def chunk(c, carry):
        sl = pl.ds(pl.multiple_of(c * ch, ch), ch)
        a_hi, a_lo = _split2(a_ref[sl, :])
        for hh in range(GLA_HG):
            head(sl, hh, a_hi, a_lo)
        return carry

    lax.fori_loop(0, GLA_RB // ch, chunk, 0)


def _gla(p, a_low, gate_w_pad, gate_b, norm_g, w_cast):
    T = p.shape[0]
    ni = T // GLA_RB
    ng = GLA_HEADS // GLA_HG
    kw, vw = GLA_HG * GLA_DK, GLA_HG * GLA_DV
    cast_spec = _side_cast_spec(w_cast, ng * ni, lambda h, i: h * ni + i)
    return pl.pallas_call(
        _gla_kernel,
        grid=(ng, ni),
        in_specs=[
            pl.BlockSpec((GLA_RB, kw), lambda h, i: (i, h)),
            pl.BlockSpec((GLA_RB, kw), lambda h, i: (i, ng + h)),
            pl.BlockSpec((GLA_RB, vw), lambda h, i: (i, ng + h)),
            pl.BlockSpec((GLA_RB, vw), lambda h, i: (i, 2 * ng + h)),
            pl.BlockSpec((GLA_RB, LANES), lambda h, i: (i, 0)),
            pl.BlockSpec((LANES, kw), lambda h, i: (0, h)),
            pl.BlockSpec((1, kw), lambda h, i: (0, h)),
            pl.BlockSpec((1, GLA_DV), lambda h, i: (0, 0)),
            cast_spec,
        ],
        out_specs=[pl.BlockSpec((GLA_RB, vw), lambda h, i: (i, h)), cast_spec],
        out_shape=[jax.ShapeDtypeStruct((T, GLA_HEADS * GLA_DV), BF16),
                   jax.ShapeDtypeStruct(w_cast.shape, BF16)],
        scratch_shapes=[pltpu.VMEM((GLA_HG, GLA_DV, GLA_DK), F32)],
        compiler_params=_params(("arbitrary", "arbitrary")),
        name="gla",
    )(p, p, p, p, a_low, gate_w_pad, gate_b, norm_g, w_cast)


def _merge_kernel(og_ref, cb_ref, cc_ref, ch_ref, ccp_ref, chp_ref, cw_ref, cbias_ref,
                  ga_ref, gb_ref, wa_ref, wb_ref, o_ref):
    tm = cc_ref.shape[0]
    u = cc_ref[...].astype(F32) * ch_ref[...].astype(F32)
    up = ccp_ref[...].astype(F32) * chp_ref[...].astype(F32)
    up = jnp.where(pl.program_id(0) == 0, 0.0, up)
    hp = up.shape[0]
    row = lax.broadcasted_iota(jnp.int32, (tm, 1), 0)
    u1 = jnp.where(row == 0, up[hp - 1:hp, :], pltpu.roll(u, 1, 0))
    u2 = jnp.where(row == 0, up[hp - 2:hp - 1, :],
                   jnp.where(row == 1, up[hp - 1:hp, :], pltpu.roll(u, 2, 0)))
    cw = cw_ref[...]
    uc = cw[0:1, :] * u2 + cw[1:2, :] * u1 + cw[2:3, :] * u + cbias_ref[...]
    zb = (cb_ref[...].astype(F32) * uc).astype(BF16)

    ya = _dot(og_ref[...], wa_ref[...])
    yb = _dot(zb, wb_ref[...])
    o_ref[...] = (_sigmoid(ga_ref[...].astype(F32)) * ya + _sigmoid(gb_ref[...].astype(F32)) * yb).astype(BF16)


def _merge(og, p, conv_w8, conv_b, gla_out_w, conv_out_w):
    T = og.shape[0]
    halo = 16
    prev = lambda blk: (lambda i: (jnp.maximum(i * (MG_TM // halo) - 1, 0), blk))
    pblk = lambda blk: pl.BlockSpec((MG_TM, D_MODEL), lambda i: (i, blk))
    const = lambda shape: pl.BlockSpec(shape, lambda i: (0, 0), pipeline_mode=pl.Buffered(1))
    return pl.pallas_call(
        _merge_kernel,
        grid=(T // MG_TM,),
        in_specs=[
            pblk(0),
            pblk(3), pblk(4), pblk(5),
            pl.BlockSpec((halo, D_MODEL), prev(4)),
            pl.BlockSpec((halo, D_MODEL), prev(5)),
            const((8, D_MODEL)),
            const((1, D_MODEL)),
            pblk(6), pblk(7),
            const((D_MODEL, D_MODEL)), const((D_MODEL, D_MODEL)),
        ],
        out_specs=pblk(0),
        out_shape=jax.ShapeDtypeStruct((T, D_MODEL), BF16),
        compiler_params=_params(("arbitrary",)),
        name="merge",
    )(og, p, p, p, p, p, conv_w8, conv_b, p, p, gla_out_w, conv_out_w)


def _memkv_kernel(m_ref, g_ref, wk_ref, wv_ref, k_ref, v_ref):
    mn = _rms(m_ref[...], g_ref[...]).astype(BF16)
    k_ref[...] = _dot(mn, wk_ref[...]).astype(BF16)
    v_ref[...] = _dot(mn, wv_ref[...]).astype(BF16)


def _mem_kv(mem, g, wk, wv):
    return pl.pallas_call(
        _memkv_kernel,
        out_shape=[jax.ShapeDtypeStruct((MEM_LEN, XATTN_DIM), BF16)] * 2,
        compiler_params=pltpu.CompilerParams(vmem_limit_bytes=VMEM_LIMIT),
        name="mem_kv",
    )(mem, g, wk, wv)


def _xattn_tile(x, g_ref, wq_ref, k_ref, v_ref, wo_ref):
    hq = _rms(x, g_ref[...]).astype(BF16)
    q = (_dot(hq, wq_ref[...]) * (XATTN_HEAD_DIM ** -0.5)).astype(BF16)
    outs = []
    for h in range(XATTN_HEADS):
        cs = slice(h * XATTN_HEAD_DIM, (h + 1) * XATTN_HEAD_DIM)
        s = _dot_nt(q[:, cs], k_ref[:, cs])
        s = s - jnp.max(s, axis=-1, keepdims=True)
        e = jnp.exp(s)
        p = e / jnp.sum(e, axis=-1, keepdims=True)
        outs.append(_dot(p.astype(BF16), v_ref[:, cs]))
    o = jnp.concatenate(outs, axis=-1).astype(BF16)
    return x + _dot(o, wo_ref[...])


def _xattn_router_kernel(x_ref, mg_ref, wm_ref, gx_ref, wq_ref, k_ref, v_ref, wo_ref, gm_ref, rw_ref, rb_ref,
                         x2_ref, hm_ref, idx_ref, gate_ref, rank_ref, cnt_ref, carry_scr):
    x1 = x_ref[...] + _dot(mg_ref[...], wm_ref[...])
    x2 = _xattn_tile(x1, gx_ref, wq_ref, k_ref, v_ref, wo_ref)
    x2_ref[...] = x2
    _route_tile(x2, gm_ref, rw_ref, rb_ref, hm_ref, idx_ref, gate_ref, rank_ref, cnt_ref, carry_scr)


def _route_tile(x2, g_ref, w_ref, b_ref, hm_ref, idx_ref, gate_ref, rank_ref, cnt_ref, carry_scr):
    tm = x2.shape[0]

    @pl.when(pl.program_id(0) == 0)
    def _():
        carry_scr[...] = jnp.zeros_like(carry_scr)

    h = _rms(x2, g_ref[...])
    hb = h.astype(BF16)
    hm_ref[...] = _pack_pairs(h)
    lane = lax.broadcasted_iota(jnp.int32, (tm, LANES), 1)
    logits = jnp.where(lane < N_EXPERTS, _dot(hb, w_ref[...]) + b_ref[...], -jnp.inf)

    vals, idxs = [], []
    work = logits
    lane_f = lane.astype(F32)
    for _ in range(TOP_K):
        m = jnp.max(work, axis=-1, keepdims=True)
        am = jnp.min(jnp.where(work == m, lane_f, float(LANES)), axis=-1, keepdims=True).astype(I32)
        vals.append(m)
        idxs.append(am)
        work = jnp.where(lane == am, -jnp.inf, work)
    es = [jnp.exp(v - vals[0]) for v in vals]
    inv = 1.0 / (es[0] + es[1] + es[2] + es[3])

    onehot = (work != logits).astype(BF16)
    r_ = lax.broadcasted_iota(jnp.int32, (tm, tm), 0)
    c_ = lax.broadcasted_iota(jnp.int32, (tm, tm), 1)
    before = _dot((r_ > c_).astype(BF16), onehot) + carry_scr[...]
    carry_scr[...] = carry_scr[...] + jnp.sum(onehot.astype(F32), axis=0, keepdims=True)
    cnt_ref[...] = carry_scr[...]

    idx_out = jnp.zeros((tm, LANES), jnp.int32)
    gate_out = jnp.zeros((tm, LANES), F32)
    rank_out = jnp.zeros((tm, LANES), jnp.int32)
    for kk in range(TOP_K):
        rk = jnp.sum(jnp.where(lane == idxs[kk], before, 0.0), axis=-1, keepdims=True)
        idx_out = jnp.where(lane == kk, idxs[kk], idx_out)
        gate_out = jnp.where(lane == kk, es[kk] * inv, gate_out)
        rank_out = jnp.where(lane == kk, rk.astype(jnp.int32), rank_out)
    idx_ref[...] = idx_out
    gate_ref[...] = gate_out
    rank_ref[...] = rank_out


def _xattn_router(x, merged, wm, gx, wq, kmem, vmem, wo, gm, rw_pad, rb_pad):
    T = x.shape[0]
    full = lambda shape: pl.BlockSpec(shape, lambda i: (0, 0))
    row_blk = pl.BlockSpec((XA_TM, LANES), lambda i: (i, 0))
    return pl.pallas_call(
        _xattn_router_kernel,
        grid=(T // XA_TM,),
        in_specs=[
            pl.BlockSpec((XA_TM, D_MODEL), lambda i: (i, 0)),
            pl.BlockSpec((XA_TM, D_MODEL), lambda i: (i, 0)),
            pl.BlockSpec((D_MODEL, D_MODEL), lambda i: (0, 0), pipeline_mode=pl.Buffered(1)),
            full((1, D_MODEL)),
            full((D_MODEL, XATTN_DIM)),
            full((MEM_LEN, XATTN_DIM)),
            full((MEM_LEN, XATTN_DIM)),
            full((XATTN_DIM, D_MODEL)),
            full((1, D_MODEL)),
            full((D_MODEL, LANES)),
            full((1, LANES)),
        ],
        out_specs=[
            pl.BlockSpec((XA_TM, D_MODEL), lambda i: (i, 0)),
            pl.BlockSpec((XA_TM, PACK_W), lambda i: (i, 0)),
            row_blk, row_blk, row_blk,
            full((1, LANES)),
        ],
        out_shape=[
            jax.ShapeDtypeStruct((T, D_MODEL), F32),
            jax.ShapeDtypeStruct((T, PACK_W), U32),
            jax.ShapeDtypeStruct((T, LANES), jnp.int32),
            jax.ShapeDtypeStruct((T, LANES), F32),
            jax.ShapeDtypeStruct((T, LANES), jnp.int32),
            jax.ShapeDtypeStruct((1, LANES), F32),
        ],
        scratch_shapes=[pltpu.VMEM((1, LANES), F32)],
        compiler_params=_params(("arbitrary",)),
        name="xattn_router",
    )(x, merged, wm, gx, wq, kmem, vmem, wo, gm, rw_pad, rb_pad)


def _dispatch_kernel(idx_ref, rank_ref, pstart_ref, dest_ref):
    tm = idx_ref.shape[0]
    lane = lax.broadcasted_iota(I32, (tm, LANES), 1)
    idx = idx_ref[...]
    pstart = pstart_ref[...]
    dest = rank_ref[...]
    for kk in range(TOP_K):
        start = jnp.sum(jnp.where(lane == idx[:, kk:kk + 1], pstart, 0.0), axis=-1, keepdims=True)
        dest = dest + jnp.where(lane == kk, start.astype(I32), 0)
    dest_ref[...] = dest


def _dispatch(idx, rank, pstart_row):
    T = idx.shape[0]
    blk = pl.BlockSpec((DP_TM, LANES), lambda i: (i, 0))
    return pl.pallas_call(
        _dispatch_kernel,
        grid=(T // DP_TM,),
        in_specs=[blk, blk, pl.BlockSpec((1, LANES), lambda i: (0, 0))],
        out_specs=blk,
        out_shape=jax.ShapeDtypeStruct((T, LANES), I32),
        compiler_params=_params(("arbitrary",)),
        name="dispatch",
    )(idx, rank, pstart_row)


def _sc_mesh():
    return plsc.VectorSubcoreMesh(core_axis_name="c", subcore_axis_name="s",
                                  num_cores=SC_CORES, num_subcores=SC_SUBCORES)


def _sc_gather_rows(table_hbm, idx_v, out_hbm, out_base, n_steps, rows_v, gsem, wsem):
    nbuf = SC_NBUF

    def gather(s, b):
        return pltpu.make_async_copy(table_hbm.at[idx_v.at[pl.ds(s * SC_GG, SC_GG)]], rows_v.at[b], gsem.at[b])

    def write(s, b):
        return pltpu.make_async_copy(rows_v.at[b], out_hbm.at[pl.ds(out_base + s * SC_GG, SC_GG)], wsem.at[b])

    for j in range(nbuf - 1):
        gather(j, j).start()

    @pl.loop(0, n_steps, step=nbuf)
    def _(s0):
        for b in range(nbuf):
            s = s0 + b
            gather(s, b).wait()
            write(s, b).start()
            refill = (b + nbuf - 1) % nbuf

            @pl.when(s + nbuf - 1 < n_steps)
            def _():
                @pl.when(s >= 1)
                def _():
                    write(s - 1, refill).wait()
                gather(s + nbuf - 1, refill).start()

    for j in range(nbuf):
        write(n_steps - nbuf + j, j).wait()


def _sc_dispatch_scatter(table, dest_chunks, n_slots):
    T, width = table.shape
    n_workers = SC_CORES * SC_SUBCORES
    per_w = T // n_workers
    n_steps = per_w // SC_G
    assert T % n_workers == 0 and per_w % (2 * SC_G) == 0
    assert dest_chunks.shape == (T // SC_G, TOP_K, SC_G)

    @functools.partial(
        pl.kernel, mesh=_sc_mesh(),
        out_type=jax.ShapeDtypeStruct((n_slots, width), table.dtype),
        scratch_types=[
            pltpu.VMEM((n_steps, TOP_K, SC_G), I32),
            pltpu.VMEM((2, SC_G, width), table.dtype),
            pltpu.SemaphoreType.DMA((2,)),
            pltpu.SemaphoreType.DMA((2,)),
        ],
        name="sc_dispatch_scatter",
    )
    def k(table_hbm, dest_hbm, out_hbm, dst_v, rows_v, rsem, wsem):
        wid = lax.axis_index("s") * SC_CORES + lax.axis_index("c")
        pltpu.sync_copy(dest_hbm.at[pl.ds(wid * n_steps, n_steps)], dst_v)

        def read(s, b):
            return pltpu.make_async_copy(table_hbm.at[pl.ds(wid * per_w + s * SC_G, SC_G)], rows_v.at[b], rsem.at[b])

        def scatter(s, kk, b):
            return pltpu.make_async_copy(rows_v.at[b], out_hbm.at[dst_v.at[s, kk]], wsem.at[b])

        read(0, 0).start()

        @pl.loop(0, n_steps, step=2)
        def _(s0):
            for b in range(2):
                s = s0 + b
                read(s, b).wait()

                @pl.when(s + 1 < n_steps)
                def _():
                    @pl.when(s >= 1)
                    def _():
                        for kk in range(TOP_K):
                            scatter(s - 1, kk, 1 - b).wait()
                    read(s + 1, 1 - b).start()

                for kk in range(TOP_K):
                    scatter(s, kk, b).start()

        for kk in range(TOP_K):
            scatter(n_steps - 2, kk, 0).wait()
            scatter(n_steps - 1, kk, 1).wait()

    return k(table, dest_chunks)


def _sc_combine_gather(table, dest_flat):
    n_rows, width = dest_flat.shape[0], table.shape[1]
    n_workers = SC_CORES * SC_SUBCORES
    per_w = n_rows // n_workers
    n_steps = per_w // SC_GG
    assert n_rows % n_workers == 0 and per_w % (SC_NBUF * SC_GG) == 0

    @functools.partial(
        pl.kernel, mesh=_sc_mesh(),
        out_type=jax.ShapeDtypeStruct((n_rows, width), table.dtype),
        scratch_types=[
            pltpu.VMEM((per_w,), I32),
            pltpu.VMEM((SC_NBUF, SC_GG, width), table.dtype),
            pltpu.SemaphoreType.DMA((SC_NBUF,)),
            pltpu.SemaphoreType.DMA((SC_NBUF,)),
        ],
        name="sc_combine_gather",
    )
    def k(table_hbm, dest_hbm, out_hbm, dst_v, rows_v, gsem, wsem):
        wid = lax.axis_index("s") * SC_CORES + lax.axis_index("c")
        lo = wid * per_w
        pltpu.sync_copy(dest_hbm.at[pl.ds(lo, per_w)], dst_v)
        _sc_gather_rows(table_hbm, dst_v, out_hbm, lo, n_steps, rows_v, gsem, wsem)

    return k(table, dest_flat)


def _expert_kernel(be_ref, nb_ref, nv_ref, x_ref, wg_ref, wu_ref, bg_ref, bu_ref, wd_ref, prev_ref, y_ref, *, last):
    i = pl.program_id(0)

    @pl.when(i < nb_ref[0])
    def _():
        row = lax.broadcasted_iota(I32, (x_ref.shape[0], 1), 0)
        x = _unpack_pairs(jnp.where(row < nv_ref[i], x_ref[...], jnp.uint32(0))).astype(BF16)
        gate = jnp.minimum(_dot(x, wg_ref[...]) + bg_ref[...], SWIGLU_LIMIT)
        up = jnp.clip(_dot(x, wu_ref[...]) + bu_ref[...], -SWIGLU_LIMIT, SWIGLU_LIMIT)
        act = (up + 1.0) * gate * _sigmoid(SWIGLU_ALPHA * gate)
        y = prev_ref[...] + _dot(act.astype(BF16), wd_ref[...])
        y_ref[...] = _pack_pairs(y) if last else y


def _expert_pass(c, block_e, nb_used, n_valid, x_sorted, gu_w, gu_b, down_w, prev):
    P = x_sorted.shape[0]
    nc = D_FF // MOE_TF
    first, last = c == 0, c == nc - 1
    rowblk = lambda i, be, nbu, nv: (jnp.minimum(i, nbu[0] - 1), 0)
    prev_spec = (pl.BlockSpec((None, 1, D_MODEL), lambda i, be, nbu, nv: (be[i], 0, 0)) if first
                 else pl.BlockSpec((MOE_TM, D_MODEL), rowblk))
    grid_spec = pltpu.PrefetchScalarGridSpec(
        num_scalar_prefetch=3,
        grid=(P // MOE_TM,),
        in_specs=[
            pl.BlockSpec((MOE_TM, PACK_W), rowblk),
            pl.BlockSpec((None, D_MODEL, MOE_TF), lambda i, be, nbu, nv: (be[i], 0, c)),
            pl.BlockSpec((None, D_MODEL, MOE_TF), lambda i, be, nbu, nv: (be[i], 0, nc + c)),
            pl.BlockSpec((None, 1, MOE_TF), lambda i, be, nbu, nv: (be[i], 0, c)),
            pl.BlockSpec((None, 1, MOE_TF), lambda i, be, nbu, nv: (be[i], 0, nc + c)),
            pl.BlockSpec((None, MOE_TF, D_MODEL), lambda i, be, nbu, nv: (be[i], c, 0)),
            prev_spec,
        ],
        out_specs=pl.BlockSpec((MOE_TM, PACK_W if last else D_MODEL), rowblk),
    )
    return pl.pallas_call(
        functools.partial(_expert_kernel, last=last),
        grid_spec=grid_spec,
        out_shape=jax.ShapeDtypeStruct((P, PACK_W), U32) if last else jax.ShapeDtypeStruct((P, D_MODEL), F32),
        compiler_params=_params(("arbitrary",)),
        name="experts_%d" % c,
    )(block_e, nb_used, n_valid, x_sorted, gu_w, gu_w, gu_b, gu_b, down_w, prev)


def _combine_kernel(x_ref, y_ref, gate_ref, g_ref, o_ref):
    acc = x_ref[...]
    gates = gate_ref[...]
    for kk in range(TOP_K):
        acc = acc + gates[:, kk:kk + 1] * _unpack_pairs(y_ref[kk])
    o_ref[...] = _rms(acc, g_ref[...])


def _combine(x2, y_g, gates, g):
    T = x2.shape[0]
    return pl.pallas_call(
        _combine_kernel,
        grid=(T // CB_TM,),
        in_specs=[
            pl.BlockSpec((CB_TM, D_MODEL), lambda i: (i, 0)),
            pl.BlockSpec((TOP_K, CB_TM, PACK_W), lambda i: (0, i, 0)),
            pl.BlockSpec((CB_TM, LANES), lambda i: (i, 0)),
            pl.BlockSpec((1, D_MODEL), lambda i: (0, 0)),
        ],
        out_specs=pl.BlockSpec((CB_TM, D_MODEL), lambda i: (i, 0)),
        out_shape=jax.ShapeDtypeStruct((T, D_MODEL), F32),
        compiler_params=_params(("arbitrary",)),
        name="combine",
    )(x2, y_g, gates, g)


def _row(v):
    return v.reshape(1, -1).astype(F32)


def _layer(x, mem, norm_mix_g, w_in, gla_gate_w, gla_gate_b, gla_norm_g, gla_out_w, conv_w, conv_b,
           conv_out_w, mix_out_w, norm_xattn_g, norm_mem_g, xq_w, xk_w, xv_w, xo_w, norm_moe_g,
           router_w, router_b, gu_w, gu_b, down_w, down_b):
    T = x.shape[0]
    w_in_t = w_in.T
    w_main = _w_in_prep(w_in_t)
    a0 = 3 * D_MODEL
    w_alow_t = jnp.pad(w_in_t[a0:a0 + GLA_GATE_RANK], ((0, LANES - GLA_GATE_RANK), (0, 0)))
    gate_w_pad = jnp.pad(gla_gate_w, ((0, LANES - GLA_GATE_RANK), (0, 0)))
    conv_w8 = jnp.pad(conv_w, ((0, 8 - CONV_WIDTH), (0, 0)))

    p, a_low, gu_wb = _in_proj(x, _row(norm_mix_g), w_main, w_alow_t, gu_w.reshape(-1, 2 * D_FF))
    og, down_wb = _gla(p, a_low, gate_w_pad, _row(gla_gate_b), _row(gla_norm_g), down_w.reshape(-1, D_MODEL))
    gu_wb = gu_wb.reshape(gu_w.shape)
    down_wb = down_wb.reshape(down_w.shape)
    merged = _merge(og, p, conv_w8, _row(conv_b), gla_out_w.astype(BF16), conv_out_w.astype(BF16))

    kmem, vmem = _mem_kv(mem, _row(norm_mem_g), xk_w.astype(BF16), xv_w.astype(BF16))
    rw = jnp.pad(router_w, ((0, 0), (0, LANES - N_EXPERTS))).astype(BF16)
    rb = jnp.pad(router_b, (0, LANES - N_EXPERTS)).reshape(1, LANES)
    x2, hm, idx, gates, rank, counts = _xattn_router(
        x, merged, mix_out_w.astype(BF16), _row(norm_xattn_g), xq_w.astype(BF16), kmem, vmem, xo_w.astype(BF16),
        _row(norm_moe_g), rw, rb)

    cnt = counts[0, :N_EXPERTS].astype(I32)
    padded = (cnt + MOE_TM - 1) // MOE_TM * MOE_TM
    pend = jnp.cumsum(padded)
    pstart = pend - padded
    pstart_row = jnp.pad(pstart, (0, LANES - N_EXPERTS)).reshape(1, LANES).astype(F32)
    nb = (T * TOP_K) // MOE_TM + N_EXPERTS
    blk_start = (jnp.arange(nb, dtype=I32) * MOE_TM)[:, None]
    block_e = jnp.minimum(jnp.sum(pend[None, :] <= blk_start, axis=1), N_EXPERTS - 1).astype(I32)
    nb_used = (pend[-1] // MOE_TM).astype(I32).reshape(1)
    in_group = (pstart[None, :] <= blk_start) & (blk_start < pend[None, :])
    n_valid = jnp.sum(jnp.where(in_group, jnp.clip((pstart + cnt)[None, :] - blk_start, 0, MOE_TM), 0),
                      axis=1).astype(I32)

    dest = _dispatch(idx, rank, pstart_row)[:, :TOP_K]
    dest_chunks = dest.reshape(T // SC_G, SC_G, TOP_K).transpose(0, 2, 1)
    x_sorted = _sc_dispatch_scatter(hm, dest_chunks, nb * MOE_TM)

    gu_b3 = gu_b.reshape(N_EXPERTS, 1, 2 * D_FF)
    y = down_b.reshape(N_EXPERTS, 1, D_MODEL)
    for c in range(D_FF // MOE_TF):
        y = _expert_pass(c, block_e, nb_used, n_valid, x_sorted, gu_wb, gu_b3, down_wb, y)
    y_g = _sc_combine_gather(y, dest.T.reshape(-1)).reshape(TOP_K, T, PACK_W)
    return x2, y_g, gates


def kernel(x, mem, norm_mix_g, w_in, gla_gate_w, gla_gate_b, gla_norm_g, gla_out_w, conv_w, conv_b, conv_out_w, mix_out_w, norm_xattn_g, norm_mem_g, xq_w, xk_w, xv_w, xo_w, norm_moe_g, router_w, router_b, expert_gu_w, expert_gu_b, expert_down_w, expert_down_b, norm_final_g):
    assert x.shape[0] == 1 and mem.shape[0] == 1 and w_in.shape[0] == 1
    x2, y_g, gates = _layer(
        x[0], mem[0], norm_mix_g[0], w_in[0], gla_gate_w[0], gla_gate_b[0], gla_norm_g[0], gla_out_w[0],
        conv_w[0], conv_b[0], conv_out_w[0], mix_out_w[0], norm_xattn_g[0], norm_mem_g[0], xq_w[0], xk_w[0],
        xv_w[0], xo_w[0], norm_moe_g[0], router_w[0], router_b[0], expert_gu_w[0], expert_gu_b[0],
        expert_down_w[0], expert_down_b[0])
    out = _combine(x2, y_g, gates, _row(norm_final_g))
    return out[None]
```

```python
import functools

import jax
import jax.numpy as jnp
from jax import lax
from jax.experimental import pallas as pl
from jax.experimental.pallas import tpu as pltpu
from jax.experimental.pallas import tpu_sc as plsc

F32 = jnp.float32
BF16 = jnp.bfloat16
I32 = jnp.int32
U32 = jnp.uint32

D_MODEL = 2048
MEM_LEN = 256
GLA_HEADS = 4
GLA_DK = 256
GLA_DV = 512
GLA_GATE_RANK = 16
GLA_GATE_TEMP = 16.0
CONV_WIDTH = 3
XATTN_HEADS = 4
XATTN_HEAD_DIM = 128
XATTN_DIM = XATTN_HEADS * XATTN_HEAD_DIM
N_EXPERTS = 32
TOP_K = 4
D_FF = D_MODEL
SWIGLU_LIMIT = 7.0
SWIGLU_ALPHA = 1.702
NORM_EPS = 1e-5

LANES = 128
SC_CORES, SC_SUBCORES, SC_LANES = 2, 16, 16
PACK_W = D_MODEL // 2
P_COLS = 8 * D_MODEL
VMEM_LIMIT = 56 * 1024 * 1024

IN_TM, IN_TN = 1024, 1024
GLA_CH = 128
GLA_RB = 512
GLA_HG = 2
MG_TM = 512
XA_TM = 512
DP_TM = 2048
MOE_TM = 512
MOE_TF = 1024
CB_TM = 512
CB_RANGES = 4
SC_G = 32
SC_GG, SC_NBUF = 16, 4


def _params(sem):
    return pltpu.CompilerParams(dimension_semantics=sem, vmem_limit_bytes=VMEM_LIMIT)


def _rms(x, g):
    return x * lax.rsqrt(jnp.mean(x * x, axis=-1, keepdims=True) + NORM_EPS) * g


def _dot(a, b):
    return jnp.dot(a, b, preferred_element_type=F32)


def _dot_nt(a, b):
    return lax.dot_general(a, b, (((1,), (1,)), ((), ())), preferred_element_type=F32)


def _dot_tn(a, b):
    return lax.dot_general(a, b, (((0,), (0,)), ((), ())), preferred_element_type=F32)


def _split2(x):
    hi = x.astype(BF16)
    lo = (x - hi.astype(F32)).astype(BF16)
    return hi, lo


def _split3(x):
    hi = x.astype(BF16)
    r = x - hi.astype(F32)
    mid = r.astype(BF16)
    lo = (r - mid.astype(F32)).astype(BF16)
    return hi, mid, lo


def _sigmoid(x):
    return 1.0 / (1.0 + jnp.exp(-x))


def _pack_pairs(x):
    w = x.shape[1] // 2
    u = lax.bitcast_convert_type(x.astype(BF16).astype(F32), U32)
    return (u[:, :w] >> 16) | u[:, w:]


def _unpack_pairs(u):
    lo = lax.bitcast_convert_type(u << 16, F32)
    hi = lax.bitcast_convert_type(u & jnp.uint32(0xFFFF0000), F32)
    return jnp.concatenate([lo, hi], axis=1)


def _side_cast_spec(w2d, n_steps, step_of):
    rows, cols = w2d.shape
    assert rows % n_steps == 0
    return pl.BlockSpec((rows // n_steps, cols), lambda *ids: (step_of(*ids), 0))


def _winprep_kernel(wt_ref, main_ref):
    main_ref[...] = wt_ref[...].T.astype(BF16)


def _w_in_prep(w_in_t):
    rows, d = w_in_t.shape
    a0 = 3 * D_MODEL
    assert rows == P_COLS + GLA_GATE_RANK and a0 % IN_TN == 0
    n_left = a0 // IN_TN
    assert IN_TN % GLA_GATE_RANK == 0
    row0 = lambda j: ((j * (IN_TN // GLA_GATE_RANK) + (j >= n_left).astype(I32)) * GLA_GATE_RANK, 0)
    return pl.pallas_call(
        _winprep_kernel,
        grid=(P_COLS // IN_TN,),
        in_specs=[pl.BlockSpec((pl.Element(IN_TN), pl.Element(d)), row0)],
        out_specs=pl.BlockSpec((d, IN_TN), lambda j: (0, j)),
        out_shape=jax.ShapeDtypeStruct((d, P_COLS), BF16),
        compiler_params=_params(("arbitrary",)),
        name="w_in_prep",
    )(w_in_t)


def _inproj_kernel(x_ref, g_ref, w_ref, wa_ref, cin_ref, p_ref, a_ref, cout_ref, h_scr):
    @pl.when(pl.program_id(1) == 0)
    def _():
        hb = _rms(x_ref[...], g_ref[...]).astype(BF16)
        h_scr[...] = hb
        a_ref[...] = _dot_nt(hb, wa_ref[...].astype(BF16))

    p_ref[...] = _dot(h_scr[...], w_ref[...]).astype(BF16)
    cout_ref[...] = cin_ref[...].astype(BF16)


def _in_proj(x, g, w_main, w_alow_t, w_cast):
    T = x.shape[0]
    ni, nj = T // IN_TM, P_COLS // IN_TN
    cast_spec = _side_cast_spec(w_cast, ni * nj, lambda i, j: i * nj + j)
    return pl.pallas_call(
        _inproj_kernel,
        grid=(ni, nj),
        in_specs=[
            pl.BlockSpec((IN_TM, D_MODEL), lambda i, j: (i, 0)),
            pl.BlockSpec((1, D_MODEL), lambda i, j: (0, 0)),
            pl.BlockSpec((D_MODEL, IN_TN), lambda i, j: (0, j)),
            pl.BlockSpec((LANES, D_MODEL), lambda i, j: (0, 0)),
            cast_spec,
        ],
        out_specs=[
            pl.BlockSpec((IN_TM, IN_TN), lambda i, j: (i, j)),
            pl.BlockSpec((IN_TM, LANES), lambda i, j: (i, 0)),
            cast_spec,
        ],
        out_shape=[
            jax.ShapeDtypeStruct((T, P_COLS), BF16),
            jax.ShapeDtypeStruct((T, LANES), F32),
            jax.ShapeDtypeStruct(w_cast.shape, BF16),
        ],
        scratch_shapes=[pltpu.VMEM((IN_TM, D_MODEL), BF16)],
        compiler_params=_params(("arbitrary", "arbitrary")),
        name="in_proj",
    )(x, g, w_main, w_alow_t, w_cast)


def _gla_level_weights(b, la, row):
    ch, dk = b.shape
    out = []
    s = ch // 2
    while s >= 4:
        bb = b.reshape(ch // (2 * s), 2 * s, dk)
        d = (bb - bb[:, s - 1:s, :]).reshape(ch, dk)
        out.append((s, jnp.exp(-jnp.abs(d))))
        s //= 2
    la_prev = pltpu.roll(la, 1, 0)
    la_next = pltpu.roll(la, ch - 1, 0)
    r4 = row & 3
    d2 = jnp.where(r4 == 0, la_next, jnp.where(r4 == 1, 0.0, jnp.where(r4 == 2, la, la + la_prev)))
    out.append((2, jnp.exp(-jnp.abs(d2))))
    d1 = jnp.where((row & 1) == 1, la, 0.0)
    out.append((1, jnp.exp(-jnp.abs(d1))))
    return out


def _gla_kernel(q_ref, k_ref, v_ref, r_ref, a_ref, gw_ref, gb_ref, ng_ref, cin_ref, o_ref, cout_ref, st_scr):
    ch = GLA_CH
    cout_ref[...] = cin_ref[...].astype(BF16)

    @pl.when(pl.program_id(1) == 0)
    def _():
        st_scr[...] = jnp.zeros_like(st_scr)

    row = lax.broadcasted_iota(jnp.int32, (ch, 1), 0)
    col = lax.broadcasted_iota(jnp.int32, (1, ch), 1)
    tril = (row >= col).astype(BF16)
    levels = []
    s = ch // 2
    while s >= 1:
        levels.append((s, ((row ^ col) < 2 * s) & ((row & (2 * s - 1)) >= s) & ((col & (2 * s - 1)) < s)))
        s //= 2
    gw_hi, gw_lo = _split2(gw_ref[...])
    gate_b = gb_ref[...]
    norm_g = ng_ref[...]
    q_scale = jnp.asarray(GLA_DK ** -0.5, BF16)

    def head(sl, hh, a_hi, a_lo):
        ks = slice(hh * GLA_DK, (hh + 1) * GLA_DK)
        vs = slice(hh * GLA_DV, (hh + 1) * GLA_DV)
        qb = q_ref[sl, ks] * q_scale
        kb = k_ref[sl, ks]
        q, k = qb.astype(F32), kb.astype(F32)
        v = v_ref[sl, vs]
        z = _dot(a_hi, gw_hi[:, ks]) + _dot(a_lo, gw_hi[:, ks]) + _dot(a_hi, gw_lo[:, ks]) + gate_b[:, ks]
        la = (jnp.minimum(z, 0.0) - jnp.log(1.0 + jnp.exp(-jnp.abs(z)))) * (1.0 / GLA_GATE_TEMP)
        l_hi, l_mid, l_lo = _split3(la)
        b = _dot(tril, l_hi) + _dot(tril, l_mid) + _dot(tril, l_lo)

        scores = jnp.where(row == col, _dot_nt(qb, kb), 0.0)
        for (s, mask), (_, w) in zip(levels, _gla_level_weights(b, la, row)):
            scores = jnp.where(mask, _dot_nt((q * w).astype(BF16), (k * w).astype(BF16)), scores)
        o = _dot(scores.astype(BF16), v)

        st = st_scr[hh]
        o = o + _dot_nt((q * jnp.exp(b)).astype(BF16), st.astype(BF16))
        b_last = b[ch - 1:ch, :]
        kd = (k * jnp.exp(b_last - b)).astype(BF16)
        st_scr[hh] = st * jnp.exp(b_last) + _dot_tn(v, kd)

        o = _rms(o, norm_g)
        r = r_ref[sl, vs].astype(F32)
        o_ref[sl, vs] = (o * (r * _sigmoid(r))).astype(BF16)

    def chunk(c, carry):
        sl = pl.ds(pl.multiple_of(c * ch, ch), ch)
        a_hi, a_lo = _split2(a_ref[sl, :])
        for hh in range(GLA_HG):
            head(sl, hh, a_hi, a_lo)
        return carry

    lax.fori_loop(0, GLA_RB // ch, chunk, 0)


def _gla(p, a_low, gate_w_pad, gate_b, norm_g, w_cast):
    T = p.shape[0]
    ni = T // GLA_RB
    ng = GLA_HEADS // GLA_HG
    kw, vw = GLA_HG * GLA_DK, GLA_HG * GLA_DV
    cast_spec = _side_cast_spec(w_cast, ng * ni, lambda h, i: h * ni + i)
    return pl.pallas_call(
        _gla_kernel,
        grid=(ng, ni),
        in_specs=[
            pl.BlockSpec((GLA_RB, kw), lambda h, i: (i, h)),
            pl.BlockSpec((GLA_RB, kw), lambda h, i: (i, ng + h)),
            pl.BlockSpec((GLA_RB, vw), lambda h, i: (i, ng + h)),
            pl.BlockSpec((GLA_RB, vw), lambda h, i: (i, 2 * ng + h)),
            pl.BlockSpec((GLA_RB, LANES), lambda h, i: (i, 0)),
            pl.BlockSpec((LANES, kw), lambda h, i: (0, h)),
            pl.BlockSpec((1, kw), lambda h, i: (0, h)),
            pl.BlockSpec((1, GLA_DV), lambda h, i: (0, 0)),
            cast_spec,
        ],
        out_specs=[pl.BlockSpec((GLA_RB, vw), lambda h, i: (i, h)), cast_spec],
        out_shape=[jax.ShapeDtypeStruct((T, GLA_HEADS * GLA_DV), BF16),
                   jax.ShapeDtypeStruct(w_cast.shape, BF16)],
        scratch_shapes=[pltpu.VMEM((GLA_HG, GLA_DV, GLA_DK), F32)],
        compiler_params=_params(("arbitrary", "arbitrary")),
        name="gla",
    )(p, p, p, p, a_low, gate_w_pad, gate_b, norm_g, w_cast)


def _merge_kernel(og_ref, cb_ref, cc_ref, ch_ref, ccp_ref, chp_ref, cw_ref, cbias_ref,
                  ga_ref, gb_ref, wa_ref, wb_ref, o_ref):
    tm = cc_ref.shape[0]
    u = cc_ref[...].astype(F32) * ch_ref[...].astype(F32)
    up = ccp_ref[...].astype(F32) * chp_ref[...].astype(F32)
    up = jnp.where(pl.program_id(0) == 0, 0.0, up)
    hp = up.shape[0]
    row = lax.broadcasted_iota(jnp.int32, (tm, 1), 0)
    u1 = jnp.where(row == 0, up[hp - 1:hp, :], pltpu.roll(u, 1, 0))
    u2 = jnp.where(row == 0, up[hp - 2:hp - 1, :],
                   jnp.where(row == 1, up[hp - 1:hp, :], pltpu.roll(u, 2, 0)))
    cw = cw_ref[...]
    uc = cw[0:1, :] * u2 + cw[1:2, :] * u1 + cw[2:3, :] * u + cbias_ref[...]
    zb = (cb_ref[...].astype(F32) * uc).astype(BF16)

    ya = _dot(og_ref[...], wa_ref[...])
    yb = _dot(zb, wb_ref[...])
    o_ref[...] = (_sigmoid(ga_ref[...].astype(F32)) * ya + _sigmoid(gb_ref[...].astype(F32)) * yb).astype(BF16)


def _merge(og, p, conv_w8, conv_b, gla_out_w, conv_out_w):
    T = og.shape[0]
    halo = 16
    prev = lambda blk: (lambda i: (jnp.maximum(i * (MG_TM // halo) - 1, 0), blk))
    pblk = lambda blk: pl.BlockSpec((MG_TM, D_MODEL), lambda i: (i, blk))
    const = lambda shape: pl.BlockSpec(shape, lambda i: (0, 0), pipeline_mode=pl.Buffered(1))
    return pl.pallas_call(
        _merge_kernel,
        grid=(T // MG_TM,),
        in_specs=[
            pblk(0),
            pblk(3), pblk(4), pblk(5),
            pl.BlockSpec((halo, D_MODEL), prev(4)),
            pl.BlockSpec((halo, D_MODEL), prev(5)),
            const((8, D_MODEL)),
            const((1, D_MODEL)),
            pblk(6), pblk(7),
            const((D_MODEL, D_MODEL)), const((D_MODEL, D_MODEL)),
        ],
        out_specs=pblk(0),
        out_shape=jax.ShapeDtypeStruct((T, D_MODEL), BF16),
        compiler_params=_params(("arbitrary",)),
        name="merge",
    )(og, p, p, p, p, p, conv_w8, conv_b, p, p, gla_out_w, conv_out_w)


def _memkv_kernel(m_ref, g_ref, wk_ref, wv_ref, k_ref, v_ref):
    mn = _rms(m_ref[...], g_ref[...]).astype(BF16)
    k_ref[...] = _dot(mn, wk_ref[...]).astype(BF16)
    v_ref[...] = _dot(mn, wv_ref[...]).astype(BF16)


def _mem_kv(mem, g, wk, wv):
    return pl.pallas_call(
        _memkv_kernel,
        out_shape=[jax.ShapeDtypeStruct((MEM_LEN, XATTN_DIM), BF16)] * 2,
        compiler_params=pltpu.CompilerParams(vmem_limit_bytes=VMEM_LIMIT),
        name="mem_kv",
    )(mem, g, wk, wv)


def _xattn_tile(x, g_ref, wq_ref, k_ref, v_ref, wo_ref):
    hq = _rms(x, g_ref[...]).astype(BF16)
    q = (_dot(hq, wq_ref[...]) * (XATTN_HEAD_DIM ** -0.5)).astype(BF16)
    outs = []
    for h in range(XATTN_HEADS):
        cs = slice(h * XATTN_HEAD_DIM, (h + 1) * XATTN_HEAD_DIM)
        s = _dot_nt(q[:, cs], k_ref[:, cs])
        s = s - jnp.max(s, axis=-1, keepdims=True)
        e = jnp.exp(s)
        p = e / jnp.sum(e, axis=-1, keepdims=True)
        outs.append(_dot(p.astype(BF16), v_ref[:, cs]))
    o = jnp.concatenate(outs, axis=-1).astype(BF16)
    return x + _dot(o, wo_ref[...])


def _xattn_router_kernel(x_ref, mg_ref, wm_ref, gx_ref, wq_ref, k_ref, v_ref, wo_ref, gm_ref, rw_ref, rb_ref,
                         x2_ref, hm_ref, idx_ref, gate_ref, rank_ref, cnt_ref, carry_scr):
    x1 = x_ref[...] + _dot(mg_ref[...], wm_ref[...])
    x2 = _xattn_tile(x1, gx_ref, wq_ref, k_ref, v_ref, wo_ref)
    x2_ref[...] = x2
    _route_tile(x2, gm_ref, rw_ref, rb_ref, hm_ref, idx_ref, gate_ref, rank_ref, cnt_ref, carry_scr)


def _route_tile(x2, g_ref, w_ref, b_ref, hm_ref, idx_ref, gate_ref, rank_ref, cnt_ref, carry_scr):
    tm = x2.shape[0]

    @pl.when(pl.program_id(0) == 0)
    def _():
        carry_scr[...] = jnp.zeros_like(carry_scr)

    h = _rms(x2, g_ref[...])
    hb = h.astype(BF16)
    hm_ref[...] = _pack_pairs(h)
    lane = lax.broadcasted_iota(jnp.int32, (tm, LANES), 1)
    logits = jnp.where(lane < N_EXPERTS, _dot(hb, w_ref[...]) + b_ref[...], -jnp.inf)

    vals, idxs = [], []
    work = logits
    lane_f = lane.astype(F32)
    for _ in range(TOP_K):
        m = jnp.max(work, axis=-1, keepdims=True)
        am = jnp.min(jnp.where(work == m, lane_f, float(LANES)), axis=-1, keepdims=True).astype(I32)
        vals.append(m)
        idxs.append(am)
        work = jnp.where(lane == am, -jnp.inf, work)
    es = [jnp.exp(v - vals[0]) for v in vals]
    inv = 1.0 / (es[0] + es[1] + es[2] + es[3])

    onehot = (work != logits).astype(BF16)
    r_ = lax.broadcasted_iota(jnp.int32, (tm, tm), 0)
    c_ = lax.broadcasted_iota(jnp.int32, (tm, tm), 1)
    before = _dot((r_ > c_).astype(BF16), onehot) + carry_scr[...]
    carry_scr[...] = carry_scr[...] + jnp.sum(onehot.astype(F32), axis=0, keepdims=True)
    cnt_ref[...] = carry_scr[...]

    idx_out = jnp.zeros((tm, LANES), jnp.int32)
    gate_out = jnp.zeros((tm, LANES), F32)
    rank_out = jnp.zeros((tm, LANES), jnp.int32)
    for kk in range(TOP_K):
        rk = jnp.sum(jnp.where(lane == idxs[kk], before, 0.0), axis=-1, keepdims=True)
        idx_out = jnp.where(lane == kk, idxs[kk], idx_out)
        gate_out = jnp.where(lane == kk, es[kk] * inv, gate_out)
        rank_out = jnp.where(lane == kk, rk.astype(jnp.int32), rank_out)
    idx_ref[...] = idx_out
    gate_ref[...] = gate_out
    rank_ref[...] = rank_out


def _xattn_router(x, merged, wm, gx, wq, kmem, vmem, wo, gm, rw_pad, rb_pad):
    T = x.shape[0]
    full = lambda shape: pl.BlockSpec(shape, lambda i: (0, 0))
    row_blk = pl.BlockSpec((XA_TM, LANES), lambda i: (i, 0))
    return pl.pallas_call(
        _xattn_router_kernel,
        grid=(T // XA_TM,),
        in_specs=[
            pl.BlockSpec((XA_TM, D_MODEL), lambda i: (i, 0)),
            pl.BlockSpec((XA_TM, D_MODEL), lambda i: (i, 0)),
            pl.BlockSpec((D_MODEL, D_MODEL), lambda i: (0, 0), pipeline_mode=pl.Buffered(1)),
            full((1, D_MODEL)),
            full((D_MODEL, XATTN_DIM)),
            full((MEM_LEN, XATTN_DIM)),
            full((MEM_LEN, XATTN_DIM)),
            full((XATTN_DIM, D_MODEL)),
            full((1, D_MODEL)),
            full((D_MODEL, LANES)),
            full((1, LANES)),
        ],
        out_specs=[
            pl.BlockSpec((XA_TM, D_MODEL), lambda i: (i, 0)),
            pl.BlockSpec((XA_TM, PACK_W), lambda i: (i, 0)),
            row_blk, row_blk, row_blk,
            full((1, LANES)),
        ],
        out_shape=[
            jax.ShapeDtypeStruct((T, D_MODEL), F32),
            jax.ShapeDtypeStruct((T, PACK_W), U32),
            jax.ShapeDtypeStruct((T, LANES), jnp.int32),
            jax.ShapeDtypeStruct((T, LANES), F32),
            jax.ShapeDtypeStruct((T, LANES), jnp.int32),
            jax.ShapeDtypeStruct((1, LANES), F32),
        ],
        scratch_shapes=[pltpu.VMEM((1, LANES), F32)],
        compiler_params=_params(("arbitrary",)),
        name="xattn_router",
    )(x, merged, wm, gx, wq, kmem, vmem, wo, gm, rw_pad, rb_pad)


def _dispatch_kernel(idx_ref, rank_ref, pstart_ref, dest_ref):
    tm = idx_ref.shape[0]
    lane = lax.broadcasted_iota(I32, (tm, LANES), 1)
    idx = idx_ref[...]
    pstart = pstart_ref[...]
    dest = rank_ref[...]
    for kk in range(TOP_K):
        start = jnp.sum(jnp.where(lane == idx[:, kk:kk + 1], pstart, 0.0), axis=-1, keepdims=True)
        dest = dest + jnp.where(lane == kk, start.astype(I32), 0)
    dest_ref[...] = dest


def _dispatch(idx, rank, pstart_row):
    T = idx.shape[0]
    blk = pl.BlockSpec((DP_TM, LANES), lambda i: (i, 0))
    return pl.pallas_call(
        _dispatch_kernel,
        grid=(T // DP_TM,),
        in_specs=[blk, blk, pl.BlockSpec((1, LANES), lambda i: (0, 0))],
        out_specs=blk,
        out_shape=jax.ShapeDtypeStruct((T, LANES), I32),
        compiler_params=_params(("arbitrary",)),
        name="dispatch",
    )(idx, rank, pstart_row)


def _sc_mesh():
    return plsc.VectorSubcoreMesh(core_axis_name="c", subcore_axis_name="s",
                                  num_cores=SC_CORES, num_subcores=SC_SUBCORES)


def _sc_gather_rows(table_hbm, idx_v, out_hbm, out_base, n_steps, rows_v, gsem, wsem):
    nbuf = SC_NBUF

    def gather(s, b):
        return pltpu.make_async_copy(table_hbm.at[idx_v.at[pl.ds(s * SC_GG, SC_GG)]], rows_v.at[b], gsem.at[b])

    def write(s, b):
        return pltpu.make_async_copy(rows_v.at[b], out_hbm.at[pl.ds(out_base + s * SC_GG, SC_GG)], wsem.at[b])

    for j in range(nbuf - 1):
        gather(j, j).start()

    @pl.loop(0, n_steps, step=nbuf)
    def _(s0):
        for b in range(nbuf):
            s = s0 + b
            gather(s, b).wait()
            write(s, b).start()
            refill = (b + nbuf - 1) % nbuf

            @pl.when(s + nbuf - 1 < n_steps)
            def _():
                @pl.when(s >= 1)
                def _():
                    write(s - 1, refill).wait()
                gather(s + nbuf - 1, refill).start()

    for j in range(nbuf):
        write(n_steps - nbuf + j, j).wait()


def _sc_dispatch_scatter(table, dest_chunks, n_slots):
    T, width = table.shape
    n_workers = SC_CORES * SC_SUBCORES
    per_w = T // n_workers
    n_steps = per_w // SC_G
    assert T % n_workers == 0 and per_w % (2 * SC_G) == 0
    assert dest_chunks.shape == (T // SC_G, TOP_K, SC_G)

    @functools.partial(
        pl.kernel, mesh=_sc_mesh(),
        out_type=jax.ShapeDtypeStruct((n_slots, width), table.dtype),
        scratch_types=[
            pltpu.VMEM((n_steps, TOP_K, SC_G), I32),
            pltpu.VMEM((2, SC_G, width), table.dtype),
            pltpu.SemaphoreType.DMA((2,)),
            pltpu.SemaphoreType.DMA((2,)),
        ],
        name="sc_dispatch_scatter",
    )
    def k(table_hbm, dest_hbm, out_hbm, dst_v, rows_v, rsem, wsem):
        wid = lax.axis_index("s") * SC_CORES + lax.axis_index("c")
        pltpu.sync_copy(dest_hbm.at[pl.ds(wid * n_steps, n_steps)], dst_v)

        def read(s, b):
            return pltpu.make_async_copy(table_hbm.at[pl.ds(wid * per_w + s * SC_G, SC_G)], rows_v.at[b], rsem.at[b])

        def scatter(s, kk, b):
            return pltpu.make_async_copy(rows_v.at[b], out_hbm.at[dst_v.at[s, kk]], wsem.at[b])

        read(0, 0).start()

        @pl.loop(0, n_steps, step=2)
        def _(s0):
            for b in range(2):
                s = s0 + b
                read(s, b).wait()

                @pl.when(s + 1 < n_steps)
                def _():
                    @pl.when(s >= 1)
                    def _():
                        for kk in range(TOP_K):
                            scatter(s - 1, kk, 1 - b).wait()
                    read(s + 1, 1 - b).start()

                for kk in range(TOP_K):
                    scatter(s, kk, b).start()

        for kk in range(TOP_K):
            scatter(n_steps - 2, kk, 0).wait()
            scatter(n_steps - 1, kk, 1).wait()

    return k(table, dest_chunks)


def _sc_combine_gather(table, dest_flat):
    n_rows, width = dest_flat.shape[0], table.shape[1]
    n_workers = SC_CORES * SC_SUBCORES
    per_w = n_rows // n_workers
    n_steps = per_w // SC_GG
    assert n_rows % n_workers == 0 and per_w % (SC_NBUF * SC_GG) == 0

    @functools.partial(
        pl.kernel, mesh=_sc_mesh(),
        out_type=jax.ShapeDtypeStruct((n_rows, width), table.dtype),
        scratch_types=[
            pltpu.VMEM((per_w,), I32),
            pltpu.VMEM((SC_NBUF, SC_GG, width), table.dtype),
            pltpu.SemaphoreType.DMA((SC_NBUF,)),
            pltpu.SemaphoreType.DMA((SC_NBUF,)),
        ],
        name="sc_combine_gather",
    )
    def k(table_hbm, dest_hbm, out_hbm, dst_v, rows_v, gsem, wsem):
        wid = lax.axis_index("s") * SC_CORES + lax.axis_index("c")
        lo = wid * per_w
        pltpu.sync_copy(dest_hbm.at[pl.ds(lo, per_w)], dst_v)
        _sc_gather_rows(table_hbm, dst_v, out_hbm, lo, n_steps, rows_v, gsem, wsem)

    return k(table, dest_flat)


def _expert_kernel(be_ref, nb_ref, nv_ref, x_ref, wg_ref, wu_ref, bg_ref, bu_ref, wd_ref, prev_ref, y_ref, *, last):
    i = pl.program_id(0)

    @pl.when(i < nb_ref[0])
    def _():
        row = lax.broadcasted_iota(I32, (x_ref.shape[0], 1), 0)
        x = _unpack_pairs(jnp.where(row < nv_ref[i], x_ref[...], jnp.uint32(0))).astype(BF16)
        gate = jnp.minimum(_dot(x, wg_ref[...]) + bg_ref[...], SWIGLU_LIMIT)
        up = jnp.clip(_dot(x, wu_ref[...]) + bu_ref[...], -SWIGLU_LIMIT, SWIGLU_LIMIT)
        act = (up + 1.0) * gate * _sigmoid(SWIGLU_ALPHA * gate)
        y = prev_ref[...] + _dot(act.astype(BF16), wd_ref[...])
        y_ref[...] = _pack_pairs(y) if last else y


def _expert_pass(c, block_e, nb_used, n_valid, x_sorted, gu_w, gu_b, down_w, prev):
    P = x_sorted.shape[0]
    nc = D_FF // MOE_TF
    first, last = c == 0, c == nc - 1
    rowblk = lambda i, be, nbu, nv: (jnp.minimum(i, nbu[0] - 1), 0)
    prev_spec = (pl.BlockSpec((None, 1, D_MODEL), lambda i, be, nbu, nv: (be[i], 0, 0)) if first
                 else pl.BlockSpec((MOE_TM, D_MODEL), rowblk))
    grid_spec = pltpu.PrefetchScalarGridSpec(
        num_scalar_prefetch=3,
        grid=(P // MOE_TM,),
        in_specs=[
            pl.BlockSpec((MOE_TM, PACK_W), rowblk),
            pl.BlockSpec((None, D_MODEL, MOE_TF), lambda i, be, nbu, nv: (be[i], 0, c)),
            pl.BlockSpec((None, D_MODEL, MOE_TF), lambda i, be, nbu, nv: (be[i], 0, nc + c)),
            pl.BlockSpec((None, 1, MOE_TF), lambda i, be, nbu, nv: (be[i], 0, c)),
            pl.BlockSpec((None, 1, MOE_TF), lambda i, be, nbu, nv: (be[i], 0, nc + c)),
            pl.BlockSpec((None, MOE_TF, D_MODEL), lambda i, be, nbu, nv: (be[i], c, 0)),
            prev_spec,
        ],
        out_specs=pl.BlockSpec((MOE_TM, PACK_W if last else D_MODEL), rowblk),
    )
    return pl.pallas_call(
        functools.partial(_expert_kernel, last=last),
        grid_spec=grid_spec,
        out_shape=jax.ShapeDtypeStruct((P, PACK_W), U32) if last else jax.ShapeDtypeStruct((P, D_MODEL), F32),
        compiler_params=_params(("arbitrary",)),
        name="experts_%d" % c,
    )(block_e, nb_used, n_valid, x_sorted, gu_w, gu_w, gu_b, gu_b, down_w, prev)


def _combine_kernel(x_ref, y_ref, gate_ref, g_ref, o_ref):
    acc = x_ref[...]
    gates = gate_ref[...]
    for kk in range(TOP_K):
        acc = acc + gates[:, kk:kk + 1] * _unpack_pairs(y_ref[kk])
    o_ref[...] = _rms(acc, g_ref[...])


def _combine(acc, y_g, gates, g, first_row):
    T, rows = acc.shape[0], y_g.shape[1]
    assert rows % CB_TM == 0 and first_row % CB_TM == 0
    b0 = first_row // CB_TM
    return pl.pallas_call(
        _combine_kernel,
        grid=(rows // CB_TM,),
        in_specs=[
            pl.BlockSpec((CB_TM, D_MODEL), lambda i: (b0 + i, 0)),
            pl.BlockSpec((TOP_K, CB_TM, PACK_W), lambda i: (0, i, 0)),
            pl.BlockSpec((CB_TM, LANES), lambda i: (b0 + i, 0)),
            pl.BlockSpec((1, D_MODEL), lambda i: (0, 0)),
        ],
        out_specs=pl.BlockSpec((CB_TM, D_MODEL), lambda i: (b0 + i, 0)),
        out_shape=jax.ShapeDtypeStruct((T, D_MODEL), F32),
        input_output_aliases={0: 0},
        compiler_params=_params(("arbitrary",)),
        name="combine",
    )(acc, y_g, gates, g)


def _row(v):
    return v.reshape(1, -1).astype(F32)


def _layer(x, mem, norm_mix_g, w_in, gla_gate_w, gla_gate_b, gla_norm_g, gla_out_w, conv_w, conv_b,
           conv_out_w, mix_out_w, norm_xattn_g, norm_mem_g, xq_w, xk_w, xv_w, xo_w, norm_moe_g,
           router_w, router_b, gu_w, gu_b, down_w, down_b):
    T = x.shape[0]
    w_in_t = w_in.T
    w_main = _w_in_prep(w_in_t)
    a0 = 3 * D_MODEL
    w_alow_t = jnp.pad(w_in_t[a0:a0 + GLA_GATE_RANK], ((0, LANES - GLA_GATE_RANK), (0, 0)))
    gate_w_pad = jnp.pad(gla_gate_w, ((0, LANES - GLA_GATE_RANK), (0, 0)))
    conv_w8 = jnp.pad(conv_w, ((0, 8 - CONV_WIDTH), (0, 0)))

    p, a_low, gu_wb = _in_proj(x, _row(norm_mix_g), w_main, w_alow_t, gu_w.reshape(-1, 2 * D_FF))
    og, down_wb = _gla(p, a_low, gate_w_pad, _row(gla_gate_b), _row(gla_norm_g), down_w.reshape(-1, D_MODEL))
    gu_wb = gu_wb.reshape(gu_w.shape)
    down_wb = down_wb.reshape(down_w.shape)
    merged = _merge(og, p, conv_w8, _row(conv_b), gla_out_w.astype(BF16), conv_out_w.astype(BF16))

    kmem, vmem = _mem_kv(mem, _row(norm_mem_g), xk_w.astype(BF16), xv_w.astype(BF16))
    rw = jnp.pad(router_w, ((0, 0), (0, LANES - N_EXPERTS))).astype(BF16)
    rb = jnp.pad(router_b, (0, LANES - N_EXPERTS)).reshape(1, LANES)
    x2, hm, idx, gates, rank, counts = _xattn_router(
        x, merged, mix_out_w.astype(BF16), _row(norm_xattn_g), xq_w.astype(BF16), kmem, vmem, xo_w.astype(BF16),
        _row(norm_moe_g), rw, rb)

    cnt = counts[0, :N_EXPERTS].astype(I32)
    padded = (cnt + MOE_TM - 1) // MOE_TM * MOE_TM
    pend = jnp.cumsum(padded)
    pstart = pend - padded
    pstart_row = jnp.pad(pstart, (0, LANES - N_EXPERTS)).reshape(1, LANES).astype(F32)
    nb = (T * TOP_K) // MOE_TM + N_EXPERTS
    blk_start = (jnp.arange(nb, dtype=I32) * MOE_TM)[:, None]
    block_e = jnp.minimum(jnp.sum(pend[None, :] <= blk_start, axis=1), N_EXPERTS - 1).astype(I32)
    nb_used = (pend[-1] // MOE_TM).astype(I32).reshape(1)
    in_group = (pstart[None, :] <= blk_start) & (blk_start < pend[None, :])
    n_valid = jnp.sum(jnp.where(in_group, jnp.clip((pstart + cnt)[None, :] - blk_start, 0, MOE_TM), 0),
                      axis=1).astype(I32)

    dest = _dispatch(idx, rank, pstart_row)[:, :TOP_K]
    dest_chunks = dest.reshape(T // SC_G, SC_G, TOP_K).transpose(0, 2, 1)
    x_sorted = _sc_dispatch_scatter(hm, dest_chunks, nb * MOE_TM)

    gu_b3 = gu_b.reshape(N_EXPERTS, 1, 2 * D_FF)
    y = down_b.reshape(N_EXPERTS, 1, D_MODEL)
    for c in range(D_FF // MOE_TF):
        y = _expert_pass(c, block_e, nb_used, n_valid, x_sorted, gu_wb, gu_b3, down_wb, y)
    return x2, y, dest, gates


def kernel(x, mem, norm_mix_g, w_in, gla_gate_w, gla_gate_b, gla_norm_g, gla_out_w, conv_w, conv_b, conv_out_w, mix_out_w, norm_xattn_g, norm_mem_g, xq_w, xk_w, xv_w, xo_w, norm_moe_g, router_w, router_b, expert_gu_w, expert_gu_b, expert_down_w, expert_down_b, norm_final_g):
    assert x.shape[0] == 1 and mem.shape[0] == 1 and w_in.shape[0] == 1
    x2, y, dest, gates = _layer(
        x[0], mem[0], norm_mix_g[0], w_in[0], gla_gate_w[0], gla_gate_b[0], gla_norm_g[0], gla_out_w[0],
        conv_w[0], conv_b[0], conv_out_w[0], mix_out_w[0], norm_xattn_g[0], norm_mem_g[0], xq_w[0], xk_w[0],
        xv_w[0], xo_w[0], norm_moe_g[0], router_w[0], router_b[0], expert_gu_w[0], expert_gu_b[0],
        expert_down_w[0], expert_down_b[0])
    T = x2.shape[0]
    rows = T // CB_RANGES
    out = x2
    for c in range(CB_RANGES):
        dest_c = dest[c * rows:(c + 1) * rows].T.reshape(-1)
        y_g = _sc_combine_gather(y, dest_c).reshape(TOP_K, rows, PACK_W)
        out = _combine(out, y_g, gates, _row(norm_final_g), c * rows)
    return out[None]
```

```python
import functools

import jax
import jax.numpy as jnp
from jax import lax
from jax.experimental import pallas as pl
from jax.experimental.pallas import tpu as pltpu
from jax.experimental.pallas import tpu_sc as plsc

F32 = jnp.float32
BF16 = jnp.bfloat16
I32 = jnp.int32
U32 = jnp.uint32

D_MODEL = 2048
MEM_LEN = 256
GLA_HEADS = 4
GLA_DK = 256
GLA_DV = 512
GLA_GATE_RANK = 16
GLA_GATE_TEMP = 16.0
CONV_WIDTH = 3
XATTN_HEADS = 4
XATTN_HEAD_DIM = 128
XATTN_DIM = XATTN_HEADS * XATTN_HEAD_DIM
N_EXPERTS = 32
TOP_K = 4
D_FF = D_MODEL
SWIGLU_LIMIT = 7.0
SWIGLU_ALPHA = 1.702
NORM_EPS = 1e-5

LANES = 128
SC_CORES, SC_SUBCORES, SC_LANES = 2, 16, 16
PACK_W = D_MODEL // 2
P_COLS = 8 * D_MODEL
VMEM_LIMIT = 56 * 1024 * 1024

IN_TM, IN_TN = 1024, 1024
GLA_CH = 128
GLA_RB = 512
GLA_HG = 2
MG_TM = 256
XA_TM = 512
DP_TM = 2048
MOE_TM = 512
MOE_TS = 128
MOE_TF = 1024
CB_TM = 512
SC_G = 32
SC_GG, SC_NBUF = 16, 4


def _params(sem):
    return pltpu.CompilerParams(dimension_semantics=sem, vmem_limit_bytes=VMEM_LIMIT)


def _rms(x, g):
    return x * lax.rsqrt(jnp.mean(x * x, axis=-1, keepdims=True) + NORM_EPS) * g


def _dot(a, b):
    return jnp.dot(a, b, preferred_element_type=F32)


def _dot_nt(a, b):
    return lax.dot_general(a, b, (((1,), (1,)), ((), ())), preferred_element_type=F32)


def _dot_tn(a, b):
    return lax.dot_general(a, b, (((0,), (0,)), ((), ())), preferred_element_type=F32)


def _split2(x):
    hi = x.astype(BF16)
    lo = (x - hi.astype(F32)).astype(BF16)
    return hi, lo


def _split3(x):
    hi = x.astype(BF16)
    r = x - hi.astype(F32)
    mid = r.astype(BF16)
    lo = (r - mid.astype(F32)).astype(BF16)
    return hi, mid, lo


def _sigmoid(x):
    return 1.0 / (1.0 + jnp.exp(-x))


def _pack_pairs(x):
    w = x.shape[1] // 2
    u = lax.bitcast_convert_type(x.astype(BF16).astype(F32), U32)
    return (u[:, :w] >> 16) | u[:, w:]


def _unpack_pairs(u):
    lo = lax.bitcast_convert_type(u << 16, F32)
    hi = lax.bitcast_convert_type(u & jnp.uint32(0xFFFF0000), F32)
    return jnp.concatenate([lo, hi], axis=1)


def _side_cast_spec(w2d, n_steps, step_of):
    rows, cols = w2d.shape
    assert rows % n_steps == 0
    return pl.BlockSpec((rows // n_steps, cols), lambda *ids: (step_of(*ids), 0))


def _winprep_kernel(wt_ref, main_ref):
    main_ref[...] = wt_ref[...].T.astype(BF16)


def _w_in_prep(w_in_t):
    rows, d = w_in_t.shape
    a0 = 3 * D_MODEL
    assert rows == P_COLS + GLA_GATE_RANK and a0 % IN_TN == 0
    n_left = a0 // IN_TN
    assert IN_TN % GLA_GATE_RANK == 0
    row0 = lambda j: ((j * (IN_TN // GLA_GATE_RANK) + (j >= n_left).astype(I32)) * GLA_GATE_RANK, 0)
    return pl.pallas_call(
        _winprep_kernel,
        grid=(P_COLS // IN_TN,),
        in_specs=[pl.BlockSpec((pl.Element(IN_TN), pl.Element(d)), row0)],
        out_specs=pl.BlockSpec((d, IN_TN), lambda j: (0, j)),
        out_shape=jax.ShapeDtypeStruct((d, P_COLS), BF16),
        compiler_params=_params(("arbitrary",)),
        name="w_in_prep",
    )(w_in_t)


def _inproj_kernel(x_ref, g_ref, w_ref, wa_ref, cin_ref, p_ref, a_ref, cout_ref, h_scr):
    @pl.when(pl.program_id(1) == 0)
    def _():
        hb = _rms(x_ref[...], g_ref[...]).astype(BF16)
        h_scr[...] = hb
        a_ref[...] = _dot_nt(hb, wa_ref[...].astype(BF16))

    p_ref[...] = _dot(h_scr[...], w_ref[...]).astype(BF16)
    cout_ref[...] = cin_ref[...].astype(BF16)


def _in_proj(x, g, w_main, w_alow_t, w_cast):
    T = x.shape[0]
    ni, nj = T // IN_TM, P_COLS // IN_TN
    cast_spec = _side_cast_spec(w_cast, ni * nj, lambda i, j: i * nj + j)
    return pl.pallas_call(
        _inproj_kernel,
        grid=(ni, nj),
        in_specs=[
            pl.BlockSpec((IN_TM, D_MODEL), lambda i, j: (i, 0)),
            pl.BlockSpec((1, D_MODEL), lambda i, j: (0, 0)),
            pl.BlockSpec((D_MODEL, IN_TN), lambda i, j: (0, j)),
            pl.BlockSpec((LANES, D_MODEL), lambda i, j: (0, 0)),
            cast_spec,
        ],
        out_specs=[
            pl.BlockSpec((IN_TM, IN_TN), lambda i, j: (i, j)),
            pl.BlockSpec((IN_TM, LANES), lambda i, j: (i, 0)),
            cast_spec,
        ],
        out_shape=[
            jax.ShapeDtypeStruct((T, P_COLS), BF16),
            jax.ShapeDtypeStruct((T, LANES), F32),
            jax.ShapeDtypeStruct(w_cast.shape, BF16),
        ],
        scratch_shapes=[pltpu.VMEM((IN_TM, D_MODEL), BF16)],
        compiler_params=_params(("arbitrary", "arbitrary")),
        name="in_proj",
    )(x, g, w_main, w_alow_t, w_cast)


def _gla_level_weights(b, la, row):
    ch, dk = b.shape
    out = []
    s = ch // 2
    while s >= 4:
        bb = b.reshape(ch // (2 * s), 2 * s, dk)
        d = (bb - bb[:, s - 1:s, :]).reshape(ch, dk)
        out.append((s, jnp.exp(-jnp.abs(d))))
        s //= 2
    la_prev = pltpu.roll(la, 1, 0)
    la_next = pltpu.roll(la, ch - 1, 0)
    r4 = row & 3
    d2 = jnp.where(r4 == 0, la_next, jnp.where(r4 == 1, 0.0, jnp.where(r4 == 2, la, la + la_prev)))
    out.append((2, jnp.exp(-jnp.abs(d2))))
    d1 = jnp.where((row & 1) == 1, la, 0.0)
    out.append((1, jnp.exp(-jnp.abs(d1))))
    return out


def _gla_kernel(q_ref, k_ref, v_ref, r_ref, a_ref, gw_ref, gb_ref, ng_ref, cin_ref, o_ref, cout_ref, st_scr):
    ch = GLA_CH
    cout_ref[...] = cin_ref[...].astype(BF16)

    @pl.when(pl.program_id(1) == 0)
    def _():
        st_scr[...] = jnp.zeros_like(st_scr)

    row = lax.broadcasted_iota(jnp.int32, (ch, 1), 0)
    col = lax.broadcasted_iota(jnp.int32, (1, ch), 1)
    tril = (row >= col).astype(BF16)
    levels = []
    s = ch // 2
    while s >= 1:
        levels.append((s, ((row ^ col) < 2 * s) & ((row & (2 * s - 1)) >= s) & ((col & (2 * s - 1)) < s)))
        s //= 2
    gw_hi, gw_lo = _split2(gw_ref[...])
    gate_b = gb_ref[...]
    norm_g = ng_ref[...]
    q_scale = jnp.asarray(GLA_DK ** -0.5, BF16)

    def head(sl, hh, a_hi, a_lo):
        ks = slice(hh * GLA_DK, (hh + 1) * GLA_DK)
        vs = slice(hh * GLA_DV, (hh + 1) * GLA_DV)
        qb = q_ref[sl, ks] * q_scale
        kb = k_ref[sl, ks]
        q, k = qb.astype(F32), kb.astype(F32)
        v = v_ref[sl, vs]
        z = _dot(a_hi, gw_hi[:, ks]) + _dot(a_lo, gw_hi[:, ks]) + _dot(a_hi, gw_lo[:, ks]) + gate_b[:, ks]
        la = (jnp.minimum(z, 0.0) - jnp.log(1.0 + jnp.exp(-jnp.abs(z)))) * (1.0 / GLA_GATE_TEMP)
        l_hi, l_mid, l_lo = _split3(la)
        b = _dot(tril, l_hi) + _dot(tril, l_mid) + _dot(tril, l_lo)

        scores = jnp.where(row == col, _dot_nt(qb, kb), 0.0)
        for (s, mask), (_, w) in zip(levels, _gla_level_weights(b, la, row)):
            scores = jnp.where(mask, _dot_nt((q * w).astype(BF16), (k * w).astype(BF16)), scores)
        o = _dot(scores.astype(BF16), v)

        st = st_scr[hh]
        o = o + _dot_nt((q * jnp.exp(b)).astype(BF16), st.astype(BF16))
        b_last = b[ch - 1:ch, :]
        kd = (k * jnp.exp(b_last - b)).astype(BF16)
        st_scr[hh] = st * jnp.exp(b_last) + _dot_tn(v, kd)

        o = _rms(o, norm_g)
        r = r_ref[sl, vs].astype(F32)
        o_ref[sl, vs] = (o * (r * _sigmoid(r))).astype(BF16)

    def chunk(c, carry):
        sl = pl.ds(pl.multiple_of(c * ch, ch), ch)
        a_hi, a_lo = _split2(a_ref[sl, :])
        for hh in range(GLA_HG):
            head(sl, hh, a_hi, a_lo)
        return carry

    lax.fori_loop(0, GLA_RB // ch, chunk, 0)


def _gla(p, a_low, gate_w_pad, gate_b, norm_g, w_cast):
    T = p.shape[0]
    ni = T // GLA_RB
    ng = GLA_HEADS // GLA_HG
    kw, vw = GLA_HG * GLA_DK, GLA_HG * GLA_DV
    cast_spec = _side_cast_spec(w_cast, ng * ni, lambda h, i: h * ni + i)
    return pl.pallas_call(
        _gla_kernel,
        grid=(ng, ni),
        in_specs=[
            pl.BlockSpec((GLA_RB, kw), lambda h, i: (i, h)),
            pl.BlockSpec((GLA_RB, kw), lambda h, i: (i, ng + h)),
            pl.BlockSpec((GLA_RB, vw), lambda h, i: (i, ng + h)),
            pl.BlockSpec((GLA_RB, vw), lambda h, i: (i, 2 * ng + h)),
            pl.BlockSpec((GLA_RB, LANES), lambda h, i: (i, 0)),
            pl.BlockSpec((LANES, kw), lambda h, i: (0, h)),
            pl.BlockSpec((1, kw), lambda h, i: (0, h)),
            pl.BlockSpec((1, GLA_DV), lambda h, i: (0, 0)),
            cast_spec,
        ],
        out_specs=[pl.BlockSpec((GLA_RB, vw), lambda h, i: (i, h)), cast_spec],
        out_shape=[jax.ShapeDtypeStruct((T, GLA_HEADS * GLA_DV), BF16),
                   jax.ShapeDtypeStruct(w_cast.shape, BF16)],
        scratch_shapes=[pltpu.VMEM((GLA_HG, GLA_DV, GLA_DK), F32)],
        compiler_params=_params(("arbitrary", "arbitrary")),
        name="gla",
    )(p, p, p, p, a_low, gate_w_pad, gate_b, norm_g, w_cast)


def _merge_kernel(og_ref, cb_ref, cc_ref, ch_ref, ccp_ref, chp_ref, cw_ref, cbias_ref,
                  ga_ref, gb_ref, wa_ref, wb_ref, wm_ref, x_ref, o_ref):
    tm = cc_ref.shape[0]
    u = cc_ref[...].astype(F32) * ch_ref[...].astype(F32)
    up = ccp_ref[...].astype(F32) * chp_ref[...].astype(F32)
    up = jnp.where(pl.program_id(0) == 0, 0.0, up)
    hp = up.shape[0]
    row = lax.broadcasted_iota(jnp.int32, (tm, 1), 0)
    u1 = jnp.where(row == 0, up[hp - 1:hp, :], pltpu.roll(u, 1, 0))
    u2 = jnp.where(row == 0, up[hp - 2:hp - 1, :],
                   jnp.where(row == 1, up[hp - 1:hp, :], pltpu.roll(u, 2, 0)))
    cw = cw_ref[...]
    uc = cw[0:1, :] * u2 + cw[1:2, :] * u1 + cw[2:3, :] * u + cbias_ref[...]
    zb = (cb_ref[...].astype(F32) * uc).astype(BF16)

    ya = _dot(og_ref[...], wa_ref[...])
    yb = _dot(zb, wb_ref[...])
    merged = (_sigmoid(ga_ref[...].astype(F32)) * ya + _sigmoid(gb_ref[...].astype(F32)) * yb).astype(BF16)
    o_ref[...] = x_ref[...] + _dot(merged, wm_ref[...])


def _merge(og, p, conv_w8, conv_b, gla_out_w, conv_out_w, mix_out_w, x):
    T = og.shape[0]
    halo = 16
    prev = lambda blk: (lambda i: (jnp.maximum(i * (MG_TM // halo) - 1, 0), blk))
    pblk = lambda blk: pl.BlockSpec((MG_TM, D_MODEL), lambda i: (i, blk))
    const = lambda shape: pl.BlockSpec(shape, lambda i: (0, 0), pipeline_mode=pl.Buffered(1))
    return pl.pallas_call(
        _merge_kernel,
        grid=(T // MG_TM,),
        in_specs=[
            pblk(0),
            pblk(3), pblk(4), pblk(5),
            pl.BlockSpec((halo, D_MODEL), prev(4)),
            pl.BlockSpec((halo, D_MODEL), prev(5)),
            const((8, D_MODEL)),
            const((1, D_MODEL)),
            pblk(6), pblk(7),
            const((D_MODEL, D_MODEL)), const((D_MODEL, D_MODEL)), const((D_MODEL, D_MODEL)),
            pblk(0),
        ],
        out_specs=pblk(0),
        out_shape=jax.ShapeDtypeStruct((T, D_MODEL), F32),
        compiler_params=_params(("arbitrary",)),
        name="merge",
    )(og, p, p, p, p, p, conv_w8, conv_b, p, p, gla_out_w, conv_out_w, mix_out_w, x)


def _memkv_kernel(m_ref, g_ref, wk_ref, wv_ref, k_ref, v_ref):
    mn = _rms(m_ref[...], g_ref[...]).astype(BF16)
    k_ref[...] = _dot(mn, wk_ref[...]).astype(BF16)
    v_ref[...] = _dot(mn, wv_ref[...]).astype(BF16)


def _mem_kv(mem, g, wk, wv):
    return pl.pallas_call(
        _memkv_kernel,
        out_shape=[jax.ShapeDtypeStruct((MEM_LEN, XATTN_DIM), BF16)] * 2,
        compiler_params=pltpu.CompilerParams(vmem_limit_bytes=VMEM_LIMIT),
        name="mem_kv",
    )(mem, g, wk, wv)


def _xattn_tile(x_ref, g_ref, wq_ref, k_ref, v_ref, wo_ref):
    x = x_ref[...]
    hq = _rms(x, g_ref[...]).astype(BF16)
    q = (_dot(hq, wq_ref[...]) * (XATTN_HEAD_DIM ** -0.5)).astype(BF16)
    outs = []
    for h in range(XATTN_HEADS):
        cs = slice(h * XATTN_HEAD_DIM, (h + 1) * XATTN_HEAD_DIM)
        s = _dot_nt(q[:, cs], k_ref[:, cs])
        s = s - jnp.max(s, axis=-1, keepdims=True)
        e = jnp.exp(s)
        p = e / jnp.sum(e, axis=-1, keepdims=True)
        outs.append(_dot(p.astype(BF16), v_ref[:, cs]))
    o = jnp.concatenate(outs, axis=-1).astype(BF16)
    return x + _dot(o, wo_ref[...])


def _xattn_router_kernel(x_ref, gx_ref, wq_ref, k_ref, v_ref, wo_ref, gm_ref, rw_ref, rb_ref,
                         x2_ref, hm_ref, idx_ref, gate_ref, rank_ref, cnt_ref, carry_scr):
    x2 = _xattn_tile(x_ref, gx_ref, wq_ref, k_ref, v_ref, wo_ref)
    x2_ref[...] = x2
    _route_tile(x2, gm_ref, rw_ref, rb_ref, hm_ref, idx_ref, gate_ref, rank_ref, cnt_ref, carry_scr)


def _route_tile(x2, g_ref, w_ref, b_ref, hm_ref, idx_ref, gate_ref, rank_ref, cnt_ref, carry_scr):
    tm = x2.shape[0]

    @pl.when(pl.program_id(0) == 0)
    def _():
        carry_scr[...] = jnp.zeros_like(carry_scr)

    h = _rms(x2, g_ref[...])
    hb = h.astype(BF16)
    hm_ref[...] = _pack_pairs(h)
    lane = lax.broadcasted_iota(jnp.int32, (tm, LANES), 1)
    logits = jnp.where(lane < N_EXPERTS, _dot(hb, w_ref[...]) + b_ref[...], -jnp.inf)

    vals, idxs = [], []
    work = logits
    lane_f = lane.astype(F32)
    for _ in range(TOP_K):
        m = jnp.max(work, axis=-1, keepdims=True)
        am = jnp.min(jnp.where(work == m, lane_f, float(LANES)), axis=-1, keepdims=True).astype(I32)
        vals.append(m)
        idxs.append(am)
        work = jnp.where(lane == am, -jnp.inf, work)
    es = [jnp.exp(v - vals[0]) for v in vals]
    inv = 1.0 / (es[0] + es[1] + es[2] + es[3])

    onehot = (work != logits).astype(BF16)
    r_ = lax.broadcasted_iota(jnp.int32, (tm, tm), 0)
    c_ = lax.broadcasted_iota(jnp.int32, (tm, tm), 1)
    before = _dot((r_ > c_).astype(BF16), onehot) + carry_scr[...]
    carry_scr[...] = carry_scr[...] + jnp.sum(onehot.astype(F32), axis=0, keepdims=True)
    cnt_ref[...] = carry_scr[...]

    idx_out = jnp.zeros((tm, LANES), jnp.int32)
    gate_out = jnp.zeros((tm, LANES), F32)
    rank_out = jnp.zeros((tm, LANES), jnp.int32)
    for kk in range(TOP_K):
        rk = jnp.sum(jnp.where(lane == idxs[kk], before, 0.0), axis=-1, keepdims=True)
        idx_out = jnp.where(lane == kk, idxs[kk], idx_out)
        gate_out = jnp.where(lane == kk, es[kk] * inv, gate_out)
        rank_out = jnp.where(lane == kk, rk.astype(jnp.int32), rank_out)
    idx_ref[...] = idx_out
    gate_ref[...] = gate_out
    rank_ref[...] = rank_out


def _xattn_router(x1, gx, wq, kmem, vmem, wo, gm, rw_pad, rb_pad):
    T = x1.shape[0]
    full = lambda shape: pl.BlockSpec(shape, lambda i: (0, 0))
    row_blk = pl.BlockSpec((XA_TM, LANES), lambda i: (i, 0))
    return pl.pallas_call(
        _xattn_router_kernel,
        grid=(T // XA_TM,),
        in_specs=[
            pl.BlockSpec((XA_TM, D_MODEL), lambda i: (i, 0)),
            full((1, D_MODEL)),
            full((D_MODEL, XATTN_DIM)),
            full((MEM_LEN, XATTN_DIM)),
            full((MEM_LEN, XATTN_DIM)),
            full((XATTN_DIM, D_MODEL)),
            full((1, D_MODEL)),
            full((D_MODEL, LANES)),
            full((1, LANES)),
        ],
        out_specs=[
            pl.BlockSpec((XA_TM, D_MODEL), lambda i: (i, 0)),
            pl.BlockSpec((XA_TM, PACK_W), lambda i: (i, 0)),
            row_blk, row_blk, row_blk,
            full((1, LANES)),
        ],
        out_shape=[
            jax.ShapeDtypeStruct((T, D_MODEL), F32),
            jax.ShapeDtypeStruct((T, PACK_W), U32),
            jax.ShapeDtypeStruct((T, LANES), jnp.int32),
            jax.ShapeDtypeStruct((T, LANES), F32),
            jax.ShapeDtypeStruct((T, LANES), jnp.int32),
            jax.ShapeDtypeStruct((1, LANES), F32),
        ],
        scratch_shapes=[pltpu.VMEM((1, LANES), F32)],
        compiler_params=_params(("arbitrary",)),
        name="xattn_router",
    )(x1, gx, wq, kmem, vmem, wo, gm, rw_pad, rb_pad)


def _dispatch_kernel(idx_ref, rank_ref, pstart_ref, dest_ref):
    tm = idx_ref.shape[0]
    lane = lax.broadcasted_iota(I32, (tm, LANES), 1)
    idx = idx_ref[...]
    pstart = pstart_ref[...]
    dest = rank_ref[...]
    for kk in range(TOP_K):
        start = jnp.sum(jnp.where(lane == idx[:, kk:kk + 1], pstart, 0.0), axis=-1, keepdims=True)
        dest = dest + jnp.where(lane == kk, start.astype(I32), 0)
    dest_ref[...] = dest


def _dispatch(idx, rank, pstart_row):
    T = idx.shape[0]
    blk = pl.BlockSpec((DP_TM, LANES), lambda i: (i, 0))
    return pl.pallas_call(
        _dispatch_kernel,
        grid=(T // DP_TM,),
        in_specs=[blk, blk, pl.BlockSpec((1, LANES), lambda i: (0, 0))],
        out_specs=blk,
        out_shape=jax.ShapeDtypeStruct((T, LANES), I32),
        compiler_params=_params(("arbitrary",)),
        name="dispatch",
    )(idx, rank, pstart_row)


def _sc_mesh():
    return plsc.VectorSubcoreMesh(core_axis_name="c", subcore_axis_name="s",
                                  num_cores=SC_CORES, num_subcores=SC_SUBCORES)


def _sc_gather_rows(table_hbm, idx_v, out_hbm, out_base, n_steps, rows_v, gsem, wsem):
    nbuf = SC_NBUF

    def gather(s, b):
        return pltpu.make_async_copy(table_hbm.at[idx_v.at[pl.ds(s * SC_GG, SC_GG)]], rows_v.at[b], gsem.at[b])

    def write(s, b):
        return pltpu.make_async_copy(rows_v.at[b], out_hbm.at[pl.ds(out_base + s * SC_GG, SC_GG)], wsem.at[b])

    for j in range(nbuf - 1):
        gather(j, j).start()

    @pl.loop(0, n_steps, step=nbuf)
    def _(s0):
        for b in range(nbuf):
            s = s0 + b
            gather(s, b).wait()
            write(s, b).start()
            refill = (b + nbuf - 1) % nbuf

            @pl.when(s + nbuf - 1 < n_steps)
            def _():
                @pl.when(s >= 1)
                def _():
                    write(s - 1, refill).wait()
                gather(s + nbuf - 1, refill).start()

    for j in range(nbuf):
        write(n_steps - nbuf + j, j).wait()


def _sc_dispatch_scatter(table, dest_chunks, n_slots):
    T, width = table.shape
    n_workers = SC_CORES * SC_SUBCORES
    per_w = T // n_workers
    n_steps = per_w // SC_G
    assert T % n_workers == 0 and per_w % (2 * SC_G) == 0
    assert dest_chunks.shape == (T // SC_G, TOP_K, SC_G)

    @functools.partial(
        pl.kernel, mesh=_sc_mesh(),
        out_type=jax.ShapeDtypeStruct((n_slots, width), table.dtype),
        scratch_types=[
            pltpu.VMEM((n_steps, TOP_K, SC_G), I32),
            pltpu.VMEM((2, SC_G, width), table.dtype),
            pltpu.SemaphoreType.DMA((2,)),
            pltpu.SemaphoreType.DMA((2,)),
        ],
        name="sc_dispatch_scatter",
    )
    def k(table_hbm, dest_hbm, out_hbm, dst_v, rows_v, rsem, wsem):
        wid = lax.axis_index("s") * SC_CORES + lax.axis_index("c")
        pltpu.sync_copy(dest_hbm.at[pl.ds(wid * n_steps, n_steps)], dst_v)

        def read(s, b):
            return pltpu.make_async_copy(table_hbm.at[pl.ds(wid * per_w + s * SC_G, SC_G)], rows_v.at[b], rsem.at[b])

        def scatter(s, kk, b):
            return pltpu.make_async_copy(rows_v.at[b], out_hbm.at[dst_v.at[s, kk]], wsem.at[b])

        read(0, 0).start()

        @pl.loop(0, n_steps, step=2)
        def _(s0):
            for b in range(2):
                s = s0 + b
                read(s, b).wait()

                @pl.when(s + 1 < n_steps)
                def _():
                    @pl.when(s >= 1)
                    def _():
                        for kk in range(TOP_K):
                            scatter(s - 1, kk, 1 - b).wait()
                    read(s + 1, 1 - b).start()

                for kk in range(TOP_K):
                    scatter(s, kk, b).start()

        for kk in range(TOP_K):
            scatter(n_steps - 2, kk, 0).wait()
            scatter(n_steps - 1, kk, 1).wait()

    return k(table, dest_chunks)


def _sc_combine_gather(table, dest_flat):
    n_rows, width = dest_flat.shape[0], table.shape[1]
    n_workers = SC_CORES * SC_SUBCORES
    per_w = n_rows // n_workers
    n_steps = per_w // SC_GG
    assert n_rows % n_workers == 0 and per_w % (SC_NBUF * SC_GG) == 0

    @functools.partial(
        pl.kernel, mesh=_sc_mesh(),
        out_type=jax.ShapeDtypeStruct((n_rows, width), table.dtype),
        scratch_types=[
            pltpu.VMEM((per_w,), I32),
            pltpu.VMEM((SC_NBUF, SC_GG, width), table.dtype),
            pltpu.SemaphoreType.DMA((SC_NBUF,)),
            pltpu.SemaphoreType.DMA((SC_NBUF,)),
        ],
        name="sc_combine_gather",
    )
    def k(table_hbm, dest_hbm, out_hbm, dst_v, rows_v, gsem, wsem):
        wid = lax.axis_index("s") * SC_CORES + lax.axis_index("c")
        lo = wid * per_w
        pltpu.sync_copy(dest_hbm.at[pl.ds(lo, per_w)], dst_v)
        _sc_gather_rows(table_hbm, dst_v, out_hbm, lo, n_steps, rows_v, gsem, wsem)

    return k(table, dest_flat)


def _expert_kernel(be_ref, nb_ref, fv_ref, x_ref, wg_ref, wu_ref, bg_ref, bu_ref, wd_ref, prev_ref, y_ref, *,
                   first, last):
    i = pl.program_id(0)
    fv = fv_ref[i]
    tm = x_ref.shape[0]

    def rows_path(r0):
        row = r0 + lax.broadcasted_iota(I32, (tm - r0, 1), 0)
        x = _unpack_pairs(jnp.where(row >= fv, x_ref[r0:, :], jnp.uint32(0))).astype(BF16)
        gate = jnp.minimum(_dot(x, wg_ref[...]) + bg_ref[...], SWIGLU_LIMIT)
        up = jnp.clip(_dot(x, wu_ref[...]) + bu_ref[...], -SWIGLU_LIMIT, SWIGLU_LIMIT)
        act = (up + 1.0) * gate * _sigmoid(SWIGLU_ALPHA * gate)
        prev = prev_ref[...] if first else prev_ref[r0:, :]
        y = prev + _dot(act.astype(BF16), wd_ref[...])
        y_ref[r0:, :] = _pack_pairs(y) if last else y
        if r0:
            y_ref[:r0, :] = jnp.zeros((r0, y_ref.shape[1]), y_ref.dtype)

    for r0 in range(0, tm, MOE_TS):
        @pl.when((i < nb_ref[0]) & (fv >= r0) & (fv < r0 + MOE_TS))
        def _():
            rows_path(r0)


def _expert_pass(c, block_e, nb_used, first_valid, x_sorted, gu_w, gu_b, down_w, prev):
    P = x_sorted.shape[0]
    nc = D_FF // MOE_TF
    first, last = c == 0, c == nc - 1
    rowblk = lambda i, be, nbu, nv: (jnp.minimum(i, nbu[0] - 1), 0)
    prev_spec = (pl.BlockSpec((None, 1, D_MODEL), lambda i, be, nbu, nv: (be[i], 0, 0)) if first
                 else pl.BlockSpec((MOE_TM, D_MODEL), rowblk))
    grid_spec = pltpu.PrefetchScalarGridSpec(
        num_scalar_prefetch=3,
        grid=(P // MOE_TM,),
        in_specs=[
            pl.BlockSpec((MOE_TM, PACK_W), rowblk),
            pl.BlockSpec((None, D_MODEL, MOE_TF), lambda i, be, nbu, nv: (be[i], 0, c)),
            pl.BlockSpec((None, D_MODEL, MOE_TF), lambda i, be, nbu, nv: (be[i], 0, nc + c)),
            pl.BlockSpec((None, 1, MOE_TF), lambda i, be, nbu, nv: (be[i], 0, c)),
            pl.BlockSpec((None, 1, MOE_TF), lambda i, be, nbu, nv: (be[i], 0, nc + c)),
            pl.BlockSpec((None, MOE_TF, D_MODEL), lambda i, be, nbu, nv: (be[i], c, 0)),
            prev_spec,
        ],
        out_specs=pl.BlockSpec((MOE_TM, PACK_W if last else D_MODEL), rowblk),
    )
    return pl.pallas_call(
        functools.partial(_expert_kernel, first=first, last=last),
        grid_spec=grid_spec,
        out_shape=jax.ShapeDtypeStruct((P, PACK_W), U32) if last else jax.ShapeDtypeStruct((P, D_MODEL), F32),
        compiler_params=_params(("arbitrary",)),
        name="experts_%d" % c,
    )(block_e, nb_used, first_valid, x_sorted, gu_w, gu_w, gu_b, gu_b, down_w, prev)


def _combine_kernel(x_ref, y_ref, gate_ref, g_ref, o_ref):
    acc = x_ref[...]
    gates = gate_ref[...]
    for kk in range(TOP_K):
        acc = acc + gates[:, kk:kk + 1] * _unpack_pairs(y_ref[kk])
    o_ref[...] = _rms(acc, g_ref[...])


def _combine(x2, y_g, gates, g):
    T = x2.shape[0]
    return pl.pallas_call(
        _combine_kernel,
        grid=(T // CB_TM,),
        in_specs=[
            pl.BlockSpec((CB_TM, D_MODEL), lambda i: (i, 0)),
            pl.BlockSpec((TOP_K, CB_TM, PACK_W), lambda i: (0, i, 0)),
            pl.BlockSpec((CB_TM, LANES), lambda i: (i, 0)),
            pl.BlockSpec((1, D_MODEL), lambda i: (0, 0)),
        ],
        out_specs=pl.BlockSpec((CB_TM, D_MODEL), lambda i: (i, 0)),
        out_shape=jax.ShapeDtypeStruct((T, D_MODEL), F32),
        compiler_params=_params(("arbitrary",)),
        name="combine",
    )(x2, y_g, gates, g)


def _row(v):
    return v.reshape(1, -1).astype(F32)


def _layer(x, mem, norm_mix_g, w_in, gla_gate_w, gla_gate_b, gla_norm_g, gla_out_w, conv_w, conv_b,
           conv_out_w, mix_out_w, norm_xattn_g, norm_mem_g, xq_w, xk_w, xv_w, xo_w, norm_moe_g,
           router_w, router_b, gu_w, gu_b, down_w, down_b):
    T = x.shape[0]
    w_in_t = w_in.T
    w_main = _w_in_prep(w_in_t)
    a0 = 3 * D_MODEL
    w_alow_t = jnp.pad(w_in_t[a0:a0 + GLA_GATE_RANK], ((0, LANES - GLA_GATE_RANK), (0, 0)))
    gate_w_pad = jnp.pad(gla_gate_w, ((0, LANES - GLA_GATE_RANK), (0, 0)))
    conv_w8 = jnp.pad(conv_w, ((0, 8 - CONV_WIDTH), (0, 0)))

    p, a_low, gu_wb = _in_proj(x, _row(norm_mix_g), w_main, w_alow_t, gu_w.reshape(-1, 2 * D_FF))
    og, down_wb = _gla(p, a_low, gate_w_pad, _row(gla_gate_b), _row(gla_norm_g), down_w.reshape(-1, D_MODEL))
    gu_wb = gu_wb.reshape(gu_w.shape)
    down_wb = down_wb.reshape(down_w.shape)
    x1 = _merge(og, p, conv_w8, _row(conv_b), gla_out_w.astype(BF16), conv_out_w.astype(BF16),
                mix_out_w.astype(BF16), x)

    kmem, vmem = _mem_kv(mem, _row(norm_mem_g), xk_w.astype(BF16), xv_w.astype(BF16))
    rw = jnp.pad(router_w, ((0, 0), (0, LANES - N_EXPERTS))).astype(BF16)
    rb = jnp.pad(router_b, (0, LANES - N_EXPERTS)).reshape(1, LANES)
    x2, hm, idx, gates, rank, counts = _xattn_router(
        x1, _row(norm_xattn_g), xq_w.astype(BF16), kmem, vmem, xo_w.astype(BF16), _row(norm_moe_g), rw, rb)

    cnt = counts[0, :N_EXPERTS].astype(I32)
    padded = (cnt + MOE_TM - 1) // MOE_TM * MOE_TM
    pend = jnp.cumsum(padded)
    pstart = pend - padded
    first_slot = pend - cnt
    first_slot_row = jnp.pad(first_slot, (0, LANES - N_EXPERTS)).reshape(1, LANES).astype(F32)
    nb = (T * TOP_K) // MOE_TM + N_EXPERTS
    blk_start = (jnp.arange(nb, dtype=I32) * MOE_TM)[:, None]
    block_e = jnp.minimum(jnp.sum(pend[None, :] <= blk_start, axis=1), N_EXPERTS - 1).astype(I32)
    nb_used = (pend[-1] // MOE_TM).astype(I32).reshape(1)
    in_group = (pstart[None, :] <= blk_start) & (blk_start < pend[None, :])
    first_valid = jnp.sum(jnp.where(in_group, jnp.clip(first_slot[None, :] - blk_start, 0, MOE_TM), 0),
                          axis=1).astype(I32)

    dest = _dispatch(idx, rank, first_slot_row)[:, :TOP_K]
    dest_chunks = dest.reshape(T // SC_G, SC_G, TOP_K).transpose(0, 2, 1)
    x_sorted = _sc_dispatch_scatter(hm, dest_chunks, nb * MOE_TM)

    gu_b3 = gu_b.reshape(N_EXPERTS, 1, 2 * D_FF)
    y = down_b.reshape(N_EXPERTS, 1, D_MODEL)
    for c in range(D_FF // MOE_TF):
        y = _expert_pass(c, block_e, nb_used, first_valid, x_sorted, gu_wb, gu_b3, down_wb, y)
    y_g = _sc_combine_gather(y, dest.T.reshape(-1)).reshape(TOP_K, T, PACK_W)
    return x2, y_g, gates


def kernel(x, mem, norm_mix_g, w_in, gla_gate_w, gla_gate_b, gla_norm_g, gla_out_w, conv_w, conv_b, conv_out_w, mix_out_w, norm_xattn_g, norm_mem_g, xq_w, xk_w, xv_w, xo_w, norm_moe_g, router_w, router_b, expert_gu_w, expert_gu_b, expert_down_w, expert_down_b, norm_final_g):
    assert x.shape[0] == 1 and mem.shape[0] == 1 and w_in.shape[0] == 1
    x2, y_g, gates = _layer(
        x[0], mem[0], norm_mix_g[0], w_in[0], gla_gate_w[0], gla_gate_b[0], gla_norm_g[0], gla_out_w[0],
        conv_w[0], conv_b[0], conv_out_w[0], mix_out_w[0], norm_xattn_g[0], norm_mem_g[0], xq_w[0], xk_w[0],
        xv_w[0], xo_w[0], norm_moe_g[0], router_w[0], router_b[0], expert_gu_w[0], expert_gu_b[0],
        expert_down_w[0], expert_down_b[0])
    out = _combine(x2, y_g, gates, _row(norm_final_g))
    return out[None]
```

```python
import functools

import jax
import jax.numpy as jnp
from jax import lax
from jax.experimental import pallas as pl
from jax.experimental.pallas import tpu as pltpu
from jax.experimental.pallas import tpu_sc as plsc

F32 = jnp.float32
BF16 = jnp.bfloat16
I32 = jnp.int32
U32 = jnp.uint32

D_MODEL = 2048
MEM_LEN = 256
GLA_HEADS = 4
GLA_DK = 256
GLA_DV = 512
GLA_GATE_RANK = 16
GLA_GATE_TEMP = 16.0
CONV_WIDTH = 3
XATTN_HEADS = 4
XATTN_HEAD_DIM = 128
XATTN_DIM = XATTN_HEADS * XATTN_HEAD_DIM
N_EXPERTS = 32
TOP_K = 4
D_FF = D_MODEL
SWIGLU_LIMIT = 7.0
SWIGLU_ALPHA = 1.702
NORM_EPS = 1e-5

LANES = 128
SC_CORES, SC_SUBCORES, SC_LANES = 2, 16, 16
PACK_W = D_MODEL // 2
P_COLS = 8 * D_MODEL
VMEM_LIMIT = 56 * 1024 * 1024

IN_TM, IN_TN = 1024, 1024
GLA_CH = 256
GLA_RB = 512
GLA_HG = 2
MG_TM = 256
XA_TM = 512
DP_TM = 2048
MOE_TM = 512
MOE_TS = 128
MOE_TF = 1024
CB_TM = 512
SC_G = 32
SC_GG, SC_NBUF = 16, 4


def _params(sem):
    return pltpu.CompilerParams(dimension_semantics=sem, vmem_limit_bytes=VMEM_LIMIT)


def _rms(x, g):
    return x * lax.rsqrt(jnp.mean(x * x, axis=-1, keepdims=True) + NORM_EPS) * g


def _dot(a, b):
    return jnp.dot(a, b, preferred_element_type=F32)


def _dot_nt(a, b):
    return lax.dot_general(a, b, (((1,), (1,)), ((), ())), preferred_element_type=F32)


def _dot_tn(a, b):
    return lax.dot_general(a, b, (((0,), (0,)), ((), ())), preferred_element_type=F32)


def _split2(x):
    hi = x.astype(BF16)
    lo = (x - hi.astype(F32)).astype(BF16)
    return hi, lo


def _split3(x):
    hi = x.astype(BF16)
    r = x - hi.astype(F32)
    mid = r.astype(BF16)
    lo = (r - mid.astype(F32)).astype(BF16)
    return hi, mid, lo


def _sigmoid(x):
    return 1.0 / (1.0 + jnp.exp(-x))


def _pack_pairs(x):
    w = x.shape[1] // 2
    u = lax.bitcast_convert_type(x.astype(BF16).astype(F32), U32)
    return (u[:, :w] >> 16) | u[:, w:]


def _unpack_pairs(u):
    lo = lax.bitcast_convert_type(u << 16, F32)
    hi = lax.bitcast_convert_type(u & jnp.uint32(0xFFFF0000), F32)
    return jnp.concatenate([lo, hi], axis=1)


def _side_cast_spec(w2d, n_steps, step_of):
    rows, cols = w2d.shape
    assert rows % n_steps == 0
    return pl.BlockSpec((rows // n_steps, cols), lambda *ids: (step_of(*ids), 0))


def _winprep_kernel(wt_ref, main_ref):
    main_ref[...] = wt_ref[...].T.astype(BF16)


def _w_in_prep(w_in_t):
    rows, d = w_in_t.shape
    a0 = 3 * D_MODEL
    assert rows == P_COLS + GLA_GATE_RANK and a0 % IN_TN == 0
    n_left = a0 // IN_TN
    assert IN_TN % GLA_GATE_RANK == 0
    row0 = lambda j: ((j * (IN_TN // GLA_GATE_RANK) + (j >= n_left).astype(I32)) * GLA_GATE_RANK, 0)
    return pl.pallas_call(
        _winprep_kernel,
        grid=(P_COLS // IN_TN,),
        in_specs=[pl.BlockSpec((pl.Element(IN_TN), pl.Element(d)), row0)],
        out_specs=pl.BlockSpec((d, IN_TN), lambda j: (0, j)),
        out_shape=jax.ShapeDtypeStruct((d, P_COLS), BF16),
        compiler_params=_params(("arbitrary",)),
        name="w_in_prep",
    )(w_in_t)


def _inproj_kernel(x_ref, g_ref, w_ref, wa_ref, cin_ref, p_ref, a_ref, cout_ref, h_scr):
    @pl.when(pl.program_id(1) == 0)
    def _():
        hb = _rms(x_ref[...], g_ref[...]).astype(BF16)
        h_scr[...] = hb
        a_ref[...] = _dot_nt(hb, wa_ref[...].astype(BF16))

    p_ref[...] = _dot(h_scr[...], w_ref[...]).astype(BF16)
    cout_ref[...] = cin_ref[...].astype(BF16)


def _in_proj(x, g, w_main, w_alow_t, w_cast):
    T = x.shape[0]
    ni, nj = T // IN_TM, P_COLS // IN_TN
    cast_spec = _side_cast_spec(w_cast, ni * nj, lambda i, j: i * nj + j)
    return pl.pallas_call(
        _inproj_kernel,
        grid=(ni, nj),
        in_specs=[
            pl.BlockSpec((IN_TM, D_MODEL), lambda i, j: (i, 0)),
            pl.BlockSpec((1, D_MODEL), lambda i, j: (0, 0)),
            pl.BlockSpec((D_MODEL, IN_TN), lambda i, j: (0, j)),
            pl.BlockSpec((LANES, D_MODEL), lambda i, j: (0, 0)),
            cast_spec,
        ],
        out_specs=[
            pl.BlockSpec((IN_TM, IN_TN), lambda i, j: (i, j)),
            pl.BlockSpec((IN_TM, LANES), lambda i, j: (i, 0)),
            cast_spec,
        ],
        out_shape=[
            jax.ShapeDtypeStruct((T, P_COLS), BF16),
            jax.ShapeDtypeStruct((T, LANES), F32),
            jax.ShapeDtypeStruct(w_cast.shape, BF16),
        ],
        scratch_shapes=[pltpu.VMEM((IN_TM, D_MODEL), BF16)],
        compiler_params=_params(("arbitrary", "arbitrary")),
        name="in_proj",
    )(x, g, w_main, w_alow_t, w_cast)


def _gla_level_weights(b, la, row):
    ch, dk = b.shape
    out = []
    s = ch // 2
    while s >= 4:
        bb = b.reshape(ch // (2 * s), 2 * s, dk)
        d = (bb - bb[:, s - 1:s, :]).reshape(ch, dk)
        out.append((s, jnp.exp(-jnp.abs(d))))
        s //= 2
    la_prev = pltpu.roll(la, 1, 0)
    la_next = pltpu.roll(la, ch - 1, 0)
    r4 = row & 3
    d2 = jnp.where(r4 == 0, la_next, jnp.where(r4 == 1, 0.0, jnp.where(r4 == 2, la, la + la_prev)))
    out.append((2, jnp.exp(-jnp.abs(d2))))
    d1 = jnp.where((row & 1) == 1, la, 0.0)
    out.append((1, jnp.exp(-jnp.abs(d1))))
    return out


def _gla_kernel(q_ref, k_ref, v_ref, r_ref, a_ref, gw_ref, gb_ref, ng_ref, cin_ref, o_ref, cout_ref, st_scr):
    ch = GLA_CH
    cout_ref[...] = cin_ref[...].astype(BF16)

    @pl.when(pl.program_id(1) == 0)
    def _():
        st_scr[...] = jnp.zeros_like(st_scr)

    row = lax.broadcasted_iota(jnp.int32, (ch, 1), 0)
    col = lax.broadcasted_iota(jnp.int32, (1, ch), 1)
    tril = (row >= col).astype(BF16)
    levels = []
    s = ch // 2
    while s >= 1:
        levels.append((s, ((row ^ col) < 2 * s) & ((row & (2 * s - 1)) >= s) & ((col & (2 * s - 1)) < s)))
        s //= 2
    gw_hi, gw_lo = _split2(gw_ref[...])
    gate_b = gb_ref[...]
    norm_g = ng_ref[...]
    q_scale = jnp.asarray(GLA_DK ** -0.5, BF16)

    def head(sl, hh, a_hi, a_lo):
        ks = slice(hh * GLA_DK, (hh + 1) * GLA_DK)
        vs = slice(hh * GLA_DV, (hh + 1) * GLA_DV)
        qb = q_ref[sl, ks] * q_scale
        kb = k_ref[sl, ks]
        q, k = qb.astype(F32), kb.astype(F32)
        v = v_ref[sl, vs]
        z = _dot(a_hi, gw_hi[:, ks]) + _dot(a_lo, gw_hi[:, ks]) + _dot(a_hi, gw_lo[:, ks]) + gate_b[:, ks]
        la = (jnp.minimum(z, 0.0) - jnp.log(1.0 + jnp.exp(-jnp.abs(z)))) * (1.0 / GLA_GATE_TEMP)
        l_hi, l_mid, l_lo = _split3(la)
        b = _dot(tril, l_hi) + _dot(tril, l_mid) + _dot(tril, l_lo)

        scores = jnp.where(row == col, _dot_nt(qb, kb), 0.0)
        for (s, mask), (_, w) in zip(levels, _gla_level_weights(b, la, row)):
            scores = jnp.where(mask, _dot_nt((q * w).astype(BF16), (k * w).astype(BF16)), scores)
        o = _dot(scores.astype(BF16), v)

        st = st_scr[hh]
        o = o + _dot_nt((q * jnp.exp(b)).astype(BF16), st.astype(BF16))
        b_last = b[ch - 1:ch, :]
        kd = (k * jnp.exp(b_last - b)).astype(BF16)
        st_scr[hh] = st * jnp.exp(b_last) + _dot_tn(v, kd)

        o = _rms(o, norm_g)
        r = r_ref[sl, vs].astype(F32)
        o_ref[sl, vs] = (o * (r * _sigmoid(r))).astype(BF16)

    def chunk(c, carry):
        sl = pl.ds(pl.multiple_of(c * ch, ch), ch)
        a_hi, a_lo = _split2(a_ref[sl, :])
        for hh in range(GLA_HG):
            head(sl, hh, a_hi, a_lo)
        return carry

    lax.fori_loop(0, GLA_RB // ch, chunk, 0)


def _gla(p, a_low, gate_w_pad, gate_b, norm_g, w_cast):
    T = p.shape[0]
    ni = T // GLA_RB
    ng = GLA_HEADS // GLA_HG
    kw, vw = GLA_HG * GLA_DK, GLA_HG * GLA_DV
    cast_spec = _side_cast_spec(w_cast, ng * ni, lambda h, i: h * ni + i)
    return pl.pallas_call(
        _gla_kernel,
        grid=(ng, ni),
        in_specs=[
            pl.BlockSpec((GLA_RB, kw), lambda h, i: (i, h)),
            pl.BlockSpec((GLA_RB, kw), lambda h, i: (i, ng + h)),
            pl.BlockSpec((GLA_RB, vw), lambda h, i: (i, ng + h)),
            pl.BlockSpec((GLA_RB, vw), lambda h, i: (i, 2 * ng + h)),
            pl.BlockSpec((GLA_RB, LANES), lambda h, i: (i, 0)),
            pl.BlockSpec((LANES, kw), lambda h, i: (0, h)),
            pl.BlockSpec((1, kw), lambda h, i: (0, h)),
            pl.BlockSpec((1, GLA_DV), lambda h, i: (0, 0)),
            cast_spec,
        ],
        out_specs=[pl.BlockSpec((GLA_RB, vw), lambda h, i: (i, h)), cast_spec],
        out_shape=[jax.ShapeDtypeStruct((T, GLA_HEADS * GLA_DV), BF16),
                   jax.ShapeDtypeStruct(w_cast.shape, BF16)],
        scratch_shapes=[pltpu.VMEM((GLA_HG, GLA_DV, GLA_DK), F32)],
        compiler_params=_params(("arbitrary", "arbitrary")),
        name="gla",
    )(p, p, p, p, a_low, gate_w_pad, gate_b, norm_g, w_cast)


def _merge_kernel(og_ref, cb_ref, cc_ref, ch_ref, ccp_ref, chp_ref, cw_ref, cbias_ref,
                  ga_ref, gb_ref, wa_ref, wb_ref, wm_ref, x_ref, o_ref):
    tm = cc_ref.shape[0]
    u = cc_ref[...].astype(F32) * ch_ref[...].astype(F32)
    up = ccp_ref[...].astype(F32) * chp_ref[...].astype(F32)
    up = jnp.where(pl.program_id(0) == 0, 0.0, up)
    hp = up.shape[0]
    row = lax.broadcasted_iota(jnp.int32, (tm, 1), 0)
    u1 = jnp.where(row == 0, up[hp - 1:hp, :], pltpu.roll(u, 1, 0))
    u2 = jnp.where(row == 0, up[hp - 2:hp - 1, :],
                   jnp.where(row == 1, up[hp - 1:hp, :], pltpu.roll(u, 2, 0)))
    cw = cw_ref[...]
    uc = cw[0:1, :] * u2 + cw[1:2, :] * u1 + cw[2:3, :] * u + cbias_ref[...]
    zb = (cb_ref[...].astype(F32) * uc).astype(BF16)

    ya = _dot(og_ref[...], wa_ref[...])
    yb = _dot(zb, wb_ref[...])
    merged = (_sigmoid(ga_ref[...].astype(F32)) * ya + _sigmoid(gb_ref[...].astype(F32)) * yb).astype(BF16)
    o_ref[...] = x_ref[...] + _dot(merged, wm_ref[...])


def _merge(og, p, conv_w8, conv_b, gla_out_w, conv_out_w, mix_out_w, x):
    T = og.shape[0]
    halo = 16
    prev = lambda blk: (lambda i: (jnp.maximum(i * (MG_TM // halo) - 1, 0), blk))
    pblk = lambda blk: pl.BlockSpec((MG_TM, D_MODEL), lambda i: (i, blk))
    const = lambda shape: pl.BlockSpec(shape, lambda i: (0, 0), pipeline_mode=pl.Buffered(1))
    return pl.pallas_call(
        _merge_kernel,
        grid=(T // MG_TM,),
        in_specs=[
            pblk(0),
            pblk(3), pblk(4), pblk(5),
            pl.BlockSpec((halo, D_MODEL), prev(4)),
            pl.BlockSpec((halo, D_MODEL), prev(5)),
            const((8, D_MODEL)),
            const((1, D_MODEL)),
            pblk(6), pblk(7),
            const((D_MODEL, D_MODEL)), const((D_MODEL, D_MODEL)), const((D_MODEL, D_MODEL)),
            pblk(0),
        ],
        out_specs=pblk(0),
        out_shape=jax.ShapeDtypeStruct((T, D_MODEL), F32),
        compiler_params=_params(("arbitrary",)),
        name="merge",
    )(og, p, p, p, p, p, conv_w8, conv_b, p, p, gla_out_w, conv_out_w, mix_out_w, x)


def _memkv_kernel(m_ref, g_ref, wk_ref, wv_ref, k_ref, v_ref):
    mn = _rms(m_ref[...], g_ref[...]).astype(BF16)
    k_ref[...] = _dot(mn, wk_ref[...]).astype(BF16)
    v_ref[...] = _dot(mn, wv_ref[...]).astype(BF16)


def _mem_kv(mem, g, wk, wv):
    return pl.pallas_call(
        _memkv_kernel,
        out_shape=[jax.ShapeDtypeStruct((MEM_LEN, XATTN_DIM), BF16)] * 2,
        compiler_params=pltpu.CompilerParams(vmem_limit_bytes=VMEM_LIMIT),
        name="mem_kv",
    )(mem, g, wk, wv)


def _xattn_tile(x_ref, g_ref, wq_ref, k_ref, v_ref, wo_ref):
    x = x_ref[...]
    hq = _rms(x, g_ref[...]).astype(BF16)
    q = (_dot(hq, wq_ref[...]) * (XATTN_HEAD_DIM ** -0.5)).astype(BF16)
    outs = []
    for h in range(XATTN_HEADS):
        cs = slice(h * XATTN_HEAD_DIM, (h + 1) * XATTN_HEAD_DIM)
        s = _dot_nt(q[:, cs], k_ref[:, cs])
        s = s - jnp.max(s, axis=-1, keepdims=True)
        e = jnp.exp(s)
        p = e / jnp.sum(e, axis=-1, keepdims=True)
        outs.append(_dot(p.astype(BF16), v_ref[:, cs]))
    o = jnp.concatenate(outs, axis=-1).astype(BF16)
    return x + _dot(o, wo_ref[...])


def _xattn_router_kernel(x_ref, gx_ref, wq_ref, k_ref, v_ref, wo_ref, gm_ref, rw_ref, rb_ref,
                         x2_ref, hm_ref, idx_ref, gate_ref, rank_ref, cnt_ref, carry_scr):
    x2 = _xattn_tile(x_ref, gx_ref, wq_ref, k_ref, v_ref, wo_ref)
    x2_ref[...] = x2
    _route_tile(x2, gm_ref, rw_ref, rb_ref, hm_ref, idx_ref, gate_ref, rank_ref, cnt_ref, carry_scr)


def _route_tile(x2, g_ref, w_ref, b_ref, hm_ref, idx_ref, gate_ref, rank_ref, cnt_ref, carry_scr):
    tm = x2.shape[0]

    @pl.when(pl.program_id(0) == 0)
    def _():
        carry_scr[...] = jnp.zeros_like(carry_scr)

    h = _rms(x2, g_ref[...])
    hb = h.astype(BF16)
    hm_ref[...] = _pack_pairs(h)
    lane = lax.broadcasted_iota(jnp.int32, (tm, LANES), 1)
    logits = jnp.where(lane < N_EXPERTS, _dot(hb, w_ref[...]) + b_ref[...], -jnp.inf)

    vals, idxs = [], []
    work = logits
    lane_f = lane.astype(F32)
    for _ in range(TOP_K):
        m = jnp.max(work, axis=-1, keepdims=True)
        am = jnp.min(jnp.where(work == m, lane_f, float(LANES)), axis=-1, keepdims=True).astype(I32)
        vals.append(m)
        idxs.append(am)
        work = jnp.where(lane == am, -jnp.inf, work)
    es = [jnp.exp(v - vals[0]) for v in vals]
    inv = 1.0 / (es[0] + es[1] + es[2] + es[3])

    onehot = (work != logits).astype(BF16)
    r_ = lax.broadcasted_iota(jnp.int32, (tm, tm), 0)
    c_ = lax.broadcasted_iota(jnp.int32, (tm, tm), 1)
    before = _dot((r_ > c_).astype(BF16), onehot) + carry_scr[...]
    carry_scr[...] = carry_scr[...] + jnp.sum(onehot.astype(F32), axis=0, keepdims=True)
    cnt_ref[...] = carry_scr[...]

    idx_out = jnp.zeros((tm, LANES), jnp.int32)
    gate_out = jnp.zeros((tm, LANES), F32)
    rank_out = jnp.zeros((tm, LANES), jnp.int32)
    for kk in range(TOP_K):
        rk = jnp.sum(jnp.where(lane == idxs[kk], before, 0.0), axis=-1, keepdims=True)
        idx_out = jnp.where(lane == kk, idxs[kk], idx_out)
        gate_out = jnp.where(lane == kk, es[kk] * inv, gate_out)
        rank_out = jnp.where(lane == kk, rk.astype(jnp.int32), rank_out)
    idx_ref[...] = idx_out
    gate_ref[...] = gate_out
    rank_ref[...] = rank_out


def _xattn_router(x1, gx, wq, kmem, vmem, wo, gm, rw_pad, rb_pad):
    T = x1.shape[0]
    full = lambda shape: pl.BlockSpec(shape, lambda i: (0, 0))
    row_blk = pl.BlockSpec((XA_TM, LANES), lambda i: (i, 0))
    return pl.pallas_call(
        _xattn_router_kernel,
        grid=(T // XA_TM,),
        in_specs=[
            pl.BlockSpec((XA_TM, D_MODEL), lambda i: (i, 0)),
            full((1, D_MODEL)),
            full((D_MODEL, XATTN_DIM)),
            full((MEM_LEN, XATTN_DIM)),
            full((MEM_LEN, XATTN_DIM)),
            full((XATTN_DIM, D_MODEL)),
            full((1, D_MODEL)),
            full((D_MODEL, LANES)),
            full((1, LANES)),
        ],
        out_specs=[
            pl.BlockSpec((XA_TM, D_MODEL), lambda i: (i, 0)),
            pl.BlockSpec((XA_TM, PACK_W), lambda i: (i, 0)),
            row_blk, row_blk, row_blk,
            full((1, LANES)),
        ],
        out_shape=[
            jax.ShapeDtypeStruct((T, D_MODEL), F32),
            jax.ShapeDtypeStruct((T, PACK_W), U32),
            jax.ShapeDtypeStruct((T, LANES), jnp.int32),
            jax.ShapeDtypeStruct((T, LANES), F32),
            jax.ShapeDtypeStruct((T, LANES), jnp.int32),
            jax.ShapeDtypeStruct((1, LANES), F32),
        ],
        scratch_shapes=[pltpu.VMEM((1, LANES), F32)],
        compiler_params=_params(("arbitrary",)),
        name="xattn_router",
    )(x1, gx, wq, kmem, vmem, wo, gm, rw_pad, rb_pad)


def _dispatch_kernel(idx_ref, rank_ref, pstart_ref, dest_ref):
    tm = idx_ref.shape[0]
    lane = lax.broadcasted_iota(I32, (tm, LANES), 1)
    idx = idx_ref[...]
    pstart = pstart_ref[...]
    dest = rank_ref[...]
    for kk in range(TOP_K):
        start = jnp.sum(jnp.where(lane == idx[:, kk:kk + 1], pstart, 0.0), axis=-1, keepdims=True)
        dest = dest + jnp.where(lane == kk, start.astype(I32), 0)
    dest_ref[...] = dest


def _dispatch(idx, rank, pstart_row):
    T = idx.shape[0]
    blk = pl.BlockSpec((DP_TM, LANES), lambda i: (i, 0))
    return pl.pallas_call(
        _dispatch_kernel,
        grid=(T // DP_TM,),
        in_specs=[blk, blk, pl.BlockSpec((1, LANES), lambda i: (0, 0))],
        out_specs=blk,
        out_shape=jax.ShapeDtypeStruct((T, LANES), I32),
        compiler_params=_params(("arbitrary",)),
        name="dispatch",
    )(idx, rank, pstart_row)


def _sc_mesh():
    return plsc.VectorSubcoreMesh(core_axis_name="c", subcore_axis_name="s",
                                  num_cores=SC_CORES, num_subcores=SC_SUBCORES)


def _sc_gather_rows(table_hbm, idx_v, out_hbm, out_base, n_steps, rows_v, gsem, wsem):
    nbuf = SC_NBUF

    def gather(s, b):
        return pltpu.make_async_copy(table_hbm.at[idx_v.at[pl.ds(s * SC_GG, SC_GG)]], rows_v.at[b], gsem.at[b])

    def write(s, b):
        return pltpu.make_async_copy(rows_v.at[b], out_hbm.at[pl.ds(out_base + s * SC_GG, SC_GG)], wsem.at[b])

    for j in range(nbuf - 1):
        gather(j, j).start()

    @pl.loop(0, n_steps, step=nbuf)
    def _(s0):
        for b in range(nbuf):
            s = s0 + b
            gather(s, b).wait()
            write(s, b).start()
            refill = (b + nbuf - 1) % nbuf

            @pl.when(s + nbuf - 1 < n_steps)
            def _():
                @pl.when(s >= 1)
                def _():
                    write(s - 1, refill).wait()
                gather(s + nbuf - 1, refill).start()

    for j in range(nbuf):
        write(n_steps - nbuf + j, j).wait()


def _sc_dispatch_scatter(table, dest_chunks, n_slots):
    T, width = table.shape
    n_workers = SC_CORES * SC_SUBCORES
    per_w = T // n_workers
    n_steps = per_w // SC_G
    assert T % n_workers == 0 and per_w % (2 * SC_G) == 0
    assert dest_chunks.shape == (T // SC_G, TOP_K, SC_G)

    @functools.partial(
        pl.kernel, mesh=_sc_mesh(),
        out_type=jax.ShapeDtypeStruct((n_slots, width), table.dtype),
        scratch_types=[
            pltpu.VMEM((n_steps, TOP_K, SC_G), I32),
            pltpu.VMEM((2, SC_G, width), table.dtype),
            pltpu.SemaphoreType.DMA((2,)),
            pltpu.SemaphoreType.DMA((2,)),
        ],
        name="sc_dispatch_scatter",
    )
    def k(table_hbm, dest_hbm, out_hbm, dst_v, rows_v, rsem, wsem):
        wid = lax.axis_index("s") * SC_CORES + lax.axis_index("c")
        pltpu.sync_copy(dest_hbm.at[pl.ds(wid * n_steps, n_steps)], dst_v)

        def read(s, b):
            return pltpu.make_async_copy(table_hbm.at[pl.ds(wid * per_w + s * SC_G, SC_G)], rows_v.at[b], rsem.at[b])

        def scatter(s, kk, b):
            return pltpu.make_async_copy(rows_v.at[b], out_hbm.at[dst_v.at[s, kk]], wsem.at[b])

        read(0, 0).start()

        @pl.loop(0, n_steps, step=2)
        def _(s0):
            for b in range(2):
                s = s0 + b
                read(s, b).wait()

                @pl.when(s + 1 < n_steps)
                def _():
                    @pl.when(s >= 1)
                    def _():
                        for kk in range(TOP_K):
                            scatter(s - 1, kk, 1 - b).wait()
                    read(s + 1, 1 - b).start()

                for kk in range(TOP_K):
                    scatter(s, kk, b).start()

        for kk in range(TOP_K):
            scatter(n_steps - 2, kk, 0).wait()
            scatter(n_steps - 1, kk, 1).wait()

    return k(table, dest_chunks)


def _sc_combine_gather(table, dest_flat):
    n_rows, width = dest_flat.shape[0], table.shape[1]
    n_workers = SC_CORES * SC_SUBCORES
    per_w = n_rows // n_workers
    n_steps = per_w // SC_GG
    assert n_rows % n_workers == 0 and per_w % (SC_NBUF * SC_GG) == 0

    @functools.partial(
        pl.kernel, mesh=_sc_mesh(),
        out_type=jax.ShapeDtypeStruct((n_rows, width), table.dtype),
        scratch_types=[
            pltpu.VMEM((per_w,), I32),
            pltpu.VMEM((SC_NBUF, SC_GG, width), table.dtype),
            pltpu.SemaphoreType.DMA((SC_NBUF,)),
            pltpu.SemaphoreType.DMA((SC_NBUF,)),
        ],
        name="sc_combine_gather",
    )
    def k(table_hbm, dest_hbm, out_hbm, dst_v, rows_v, gsem, wsem):
        wid = lax.axis_index("s") * SC_CORES + lax.axis_index("c")
        lo = wid * per_w
        pltpu.sync_copy(dest_hbm.at[pl.ds(lo, per_w)], dst_v)
        _sc_gather_rows(table_hbm, dst_v, out_hbm, lo, n_steps, rows_v, gsem, wsem)

    return k(table, dest_flat)


def _expert_kernel(be_ref, nb_ref, fv_ref, x_ref, wg_ref, wu_ref, bg_ref, bu_ref, wd_ref, prev_ref, y_ref, *,
                   first, last):
    i = pl.program_id(0)
    fv = fv_ref[i]
    tm = x_ref.shape[0]

    def rows_path(r0):
        row = r0 + lax.broadcasted_iota(I32, (tm - r0, 1), 0)
        x = _unpack_pairs(jnp.where(row >= fv, x_ref[r0:, :], jnp.uint32(0))).astype(BF16)
        gate = jnp.minimum(_dot(x, wg_ref[...]) + bg_ref[...], SWIGLU_LIMIT)
        up = jnp.clip(_dot(x, wu_ref[...]) + bu_ref[...], -SWIGLU_LIMIT, SWIGLU_LIMIT)
        act = (up + 1.0) * gate * _sigmoid(SWIGLU_ALPHA * gate)
        prev = prev_ref[...] if first else prev_ref[r0:, :]
        y = prev + _dot(act.astype(BF16), wd_ref[...])
        y_ref[r0:, :] = _pack_pairs(y) if last else y
        if r0:
            y_ref[:r0, :] = jnp.zeros((r0, y_ref.shape[1]), y_ref.dtype)

    for r0 in range(0, tm, MOE_TS):
        @pl.when((i < nb_ref[0]) & (fv >= r0) & (fv < r0 + MOE_TS))
        def _():
            rows_path(r0)


def _expert_pass(c, block_e, nb_used, first_valid, x_sorted, gu_w, gu_b, down_w, prev):
    P = x_sorted.shape[0]
    nc = D_FF // MOE_TF
    first, last = c == 0, c == nc - 1
    rowblk = lambda i, be, nbu, nv: (jnp.minimum(i, nbu[0] - 1), 0)
    prev_spec = (pl.BlockSpec((None, 1, D_MODEL), lambda i, be, nbu, nv: (be[i], 0, 0)) if first
                 else pl.BlockSpec((MOE_TM, D_MODEL), rowblk))
    grid_spec = pltpu.PrefetchScalarGridSpec(
        num_scalar_prefetch=3,
        grid=(P // MOE_TM,),
        in_specs=[
            pl.BlockSpec((MOE_TM, PACK_W), rowblk),
            pl.BlockSpec((None, D_MODEL, MOE_TF), lambda i, be, nbu, nv: (be[i], 0, c)),
            pl.BlockSpec((None, D_MODEL, MOE_TF), lambda i, be, nbu, nv: (be[i], 0, nc + c)),
            pl.BlockSpec((None, 1, MOE_TF), lambda i, be, nbu, nv: (be[i], 0, c)),
            pl.BlockSpec((None, 1, MOE_TF), lambda i, be, nbu, nv: (be[i], 0, nc + c)),
            pl.BlockSpec((None, MOE_TF, D_MODEL), lambda i, be, nbu, nv: (be[i], c, 0)),
            prev_spec,
        ],
        out_specs=pl.BlockSpec((MOE_TM, PACK_W if last else D_MODEL), rowblk),
    )
    return pl.pallas_call(
        functools.partial(_expert_kernel, first=first, last=last),
        grid_spec=grid_spec,
        out_shape=jax.ShapeDtypeStruct((P, PACK_W), U32) if last else jax.ShapeDtypeStruct((P, D_MODEL), F32),
        compiler_params=_params(("arbitrary",)),
        name="experts_%d" % c,
    )(block_e, nb_used, first_valid, x_sorted, gu_w, gu_w, gu_b, gu_b, down_w, prev)


def _combine_kernel(x_ref, y_ref, gate_ref, g_ref, o_ref):
    acc = x_ref[...]
    gates = gate_ref[...]
    for kk in range(TOP_K):
        acc = acc + gates[:, kk:kk + 1] * _unpack_pairs(y_ref[kk])
    o_ref[...] = _rms(acc, g_ref[...])


def _combine(x2, y_g, gates, g):
    T = x2.shape[0]
    return pl.pallas_call(
        _combine_kernel,
        grid=(T // CB_TM,),
        in_specs=[
            pl.BlockSpec((CB_TM, D_MODEL), lambda i: (i, 0)),
            pl.BlockSpec((TOP_K, CB_TM, PACK_W), lambda i: (0, i, 0)),
            pl.BlockSpec((CB_TM, LANES), lambda i: (i, 0)),
            pl.BlockSpec((1, D_MODEL), lambda i: (0, 0)),
        ],
        out_specs=pl.BlockSpec((CB_TM, D_MODEL), lambda i: (i, 0)),
        out_shape=jax.ShapeDtypeStruct((T, D_MODEL), F32),
        compiler_params=_params(("arbitrary",)),
        name="combine",
    )(x2, y_g, gates, g)


def _row(v):
    return v.reshape(1, -1).astype(F32)


def _layer(x, mem, norm_mix_g, w_in, gla_gate_w, gla_gate_b, gla_norm_g, gla_out_w, conv_w, conv_b,
           conv_out_w, mix_out_w, norm_xattn_g, norm_mem_g, xq_w, xk_w, xv_w, xo_w, norm_moe_g,
           router_w, router_b, gu_w, gu_b, down_w, down_b):
    T = x.shape[0]
    w_in_t = w_in.T
    w_main = _w_in_prep(w_in_t)
    a0 = 3 * D_MODEL
    w_alow_t = jnp.pad(w_in_t[a0:a0 + GLA_GATE_RANK], ((0, LANES - GLA_GATE_RANK), (0, 0)))
    gate_w_pad = jnp.pad(gla_gate_w, ((0, LANES - GLA_GATE_RANK), (0, 0)))
    conv_w8 = jnp.pad(conv_w, ((0, 8 - CONV_WIDTH), (0, 0)))

    p, a_low, gu_wb = _in_proj(x, _row(norm_mix_g), w_main, w_alow_t, gu_w.reshape(-1, 2 * D_FF))
    og, down_wb = _gla(p, a_low, gate_w_pad, _row(gla_gate_b), _row(gla_norm_g), down_w.reshape(-1, D_MODEL))
    gu_wb = gu_wb.reshape(gu_w.shape)
    down_wb = down_wb.reshape(down_w.shape)
    x1 = _merge(og, p, conv_w8, _row(conv_b), gla_out_w.astype(BF16), conv_out_w.astype(BF16),
                mix_out_w.astype(BF16), x)

    kmem, vmem = _mem_kv(mem, _row(norm_mem_g), xk_w.astype(BF16), xv_w.astype(BF16))
    rw = jnp.pad(router_w, ((0, 0), (0, LANES - N_EXPERTS))).astype(BF16)
    rb = jnp.pad(router_b, (0, LANES - N_EXPERTS)).reshape(1, LANES)
    x2, hm, idx, gates, rank, counts = _xattn_router(
        x1, _row(norm_xattn_g), xq_w.astype(BF16), kmem, vmem, xo_w.astype(BF16), _row(norm_moe_g), rw, rb)

    cnt = counts[0, :N_EXPERTS].astype(I32)
    padded = (cnt + MOE_TM - 1) // MOE_TM * MOE_TM
    pend = jnp.cumsum(padded)
    pstart = pend - padded
    first_slot = pend - cnt
    first_slot_row = jnp.pad(first_slot, (0, LANES - N_EXPERTS)).reshape(1, LANES).astype(F32)
    nb = (T * TOP_K) // MOE_TM + N_EXPERTS
    blk_start = (jnp.arange(nb, dtype=I32) * MOE_TM)[:, None]
    block_e = jnp.minimum(jnp.sum(pend[None, :] <= blk_start, axis=1), N_EXPERTS - 1).astype(I32)
    nb_used = (pend[-1] // MOE_TM).astype(I32).reshape(1)
    in_group = (pstart[None, :] <= blk_start) & (blk_start < pend[None, :])
    first_valid = jnp.sum(jnp.where(in_group, jnp.clip(first_slot[None, :] - blk_start, 0, MOE_TM), 0),
                          axis=1).astype(I32)

    dest = _dispatch(idx, rank, first_slot_row)[:, :TOP_K]
    dest_chunks = dest.reshape(T // SC_G, SC_G, TOP_K).transpose(0, 2, 1)
    x_sorted = _sc_dispatch_scatter(hm, dest_chunks, nb * MOE_TM)

    gu_b3 = gu_b.reshape(N_EXPERTS, 1, 2 * D_FF)
    y = down_b.reshape(N_EXPERTS, 1, D_MODEL)
    for c in range(D_FF // MOE_TF):
        y = _expert_pass(c, block_e, nb_used, first_valid, x_sorted, gu_wb, gu_b3, down_wb, y)
    y_g = _sc_combine_gather(y, dest.T.reshape(-1)).reshape(TOP_K, T, PACK_W)
    return x2, y_g, gates


def kernel(x, mem, norm_mix_g, w_in, gla_gate_w, gla_gate_b, gla_norm_g, gla_out_w, conv_w, conv_b, conv_out_w, mix_out_w, norm_xattn_g, norm_mem_g, xq_w, xk_w, xv_w, xo_w, norm_moe_g, router_w, router_b, expert_gu_w, expert_gu_b, expert_down_w, expert_down_b, norm_final_g):
    assert x.shape[0] == 1 and mem.shape[0] == 1 and w_in.shape[0] == 1
    x2, y_g, gates = _layer(
        x[0], mem[0], norm_mix_g[0], w_in[0], gla_gate_w[0], gla_gate_b[0], gla_norm_g[0], gla_out_w[0],
        conv_w[0], conv_b[0], conv_out_w[0], mix_out_w[0], norm_xattn_g[0], norm_mem_g[0], xq_w[0], xk_w[0],
        xv_w[0], xo_w[0], norm_moe_g[0], router_w[0], router_b[0], expert_gu_w[0], expert_gu_b[0],
        expert_down_w[0], expert_down_b[0])
    out = _combine(x2, y_g, gates, _row(norm_final_g))
    return out[None]
```

```python
import functools

import jax
import jax.numpy as jnp
from jax import lax
from jax.experimental import pallas as pl
from jax.experimental.pallas import tpu as pltpu
from jax.experimental.pallas import tpu_sc as plsc

F32 = jnp.float32
BF16 = jnp.bfloat16
I32 = jnp.int32
U32 = jnp.uint32

D_MODEL = 2048
MEM_LEN = 256
GLA_HEADS = 4
GLA_DK = 256
GLA_DV = 512
GLA_GATE_RANK = 16
GLA_GATE_TEMP = 16.0
CONV_WIDTH = 3
XATTN_HEADS = 4
XATTN_HEAD_DIM = 128
XATTN_DIM = XATTN_HEADS * XATTN_HEAD_DIM
N_EXPERTS = 32
TOP_K = 4
D_FF = D_MODEL
SWIGLU_LIMIT = 7.0
SWIGLU_ALPHA = 1.702
NORM_EPS = 1e-5

LANES = 128
SC_CORES, SC_SUBCORES, SC_LANES = 2, 16, 16
PACK_W = D_MODEL // 2
P_COLS = 8 * D_MODEL
VMEM_LIMIT = 56 * 1024 * 1024

IN_TM, IN_TN = 1024, 1024
GLA_CH = 256
GLA_RB = 512
GLA_HG = 2
MG_TM = 256
XA_TM = 512
DP_TM = 2048
MOE_TM = 512
MOE_TS = 64
MOE_TF = 1024
CB_TM = 512
SC_G = 32
SC_GG, SC_NBUF = 16, 4


def _params(sem):
    return pltpu.CompilerParams(dimension_semantics=sem, vmem_limit_bytes=VMEM_LIMIT)


def _rms(x, g):
    return x * lax.rsqrt(jnp.mean(x * x, axis=-1, keepdims=True) + NORM_EPS) * g


def _dot(a, b):
    return jnp.dot(a, b, preferred_element_type=F32)


def _dot_nt(a, b):
    return lax.dot_general(a, b, (((1,), (1,)), ((), ())), preferred_element_type=F32)


def _dot_tn(a, b):
    return lax.dot_general(a, b, (((0,), (0,)), ((), ())), preferred_element_type=F32)


def _split2(x):
    hi = x.astype(BF16)
    lo = (x - hi.astype(F32)).astype(BF16)
    return hi, lo


def _split3(x):
    hi = x.astype(BF16)
    r = x - hi.astype(F32)
    mid = r.astype(BF16)
    lo = (r - mid.astype(F32)).astype(BF16)
    return hi, mid, lo


def _sigmoid(x):
    return 1.0 / (1.0 + jnp.exp(-x))


def _pack_pairs(x):
    w = x.shape[1] // 2
    u = lax.bitcast_convert_type(x.astype(BF16).astype(F32), U32)
    return (u[:, :w] >> 16) | u[:, w:]


def _unpack_pairs(u):
    lo = lax.bitcast_convert_type(u << 16, F32)
    hi = lax.bitcast_convert_type(u & jnp.uint32(0xFFFF0000), F32)
    return jnp.concatenate([lo, hi], axis=1)


def _side_cast_spec(w2d, n_steps, step_of):
    rows, cols = w2d.shape
    assert rows % n_steps == 0
    return pl.BlockSpec((rows // n_steps, cols), lambda *ids: (step_of(*ids), 0))


def _winprep_kernel(wt_ref, main_ref):
    main_ref[...] = wt_ref[...].T.astype(BF16)


def _w_in_prep(w_in_t):
    rows, d = w_in_t.shape
    a0 = 3 * D_MODEL
    assert rows == P_COLS + GLA_GATE_RANK and a0 % IN_TN == 0
    n_left = a0 // IN_TN
    assert IN_TN % GLA_GATE_RANK == 0
    row0 = lambda j: ((j * (IN_TN // GLA_GATE_RANK) + (j >= n_left).astype(I32)) * GLA_GATE_RANK, 0)
    return pl.pallas_call(
        _winprep_kernel,
        grid=(P_COLS // IN_TN,),
        in_specs=[pl.BlockSpec((pl.Element(IN_TN), pl.Element(d)), row0)],
        out_specs=pl.BlockSpec((d, IN_TN), lambda j: (0, j)),
        out_shape=jax.ShapeDtypeStruct((d, P_COLS), BF16),
        compiler_params=_params(("arbitrary",)),
        name="w_in_prep",
    )(w_in_t)


def _inproj_kernel(x_ref, g_ref, w_ref, wa_ref, cin_ref, p_ref, a_ref, cout_ref, h_scr):
    @pl.when(pl.program_id(1) == 0)
    def _():
        hb = _rms(x_ref[...], g_ref[...]).astype(BF16)
        h_scr[...] = hb
        a_ref[...] = _dot_nt(hb, wa_ref[...].astype(BF16))

    p_ref[...] = _dot(h_scr[...], w_ref[...]).astype(BF16)
    cout_ref[...] = cin_ref[...].astype(BF16)


def _in_proj(x, g, w_main, w_alow_t, w_cast):
    T = x.shape[0]
    ni, nj = T // IN_TM, P_COLS // IN_TN
    cast_spec = _side_cast_spec(w_cast, ni * nj, lambda i, j: i * nj + j)
    return pl.pallas_call(
        _inproj_kernel,
        grid=(ni, nj),
        in_specs=[
            pl.BlockSpec((IN_TM, D_MODEL), lambda i, j: (i, 0)),
            pl.BlockSpec((1, D_MODEL), lambda i, j: (0, 0)),
            pl.BlockSpec((D_MODEL, IN_TN), lambda i, j: (0, j)),
            pl.BlockSpec((LANES, D_MODEL), lambda i, j: (0, 0)),
            cast_spec,
        ],
        out_specs=[
            pl.BlockSpec((IN_TM, IN_TN), lambda i, j: (i, j)),
            pl.BlockSpec((IN_TM, LANES), lambda i, j: (i, 0)),
            cast_spec,
        ],
        out_shape=[
            jax.ShapeDtypeStruct((T, P_COLS), BF16),
            jax.ShapeDtypeStruct((T, LANES), F32),
            jax.ShapeDtypeStruct(w_cast.shape, BF16),
        ],
        scratch_shapes=[pltpu.VMEM((IN_TM, D_MODEL), BF16)],
        compiler_params=_params(("arbitrary", "arbitrary")),
        name="in_proj",
    )(x, g, w_main, w_alow_t, w_cast)


def _gla_level_weights(b, la, row):
    ch, dk = b.shape
    out = []
    s = ch // 2
    while s >= 4:
        bb = b.reshape(ch // (2 * s), 2 * s, dk)
        d = (bb - bb[:, s - 1:s, :]).reshape(ch, dk)
        out.append((s, jnp.exp(-jnp.abs(d))))
        s //= 2
    la_prev = pltpu.roll(la, 1, 0)
    la_next = pltpu.roll(la, ch - 1, 0)
    r4 = row & 3
    d2 = jnp.where(r4 == 0, la_next, jnp.where(r4 == 1, 0.0, jnp.where(r4 == 2, la, la + la_prev)))
    out.append((2, jnp.exp(-jnp.abs(d2))))
    d1 = jnp.where((row & 1) == 1, la, 0.0)
    out.append((1, jnp.exp(-jnp.abs(d1))))
    return out


def _gla_kernel(q_ref, k_ref, v_ref, r_ref, a_ref, gw_ref, gb_ref, ng_ref, cin_ref, o_ref, cout_ref, st_scr):
    ch = GLA_CH
    cout_ref[...] = cin_ref[...].astype(BF16)

    @pl.when(pl.program_id(1) == 0)
    def _():
        st_scr[...] = jnp.zeros_like(st_scr)

    row = lax.broadcasted_iota(jnp.int32, (ch, 1), 0)
    col = lax.broadcasted_iota(jnp.int32, (1, ch), 1)
    tril = (row >= col).astype(BF16)
    levels = []
    s = ch // 2
    while s >= 1:
        levels.append((s, ((row ^ col) < 2 * s) & ((row & (2 * s - 1)) >= s) & ((col & (2 * s - 1)) < s)))
        s //= 2
    gw_hi, gw_lo = _split2(gw_ref[...])
    gate_b = gb_ref[...]
    norm_g = ng_ref[...]
    q_scale = jnp.asarray(GLA_DK ** -0.5, BF16)

    def head(sl, hh, a_hi, a_lo):
        ks = slice(hh * GLA_DK, (hh + 1) * GLA_DK)
        vs = slice(hh * GLA_DV, (hh + 1) * GLA_DV)
        qb = q_ref[sl, ks] * q_scale
        kb = k_ref[sl, ks]
        q, k = qb.astype(F32), kb.astype(F32)
        v = v_ref[sl, vs]
        z = _dot(a_hi, gw_hi[:, ks]) + _dot(a_lo, gw_hi[:, ks]) + _dot(a_hi, gw_lo[:, ks]) + gate_b[:, ks]
        la = (jnp.minimum(z, 0.0) - jnp.log(1.0 + jnp.exp(-jnp.abs(z)))) * (1.0 / GLA_GATE_TEMP)
        l_hi, l_mid, l_lo = _split3(la)
        b = _dot(tril, l_hi) + _dot(tril, l_mid) + _dot(tril, l_lo)

        scores = jnp.where(row == col, _dot_nt(qb, kb), 0.0)
        for (s, mask), (_, w) in zip(levels, _gla_level_weights(b, la, row)):
            scores = jnp.where(mask, _dot_nt((q * w).astype(BF16), (k * w).astype(BF16)), scores)
        o = _dot(scores.astype(BF16), v)

        st = st_scr[hh]
        o = o + _dot_nt((q * jnp.exp(b)).astype(BF16), st.astype(BF16))
        b_last = b[ch - 1:ch, :]
        kd = (k * jnp.exp(b_last - b)).astype(BF16)
        st_scr[hh] = st * jnp.exp(b_last) + _dot_tn(v, kd)

        o = _rms(o, norm_g)
        r = r_ref[sl, vs].astype(F32)
        o_ref[sl, vs] = (o * (r * _sigmoid(r))).astype(BF16)

    def chunk(c, carry):
        sl = pl.ds(pl.multiple_of(c * ch, ch), ch)
        a_hi, a_lo = _split2(a_ref[sl, :])
        for hh in range(GLA_HG):
            head(sl, hh, a_hi, a_lo)
        return carry

    lax.fori_loop(0, GLA_RB // ch, chunk, 0)


def _gla(p, a_low, gate_w_pad, gate_b, norm_g, w_cast):
    T = p.shape[0]
    ni = T // GLA_RB
    ng = GLA_HEADS // GLA_HG
    kw, vw = GLA_HG * GLA_DK, GLA_HG * GLA_DV
    cast_spec = _side_cast_spec(w_cast, ng * ni, lambda h, i: h * ni + i)
    return pl.pallas_call(
        _gla_kernel,
        grid=(ng, ni),
        in_specs=[
            pl.BlockSpec((GLA_RB, kw), lambda h, i: (i, h)),
            pl.BlockSpec((GLA_RB, kw), lambda h, i: (i, ng + h)),
            pl.BlockSpec((GLA_RB, vw), lambda h, i: (i, ng + h)),
            pl.BlockSpec((GLA_RB, vw), lambda h, i: (i, 2 * ng + h)),
            pl.BlockSpec((GLA_RB, LANES), lambda h, i: (i, 0)),
            pl.BlockSpec((LANES, kw), lambda h, i: (0, h)),
            pl.BlockSpec((1, kw), lambda h, i: (0, h)),
            pl.BlockSpec((1, GLA_DV), lambda h, i: (0, 0)),
            cast_spec,
        ],
        out_specs=[pl.BlockSpec((GLA_RB, vw), lambda h, i: (i, h)), cast_spec],
        out_shape=[jax.ShapeDtypeStruct((T, GLA_HEADS * GLA_DV), BF16),
                   jax.ShapeDtypeStruct(w_cast.shape, BF16)],
        scratch_shapes=[pltpu.VMEM((GLA_HG, GLA_DV, GLA_DK), F32)],
        compiler_params=_params(("arbitrary", "arbitrary")),
        name="gla",
    )(p, p, p, p, a_low, gate_w_pad, gate_b, norm_g, w_cast)


def _merge_kernel(og_ref, cb_ref, cc_ref, ch_ref, ccp_ref, chp_ref, cw_ref, cbias_ref,
                  ga_ref, gb_ref, wa_ref, wb_ref, wm_ref, x_ref, o_ref):
    tm = cc_ref.shape[0]
    u = cc_ref[...].astype(F32) * ch_ref[...].astype(F32)
    up = ccp_ref[...].astype(F32) * chp_ref[...].astype(F32)
    up = jnp.where(pl.program_id(0) == 0, 0.0, up)
    hp = up.shape[0]
    row = lax.broadcasted_iota(jnp.int32, (tm, 1), 0)
    u1 = jnp.where(row == 0, up[hp - 1:hp, :], pltpu.roll(u, 1, 0))
    u2 = jnp.where(row == 0, up[hp - 2:hp - 1, :],
                   jnp.where(row == 1, up[hp - 1:hp, :], pltpu.roll(u, 2, 0)))
    cw = cw_ref[...]
    uc = cw[0:1, :] * u2 + cw[1:2, :] * u1 + cw[2:3, :] * u + cbias_ref[...]
    zb = (cb_ref[...].astype(F32) * uc).astype(BF16)

    ya = _dot(og_ref[...], wa_ref[...])
    yb = _dot(zb, wb_ref[...])
    merged = (_sigmoid(ga_ref[...].astype(F32)) * ya + _sigmoid(gb_ref[...].astype(F32)) * yb).astype(BF16)
    o_ref[...] = x_ref[...] + _dot(merged, wm_ref[...])


def _merge(og, p, conv_w8, conv_b, gla_out_w, conv_out_w, mix_out_w, x):
    T = og.shape[0]
    halo = 16
    prev = lambda blk: (lambda i: (jnp.maximum(i * (MG_TM // halo) - 1, 0), blk))
    pblk = lambda blk: pl.BlockSpec((MG_TM, D_MODEL), lambda i: (i, blk))
    const = lambda shape: pl.BlockSpec(shape, lambda i: (0, 0), pipeline_mode=pl.Buffered(1))
    return pl.pallas_call(
        _merge_kernel,
        grid=(T // MG_TM,),
        in_specs=[
            pblk(0),
            pblk(3), pblk(4), pblk(5),
            pl.BlockSpec((halo, D_MODEL), prev(4)),
            pl.BlockSpec((halo, D_MODEL), prev(5)),
            const((8, D_MODEL)),
            const((1, D_MODEL)),
            pblk(6), pblk(7),
            const((D_MODEL, D_MODEL)), const((D_MODEL, D_MODEL)), const((D_MODEL, D_MODEL)),
            pblk(0),
        ],
        out_specs=pblk(0),
        out_shape=jax.ShapeDtypeStruct((T, D_MODEL), F32),
        compiler_params=_params(("arbitrary",)),
        name="merge",
    )(og, p, p, p, p, p, conv_w8, conv_b, p, p, gla_out_w, conv_out_w, mix_out_w, x)


def _memkv_kernel(m_ref, g_ref, wk_ref, wv_ref, k_ref, v_ref):
    mn = _rms(m_ref[...], g_ref[...]).astype(BF16)
    k_ref[...] = _dot(mn, wk_ref[...]).astype(BF16)
    v_ref[...] = _dot(mn, wv_ref[...]).astype(BF16)


def _mem_kv(mem, g, wk, wv):
    return pl.pallas_call(
        _memkv_kernel,
        out_shape=[jax.ShapeDtypeStruct((MEM_LEN, XATTN_DIM), BF16)] * 2,
        compiler_params=pltpu.CompilerParams(vmem_limit_bytes=VMEM_LIMIT),
        name="mem_kv",
    )(mem, g, wk, wv)


def _xattn_tile(x_ref, g_ref, wq_ref, k_ref, v_ref, wo_ref):
    x = x_ref[...]
    hq = _rms(x, g_ref[...]).astype(BF16)
    q = (_dot(hq, wq_ref[...]) * (XATTN_HEAD_DIM ** -0.5)).astype(BF16)
    outs = []
    for h in range(XATTN_HEADS):
        cs = slice(h * XATTN_HEAD_DIM, (h + 1) * XATTN_HEAD_DIM)
        s = _dot_nt(q[:, cs], k_ref[:, cs])
        s = s - jnp.max(s, axis=-1, keepdims=True)
        e = jnp.exp(s)
        p = e / jnp.sum(e, axis=-1, keepdims=True)
        outs.append(_dot(p.astype(BF16), v_ref[:, cs]))
    o = jnp.concatenate(outs, axis=-1).astype(BF16)
    return x + _dot(o, wo_ref[...])


def _xattn_router_kernel(x_ref, gx_ref, wq_ref, k_ref, v_ref, wo_ref, gm_ref, rw_ref, rb_ref,
                         x2_ref, hm_ref, idx_ref, gate_ref, rank_ref, cnt_ref, carry_scr):
    x2 = _xattn_tile(x_ref, gx_ref, wq_ref, k_ref, v_ref, wo_ref)
    x2_ref[...] = x2
    _route_tile(x2, gm_ref, rw_ref, rb_ref, hm_ref, idx_ref, gate_ref, rank_ref, cnt_ref, carry_scr)


def _route_tile(x2, g_ref, w_ref, b_ref, hm_ref, idx_ref, gate_ref, rank_ref, cnt_ref, carry_scr):
    tm = x2.shape[0]

    @pl.when(pl.program_id(0) == 0)
    def _():
        carry_scr[...] = jnp.zeros_like(carry_scr)

    h = _rms(x2, g_ref[...])
    hb = h.astype(BF16)
    hm_ref[...] = _pack_pairs(h)
    lane = lax.broadcasted_iota(jnp.int32, (tm, LANES), 1)
    logits = jnp.where(lane < N_EXPERTS, _dot(hb, w_ref[...]) + b_ref[...], -jnp.inf)

    vals, idxs = [], []
    work = logits
    lane_f = lane.astype(F32)
    for _ in range(TOP_K):
        m = jnp.max(work, axis=-1, keepdims=True)
        am = jnp.min(jnp.where(work == m, lane_f, float(LANES)), axis=-1, keepdims=True).astype(I32)
        vals.append(m)
        idxs.append(am)
        work = jnp.where(lane == am, -jnp.inf, work)
    es = [jnp.exp(v - vals[0]) for v in vals]
    inv = 1.0 / (es[0] + es[1] + es[2] + es[3])

    onehot = (work != logits).astype(BF16)
    r_ = lax.broadcasted_iota(jnp.int32, (tm, tm), 0)
    c_ = lax.broadcasted_iota(jnp.int32, (tm, tm), 1)
    before = _dot((r_ > c_).astype(BF16), onehot) + carry_scr[...]
    carry_scr[...] = carry_scr[...] + jnp.sum(onehot.astype(F32), axis=0, keepdims=True)
    cnt_ref[...] = carry_scr[...]

    idx_out = jnp.zeros((tm, LANES), jnp.int32)
    gate_out = jnp.zeros((tm, LANES), F32)
    rank_out = jnp.zeros((tm, LANES), jnp.int32)
    for kk in range(TOP_K):
        rk = jnp.sum(jnp.where(lane == idxs[kk], before, 0.0), axis=-1, keepdims=True)
        idx_out = jnp.where(lane == kk, idxs[kk], idx_out)
        gate_out = jnp.where(lane == kk, es[kk] * inv, gate_out)
        rank_out = jnp.where(lane == kk, rk.astype(jnp.int32), rank_out)
    idx_ref[...] = idx_out
    gate_ref[...] = gate_out
    rank_ref[...] = rank_out


def _xattn_router(x1, gx, wq, kmem, vmem, wo, gm, rw_pad, rb_pad):
    T = x1.shape[0]
    full = lambda shape: pl.BlockSpec(shape, lambda i: (0, 0))
    row_blk = pl.BlockSpec((XA_TM, LANES), lambda i: (i, 0))
    return pl.pallas_call(
        _xattn_router_kernel,
        grid=(T // XA_TM,),
        in_specs=[
            pl.BlockSpec((XA_TM, D_MODEL), lambda i: (i, 0)),
            full((1, D_MODEL)),
            full((D_MODEL, XATTN_DIM)),
            full((MEM_LEN, XATTN_DIM)),
            full((MEM_LEN, XATTN_DIM)),
            full((XATTN_DIM, D_MODEL)),
            full((1, D_MODEL)),
            full((D_MODEL, LANES)),
            full((1, LANES)),
        ],
        out_specs=[
            pl.BlockSpec((XA_TM, D_MODEL), lambda i: (i, 0)),
            pl.BlockSpec((XA_TM, PACK_W), lambda i: (i, 0)),
            row_blk, row_blk, row_blk,
            full((1, LANES)),
        ],
        out_shape=[
            jax.ShapeDtypeStruct((T, D_MODEL), F32),
            jax.ShapeDtypeStruct((T, PACK_W), U32),
            jax.ShapeDtypeStruct((T, LANES), jnp.int32),
            jax.ShapeDtypeStruct((T, LANES), F32),
            jax.ShapeDtypeStruct((T, LANES), jnp.int32),
            jax.ShapeDtypeStruct((1, LANES), F32),
        ],
        scratch_shapes=[pltpu.VMEM((1, LANES), F32)],
        compiler_params=_params(("arbitrary",)),
        name="xattn_router",
    )(x1, gx, wq, kmem, vmem, wo, gm, rw_pad, rb_pad)


def _dispatch_kernel(idx_ref, rank_ref, pstart_ref, dest_ref):
    tm = idx_ref.shape[0]
    lane = lax.broadcasted_iota(I32, (tm, LANES), 1)
    idx = idx_ref[...]
    pstart = pstart_ref[...]
    dest = rank_ref[...]
    for kk in range(TOP_K):
        start = jnp.sum(jnp.where(lane == idx[:, kk:kk + 1], pstart, 0.0), axis=-1, keepdims=True)
        dest = dest + jnp.where(lane == kk, start.astype(I32), 0)
    dest_ref[...] = dest


def _dispatch(idx, rank, pstart_row):
    T = idx.shape[0]
    blk = pl.BlockSpec((DP_TM, LANES), lambda i: (i, 0))
    return pl.pallas_call(
        _dispatch_kernel,
        grid=(T // DP_TM,),
        in_specs=[blk, blk, pl.BlockSpec((1, LANES), lambda i: (0, 0))],
        out_specs=blk,
        out_shape=jax.ShapeDtypeStruct((T, LANES), I32),
        compiler_params=_params(("arbitrary",)),
        name="dispatch",
    )(idx, rank, pstart_row)


def _sc_mesh():
    return plsc.VectorSubcoreMesh(core_axis_name="c", subcore_axis_name="s",
                                  num_cores=SC_CORES, num_subcores=SC_SUBCORES)


def _sc_gather_rows(table_hbm, idx_v, out_hbm, out_base, n_steps, rows_v, gsem, wsem):
    nbuf = SC_NBUF

    def gather(s, b):
        return pltpu.make_async_copy(table_hbm.at[idx_v.at[pl.ds(s * SC_GG, SC_GG)]], rows_v.at[b], gsem.at[b])

    def write(s, b):
        return pltpu.make_async_copy(rows_v.at[b], out_hbm.at[pl.ds(out_base + s * SC_GG, SC_GG)], wsem.at[b])

    for j in range(nbuf - 1):
        gather(j, j).start()

    @pl.loop(0, n_steps, step=nbuf)
    def _(s0):
        for b in range(nbuf):
            s = s0 + b
            gather(s, b).wait()
            write(s, b).start()
            refill = (b + nbuf - 1) % nbuf

            @pl.when(s + nbuf - 1 < n_steps)
            def _():
                @pl.when(s >= 1)
                def _():
                    write(s - 1, refill).wait()
                gather(s + nbuf - 1, refill).start()

    for j in range(nbuf):
        write(n_steps - nbuf + j, j).wait()


def _sc_dispatch_scatter(table, dest_chunks, n_slots):
    T, width = table.shape
    n_workers = SC_CORES * SC_SUBCORES
    per_w = T // n_workers
    n_steps = per_w // SC_G
    assert T % n_workers == 0 and per_w % (2 * SC_G) == 0
    assert dest_chunks.shape == (T // SC_G, TOP_K, SC_G)

    @functools.partial(
        pl.kernel, mesh=_sc_mesh(),
        out_type=jax.ShapeDtypeStruct((n_slots, width), table.dtype),
        scratch_types=[
            pltpu.VMEM((n_steps, TOP_K, SC_G), I32),
            pltpu.VMEM((2, SC_G, width), table.dtype),
            pltpu.SemaphoreType.DMA((2,)),
            pltpu.SemaphoreType.DMA((2,)),
        ],
        name="sc_dispatch_scatter",
    )
    def k(table_hbm, dest_hbm, out_hbm, dst_v, rows_v, rsem, wsem):
        wid = lax.axis_index("s") * SC_CORES + lax.axis_index("c")
        pltpu.sync_copy(dest_hbm.at[pl.ds(wid * n_steps, n_steps)], dst_v)

        def read(s, b):
            return pltpu.make_async_copy(table_hbm.at[pl.ds(wid * per_w + s * SC_G, SC_G)], rows_v.at[b], rsem.at[b])

        def scatter(s, kk, b):
            return pltpu.make_async_copy(rows_v.at[b], out_hbm.at[dst_v.at[s, kk]], wsem.at[b])

        read(0, 0).start()

        @pl.loop(0, n_steps, step=2)
        def _(s0):
            for b in range(2):
                s = s0 + b
                read(s, b).wait()

                @pl.when(s + 1 < n_steps)
                def _():
                    @pl.when(s >= 1)
                    def _():
                        for kk in range(TOP_K):
                            scatter(s - 1, kk, 1 - b).wait()
                    read(s + 1, 1 - b).start()

                for kk in range(TOP_K):
                    scatter(s, kk, b).start()

        for kk in range(TOP_K):
            scatter(n_steps - 2, kk, 0).wait()
            scatter(n_steps - 1, kk, 1).wait()

    return k(table, dest_chunks)


def _sc_combine_gather(table, dest_flat):
    n_rows, width = dest_flat.shape[0], table.shape[1]
    n_workers = SC_CORES * SC_SUBCORES
    per_w = n_rows // n_workers
    n_steps = per_w // SC_GG
    assert n_rows % n_workers == 0 and per_w % (SC_NBUF * SC_GG) == 0

    @functools.partial(
        pl.kernel, mesh=_sc_mesh(),
        out_type=jax.ShapeDtypeStruct((n_rows, width), table.dtype),
        scratch_types=[
            pltpu.VMEM((per_w,), I32),
            pltpu.VMEM((SC_NBUF, SC_GG, width), table.dtype),
            pltpu.SemaphoreType.DMA((SC_NBUF,)),
            pltpu.SemaphoreType.DMA((SC_NBUF,)),
        ],
        name="sc_combine_gather",
    )
    def k(table_hbm, dest_hbm, out_hbm, dst_v, rows_v, gsem, wsem):
        wid = lax.axis_index("s") * SC_CORES + lax.axis_index("c")
        lo = wid * per_w
        pltpu.sync_copy(dest_hbm.at[pl.ds(lo, per_w)], dst_v)
        _sc_gather_rows(table_hbm, dst_v, out_hbm, lo, n_steps, rows_v, gsem, wsem)

    return k(table, dest_flat)


def _expert_kernel(be_ref, nb_ref, fv_ref, x_ref, wg_ref, wu_ref, bg_ref, bu_ref, wd_ref, prev_ref, y_ref, *,
                   first, last):
    i = pl.program_id(0)
    fv = fv_ref[i]
    tm = x_ref.shape[0]

    def rows_path(r0):
        row = r0 + lax.broadcasted_iota(I32, (tm - r0, 1), 0)
        x = _unpack_pairs(jnp.where(row >= fv, x_ref[r0:, :], jnp.uint32(0))).astype(BF16)
        gate = jnp.minimum(_dot(x, wg_ref[...]) + bg_ref[...], SWIGLU_LIMIT)
        up = jnp.clip(_dot(x, wu_ref[...]) + bu_ref[...], -SWIGLU_LIMIT, SWIGLU_LIMIT)
        act = (up + 1.0) * gate * _sigmoid(SWIGLU_ALPHA * gate)
        prev = prev_ref[...] if first else prev_ref[r0:, :]
        y = prev + _dot(act.astype(BF16), wd_ref[...])
        y_ref[r0:, :] = _pack_pairs(y) if last else y
        if r0:
            y_ref[:r0, :] = jnp.zeros((r0, y_ref.shape[1]), y_ref.dtype)

    for r0 in range(0, tm, MOE_TS):
        @pl.when((i < nb_ref[0]) & (fv >= r0) & (fv < r0 + MOE_TS))
        def _():
            rows_path(r0)


def _expert_pass(c, block_e, nb_used, first_valid, x_sorted, gu_w, gu_b, down_w, prev):
    P = x_sorted.shape[0]
    nc = D_FF // MOE_TF
    first, last = c == 0, c == nc - 1
    rowblk = lambda i, be, nbu, nv: (jnp.minimum(i, nbu[0] - 1), 0)
    prev_spec = (pl.BlockSpec((None, 1, D_MODEL), lambda i, be, nbu, nv: (be[i], 0, 0)) if first
                 else pl.BlockSpec((MOE_TM, D_MODEL), rowblk))
    grid_spec = pltpu.PrefetchScalarGridSpec(
        num_scalar_prefetch=3,
        grid=(P // MOE_TM,),
        in_specs=[
            pl.BlockSpec((MOE_TM, PACK_W), rowblk),
            pl.BlockSpec((None, D_MODEL, MOE_TF), lambda i, be, nbu, nv: (be[i], 0, c)),
            pl.BlockSpec((None, D_MODEL, MOE_TF), lambda i, be, nbu, nv: (be[i], 0, nc + c)),
            pl.BlockSpec((None, 1, MOE_TF), lambda i, be, nbu, nv: (be[i], 0, c)),
            pl.BlockSpec((None, 1, MOE_TF), lambda i, be, nbu, nv: (be[i], 0, nc + c)),
            pl.BlockSpec((None, MOE_TF, D_MODEL), lambda i, be, nbu, nv: (be[i], c, 0)),
            prev_spec,
        ],
        out_specs=pl.BlockSpec((MOE_TM, PACK_W if last else D_MODEL), rowblk),
    )
    return pl.pallas_call(
        functools.partial(_expert_kernel, first=first, last=last),
        grid_spec=grid_spec,
        out_shape=jax.ShapeDtypeStruct((P, PACK_W), U32) if last else jax.ShapeDtypeStruct((P, D_MODEL), F32),
        compiler_params=_params(("arbitrary",)),
        name="experts_%d" % c,
    )(block_e, nb_used, first_valid, x_sorted, gu_w, gu_w, gu_b, gu_b, down_w, prev)


def _combine_kernel(x_ref, y_ref, gate_ref, g_ref, o_ref):
    acc = x_ref[...]
    gates = gate_ref[...]
    for kk in range(TOP_K):
        acc = acc + gates[:, kk:kk + 1] * _unpack_pairs(y_ref[kk])
    o_ref[...] = _rms(acc, g_ref[...])


def _combine(x2, y_g, gates, g):
    T = x2.shape[0]
    return pl.pallas_call(
        _combine_kernel,
        grid=(T // CB_TM,),
        in_specs=[
            pl.BlockSpec((CB_TM, D_MODEL), lambda i: (i, 0)),
            pl.BlockSpec((TOP_K, CB_TM, PACK_W), lambda i: (0, i, 0)),
            pl.BlockSpec((CB_TM, LANES), lambda i: (i, 0)),
            pl.BlockSpec((1, D_MODEL), lambda i: (0, 0)),
        ],
        out_specs=pl.BlockSpec((CB_TM, D_MODEL), lambda i: (i, 0)),
        out_shape=jax.ShapeDtypeStruct((T, D_MODEL), F32),
        compiler_params=_params(("arbitrary",)),
        name="combine",
    )(x2, y_g, gates, g)


def _row(v):
    return v.reshape(1, -1).astype(F32)


def _layer(x, mem, norm_mix_g, w_in, gla_gate_w, gla_gate_b, gla_norm_g, gla_out_w, conv_w, conv_b,
           conv_out_w, mix_out_w, norm_xattn_g, norm_mem_g, xq_w, xk_w, xv_w, xo_w, norm_moe_g,
           router_w, router_b, gu_w, gu_b, down_w, down_b):
    T = x.shape[0]
    w_in_t = w_in.T
    w_main = _w_in_prep(w_in_t)
    a0 = 3 * D_MODEL
    w_alow_t = jnp.pad(w_in_t[a0:a0 + GLA_GATE_RANK], ((0, LANES - GLA_GATE_RANK), (0, 0)))
    gate_w_pad = jnp.pad(gla_gate_w, ((0, LANES - GLA_GATE_RANK), (0, 0)))
    conv_w8 = jnp.pad(conv_w, ((0, 8 - CONV_WIDTH), (0, 0)))

    p, a_low, gu_wb = _in_proj(x, _row(norm_mix_g), w_main, w_alow_t, gu_w.reshape(-1, 2 * D_FF))
    og, down_wb = _gla(p, a_low, gate_w_pad, _row(gla_gate_b), _row(gla_norm_g), down_w.reshape(-1, D_MODEL))
    gu_wb = gu_wb.reshape(gu_w.shape)
    down_wb = down_wb.reshape(down_w.shape)
    x1 = _merge(og, p, conv_w8, _row(conv_b), gla_out_w.astype(BF16), conv_out_w.astype(BF16),
                mix_out_w.astype(BF16), x)

    kmem, vmem = _mem_kv(mem, _row(norm_mem_g), xk_w.astype(BF16), xv_w.astype(BF16))
    rw = jnp.pad(router_w, ((0, 0), (0, LANES - N_EXPERTS))).astype(BF16)
    rb = jnp.pad(router_b, (0, LANES - N_EXPERTS)).reshape(1, LANES)
    x2, hm, idx, gates, rank, counts = _xattn_router(
        x1, _row(norm_xattn_g), xq_w.astype(BF16), kmem, vmem, xo_w.astype(BF16), _row(norm_moe_g), rw, rb)

    cnt = counts[0, :N_EXPERTS].astype(I32)
    padded = (cnt + MOE_TM - 1) // MOE_TM * MOE_TM
    pend = jnp.cumsum(padded)
    pstart = pend - padded
    first_slot = pend - cnt
    first_slot_row = jnp.pad(first_slot, (0, LANES - N_EXPERTS)).reshape(1, LANES).astype(F32)
    nb = (T * TOP_K) // MOE_TM + N_EXPERTS
    blk_start = (jnp.arange(nb, dtype=I32) * MOE_TM)[:, None]
    block_e = jnp.minimum(jnp.sum(pend[None, :] <= blk_start, axis=1), N_EXPERTS - 1).astype(I32)
    nb_used = (pend[-1] // MOE_TM).astype(I32).reshape(1)
    in_group = (pstart[None, :] <= blk_start) & (blk_start < pend[None, :])
    first_valid = jnp.sum(jnp.where(in_group, jnp.clip(first_slot[None, :] - blk_start, 0, MOE_TM), 0),
                          axis=1).astype(I32)

    dest = _dispatch(idx, rank, first_slot_row)[:, :TOP_K]
    dest_chunks = dest.reshape(T // SC_G, SC_G, TOP_K).transpose(0, 2, 1)
    x_sorted = _sc_dispatch_scatter(hm, dest_chunks, nb * MOE_TM)

    gu_b3 = gu_b.reshape(N_EXPERTS, 1, 2 * D_FF)
    y = down_b.reshape(N_EXPERTS, 1, D_MODEL)
    for c in range(D_FF // MOE_TF):
        y = _expert_pass(c, block_e, nb_used, first_valid, x_sorted, gu_wb, gu_b3, down_wb, y)
    y_g = _sc_combine_gather(y, dest.T.reshape(-1)).reshape(TOP_K, T, PACK_W)
    return x2, y_g, gates


def kernel(x, mem, norm_mix_g, w_in, gla_gate_w, gla_gate_b, gla_norm_g, gla_out_w, conv_w, conv_b, conv_out_w, mix_out_w, norm_xattn_g, norm_mem_g, xq_w, xk_w, xv_w, xo_w, norm_moe_g, router_w, router_b, expert_gu_w, expert_gu_b, expert_down_w, expert_down_b, norm_final_g):
    assert x.shape[0] == 1 and mem.shape[0] == 1 and w_in.shape[0] == 1
    x2, y_g, gates = _layer(
        x[0], mem[0], norm_mix_g[0], w_in[0], gla_gate_w[0], gla_gate_b[0], gla_norm_g[0], gla_out_w[0],
        conv_w[0], conv_b[0], conv_out_w[0], mix_out_w[0], norm_xattn_g[0], norm_mem_g[0], xq_w[0], xk_w[0],
        xv_w[0], xo_w[0], norm_moe_g[0], router_w[0], router_b[0], expert_gu_w[0], expert_gu_b[0],
        expert_down_w[0], expert_down_b[0])
    out = _combine(x2, y_g, gates, _row(norm_final_g))
    return out[None]
```

```python
import functools
import math

import jax
import jax.numpy as jnp
from jax import lax
from jax.experimental import pallas as pl
from jax.experimental.pallas import tpu as pltpu
from jax.experimental.pallas import tpu_sc as plsc

F32 = jnp.float32
BF16 = jnp.bfloat16
I32 = jnp.int32
U32 = jnp.uint32

D_MODEL = 2048
MEM_LEN = 256
GLA_HEADS = 4
GLA_DK = 256
GLA_DV = 512
GLA_GATE_RANK = 16
GLA_GATE_TEMP = 16.0
CONV_WIDTH = 3
XATTN_HEADS = 4
XATTN_HEAD_DIM = 128
XATTN_DIM = XATTN_HEADS * XATTN_HEAD_DIM
N_EXPERTS = 32
TOP_K = 4
D_FF = D_MODEL
SWIGLU_LIMIT = 7.0
SWIGLU_ALPHA = 1.702
NORM_EPS = 1e-5
LOG2E = 1.4426950408889634
assert math.log2(GLA_GATE_TEMP).is_integer()

LANES = 128
SC_CORES, SC_SUBCORES, SC_LANES = 2, 16, 16
PACK_W = D_MODEL // 2
P_COLS = 8 * D_MODEL
VMEM_LIMIT = 56 * 1024 * 1024

IN_TM, IN_TN = 1024, 1024
GLA_CH = 256
GLA_RB = 512
GLA_HG = 2
MG_TM = 256
XA_TM = 512
DP_TM = 2048
MOE_TM = 512
MOE_TS = 128
MOE_TF = 1024
CB_TM = 512
SC_G = 32
SC_GG, SC_NBUF = 16, 4


def _params(sem):
    return pltpu.CompilerParams(dimension_semantics=sem, vmem_limit_bytes=VMEM_LIMIT)


def _rms(x, g):
    return x * lax.rsqrt(jnp.mean(x * x, axis=-1, keepdims=True) + NORM_EPS) * g


def _dot(a, b):
    return jnp.dot(a, b, preferred_element_type=F32)


def _dot_nt(a, b):
    return lax.dot_general(a, b, (((1,), (1,)), ((), ())), preferred_element_type=F32)


def _dot_tn(a, b):
    return lax.dot_general(a, b, (((0,), (0,)), ((), ())), preferred_element_type=F32)


def _split2(x):
    hi = x.astype(BF16)
    lo = (x - hi.astype(F32)).astype(BF16)
    return hi, lo


def _split3(x):
    hi = x.astype(BF16)
    r = x - hi.astype(F32)
    mid = r.astype(BF16)
    lo = (r - mid.astype(F32)).astype(BF16)
    return hi, mid, lo


def _sigmoid(x):
    return 1.0 / (1.0 + jnp.exp(-x))


def _pack_pairs(x):
    w = x.shape[1] // 2
    u = lax.bitcast_convert_type(x.astype(BF16).astype(F32), U32)
    return (u[:, :w] >> 16) | u[:, w:]


def _unpack_pairs(u):
    lo = lax.bitcast_convert_type(u << 16, F32)
    hi = lax.bitcast_convert_type(u & jnp.uint32(0xFFFF0000), F32)
    return jnp.concatenate([lo, hi], axis=1)


def _side_cast_spec(w2d, n_steps, step_of):
    rows, cols = w2d.shape
    assert rows % n_steps == 0
    return pl.BlockSpec((rows // n_steps, cols), lambda *ids: (step_of(*ids), 0))


def _winprep_kernel(wt_ref, main_ref):
    main_ref[...] = wt_ref[...].T.astype(BF16)


def _w_in_prep(w_in_t):
    rows, d = w_in_t.shape
    a0 = 3 * D_MODEL
    assert rows == P_COLS + GLA_GATE_RANK and a0 % IN_TN == 0
    n_left = a0 // IN_TN
    assert IN_TN % GLA_GATE_RANK == 0
    row0 = lambda j: ((j * (IN_TN // GLA_GATE_RANK) + (j >= n_left).astype(I32)) * GLA_GATE_RANK, 0)
    return pl.pallas_call(
        _winprep_kernel,
        grid=(P_COLS // IN_TN,),
        in_specs=[pl.BlockSpec((pl.Element(IN_TN), pl.Element(d)), row0)],
        out_specs=pl.BlockSpec((d, IN_TN), lambda j: (0, j)),
        out_shape=jax.ShapeDtypeStruct((d, P_COLS), BF16),
        compiler_params=_params(("arbitrary",)),
        name="w_in_prep",
    )(w_in_t)


def _inproj_kernel(x_ref, g_ref, w_ref, wa_ref, cin_ref, p_ref, a_ref, cout_ref, h_scr):
    @pl.when(pl.program_id(1) == 0)
    def _():
        hb = _rms(x_ref[...], g_ref[...]).astype(BF16)
        h_scr[...] = hb
        a_ref[...] = _dot_nt(hb, wa_ref[...].astype(BF16))

    p_ref[...] = _dot(h_scr[...], w_ref[...]).astype(BF16)
    cout_ref[...] = cin_ref[...].astype(BF16)


def _in_proj(x, g, w_main, w_alow_t, w_cast):
    T = x.shape[0]
    ni, nj = T // IN_TM, P_COLS // IN_TN
    cast_spec = _side_cast_spec(w_cast, ni * nj, lambda i, j: i * nj + j)
    return pl.pallas_call(
        _inproj_kernel,
        grid=(ni, nj),
        in_specs=[
            pl.BlockSpec((IN_TM, D_MODEL), lambda i, j: (i, 0)),
            pl.BlockSpec((1, D_MODEL), lambda i, j: (0, 0)),
            pl.BlockSpec((D_MODEL, IN_TN), lambda i, j: (0, j)),
            pl.BlockSpec((LANES, D_MODEL), lambda i, j: (0, 0)),
            cast_spec,
        ],
        out_specs=[
            pl.BlockSpec((IN_TM, IN_TN), lambda i, j: (i, j)),
            pl.BlockSpec((IN_TM, LANES), lambda i, j: (i, 0)),
            cast_spec,
        ],
        out_shape=[
            jax.ShapeDtypeStruct((T, P_COLS), BF16),
            jax.ShapeDtypeStruct((T, LANES), F32),
            jax.ShapeDtypeStruct(w_cast.shape, BF16),
        ],
        scratch_shapes=[pltpu.VMEM((IN_TM, D_MODEL), BF16)],
        compiler_params=_params(("arbitrary", "arbitrary")),
        name="in_proj",
    )(x, g, w_main, w_alow_t, w_cast)


def _gla_level_weights(b, la, row, signs):
    ch, dk = b.shape
    out = []
    s = ch // 2
    while s >= 4:
        bb = b.reshape(ch // (2 * s), 2 * s, dk)
        d = (bb - bb[:, s - 1:s, :]).reshape(ch, dk)
        out.append((s, jnp.exp2(d * signs[s])))
        s //= 2
    la_prev = pltpu.roll(la, 1, 0)
    la_next = pltpu.roll(la, ch - 1, 0)
    r4 = row & 3
    d2 = jnp.where(r4 == 0, la_next, jnp.where(r4 == 1, 0.0, jnp.where(r4 == 2, la, la + la_prev)))
    out.append((2, jnp.exp(d2)))
    d1 = jnp.where((row & 1) == 1, la, 0.0)
    out.append((1, jnp.exp(d1)))
    return out


def _gla_kernel(q_ref, k_ref, v_ref, r_ref, a_ref, gw_ref, gb_ref, ng_ref, cin_ref, o_ref, cout_ref, st_scr):
    ch = GLA_CH
    cout_ref[...] = cin_ref[...].astype(BF16)

    @pl.when(pl.program_id(1) == 0)
    def _():
        st_scr[...] = jnp.zeros_like(st_scr)

    row = lax.broadcasted_iota(jnp.int32, (ch, 1), 0)
    col = lax.broadcasted_iota(jnp.int32, (1, ch), 1)
    tril = (row >= col).astype(BF16)
    levels, signs = [], {}
    s = ch // 2
    while s >= 1:
        levels.append((s, ((row ^ col) < 2 * s) & ((row & (2 * s - 1)) >= s) & ((col & (2 * s - 1)) < s)))
        signs[s] = jnp.broadcast_to(jnp.where((row & (2 * s - 1)) >= s, LOG2E, -LOG2E), (ch, GLA_DK))
        s //= 2
    gw_hi, gw_lo = _split2(gw_ref[...] * (1.0 / GLA_GATE_TEMP))
    gate_b = gb_ref[...] * (1.0 / GLA_GATE_TEMP)
    norm_g = ng_ref[...]
    q_scale = jnp.asarray(GLA_DK ** -0.5, BF16)

    def head(sl, hh, a_hi, a_lo):
        ks = slice(hh * GLA_DK, (hh + 1) * GLA_DK)
        vs = slice(hh * GLA_DV, (hh + 1) * GLA_DV)
        qb = q_ref[sl, ks] * q_scale
        kb = k_ref[sl, ks]
        q, k = qb.astype(F32), kb.astype(F32)
        v = v_ref[sl, vs]
        z = _dot(a_hi, gw_hi[:, ks]) + _dot(a_lo, gw_hi[:, ks]) + _dot(a_hi, gw_lo[:, ks]) + gate_b[:, ks]
        soft = jnp.log2(1.0 + jnp.exp2(jnp.abs(z) * (-GLA_GATE_TEMP * LOG2E)))
        la = jnp.minimum(z, 0.0) - soft * (1.0 / (GLA_GATE_TEMP * LOG2E))
        l_hi, l_mid, l_lo = _split3(la)
        b = _dot(tril, l_hi) + _dot(tril, l_mid) + _dot(tril, l_lo)

        scores = jnp.where(row == col, _dot_nt(qb, kb), 0.0)
        for (s, mask), (_, w) in zip(levels, _gla_level_weights(b, la, row, signs)):
            scores = jnp.where(mask, _dot_nt((q * w).astype(BF16), (k * w).astype(BF16)), scores)
        o = _dot(scores.astype(BF16), v)

        st = st_scr[hh]
        o = o + _dot_nt((q * jnp.exp(b)).astype(BF16), st.astype(BF16))
        b_last = b[ch - 1:ch, :]
        kd = (k * jnp.exp(b_last - b)).astype(BF16)
        st_scr[hh] = st * jnp.exp(b_last) + _dot_tn(v, kd)

        o = _rms(o, norm_g)
        r = r_ref[sl, vs].astype(F32)
        o_ref[sl, vs] = (o * (r * _sigmoid(r))).astype(BF16)

    def chunk(c, carry):
        sl = pl.ds(pl.multiple_of(c * ch, ch), ch)
        a_hi, a_lo = _split2(a_ref[sl, :])
        for hh in range(GLA_HG):
            head(sl, hh, a_hi, a_lo)
        return carry

    lax.fori_loop(0, GLA_RB // ch, chunk, 0)


def _gla(p, a_low, gate_w_pad, gate_b, norm_g, w_cast):
    T = p.shape[0]
    ni = T // GLA_RB
    ng = GLA_HEADS // GLA_HG
    kw, vw = GLA_HG * GLA_DK, GLA_HG * GLA_DV
    cast_spec = _side_cast_spec(w_cast, ng * ni, lambda h, i: h * ni + i)
    return pl.pallas_call(
        _gla_kernel,
        grid=(ng, ni),
        in_specs=[
            pl.BlockSpec((GLA_RB, kw), lambda h, i: (i, h)),
            pl.BlockSpec((GLA_RB, kw), lambda h, i: (i, ng + h)),
            pl.BlockSpec((GLA_RB, vw), lambda h, i: (i, ng + h)),
            pl.BlockSpec((GLA_RB, vw), lambda h, i: (i, 2 * ng + h)),
            pl.BlockSpec((GLA_RB, LANES), lambda h, i: (i, 0)),
            pl.BlockSpec((LANES, kw), lambda h, i: (0, h)),
            pl.BlockSpec((1, kw), lambda h, i: (0, h)),
            pl.BlockSpec((1, GLA_DV), lambda h, i: (0, 0)),
            cast_spec,
        ],
        out_specs=[pl.BlockSpec((GLA_RB, vw), lambda h, i: (i, h)), cast_spec],
        out_shape=[jax.ShapeDtypeStruct((T, GLA_HEADS * GLA_DV), BF16),
                   jax.ShapeDtypeStruct(w_cast.shape, BF16)],
        scratch_shapes=[pltpu.VMEM((GLA_HG, GLA_DV, GLA_DK), F32)],
        compiler_params=_params(("arbitrary", "arbitrary")),
        name="gla",
    )(p, p, p, p, a_low, gate_w_pad, gate_b, norm_g, w_cast)


def _merge_kernel(og_ref, cb_ref, cc_ref, ch_ref, ccp_ref, chp_ref, cw_ref, cbias_ref,
                  ga_ref, gb_ref, wa_ref, wb_ref, wm_ref, x_ref, o_ref):
    tm = cc_ref.shape[0]
    u = cc_ref[...].astype(F32) * ch_ref[...].astype(F32)
    up = ccp_ref[...].astype(F32) * chp_ref[...].astype(F32)
    up = jnp.where(pl.program_id(0) == 0, 0.0, up)
    hp = up.shape[0]
    row = lax.broadcasted_iota(jnp.int32, (tm, 1), 0)
    u1 = jnp.where(row == 0, up[hp - 1:hp, :], pltpu.roll(u, 1, 0))
    u2 = jnp.where(row == 0, up[hp - 2:hp - 1, :],
                   jnp.where(row == 1, up[hp - 1:hp, :], pltpu.roll(u, 2, 0)))
    cw = cw_ref[...]
    uc = cw[0:1, :] * u2 + cw[1:2, :] * u1 + cw[2:3, :] * u + cbias_ref[...]
    zb = (cb_ref[...].astype(F32) * uc).astype(BF16)

    ya = _dot(og_ref[...], wa_ref[...])
    yb = _dot(zb, wb_ref[...])
    merged = (_sigmoid(ga_ref[...].astype(F32)) * ya + _sigmoid(gb_ref[...].astype(F32)) * yb).astype(BF16)
    o_ref[...] = x_ref[...] + _dot(merged, wm_ref[...])


def _merge(og, p, conv_w8, conv_b, gla_out_w, conv_out_w, mix_out_w, x):
    T = og.shape[0]
    halo = 16
    prev = lambda blk: (lambda i: (jnp.maximum(i * (MG_TM // halo) - 1, 0), blk))
    pblk = lambda blk: pl.BlockSpec((MG_TM, D_MODEL), lambda i: (i, blk))
    const = lambda shape: pl.BlockSpec(shape, lambda i: (0, 0), pipeline_mode=pl.Buffered(1))
    return pl.pallas_call(
        _merge_kernel,
        grid=(T // MG_TM,),
        in_specs=[
            pblk(0),
            pblk(3), pblk(4), pblk(5),
            pl.BlockSpec((halo, D_MODEL), prev(4)),
            pl.BlockSpec((halo, D_MODEL), prev(5)),
            const((8, D_MODEL)),
            const((1, D_MODEL)),
            pblk(6), pblk(7),
            const((D_MODEL, D_MODEL)), const((D_MODEL, D_MODEL)), const((D_MODEL, D_MODEL)),
            pblk(0),
        ],
        out_specs=pblk(0),
        out_shape=jax.ShapeDtypeStruct((T, D_MODEL), F32),
        compiler_params=_params(("arbitrary",)),
        name="merge",
    )(og, p, p, p, p, p, conv_w8, conv_b, p, p, gla_out_w, conv_out_w, mix_out_w, x)


def _memkv_kernel(m_ref, g_ref, wk_ref, wv_ref, k_ref, v_ref):
    mn = _rms(m_ref[...], g_ref[...]).astype(BF16)
    k_ref[...] = _dot(mn, wk_ref[...]).astype(BF16)
    v_ref[...] = _dot(mn, wv_ref[...]).astype(BF16)


def _mem_kv(mem, g, wk, wv):
    return pl.pallas_call(
        _memkv_kernel,
        out_shape=[jax.ShapeDtypeStruct((MEM_LEN, XATTN_DIM), BF16)] * 2,
        compiler_params=pltpu.CompilerParams(vmem_limit_bytes=VMEM_LIMIT),
        name="mem_kv",
    )(mem, g, wk, wv)


def _xattn_tile(x_ref, g_ref, wq_ref, k_ref, v_ref, wo_ref):
    x = x_ref[...]
    hq = _rms(x, g_ref[...]).astype(BF16)
    q = (_dot(hq, wq_ref[...]) * (XATTN_HEAD_DIM ** -0.5)).astype(BF16)
    outs = []
    for h in range(XATTN_HEADS):
        cs = slice(h * XATTN_HEAD_DIM, (h + 1) * XATTN_HEAD_DIM)
        s = _dot_nt(q[:, cs], k_ref[:, cs])
        s = s - jnp.max(s, axis=-1, keepdims=True)
        e = jnp.exp(s)
        p = e / jnp.sum(e, axis=-1, keepdims=True)
        outs.append(_dot(p.astype(BF16), v_ref[:, cs]))
    o = jnp.concatenate(outs, axis=-1).astype(BF16)
    return x + _dot(o, wo_ref[...])


def _xattn_router_kernel(x_ref, gx_ref, wq_ref, k_ref, v_ref, wo_ref, gm_ref, rw_ref, rb_ref,
                         x2_ref, hm_ref, idx_ref, gate_ref, rank_ref, cnt_ref, carry_scr):
    x2 = _xattn_tile(x_ref, gx_ref, wq_ref, k_ref, v_ref, wo_ref)
    x2_ref[...] = x2
    _route_tile(x2, gm_ref, rw_ref, rb_ref, hm_ref, idx_ref, gate_ref, rank_ref, cnt_ref, carry_scr)


def _route_tile(x2, g_ref, w_ref, b_ref, hm_ref, idx_ref, gate_ref, rank_ref, cnt_ref, carry_scr):
    tm = x2.shape[0]

    @pl.when(pl.program_id(0) == 0)
    def _():
        carry_scr[...] = jnp.zeros_like(carry_scr)

    h = _rms(x2, g_ref[...])
    hb = h.astype(BF16)
    hm_ref[...] = _pack_pairs(h)
    lane = lax.broadcasted_iota(jnp.int32, (tm, LANES), 1)
    logits = jnp.where(lane < N_EXPERTS, _dot(hb, w_ref[...]) + b_ref[...], -jnp.inf)

    vals, idxs = [], []
    work = logits
    lane_f = lane.astype(F32)
    for _ in range(TOP_K):
        m = jnp.max(work, axis=-1, keepdims=True)
        am = jnp.min(jnp.where(work == m, lane_f, float(LANES)), axis=-1, keepdims=True).astype(I32)
        vals.append(m)
        idxs.append(am)
        work = jnp.where(lane == am, -jnp.inf, work)
    es = [jnp.exp(v - vals[0]) for v in vals]
    inv = 1.0 / (es[0] + es[1] + es[2] + es[3])

    onehot = (work != logits).astype(BF16)
    r_ = lax.broadcasted_iota(jnp.int32, (tm, tm), 0)
    c_ = lax.broadcasted_iota(jnp.int32, (tm, tm), 1)
    before = _dot((r_ > c_).astype(BF16), onehot) + carry_scr[...]
    carry_scr[...] = carry_scr[...] + jnp.sum(onehot.astype(F32), axis=0, keepdims=True)
    cnt_ref[...] = carry_scr[...]

    idx_out = jnp.zeros((tm, LANES), jnp.int32)
    gate_out = jnp.zeros((tm, LANES), F32)
    rank_out = jnp.zeros((tm, LANES), jnp.int32)
    for kk in range(TOP_K):
        rk = jnp.sum(jnp.where(lane == idxs[kk], before, 0.0), axis=-1, keepdims=True)
        idx_out = jnp.where(lane == kk, idxs[kk], idx_out)
        gate_out = jnp.where(lane == kk, es[kk] * inv, gate_out)
        rank_out = jnp.where(lane == kk, rk.astype(jnp.int32), rank_out)
    idx_ref[...] = idx_out
    gate_ref[...] = gate_out
    rank_ref[...] = rank_out


def _xattn_router(x1, gx, wq, kmem, vmem, wo, gm, rw_pad, rb_pad):
    T = x1.shape[0]
    full = lambda shape: pl.BlockSpec(shape, lambda i: (0, 0))
    row_blk = pl.BlockSpec((XA_TM, LANES), lambda i: (i, 0))
    return pl.pallas_call(
        _xattn_router_kernel,
        grid=(T // XA_TM,),
        in_specs=[
            pl.BlockSpec((XA_TM, D_MODEL), lambda i: (i, 0)),
            full((1, D_MODEL)),
            full((D_MODEL, XATTN_DIM)),
            full((MEM_LEN, XATTN_DIM)),
            full((MEM_LEN, XATTN_DIM)),
            full((XATTN_DIM, D_MODEL)),
            full((1, D_MODEL)),
            full((D_MODEL, LANES)),
            full((1, LANES)),
        ],
        out_specs=[
            pl.BlockSpec((XA_TM, D_MODEL), lambda i: (i, 0)),
            pl.BlockSpec((XA_TM, PACK_W), lambda i: (i, 0)),
            row_blk, row_blk, row_blk,
            full((1, LANES)),
        ],
        out_shape=[
            jax.ShapeDtypeStruct((T, D_MODEL), F32),
            jax.ShapeDtypeStruct((T, PACK_W), U32),
            jax.ShapeDtypeStruct((T, LANES), jnp.int32),
            jax.ShapeDtypeStruct((T, LANES), F32),
            jax.ShapeDtypeStruct((T, LANES), jnp.int32),
            jax.ShapeDtypeStruct((1, LANES), F32),
        ],
        scratch_shapes=[pltpu.VMEM((1, LANES), F32)],
        compiler_params=_params(("arbitrary",)),
        name="xattn_router",
    )(x1, gx, wq, kmem, vmem, wo, gm, rw_pad, rb_pad)


def _dispatch_kernel(idx_ref, rank_ref, pstart_ref, dest_ref):
    tm = idx_ref.shape[0]
    lane = lax.broadcasted_iota(I32, (tm, LANES), 1)
    idx = idx_ref[...]
    pstart = pstart_ref[...]
    dest = rank_ref[...]
    for kk in range(TOP_K):
        start = jnp.sum(jnp.where(lane == idx[:, kk:kk + 1], pstart, 0.0), axis=-1, keepdims=True)
        dest = dest + jnp.where(lane == kk, start.astype(I32), 0)
    dest_ref[...] = dest


def _dispatch(idx, rank, pstart_row):
    T = idx.shape[0]
    blk = pl.BlockSpec((DP_TM, LANES), lambda i: (i, 0))
    return pl.pallas_call(
        _dispatch_kernel,
        grid=(T // DP_TM,),
        in_specs=[blk, blk, pl.BlockSpec((1, LANES), lambda i: (0, 0))],
        out_specs=blk,
        out_shape=jax.ShapeDtypeStruct((T, LANES), I32),
        compiler_params=_params(("arbitrary",)),
        name="dispatch",
    )(idx, rank, pstart_row)


def _sc_mesh():
    return plsc.VectorSubcoreMesh(core_axis_name="c", subcore_axis_name="s",
                                  num_cores=SC_CORES, num_subcores=SC_SUBCORES)


def _sc_gather_rows(table_hbm, idx_v, out_hbm, out_base, n_steps, rows_v, gsem, wsem):
    nbuf = SC_NBUF

    def gather(s, b):
        return pltpu.make_async_copy(table_hbm.at[idx_v.at[pl.ds(s * SC_GG, SC_GG)]], rows_v.at[b], gsem.at[b])

    def write(s, b):
        return pltpu.make_async_copy(rows_v.at[b], out_hbm.at[pl.ds(out_base + s * SC_GG, SC_GG)], wsem.at[b])

    for j in range(nbuf - 1):
        gather(j, j).start()

    @pl.loop(0, n_steps, step=nbuf)
    def _(s0):
        for b in range(nbuf):
            s = s0 + b
            gather(s, b).wait()
            write(s, b).start()
            refill = (b + nbuf - 1) % nbuf

            @pl.when(s + nbuf - 1 < n_steps)
            def _():
                @pl.when(s >= 1)
                def _():
                    write(s - 1, refill).wait()
                gather(s + nbuf - 1, refill).start()

    for j in range(nbuf):
        write(n_steps - nbuf + j, j).wait()


def _sc_dispatch_scatter(table, dest_chunks, n_slots):
    T, width = table.shape
    n_workers = SC_CORES * SC_SUBCORES
    per_w = T // n_workers
    n_steps = per_w // SC_G
    assert T % n_workers == 0 and per_w % (2 * SC_G) == 0
    assert dest_chunks.shape == (T // SC_G, TOP_K, SC_G)

    @functools.partial(
        pl.kernel, mesh=_sc_mesh(),
        out_type=jax.ShapeDtypeStruct((n_slots, width), table.dtype),
        scratch_types=[
            pltpu.VMEM((n_steps, TOP_K, SC_G), I32),
            pltpu.VMEM((2, SC_G, width), table.dtype),
            pltpu.SemaphoreType.DMA((2,)),
            pltpu.SemaphoreType.DMA((2,)),
        ],
        name="sc_dispatch_scatter",
    )
    def k(table_hbm, dest_hbm, out_hbm, dst_v, rows_v, rsem, wsem):
        wid = lax.axis_index("s") * SC_CORES + lax.axis_index("c")
        pltpu.sync_copy(dest_hbm.at[pl.ds(wid * n_steps, n_steps)], dst_v)

        def read(s, b):
            return pltpu.make_async_copy(table_hbm.at[pl.ds(wid * per_w + s * SC_G, SC_G)], rows_v.at[b], rsem.at[b])

        def scatter(s, kk, b):
            return pltpu.make_async_copy(rows_v.at[b], out_hbm.at[dst_v.at[s, kk]], wsem.at[b])

        read(0, 0).start()

        @pl.loop(0, n_steps, step=2)
        def _(s0):
            for b in range(2):
                s = s0 + b
                read(s, b).wait()

                @pl.when(s + 1 < n_steps)
                def _():
                    @pl.when(s >= 1)
                    def _():
                        for kk in range(TOP_K):
                            scatter(s - 1, kk, 1 - b).wait()
                    read(s + 1, 1 - b).start()

                for kk in range(TOP_K):
                    scatter(s, kk, b).start()

        for kk in range(TOP_K):
            scatter(n_steps - 2, kk, 0).wait()
            scatter(n_steps - 1, kk, 1).wait()

    return k(table, dest_chunks)


def _sc_combine_gather(table, dest_flat):
    n_rows, width = dest_flat.shape[0], table.shape[1]
    n_workers = SC_CORES * SC_SUBCORES
    per_w = n_rows // n_workers
    n_steps = per_w // SC_GG
    assert n_rows % n_workers == 0 and per_w % (SC_NBUF * SC_GG) == 0

    @functools.partial(
        pl.kernel, mesh=_sc_mesh(),
        out_type=jax.ShapeDtypeStruct((n_rows, width), table.dtype),
        scratch_types=[
            pltpu.VMEM((per_w,), I32),
            pltpu.VMEM((SC_NBUF, SC_GG, width), table.dtype),
            pltpu.SemaphoreType.DMA((SC_NBUF,)),
            pltpu.SemaphoreType.DMA((SC_NBUF,)),
        ],
        name="sc_combine_gather",
    )
    def k(table_hbm, dest_hbm, out_hbm, dst_v, rows_v, gsem, wsem):
        wid = lax.axis_index("s") * SC_CORES + lax.axis_index("c")
        lo = wid * per_w
        pltpu.sync_copy(dest_hbm.at[pl.ds(lo, per_w)], dst_v)
        _sc_gather_rows(table_hbm, dst_v, out_hbm, lo, n_steps, rows_v, gsem, wsem)

    return k(table, dest_flat)


def _expert_kernel(be_ref, nb_ref, fv_ref, x_ref, wg_ref, wu_ref, bg_ref, bu_ref, wd_ref, prev_ref, y_ref, *,
                   first, last):
    i = pl.program_id(0)
    fv = fv_ref[i]
    tm = x_ref.shape[0]

    def rows_path(r0):
        row = r0 + lax.broadcasted_iota(I32, (tm - r0, 1), 0)
        x = _unpack_pairs(jnp.where(row >= fv, x_ref[r0:, :], jnp.uint32(0))).astype(BF16)
        gate = jnp.minimum(_dot(x, wg_ref[...]) + bg_ref[...], SWIGLU_LIMIT)
        up = jnp.clip(_dot(x, wu_ref[...]) + bu_ref[...], -SWIGLU_LIMIT, SWIGLU_LIMIT)
        act = (up + 1.0) * gate * _sigmoid(SWIGLU_ALPHA * gate)
        prev = prev_ref[...] if first else prev_ref[r0:, :]
        y = prev + _dot(act.astype(BF16), wd_ref[...])
        y_ref[r0:, :] = _pack_pairs(y) if last else y
        if r0:
            y_ref[:r0, :] = jnp.zeros((r0, y_ref.shape[1]), y_ref.dtype)

    for r0 in range(0, tm, MOE_TS):
        @pl.when((i < nb_ref[0]) & (fv >= r0) & (fv < r0 + MOE_TS))
        def _():
            rows_path(r0)


def _expert_pass(c, block_e, nb_used, first_valid, x_sorted, gu_w, gu_b, down_w, prev):
    P = x_sorted.shape[0]
    nc = D_FF // MOE_TF
    first, last = c == 0, c == nc - 1
    rowblk = lambda i, be, nbu, nv: (jnp.minimum(i, nbu[0] - 1), 0)
    prev_spec = (pl.BlockSpec((None, 1, D_MODEL), lambda i, be, nbu, nv: (be[i], 0, 0)) if first
                 else pl.BlockSpec((MOE_TM, D_MODEL), rowblk))
    grid_spec = pltpu.PrefetchScalarGridSpec(
        num_scalar_prefetch=3,
        grid=(P // MOE_TM,),
        in_specs=[
            pl.BlockSpec((MOE_TM, PACK_W), rowblk),
            pl.BlockSpec((None, D_MODEL, MOE_TF), lambda i, be, nbu, nv: (be[i], 0, c)),
            pl.BlockSpec((None, D_MODEL, MOE_TF), lambda i, be, nbu, nv: (be[i], 0, nc + c)),
            pl.BlockSpec((None, 1, MOE_TF), lambda i, be, nbu, nv: (be[i], 0, c)),
            pl.BlockSpec((None, 1, MOE_TF), lambda i, be, nbu, nv: (be[i], 0, nc + c)),
            pl.BlockSpec((None, MOE_TF, D_MODEL), lambda i, be, nbu, nv: (be[i], c, 0)),
            prev_spec,
        ],
        out_specs=pl.BlockSpec((MOE_TM, PACK_W if last else D_MODEL), rowblk),
    )
    return pl.pallas_call(
        functools.partial(_expert_kernel, first=first, last=last),
        grid_spec=grid_spec,
        out_shape=jax.ShapeDtypeStruct((P, PACK_W), U32) if last else jax.ShapeDtypeStruct((P, D_MODEL), F32),
        compiler_params=_params(("arbitrary",)),
        name="experts_%d" % c,
    )(block_e, nb_used, first_valid, x_sorted, gu_w, gu_w, gu_b, gu_b, down_w, prev)


def _combine_kernel(x_ref, y_ref, gate_ref, g_ref, o_ref):
    acc = x_ref[...]
    gates = gate_ref[...]
    for kk in range(TOP_K):
        acc = acc + gates[:, kk:kk + 1] * _unpack_pairs(y_ref[kk])
    o_ref[...] = _rms(acc, g_ref[...])


def _combine(x2, y_g, gates, g):
    T = x2.shape[0]
    return pl.pallas_call(
        _combine_kernel,
        grid=(T // CB_TM,),
        in_specs=[
            pl.BlockSpec((CB_TM, D_MODEL), lambda i: (i, 0)),
            pl.BlockSpec((TOP_K, CB_TM, PACK_W), lambda i: (0, i, 0)),
            pl.BlockSpec((CB_TM, LANES), lambda i: (i, 0)),
            pl.BlockSpec((1, D_MODEL), lambda i: (0, 0)),
        ],
        out_specs=pl.BlockSpec((CB_TM, D_MODEL), lambda i: (i, 0)),
        out_shape=jax.ShapeDtypeStruct((T, D_MODEL), F32),
        compiler_params=_params(("arbitrary",)),
        name="combine",
    )(x2, y_g, gates, g)


def _row(v):
    return v.reshape(1, -1).astype(F32)


def _layer(x, mem, norm_mix_g, w_in, gla_gate_w, gla_gate_b, gla_norm_g, gla_out_w, conv_w, conv_b,
           conv_out_w, mix_out_w, norm_xattn_g, norm_mem_g, xq_w, xk_w, xv_w, xo_w, norm_moe_g,
           router_w, router_b, gu_w, gu_b, down_w, down_b):
    T = x.shape[0]
    w_in_t = w_in.T
    w_main = _w_in_prep(w_in_t)
    a0 = 3 * D_MODEL
    w_alow_t = jnp.pad(w_in_t[a0:a0 + GLA_GATE_RANK], ((0, LANES - GLA_GATE_RANK), (0, 0)))
    gate_w_pad = jnp.pad(gla_gate_w, ((0, LANES - GLA_GATE_RANK), (0, 0)))
    conv_w8 = jnp.pad(conv_w, ((0, 8 - CONV_WIDTH), (0, 0)))

    p, a_low, gu_wb = _in_proj(x, _row(norm_mix_g), w_main, w_alow_t, gu_w.reshape(-1, 2 * D_FF))
    og, down_wb = _gla(p, a_low, gate_w_pad, _row(gla_gate_b), _row(gla_norm_g), down_w.reshape(-1, D_MODEL))
    gu_wb = gu_wb.reshape(gu_w.shape)
    down_wb = down_wb.reshape(down_w.shape)
    x1 = _merge(og, p, conv_w8, _row(conv_b), gla_out_w.astype(BF16), conv_out_w.astype(BF16),
                mix_out_w.astype(BF16), x)

    kmem, vmem = _mem_kv(mem, _row(norm_mem_g), xk_w.astype(BF16), xv_w.astype(BF16))
    rw = jnp.pad(router_w, ((0, 0), (0, LANES - N_EXPERTS))).astype(BF16)
    rb = jnp.pad(router_b, (0, LANES - N_EXPERTS)).reshape(1, LANES)
    x2, hm, idx, gates, rank, counts = _xattn_router(
        x1, _row(norm_xattn_g), xq_w.astype(BF16), kmem, vmem, xo_w.astype(BF16), _row(norm_moe_g), rw, rb)

    cnt = counts[0, :N_EXPERTS].astype(I32)
    padded = (cnt + MOE_TM - 1) // MOE_TM * MOE_TM
    pend = jnp.cumsum(padded)
    pstart = pend - padded
    first_slot = pend - cnt
    first_slot_row = jnp.pad(first_slot, (0, LANES - N_EXPERTS)).reshape(1, LANES).astype(F32)
    nb = (T * TOP_K) // MOE_TM + N_EXPERTS
    blk_start = (jnp.arange(nb, dtype=I32) * MOE_TM)[:, None]
    block_e = jnp.minimum(jnp.sum(pend[None, :] <= blk_start, axis=1), N_EXPERTS - 1).astype(I32)
    nb_used = (pend[-1] // MOE_TM).astype(I32).reshape(1)
    in_group = (pstart[None, :] <= blk_start) & (blk_start < pend[None, :])
    first_valid = jnp.sum(jnp.where(in_group, jnp.clip(first_slot[None, :] - blk_start, 0, MOE_TM), 0),
                          axis=1).astype(I32)

    dest = _dispatch(idx, rank, first_slot_row)[:, :TOP_K]
    dest_chunks = dest.reshape(T // SC_G, SC_G, TOP_K).transpose(0, 2, 1)
    x_sorted = _sc_dispatch_scatter(hm, dest_chunks, nb * MOE_TM)

    gu_b3 = gu_b.reshape(N_EXPERTS, 1, 2 * D_FF)
    y = down_b.reshape(N_EXPERTS, 1, D_MODEL)
    for c in range(D_FF // MOE_TF):
        y = _expert_pass(c, block_e, nb_used, first_valid, x_sorted, gu_wb, gu_b3, down_wb, y)
    y_g = _sc_combine_gather(y, dest.T.reshape(-1)).reshape(TOP_K, T, PACK_W)
    return x2, y_g, gates


def kernel(x, mem, norm_mix_g, w_in, gla_gate_w, gla_gate_b, gla_norm_g, gla_out_w, conv_w, conv_b, conv_out_w, mix_out_w, norm_xattn_g, norm_mem_g, xq_w, xk_w, xv_w, xo_w, norm_moe_g, router_w, router_b, expert_gu_w, expert_gu_b, expert_down_w, expert_down_b, norm_final_g):
    assert x.shape[0] == 1 and mem.shape[0] == 1 and w_in.shape[0] == 1
    x2, y_g, gates = _layer(
        x[0], mem[0], norm_mix_g[0], w_in[0], gla_gate_w[0], gla_gate_b[0], gla_norm_g[0], gla_out_w[0],
        conv_w[0], conv_b[0], conv_out_w[0], mix_out_w[0], norm_xattn_g[0], norm_mem_g[0], xq_w[0], xk_w[0],
        xv_w[0], xo_w[0], norm_moe_g[0], router_w[0], router_b[0], expert_gu_w[0], expert_gu_b[0],
        expert_down_w[0], expert_down_b[0])
    out = _combine(x2, y_g, gates, _row(norm_final_g))
    return out[None]
```

```python
import functools

import jax
import jax.numpy as jnp
from jax import lax
from jax.experimental import pallas as pl
from jax.experimental.pallas import tpu as pltpu
from jax.experimental.pallas import tpu_sc as plsc

F32 = jnp.float32
BF16 = jnp.bfloat16
I32 = jnp.int32
U32 = jnp.uint32

D_MODEL = 2048
MEM_LEN = 256
GLA_HEADS = 4
GLA_DK = 256
GLA_DV = 512
GLA_GATE_RANK = 16
GLA_GATE_TEMP = 16.0
CONV_WIDTH = 3
XATTN_HEADS = 4
XATTN_HEAD_DIM = 128
XATTN_DIM = XATTN_HEADS * XATTN_HEAD_DIM
N_EXPERTS = 32
TOP_K = 4
D_FF = D_MODEL
SWIGLU_LIMIT = 7.0
SWIGLU_ALPHA = 1.702
NORM_EPS = 1e-5
LOG2E = 1.4426950408889634

LANES = 128
SC_CORES, SC_SUBCORES, SC_LANES = 2, 16, 16
PACK_W = D_MODEL // 2
P_COLS = 8 * D_MODEL
VMEM_LIMIT = 56 * 1024 * 1024

IN_TM, IN_TN = 1024, 1024
GLA_CH = 256
GLA_RB = 512
GLA_HG = 2
MG_TM = 256
XA_TM = 512
DP_TM = 2048
MOE_TM = 512
MOE_TS = 128
MOE_TF = 1024
CB_TM = 512
SC_G = 32
SC_GG, SC_NBUF = 16, 4


def _params(sem):
    return pltpu.CompilerParams(dimension_semantics=sem, vmem_limit_bytes=VMEM_LIMIT)


def _rms(x, g):
    return x * lax.rsqrt(jnp.mean(x * x, axis=-1, keepdims=True) + NORM_EPS) * g


def _dot(a, b):
    return jnp.dot(a, b, preferred_element_type=F32)


def _dot_nt(a, b):
    return lax.dot_general(a, b, (((1,), (1,)), ((), ())), preferred_element_type=F32)


def _dot_tn(a, b):
    return lax.dot_general(a, b, (((0,), (0,)), ((), ())), preferred_element_type=F32)


def _split2(x):
    hi = x.astype(BF16)
    lo = (x - hi.astype(F32)).astype(BF16)
    return hi, lo


def _split3(x):
    hi = x.astype(BF16)
    r = x - hi.astype(F32)
    mid = r.astype(BF16)
    lo = (r - mid.astype(F32)).astype(BF16)
    return hi, mid, lo


def _sigmoid(x):
    return 1.0 / (1.0 + jnp.exp(-x))


def _pack_pairs(x):
    w = x.shape[1] // 2
    u = lax.bitcast_convert_type(x.astype(BF16).astype(F32), U32)
    return (u[:, :w] >> 16) | u[:, w:]


def _unpack_pairs(u):
    lo = lax.bitcast_convert_type(u << 16, F32)
    hi = lax.bitcast_convert_type(u & jnp.uint32(0xFFFF0000), F32)
    return jnp.concatenate([lo, hi], axis=1)


def _side_cast_spec(w2d, n_steps, step_of):
    rows, cols = w2d.shape
    assert rows % n_steps == 0
    return pl.BlockSpec((rows // n_steps, cols), lambda *ids: (step_of(*ids), 0))


def _winprep_kernel(wt_ref, main_ref):
    main_ref[...] = wt_ref[...].T.astype(BF16)


def _w_in_prep(w_in_t):
    rows, d = w_in_t.shape
    a0 = 3 * D_MODEL
    assert rows == P_COLS + GLA_GATE_RANK and a0 % IN_TN == 0
    n_left = a0 // IN_TN
    assert IN_TN % GLA_GATE_RANK == 0
    row0 = lambda j: ((j * (IN_TN // GLA_GATE_RANK) + (j >= n_left).astype(I32)) * GLA_GATE_RANK, 0)
    return pl.pallas_call(
        _winprep_kernel,
        grid=(P_COLS // IN_TN,),
        in_specs=[pl.BlockSpec((pl.Element(IN_TN), pl.Element(d)), row0)],
        out_specs=pl.BlockSpec((d, IN_TN), lambda j: (0, j)),
        out_shape=jax.ShapeDtypeStruct((d, P_COLS), BF16),
        compiler_params=_params(("arbitrary",)),
        name="w_in_prep",
    )(w_in_t)


def _inproj_kernel(x_ref, g_ref, w_ref, wa_ref, cin_ref, p_ref, a_ref, cout_ref, h_scr):
    @pl.when(pl.program_id(1) == 0)
    def _():
        hb = _rms(x_ref[...], g_ref[...]).astype(BF16)
        h_scr[...] = hb
        a_ref[...] = _dot_nt(hb, wa_ref[...].astype(BF16))

    p_ref[...] = _dot(h_scr[...], w_ref[...]).astype(BF16)
    cout_ref[...] = cin_ref[...].astype(BF16)


def _in_proj(x, g, w_main, w_alow_t, w_cast):
    T = x.shape[0]
    ni, nj = T // IN_TM, P_COLS // IN_TN
    cast_spec = _side_cast_spec(w_cast, ni * nj, lambda i, j: i * nj + j)
    return pl.pallas_call(
        _inproj_kernel,
        grid=(ni, nj),
        in_specs=[
            pl.BlockSpec((IN_TM, D_MODEL), lambda i, j: (i, 0)),
            pl.BlockSpec((1, D_MODEL), lambda i, j: (0, 0)),
            pl.BlockSpec((D_MODEL, IN_TN), lambda i, j: (0, j)),
            pl.BlockSpec((LANES, D_MODEL), lambda i, j: (0, 0)),
            cast_spec,
        ],
        out_specs=[
            pl.BlockSpec((IN_TM, IN_TN), lambda i, j: (i, j)),
            pl.BlockSpec((IN_TM, LANES), lambda i, j: (i, 0)),
            cast_spec,
        ],
        out_shape=[
            jax.ShapeDtypeStruct((T, P_COLS), BF16),
            jax.ShapeDtypeStruct((T, LANES), F32),
            jax.ShapeDtypeStruct(w_cast.shape, BF16),
        ],
        scratch_shapes=[pltpu.VMEM((IN_TM, D_MODEL), BF16)],
        compiler_params=_params(("arbitrary", "arbitrary")),
        name="in_proj",
    )(x, g, w_main, w_alow_t, w_cast)


def _gla_level_weights(b, la, row, signs):
    ch, dk = b.shape
    out = []
    s = ch // 2
    while s >= 4:
        bb = b.reshape(ch // (2 * s), 2 * s, dk)
        d = (bb - bb[:, s - 1:s, :]).reshape(ch, dk)
        out.append((s, jnp.exp2(d * signs[s])))
        s //= 2
    la_prev = pltpu.roll(la, 1, 0)
    la_next = pltpu.roll(la, ch - 1, 0)
    r4 = row & 3
    d2 = jnp.where(r4 == 0, la_next, jnp.where(r4 == 1, 0.0, jnp.where(r4 == 2, la, la + la_prev)))
    out.append((2, jnp.exp2(d2)))
    d1 = jnp.where((row & 1) == 1, la, 0.0)
    out.append((1, jnp.exp2(d1)))
    return out


def _gla_kernel(q_ref, k_ref, v_ref, r_ref, a_ref, gw_ref, gb_ref, ng_ref, cin_ref, o_ref, cout_ref, st_scr):
    ch = GLA_CH
    cout_ref[...] = cin_ref[...].astype(BF16)

    @pl.when(pl.program_id(1) == 0)
    def _():
        st_scr[...] = jnp.zeros_like(st_scr)

    row = lax.broadcasted_iota(jnp.int32, (ch, 1), 0)
    col = lax.broadcasted_iota(jnp.int32, (1, ch), 1)
    tril = (row >= col).astype(BF16)
    levels, signs = [], {}
    s = ch // 2
    while s >= 1:
        levels.append((s, ((row ^ col) < 2 * s) & ((row & (2 * s - 1)) >= s) & ((col & (2 * s - 1)) < s)))
        signs[s] = jnp.broadcast_to(jnp.where((row & (2 * s - 1)) >= s, 1.0, -1.0), (ch, GLA_DK))
        s //= 2
    gw_hi, gw_lo = _split2(gw_ref[...] * (LOG2E / GLA_GATE_TEMP))
    gate_b = gb_ref[...] * (LOG2E / GLA_GATE_TEMP)
    norm_g = ng_ref[...]
    q_scale = jnp.asarray(GLA_DK ** -0.5, BF16)

    def head(sl, hh, a_hi, a_lo):
        ks = slice(hh * GLA_DK, (hh + 1) * GLA_DK)
        vs = slice(hh * GLA_DV, (hh + 1) * GLA_DV)
        qb = q_ref[sl, ks] * q_scale
        kb = k_ref[sl, ks]
        q, k = qb.astype(F32), kb.astype(F32)
        v = v_ref[sl, vs]
        z = _dot(a_hi, gw_hi[:, ks]) + _dot(a_lo, gw_hi[:, ks]) + _dot(a_hi, gw_lo[:, ks]) + gate_b[:, ks]
        soft = jnp.log2(1.0 + jnp.exp2(jnp.abs(z) * -GLA_GATE_TEMP))
        la = jnp.minimum(z, 0.0) - soft * (1.0 / GLA_GATE_TEMP)
        l_hi, l_mid, l_lo = _split3(la)
        b = _dot(tril, l_hi) + _dot(tril, l_mid) + _dot(tril, l_lo)

        scores = jnp.where(row == col, _dot_nt(qb, kb), 0.0)
        for (s, mask), (_, w) in zip(levels, _gla_level_weights(b, la, row, signs)):
            scores = jnp.where(mask, _dot_nt((q * w).astype(BF16), (k * w).astype(BF16)), scores)
        o = _dot(scores.astype(BF16), v)

        st = st_scr[hh]
        o = o + _dot_nt((q * jnp.exp2(b)).astype(BF16), st.astype(BF16))
        b_last = b[ch - 1:ch, :]
        kd = (k * jnp.exp2(b_last - b)).astype(BF16)
        st_scr[hh] = st * jnp.exp2(b_last) + _dot_tn(v, kd)

        o = _rms(o, norm_g)
        r = r_ref[sl, vs].astype(F32)
        o_ref[sl, vs] = (o * (r * _sigmoid(r))).astype(BF16)

    def chunk(c, carry):
        sl = pl.ds(pl.multiple_of(c * ch, ch), ch)
        a_hi, a_lo = _split2(a_ref[sl, :])
        for hh in range(GLA_HG):
            head(sl, hh, a_hi, a_lo)
        return carry

    lax.fori_loop(0, GLA_RB // ch, chunk, 0)


def _gla(p, a_low, gate_w_pad, gate_b, norm_g, w_cast):
    T = p.shape[0]
    ni = T // GLA_RB
    ng = GLA_HEADS // GLA_HG
    kw, vw = GLA_HG * GLA_DK, GLA_HG * GLA_DV
    cast_spec = _side_cast_spec(w_cast, ng * ni, lambda h, i: h * ni + i)
    return pl.pallas_call(
        _gla_kernel,
        grid=(ng, ni),
        in_specs=[
            pl.BlockSpec((GLA_RB, kw), lambda h, i: (i, h)),
            pl.BlockSpec((GLA_RB, kw), lambda h, i: (i, ng + h)),
            pl.BlockSpec((GLA_RB, vw), lambda h, i: (i, ng + h)),
            pl.BlockSpec((GLA_RB, vw), lambda h, i: (i, 2 * ng + h)),
            pl.BlockSpec((GLA_RB, LANES), lambda h, i: (i, 0)),
            pl.BlockSpec((LANES, kw), lambda h, i: (0, h)),
            pl.BlockSpec((1, kw), lambda h, i: (0, h)),
            pl.BlockSpec((1, GLA_DV), lambda h, i: (0, 0)),
            cast_spec,
        ],
        out_specs=[pl.BlockSpec((GLA_RB, vw), lambda h, i: (i, h)), cast_spec],
        out_shape=[jax.ShapeDtypeStruct((T, GLA_HEADS * GLA_DV), BF16),
                   jax.ShapeDtypeStruct(w_cast.shape, BF16)],
        scratch_shapes=[pltpu.VMEM((GLA_HG, GLA_DV, GLA_DK), F32)],
        compiler_params=_params(("arbitrary", "arbitrary")),
        name="gla",
    )(p, p, p, p, a_low, gate_w_pad, gate_b, norm_g, w_cast)


def _merge_kernel(og_ref, cb_ref, cc_ref, ch_ref, ccp_ref, chp_ref, cw_ref, cbias_ref,
                  ga_ref, gb_ref, wa_ref, wb_ref, wm_ref, x_ref, o_ref):
    tm = cc_ref.shape[0]
    u = cc_ref[...].astype(F32) * ch_ref[...].astype(F32)
    up = ccp_ref[...].astype(F32) * chp_ref[...].astype(F32)
    up = jnp.where(pl.program_id(0) == 0, 0.0, up)
    hp = up.shape[0]
    row = lax.broadcasted_iota(jnp.int32, (tm, 1), 0)
    u1 = jnp.where(row == 0, up[hp - 1:hp, :], pltpu.roll(u, 1, 0))
    u2 = jnp.where(row == 0, up[hp - 2:hp - 1, :],
                   jnp.where(row == 1, up[hp - 1:hp, :], pltpu.roll(u, 2, 0)))
    cw = cw_ref[...]
    uc = cw[0:1, :] * u2 + cw[1:2, :] * u1 + cw[2:3, :] * u + cbias_ref[...]
    zb = (cb_ref[...].astype(F32) * uc).astype(BF16)

    ya = _dot(og_ref[...], wa_ref[...])
    yb = _dot(zb, wb_ref[...])
    merged = (_sigmoid(ga_ref[...].astype(F32)) * ya + _sigmoid(gb_ref[...].astype(F32)) * yb).astype(BF16)
    o_ref[...] = x_ref[...] + _dot(merged, wm_ref[...])


def _merge(og, p, conv_w8, conv_b, gla_out_w, conv_out_w, mix_out_w, x):
    T = og.shape[0]
    halo = 16
    prev = lambda blk: (lambda i: (jnp.maximum(i * (MG_TM // halo) - 1, 0), blk))
    pblk = lambda blk: pl.BlockSpec((MG_TM, D_MODEL), lambda i: (i, blk))
    const = lambda shape: pl.BlockSpec(shape, lambda i: (0, 0), pipeline_mode=pl.Buffered(1))
    return pl.pallas_call(
        _merge_kernel,
        grid=(T // MG_TM,),
        in_specs=[
            pblk(0),
            pblk(3), pblk(4), pblk(5),
            pl.BlockSpec((halo, D_MODEL), prev(4)),
            pl.BlockSpec((halo, D_MODEL), prev(5)),
            const((8, D_MODEL)),
            const((1, D_MODEL)),
            pblk(6), pblk(7),
            const((D_MODEL, D_MODEL)), const((D_MODEL, D_MODEL)), const((D_MODEL, D_MODEL)),
            pblk(0),
        ],
        out_specs=pblk(0),
        out_shape=jax.ShapeDtypeStruct((T, D_MODEL), F32),
        compiler_params=_params(("arbitrary",)),
        name="merge",
    )(og, p, p, p, p, p, conv_w8, conv_b, p, p, gla_out_w, conv_out_w, mix_out_w, x)


def _memkv_kernel(m_ref, g_ref, wk_ref, wv_ref, k_ref, v_ref):
    mn = _rms(m_ref[...], g_ref[...]).astype(BF16)
    k_ref[...] = _dot(mn, wk_ref[...]).astype(BF16)
    v_ref[...] = _dot(mn, wv_ref[...]).astype(BF16)


def _mem_kv(mem, g, wk, wv):
    return pl.pallas_call(
        _memkv_kernel,
        out_shape=[jax.ShapeDtypeStruct((MEM_LEN, XATTN_DIM), BF16)] * 2,
        compiler_params=pltpu.CompilerParams(vmem_limit_bytes=VMEM_LIMIT),
        name="mem_kv",
    )(mem, g, wk, wv)


def _xattn_tile(x_ref, g_ref, wq_ref, k_ref, v_ref, wo_ref):
    x = x_ref[...]
    hq = _rms(x, g_ref[...]).astype(BF16)
    q = (_dot(hq, wq_ref[...]) * (XATTN_HEAD_DIM ** -0.5)).astype(BF16)
    outs = []
    for h in range(XATTN_HEADS):
        cs = slice(h * XATTN_HEAD_DIM, (h + 1) * XATTN_HEAD_DIM)
        s = _dot_nt(q[:, cs], k_ref[:, cs])
        s = s - jnp.max(s, axis=-1, keepdims=True)
        e = jnp.exp(s)
        p = e / jnp.sum(e, axis=-1, keepdims=True)
        outs.append(_dot(p.astype(BF16), v_ref[:, cs]))
    o = jnp.concatenate(outs, axis=-1).astype(BF16)
    return x + _dot(o, wo_ref[...])


def _xattn_router_kernel(x_ref, gx_ref, wq_ref, k_ref, v_ref, wo_ref, gm_ref, rw_ref, rb_ref,
                         x2_ref, hm_ref, idx_ref, gate_ref, rank_ref, cnt_ref, carry_scr):
    x2 = _xattn_tile(x_ref, gx_ref, wq_ref, k_ref, v_ref, wo_ref)
    x2_ref[...] = x2
    _route_tile(x2, gm_ref, rw_ref, rb_ref, hm_ref, idx_ref, gate_ref, rank_ref, cnt_ref, carry_scr)


def _route_tile(x2, g_ref, w_ref, b_ref, hm_ref, idx_ref, gate_ref, rank_ref, cnt_ref, carry_scr):
    tm = x2.shape[0]

    @pl.when(pl.program_id(0) == 0)
    def _():
        carry_scr[...] = jnp.zeros_like(carry_scr)

    h = _rms(x2, g_ref[...])
    hb = h.astype(BF16)
    hm_ref[...] = _pack_pairs(h)
    lane = lax.broadcasted_iota(jnp.int32, (tm, LANES), 1)
    logits = jnp.where(lane < N_EXPERTS, _dot(hb, w_ref[...]) + b_ref[...], -jnp.inf)

    vals, idxs = [], []
    work = logits
    lane_f = lane.astype(F32)
    for _ in range(TOP_K):
        m = jnp.max(work, axis=-1, keepdims=True)
        am = jnp.min(jnp.where(work == m, lane_f, float(LANES)), axis=-1, keepdims=True).astype(I32)
        vals.append(m)
        idxs.append(am)
        work = jnp.where(lane == am, -jnp.inf, work)
    es = [jnp.exp(v - vals[0]) for v in vals]
    inv = 1.0 / (es[0] + es[1] + es[2] + es[3])

    onehot = (work != logits).astype(BF16)
    r_ = lax.broadcasted_iota(jnp.int32, (tm, tm), 0)
    c_ = lax.broadcasted_iota(jnp.int32, (tm, tm), 1)
    before = _dot((r_ > c_).astype(BF16), onehot) + carry_scr[...]
    carry_scr[...] = carry_scr[...] + jnp.sum(onehot.astype(F32), axis=0, keepdims=True)
    cnt_ref[...] = carry_scr[...]

    idx_out = jnp.zeros((tm, LANES), jnp.int32)
    gate_out = jnp.zeros((tm, LANES), F32)
    rank_out = jnp.zeros((tm, LANES), jnp.int32)
    for kk in range(TOP_K):
        rk = jnp.sum(jnp.where(lane == idxs[kk], before, 0.0), axis=-1, keepdims=True)
        idx_out = jnp.where(lane == kk, idxs[kk], idx_out)
        gate_out = jnp.where(lane == kk, es[kk] * inv, gate_out)
        rank_out = jnp.where(lane == kk, rk.astype(jnp.int32), rank_out)
    idx_ref[...] = idx_out
    gate_ref[...] = gate_out
    rank_ref[...] = rank_out


def _xattn_router(x1, gx, wq, kmem, vmem, wo, gm, rw_pad, rb_pad):
    T = x1.shape[0]
    full = lambda shape: pl.BlockSpec(shape, lambda i: (0, 0))
    row_blk = pl.BlockSpec((XA_TM, LANES), lambda i: (i, 0))
    return pl.pallas_call(
        _xattn_router_kernel,
        grid=(T // XA_TM,),
        in_specs=[
            pl.BlockSpec((XA_TM, D_MODEL), lambda i: (i, 0)),
            full((1, D_MODEL)),
            full((D_MODEL, XATTN_DIM)),
            full((MEM_LEN, XATTN_DIM)),
            full((MEM_LEN, XATTN_DIM)),
            full((XATTN_DIM, D_MODEL)),
            full((1, D_MODEL)),
            full((D_MODEL, LANES)),
            full((1, LANES)),
        ],
        out_specs=[
            pl.BlockSpec((XA_TM, D_MODEL), lambda i: (i, 0)),
            pl.BlockSpec((XA_TM, PACK_W), lambda i: (i, 0)),
            row_blk, row_blk, row_blk,
            full((1, LANES)),
        ],
        out_shape=[
            jax.ShapeDtypeStruct((T, D_MODEL), F32),
            jax.ShapeDtypeStruct((T, PACK_W), U32),
            jax.ShapeDtypeStruct((T, LANES), jnp.int32),
            jax.ShapeDtypeStruct((T, LANES), F32),
            jax.ShapeDtypeStruct((T, LANES), jnp.int32),
            jax.ShapeDtypeStruct((1, LANES), F32),
        ],
        scratch_shapes=[pltpu.VMEM((1, LANES), F32)],
        compiler_params=_params(("arbitrary",)),
        name="xattn_router",
    )(x1, gx, wq, kmem, vmem, wo, gm, rw_pad, rb_pad)


def _dispatch_kernel(idx_ref, rank_ref, pstart_ref, dest_ref):
    tm = idx_ref.shape[0]
    lane = lax.broadcasted_iota(I32, (tm, LANES), 1)
    idx = idx_ref[...]
    pstart = pstart_ref[...]
    dest = rank_ref[...]
    for kk in range(TOP_K):
        start = jnp.sum(jnp.where(lane == idx[:, kk:kk + 1], pstart, 0.0), axis=-1, keepdims=True)
        dest = dest + jnp.where(lane == kk, start.astype(I32), 0)
    dest_ref[...] = dest


def _dispatch(idx, rank, pstart_row):
    T = idx.shape[0]
    blk = pl.BlockSpec((DP_TM, LANES), lambda i: (i, 0))
    return pl.pallas_call(
        _dispatch_kernel,
        grid=(T // DP_TM,),
        in_specs=[blk, blk, pl.BlockSpec((1, LANES), lambda i: (0, 0))],
        out_specs=blk,
        out_shape=jax.ShapeDtypeStruct((T, LANES), I32),
        compiler_params=_params(("arbitrary",)),
        name="dispatch",
    )(idx, rank, pstart_row)


def _sc_mesh():
    return plsc.VectorSubcoreMesh(core_axis_name="c", subcore_axis_name="s",
                                  num_cores=SC_CORES, num_subcores=SC_SUBCORES)


def _sc_gather_rows(table_hbm, idx_v, out_hbm, out_base, n_steps, rows_v, gsem, wsem):
    nbuf = SC_NBUF

    def gather(s, b):
        return pltpu.make_async_copy(table_hbm.at[idx_v.at[pl.ds(s * SC_GG, SC_GG)]], rows_v.at[b], gsem.at[b])

    def write(s, b):
        return pltpu.make_async_copy(rows_v.at[b], out_hbm.at[pl.ds(out_base + s * SC_GG, SC_GG)], wsem.at[b])

    for j in range(nbuf - 1):
        gather(j, j).start()

    @pl.loop(0, n_steps, step=nbuf)
    def _(s0):
        for b in range(nbuf):
            s = s0 + b
            gather(s, b).wait()
            write(s, b).start()
            refill = (b + nbuf - 1) % nbuf

            @pl.when(s + nbuf - 1 < n_steps)
            def _():
                @pl.when(s >= 1)
                def _():
                    write(s - 1, refill).wait()
                gather(s + nbuf - 1, refill).start()

    for j in range(nbuf):
        write(n_steps - nbuf + j, j).wait()


def _sc_dispatch_scatter(table, dest_chunks, n_slots):
    T, width = table.shape
    n_workers = SC_CORES * SC_SUBCORES
    per_w = T // n_workers
    n_steps = per_w // SC_G
    assert T % n_workers == 0 and per_w % (2 * SC_G) == 0
    assert dest_chunks.shape == (T // SC_G, TOP_K, SC_G)

    @functools.partial(
        pl.kernel, mesh=_sc_mesh(),
        out_type=jax.ShapeDtypeStruct((n_slots, width), table.dtype),
        scratch_types=[
            pltpu.VMEM((n_steps, TOP_K, SC_G), I32),
            pltpu.VMEM((2, SC_G, width), table.dtype),
            pltpu.SemaphoreType.DMA((2,)),
            pltpu.SemaphoreType.DMA((2,)),
        ],
        name="sc_dispatch_scatter",
    )
    def k(table_hbm, dest_hbm, out_hbm, dst_v, rows_v, rsem, wsem):
        wid = lax.axis_index("s") * SC_CORES + lax.axis_index("c")
        pltpu.sync_copy(dest_hbm.at[pl.ds(wid * n_steps, n_steps)], dst_v)

        def read(s, b):
            return pltpu.make_async_copy(table_hbm.at[pl.ds(wid * per_w + s * SC_G, SC_G)], rows_v.at[b], rsem.at[b])

        def scatter(s, kk, b):
            return pltpu.make_async_copy(rows_v.at[b], out_hbm.at[dst_v.at[s, kk]], wsem.at[b])

        read(0, 0).start()

        @pl.loop(0, n_steps, step=2)
        def _(s0):
            for b in range(2):
                s = s0 + b
                read(s, b).wait()

                @pl.when(s + 1 < n_steps)
                def _():
                    @pl.when(s >= 1)
                    def _():
                        for kk in range(TOP_K):
                            scatter(s - 1, kk, 1 - b).wait()
                    read(s + 1, 1 - b).start()

                for kk in range(TOP_K):
                    scatter(s, kk, b).start()

        for kk in range(TOP_K):
            scatter(n_steps - 2, kk, 0).wait()
            scatter(n_steps - 1, kk, 1).wait()

    return k(table, dest_chunks)


def _sc_combine_gather(table, dest_flat):
    n_rows, width = dest_flat.shape[0], table.shape[1]
    n_workers = SC_CORES * SC_SUBCORES
    per_w = n_rows // n_workers
    n_steps = per_w // SC_GG
    assert n_rows % n_workers == 0 and per_w % (SC_NBUF * SC_GG) == 0

    @functools.partial(
        pl.kernel, mesh=_sc_mesh(),
        out_type=jax.ShapeDtypeStruct((n_rows, width), table.dtype),
        scratch_types=[
            pltpu.VMEM((per_w,), I32),
            pltpu.VMEM((SC_NBUF, SC_GG, width), table.dtype),
            pltpu.SemaphoreType.DMA((SC_NBUF,)),
            pltpu.SemaphoreType.DMA((SC_NBUF,)),
        ],
        name="sc_combine_gather",
    )
    def k(table_hbm, dest_hbm, out_hbm, dst_v, rows_v, gsem, wsem):
        wid = lax.axis_index("s") * SC_CORES + lax.axis_index("c")
        lo = wid * per_w
        pltpu.sync_copy(dest_hbm.at[pl.ds(lo, per_w)], dst_v)
        _sc_gather_rows(table_hbm, dst_v, out_hbm, lo, n_steps, rows_v, gsem, wsem)

    return k(table, dest_flat)


def _expert_kernel(be_ref, nb_ref, fv_ref, x_ref, wg_ref, wu_ref, bg_ref, bu_ref, wd_ref, prev_ref, y_ref, *,
                   first, last):
    i = pl.program_id(0)
    fv = fv_ref[i]
    tm = x_ref.shape[0]

    def rows_path(r0):
        row = r0 + lax.broadcasted_iota(I32, (tm - r0, 1), 0)
        x = _unpack_pairs(jnp.where(row >= fv, x_ref[r0:, :], jnp.uint32(0))).astype(BF16)
        gate = jnp.minimum(_dot(x, wg_ref[...]) + bg_ref[...], SWIGLU_LIMIT)
        up = jnp.clip(_dot(x, wu_ref[...]) + bu_ref[...], -SWIGLU_LIMIT, SWIGLU_LIMIT)
        act = (up + 1.0) * gate * _sigmoid(SWIGLU_ALPHA * gate)
        prev = prev_ref[...] if first else prev_ref[r0:, :]
        y = prev + _dot(act.astype(BF16), wd_ref[...])
        y_ref[r0:, :] = _pack_pairs(y) if last else y
        if r0:
            y_ref[:r0, :] = jnp.zeros((r0, y_ref.shape[1]), y_ref.dtype)

    for r0 in range(0, tm, MOE_TS):
        @pl.when((i < nb_ref[0]) & (fv >= r0) & (fv < r0 + MOE_TS))
        def _():
            rows_path(r0)


def _expert_pass(c, block_e, nb_used, first_valid, x_sorted, gu_w, gu_b, down_w, prev):
    P = x_sorted.shape[0]
    nc = D_FF // MOE_TF
    first, last = c == 0, c == nc - 1
    rowblk = lambda i, be, nbu, nv: (jnp.minimum(i, nbu[0] - 1), 0)
    prev_spec = (pl.BlockSpec((None, 1, D_MODEL), lambda i, be, nbu, nv: (be[i], 0, 0)) if first
                 else pl.BlockSpec((MOE_TM, D_MODEL), rowblk))
    grid_spec = pltpu.PrefetchScalarGridSpec(
        num_scalar_prefetch=3,
        grid=(P // MOE_TM,),
        in_specs=[
            pl.BlockSpec((MOE_TM, PACK_W), rowblk),
            pl.BlockSpec((None, D_MODEL, MOE_TF), lambda i, be, nbu, nv: (be[i], 0, c)),
            pl.BlockSpec((None, D_MODEL, MOE_TF), lambda i, be, nbu, nv: (be[i], 0, nc + c)),
            pl.BlockSpec((None, 1, MOE_TF), lambda i, be, nbu, nv: (be[i], 0, c)),
            pl.BlockSpec((None, 1, MOE_TF), lambda i, be, nbu, nv: (be[i], 0, nc + c)),
            pl.BlockSpec((None, MOE_TF, D_MODEL), lambda i, be, nbu, nv: (be[i], c, 0)),
            prev_spec,
        ],
        out_specs=pl.BlockSpec((MOE_TM, PACK_W if last else D_MODEL), rowblk),
    )
    return pl.pallas_call(
        functools.partial(_expert_kernel, first=first, last=last),
        grid_spec=grid_spec,
        out_shape=jax.ShapeDtypeStruct((P, PACK_W), U32) if last else jax.ShapeDtypeStruct((P, D_MODEL), F32),
        compiler_params=_params(("arbitrary",)),
        name="experts_%d" % c,
    )(block_e, nb_used, first_valid, x_sorted, gu_w, gu_w, gu_b, gu_b, down_w, prev)


def _combine_kernel(x_ref, y_ref, gate_ref, g_ref, o_ref):
    acc = x_ref[...]
    gates = gate_ref[...]
    for kk in range(TOP_K):
        acc = acc + gates[:, kk:kk + 1] * _unpack_pairs(y_ref[kk])
    o_ref[...] = _rms(acc, g_ref[...])


def _combine(x2, y_g, gates, g):
    T = x2.shape[0]
    return pl.pallas_call(
        _combine_kernel,
        grid=(T // CB_TM,),
        in_specs=[
            pl.BlockSpec((CB_TM, D_MODEL), lambda i: (i, 0)),
            pl.BlockSpec((TOP_K, CB_TM, PACK_W), lambda i: (0, i, 0)),
            pl.BlockSpec((CB_TM, LANES), lambda i: (i, 0)),
            pl.BlockSpec((1, D_MODEL), lambda i: (0, 0)),
        ],
        out_specs=pl.BlockSpec((CB_TM, D_MODEL), lambda i: (i, 0)),
        out_shape=jax.ShapeDtypeStruct((T, D_MODEL), F32),
        compiler_params=_params(("arbitrary",)),
        name="combine",
    )(x2, y_g, gates, g)


def _row(v):
    return v.reshape(1, -1).astype(F32)


def _layer(x, mem, norm_mix_g, w_in, gla_gate_w, gla_gate_b, gla_norm_g, gla_out_w, conv_w, conv_b,
           conv_out_w, mix_out_w, norm_xattn_g, norm_mem_g, xq_w, xk_w, xv_w, xo_w, norm_moe_g,
           router_w, router_b, gu_w, gu_b, down_w, down_b):
    T = x.shape[0]
    w_in_t = w_in.T
    w_main = _w_in_prep(w_in_t)
    a0 = 3 * D_MODEL
    w_alow_t = jnp.pad(w_in_t[a0:a0 + GLA_GATE_RANK], ((0, LANES - GLA_GATE_RANK), (0, 0)))
    gate_w_pad = jnp.pad(gla_gate_w, ((0, LANES - GLA_GATE_RANK), (0, 0)))
    conv_w8 = jnp.pad(conv_w, ((0, 8 - CONV_WIDTH), (0, 0)))

    p, a_low, gu_wb = _in_proj(x, _row(norm_mix_g), w_main, w_alow_t, gu_w.reshape(-1, 2 * D_FF))
    og, down_wb = _gla(p, a_low, gate_w_pad, _row(gla_gate_b), _row(gla_norm_g), down_w.reshape(-1, D_MODEL))
    gu_wb = gu_wb.reshape(gu_w.shape)
    down_wb = down_wb.reshape(down_w.shape)
    x1 = _merge(og, p, conv_w8, _row(conv_b), gla_out_w.astype(BF16), conv_out_w.astype(BF16),
                mix_out_w.astype(BF16), x)

    kmem, vmem = _mem_kv(mem, _row(norm_mem_g), xk_w.astype(BF16), xv_w.astype(BF16))
    rw = jnp.pad(router_w, ((0, 0), (0, LANES - N_EXPERTS))).astype(BF16)
    rb = jnp.pad(router_b, (0, LANES - N_EXPERTS)).reshape(1, LANES)
    x2, hm, idx, gates, rank, counts = _xattn_router(
        x1, _row(norm_xattn_g), xq_w.astype(BF16), kmem, vmem, xo_w.astype(BF16), _row(norm_moe_g), rw, rb)

    cnt = counts[0, :N_EXPERTS].astype(I32)
    padded = (cnt + MOE_TM - 1) // MOE_TM * MOE_TM
    pend = jnp.cumsum(padded)
    pstart = pend - padded
    first_slot = pend - cnt
    first_slot_row = jnp.pad(first_slot, (0, LANES - N_EXPERTS)).reshape(1, LANES).astype(F32)
    nb = (T * TOP_K) // MOE_TM + N_EXPERTS
    blk_start = (jnp.arange(nb, dtype=I32) * MOE_TM)[:, None]
    block_e = jnp.minimum(jnp.sum(pend[None, :] <= blk_start, axis=1), N_EXPERTS - 1).astype(I32)
    nb_used = (pend[-1] // MOE_TM).astype(I32).reshape(1)
    in_group = (pstart[None, :] <= blk_start) & (blk_start < pend[None, :])
    first_valid = jnp.sum(jnp.where(in_group, jnp.clip(first_slot[None, :] - blk_start, 0, MOE_TM), 0),
                          axis=1).astype(I32)

    dest = _dispatch(idx, rank, first_slot_row)[:, :TOP_K]
    dest_chunks = dest.reshape(T // SC_G, SC_G, TOP_K).transpose(0, 2, 1)
    x_sorted = _sc_dispatch_scatter(hm, dest_chunks, nb * MOE_TM)

    gu_b3 = gu_b.reshape(N_EXPERTS, 1, 2 * D_FF)
    y = down_b.reshape(N_EXPERTS, 1, D_MODEL)
    for c in range(D_FF // MOE_TF):
        y = _expert_pass(c, block_e, nb_used, first_valid, x_sorted, gu_wb, gu_b3, down_wb, y)
    y_g = _sc_combine_gather(y, dest.T.reshape(-1)).reshape(TOP_K, T, PACK_W)
    return x2, y_g, gates


def kernel(x, mem, norm_mix_g, w_in, gla_gate_w, gla_gate_b, gla_norm_g, gla_out_w, conv_w, conv_b, conv_out_w, mix_out_w, norm_xattn_g, norm_mem_g, xq_w, xk_w, xv_w, xo_w, norm_moe_g, router_w, router_b, expert_gu_w, expert_gu_b, expert_down_w, expert_down_b, norm_final_g):
    assert x.shape[0] == 1 and mem.shape[0] == 1 and w_in.shape[0] == 1
    x2, y_g, gates = _layer(
        x[0], mem[0], norm_mix_g[0], w_in[0], gla_gate_w[0], gla_gate_b[0], gla_norm_g[0], gla_out_w[0],
        conv_w[0], conv_b[0], conv_out_w[0], mix_out_w[0], norm_xattn_g[0], norm_mem_g[0], xq_w[0], xk_w[0],
        xv_w[0], xo_w[0], norm_moe_g[0], router_w[0], router_b[0], expert_gu_w[0], expert_gu_b[0],
        expert_down_w[0], expert_down_b[0])
    out = _combine(x2, y_g, gates, _row(norm_final_g))
    return out[None]
```
